```python
import jax, jax.numpy as jnp
from jax import lax
import numpy as np

D_MODEL = 1024
BATCH = 8
SEQ = 4096
DEPTH = 1

HEAD_DIM = 64
DIL_PAIRS = ((128, 1), (512, 4), (2048, 16))
N_HEADS_DIL = 6
HEADS_PER_DIL = N_HEADS_DIL // len(DIL_PAIRS)
N_HEADS_MOBA = 6
N_HEADS_MEM = 4
MEM_LEN = 256
MOBA_BLOCK = 256
MOBA_TOPK = 3
MOBA_Q_CHUNK = 128
BAND_BLOCK = 128
ROPE_THETA = 500000.0
ROPE_DIM = HEAD_DIM // 4
W_DIL = N_HEADS_DIL * HEAD_DIM
W_MOBA = N_HEADS_MOBA * HEAD_DIM
W_MEM = N_HEADS_MEM * HEAD_DIM
W_DIL_OUT = HEADS_PER_DIL * HEAD_DIM
IN_SPLITS = (W_DIL, 2 * W_DIL, 3 * W_DIL, 3 * W_DIL + W_MOBA, 3 * W_DIL + 2 * W_MOBA, 3 * W_DIL + 3 * W_MOBA)
W_IN = 3 * W_DIL + 3 * W_MOBA + W_MEM
N_BRANCH = 3
N_EXPERTS = 256
N_EXPERT_GROUPS = 8
TOPK_GROUPS = 4
TOP_K = 8
D_EXPERT = 256
D_SHARED = 256
ROUTED_SCALE = 2.5
EXPERT_BLOCK = 128
LN_EPS = 1e-5
ALPHA = (2.0 * DEPTH) ** 0.25
BETA = (8.0 * DEPTH) ** -0.25

kernel_name = 'hybrid_dilated_moba_memory_moe_deepnorm'


def layer_norm(x, g, b):
    xf = x.astype(jnp.float32)
    mu = jnp.mean(xf, -1, keepdims=True)
    var = jnp.mean(jnp.square(xf - mu), -1, keepdims=True)
    return ((xf - mu) * lax.rsqrt(var + LN_EPS) * g + b).astype(x.dtype)


def partial_rope(t, positions):
    half = ROPE_DIM // 2
    inv_freq = ROPE_THETA ** (-jnp.arange(half, dtype=jnp.float32) / half)
    ang = positions.astype(jnp.float32)[:, :, None] * inv_freq
    cos = jnp.cos(ang)[:, :, None, :]
    sin = jnp.sin(ang)[:, :, None, :]
    r1 = t[..., :half].astype(jnp.float32)
    r2 = t[..., half:ROPE_DIM].astype(jnp.float32)
    rot = jnp.concatenate([r1 * cos - r2 * sin, r2 * cos + r1 * sin], -1).astype(t.dtype)
    return jnp.concatenate([rot, t[..., ROPE_DIM:]], -1)


def dilated_group_attention(q, k, v, window, dilation):
    B, S, H, Dh = q.shape
    L = S // dilation
    span = window // dilation
    assert span <= BAND_BLOCK
    nblk = -(-L // BAND_BLOCK)
    Lp = nblk * BAND_BLOCK

    def strided(t):
        t = t.reshape(B, L, dilation, H, Dh).transpose(0, 2, 3, 1, 4)
        t = jnp.pad(t, ((0, 0), (0, 0), (0, 0), (0, Lp - L), (0, 0)))
        return t.reshape(B, dilation, H, nblk, BAND_BLOCK, Dh)

    qs, ks, vs = strided(q), strided(k), strided(v)

    def with_prev(t):
        prev = jnp.pad(t, ((0, 0), (0, 0), (0, 0), (1, 0), (0, 0), (0, 0)))[:, :, :, :-1]
        return jnp.concatenate([prev, t], axis=4)

    kb, vb = with_prev(ks), with_prev(vs)
    qi = jnp.arange(BAND_BLOCK)[:, None]
    kj = jnp.arange(2 * BAND_BLOCK)[None, :]
    dist = qi + BAND_BLOCK - kj
    blk = jnp.arange(nblk)[:, None, None]
    allowed = (dist >= 0) & (dist <= span) & (blk * BAND_BLOCK + kj - BAND_BLOCK >= 0)
    s = jnp.einsum('bdhnqc,bdhnkc->bdhnqk', qs, kb).astype(jnp.float32) * (Dh ** -0.5)
    s = jnp.where(allowed, s, -jnp.inf)
    lse = jax.nn.logsumexp(s, axis=-1)
    p = jnp.exp(s - lse[..., None]).astype(v.dtype)
    o = jnp.einsum('bdhnqk,bdhnkc->bdhnqc', p, vb)
    o = o.reshape(B, dilation, H, Lp, Dh)[:, :, :, :L].transpose(0, 3, 1, 2, 4).reshape(B, S, H, Dh)
    lse = lse.reshape(B, dilation, H, Lp)[..., :L].transpose(0, 3, 1, 2).reshape(B, S, H)
    return o, lse


def moba_attention(q, k, v):
    B, S, H, Dh = q.shape
    nb = -(-S // MOBA_BLOCK)
    sp = nb * MOBA_BLOCK
    nc = sp // MOBA_Q_CHUNK
    k_sel = min(MOBA_TOPK, nb)
    padw = ((0, 0), (0, sp - S), (0, 0), (0, 0))
    qp, kp, vp = jnp.pad(q, padw), jnp.pad(k, padw), jnp.pad(v, padw)
    kb = kp.reshape(B, nb, MOBA_BLOCK, H, Dh).transpose(0, 3, 1, 2, 4)
    vb = vp.reshape(B, nb, MOBA_BLOCK, H, Dh).transpose(0, 3, 1, 2, 4)
    kmean = jnp.mean(kb.astype(jnp.float32), axis=3)
    gate = jnp.einsum('bshc,bhnc->bhsn', qp.astype(jnp.float32), kmean)
    q_blk = jnp.arange(sp) // MOBA_BLOCK
    past = jnp.arange(nb)[None, :] < q_blk[:, None]
    gate = jnp.where(past, gate, -jnp.inf)
    _, sel = lax.top_k(gate, k_sel)
    qc = qp.reshape(B, nc, MOBA_Q_CHUNK, H, Dh).transpose(0, 1, 3, 2, 4).reshape(B * nc, H, MOBA_Q_CHUNK, Dh)
    selc = sel.reshape(B, H, nc, MOBA_Q_CHUNK, k_sel).transpose(0, 2, 1, 3, 4).reshape(B * nc, H, MOBA_Q_CHUNK, k_sel)
    flat_ids = jnp.arange(B * nc)
    b_ids = flat_ids // nc
    c_ids = flat_ids % nc
    scale = Dh ** -0.5
    h_ar = jnp.arange(H)[:, None, None]

    def chunk(args):
        qch, sch, b, c = args
        kbb, vbb = kb[b], vb[b]
        own = (c * MOBA_Q_CHUNK) // MOBA_BLOCK
        qpos = c * MOBA_Q_CHUNK + jnp.arange(MOBA_Q_CHUNK)
        ks, vs = kbb[h_ar, sch], vbb[h_ar, sch]
        k_own = lax.dynamic_index_in_dim(kbb, own, axis=1, keepdims=False)
        v_own = lax.dynamic_index_in_dim(vbb, own, axis=1, keepdims=False)
        s_sel = jnp.einsum('hqc,hqnkc->hqnk', qch, ks).astype(jnp.float32) * scale
        s_sel = jnp.where((jnp.arange(k_sel) < own)[None, None, :, None], s_sel, -jnp.inf)
        s_own = jnp.einsum('hqc,hkc->hqk', qch, k_own).astype(jnp.float32) * scale
        kpos = own * MOBA_BLOCK + jnp.arange(MOBA_BLOCK)
        s_own = jnp.where((kpos[None, :] <= qpos[:, None])[None], s_own, -jnp.inf)
        s_all = jnp.concatenate([s_sel.reshape(H, MOBA_Q_CHUNK, k_sel * MOBA_BLOCK), s_own], -1)
        p = jax.nn.softmax(s_all, axis=-1).astype(v.dtype)
        p_sel = p[..., :k_sel * MOBA_BLOCK].reshape(H, MOBA_Q_CHUNK, k_sel, MOBA_BLOCK)
        p_own = p[..., k_sel * MOBA_BLOCK:]
        return jnp.einsum('hqnk,hqnkc->hqc', p_sel, vs) + jnp.einsum('hqk,hkc->hqc', p_own, v_own)

    out = lax.map(chunk, (qc, selc, b_ids, c_ids))
    out = out.reshape(B, nc, H, MOBA_Q_CHUNK, Dh).transpose(0, 1, 3, 2, 4).reshape(B, sp, H * Dh)
    return out[:, :S]


def memory_cross_attention(q, mem, w_mem_kv):
    B, S, H, Dh = q.shape
    kv = (mem @ w_mem_kv).reshape(B, mem.shape[1], 2, H, Dh)
    k, v = kv[:, :, 0], kv[:, :, 1]
    s = jnp.einsum('bshc,bmhc->bhsm', q, k).astype(jnp.float32) * (Dh ** -0.5)
    p = jax.nn.softmax(s, axis=-1).astype(v.dtype)
    return jnp.einsum('bhsm,bmhc->bshc', p, v).reshape(B, S, H * Dh)


def moe_ffn(h, w_router, router_bias, w_e_gate, w_e_up, w_e_down, w_s_gate, w_s_up, w_s_down):
    B, S, D = h.shape
    N = B * S
    xt = h.reshape(N, D)
    scores = jax.nn.sigmoid((xt @ w_router).astype(jnp.float32))
    biased = scores + router_bias.astype(jnp.float32)
    grouped = biased.reshape(N, N_EXPERT_GROUPS, N_EXPERTS // N_EXPERT_GROUPS)
    group_score = jnp.sum(lax.top_k(grouped, 2)[0], axis=-1)
    _, top_groups = lax.top_k(group_score, TOPK_GROUPS)
    keep = jnp.any(top_groups[:, :, None] == jnp.arange(N_EXPERT_GROUPS)[None, None, :], axis=1)
    masked = jnp.where(keep[:, :, None], grouped, -jnp.inf).reshape(N, N_EXPERTS)
    _, eidx = lax.top_k(masked, TOP_K)
    gw = jnp.take_along_axis(scores, eidx, axis=-1)
    gw = gw / jnp.sum(gw, -1, keepdims=True) * ROUTED_SCALE
    A = N * TOP_K
    e_flat = eidx.reshape(A)
    tok_flat = jnp.arange(A, dtype=jnp.int32) // TOP_K
    w_flat = gw.reshape(A)
    order = jnp.argsort(e_flat)
    e_sorted = e_flat[order]
    counts = jnp.bincount(e_flat, length=N_EXPERTS)
    padded = (counts + EXPERT_BLOCK - 1) // EXPERT_BLOCK * EXPERT_BLOCK
    pad_end = jnp.cumsum(padded)
    pad_start = pad_end - padded
    start = jnp.cumsum(counts) - counts
    dest = pad_start[e_sorted] + jnp.arange(A) - start[e_sorted]
    R = A + N_EXPERTS * EXPERT_BLOCK
    nblk = R // EXPERT_BLOCK
    row_tok = jnp.zeros((R,), jnp.int32).at[dest].set(tok_flat[order])
    row_w = jnp.zeros((R,), jnp.float32).at[dest].set(w_flat[order])
    blk_e = jnp.minimum(jnp.searchsorted(pad_end, jnp.arange(nblk) * EXPERT_BLOCK, side='right'), N_EXPERTS - 1)

    def body(acc, blk):
        e, toks, ws = blk
        xb = xt[toks]
        hid = jax.nn.silu(xb @ w_e_gate[e]) * (xb @ w_e_up[e])
        yb = (hid @ w_e_down[e]) * ws[:, None].astype(xb.dtype)
        return acc.at[toks].add(yb.astype(acc.dtype)), None

    routed, _ = lax.scan(body, jnp.zeros((N, D), xt.dtype),
                         (blk_e, row_tok.reshape(nblk, EXPERT_BLOCK), row_w.reshape(nblk, EXPERT_BLOCK)))
    shared = (jax.nn.silu(xt @ w_s_gate) * (xt @ w_s_up)) @ w_s_down
    return (routed + shared).reshape(B, S, D)


def hybrid_layer(x, mem, positions, w_in, w_mem_kv, w_gate, b_gate, w_br_dil, w_br_moba, w_br_mem,
                 w_out, ln1_g, ln1_b, w_router, router_bias, w_e_gate, w_e_up, w_e_down,
                 w_s_gate, w_s_up, w_s_down, ln2_g, ln2_b):
    B, S, D = x.shape
    parts = jnp.split(x @ w_in, IN_SPLITS, axis=-1)
    q_d, k_d, v_d, q_m, k_m, v_m, q_x = [t.reshape(B, S, -1, HEAD_DIM) for t in parts]
    q_d, k_d = partial_rope(q_d, positions), partial_rope(k_d, positions)
    q_m, k_m = partial_rope(q_m, positions), partial_rope(k_m, positions)
    outs, lses = [], []
    for g, (window, dilation) in enumerate(DIL_PAIRS):
        hs = slice(g * HEADS_PER_DIL, (g + 1) * HEADS_PER_DIL)
        o, l = dilated_group_attention(q_d[:, :, hs], k_d[:, :, hs], v_d[:, :, hs], window, dilation)
        outs.append(o)
        lses.append(l)
    wts = jax.nn.softmax(jnp.stack(lses, axis=2), axis=2).astype(x.dtype)
    y_dil = jnp.sum(wts[..., None] * jnp.stack(outs, axis=2), axis=2).reshape(B, S, W_DIL_OUT)
    y_moba = moba_attention(q_m, k_m, v_m)
    y_mem = memory_cross_attention(q_x, mem, w_mem_kv)
    gates = jax.nn.sigmoid((x @ w_gate + b_gate).astype(jnp.float32)).astype(x.dtype).reshape(B, S, N_BRANCH, D)
    merged = (gates[:, :, 0] * (y_dil @ w_br_dil) + gates[:, :, 1] * (y_moba @ w_br_moba)
              + gates[:, :, 2] * (y_mem @ w_br_mem))
    x = layer_norm(ALPHA * x + merged @ w_out, ln1_g, ln1_b)
    x = layer_norm(ALPHA * x + moe_ffn(x, w_router, router_bias, w_e_gate, w_e_up, w_e_down,
                                       w_s_gate, w_s_up, w_s_down), ln2_g, ln2_b)
    return x


def setup_inputs(seed: int = 0) -> dict:
    key = jax.random.key(seed)
    ks = jax.random.split(key, 24)
    D = D_MODEL
    L = DEPTH

    def nrm(k, shape, scale):
        return jax.random.normal(k, shape, jnp.float32) * scale

    x = nrm(ks[0], (BATCH, SEQ, D), 1.0)
    mem = nrm(ks[1], (BATCH, MEM_LEN, D), 1.0)
    offset = jax.random.randint(ks[2], (BATCH, 1), 0, 1024, dtype=jnp.int32)
    positions = jnp.arange(SEQ, dtype=jnp.int32)[None, :] + offset
    return {
        'x': x,
        'mem': mem,
        'positions': positions,
        'w_in': nrm(ks[3], (L, D, W_IN), D ** -0.5),
        'w_mem_kv': nrm(ks[4], (L, D, 2 * W_MEM), D ** -0.5),
        'w_gate': nrm(ks[5], (L, D, N_BRANCH * D), D ** -0.5),
        'b_gate': nrm(ks[6], (L, N_BRANCH * D), 0.01),
        'w_br_dil': nrm(ks[7], (L, W_DIL_OUT, D), W_DIL_OUT ** -0.5),
        'w_br_moba': nrm(ks[8], (L, W_MOBA, D), W_MOBA ** -0.5),
        'w_br_mem': nrm(ks[9], (L, W_MEM, D), W_MEM ** -0.5),
        'w_out': nrm(ks[10], (L, D, D), BETA * D ** -0.5),
        'ln1_g': 1.0 + nrm(ks[11], (L, D), 0.02),
        'ln1_b': nrm(ks[12], (L, D), 0.02),
        'w_router': nrm(ks[13], (L, D, N_EXPERTS), D ** -0.5),
        'router_bias': nrm(ks[14], (L, N_EXPERTS), 0.01),
        'w_e_gate': nrm(ks[15], (L, N_EXPERTS, D, D_EXPERT), D ** -0.5),
        'w_e_up': nrm(ks[16], (L, N_EXPERTS, D, D_EXPERT), D ** -0.5),
        'w_e_down': nrm(ks[17], (L, N_EXPERTS, D_EXPERT, D), BETA * D_EXPERT ** -0.5),
        'w_s_gate': nrm(ks[18], (L, D, D_SHARED), D ** -0.5),
        'w_s_up': nrm(ks[19], (L, D, D_SHARED), D ** -0.5),
        'w_s_down': nrm(ks[20], (L, D_SHARED, D), BETA * D_SHARED ** -0.5),
        'ln2_g': 1.0 + nrm(ks[21], (L, D), 0.02),
        'ln2_b': nrm(ks[22], (L, D), 0.02),
    }


def reference(x, mem, positions, w_in, w_mem_kv, w_gate, b_gate, w_br_dil, w_br_moba, w_br_mem,
              w_out, ln1_g, ln1_b, w_router, router_bias, w_e_gate, w_e_up, w_e_down,
              w_s_gate, w_s_up, w_s_down, ln2_g, ln2_b):
    for l in range(DEPTH):
        x = hybrid_layer(x, mem, positions, w_in[l], w_mem_kv[l], w_gate[l], b_gate[l], w_br_dil[l],
                         w_br_moba[l], w_br_mem[l], w_out[l], ln1_g[l], ln1_b[l], w_router[l],
                         router_bias[l], w_e_gate[l], w_e_up[l], w_e_down[l], w_s_gate[l],
                         w_s_up[l], w_s_down[l], ln2_g[l], ln2_b[l])
    return x
```

```python
import functools

import jax
import jax.numpy as jnp
from jax import lax
from jax.experimental import pallas as pl
from jax.experimental.pallas import tpu as pltpu

F32 = jnp.float32
BF16 = jnp.bfloat16

LANES = 128
VMEM_LIMIT_BYTES = 48 * 1024 * 1024

HEAD_DIM = 64
ROPE_DIM = HEAD_DIM // 4
ROPE_HALF = ROPE_DIM // 2
ROPE_THETA = 500000.0
DIL_PAIRS = ((128, 1), (512, 4), (2048, 16))
BAND = 128
W_DIL = 384
W_MOBA = 384
W_MEM = 256
MOBA_BLOCK = 256
MOBA_TOPK = 3
N_EXPERTS = 256
N_GROUPS = 8
GROUP_SIZE = N_EXPERTS // N_GROUPS
TOPK_GROUPS = 4
TOP_K = 8
D_EXPERT = 256
ROUTED_SCALE = 2.5
LN_EPS = 1e-5
NEG = -1e30
QK_SCALE = HEAD_DIM ** -0.5


def _cparams(*sem):
    return pltpu.CompilerParams(dimension_semantics=sem, vmem_limit_bytes=VMEM_LIMIT_BYTES)


def _dot_nt(a, b):
    return lax.dot_general(a, b, (((1,), (1,)), ((), ())), preferred_element_type=F32)


def _layer_norm(v, g, b):
    mu = jnp.mean(v, axis=-1, keepdims=True)
    c = v - mu
    var = jnp.mean(c * c, axis=-1, keepdims=True)
    return c * lax.rsqrt(var + LN_EPS) * g + b


_IN_SECTIONS = (
    (0, W_DIL, True, True),
    (W_DIL, W_DIL, True, False),
    (2 * W_DIL, W_DIL, False, False),
    (3 * W_DIL, W_MOBA, True, True),
    (3 * W_DIL + W_MOBA, W_MOBA, True, False),
    (3 * W_DIL + 2 * W_MOBA, W_MOBA, False, False),
    (3 * W_DIL + 3 * W_MOBA, W_MEM, False, True),
)


def _inproj_kernel(x_ref, w_ref, cos_ref, sin_ref, *out_refs):
    xb = x_ref[...].astype(BF16)
    cos = cos_ref[...]
    sin = sin_ref[...]
    tm = xb.shape[0]
    lane = lax.broadcasted_iota(jnp.int32, (tm, LANES), 1)
    first_half = (lane % ROPE_DIM) < ROPE_HALF

    def rope(t):
        partner = jnp.where(first_half, pltpu.roll(t, LANES - ROPE_HALF, 1), pltpu.roll(t, ROPE_HALF, 1))
        return t * cos + partner * sin

    for (off, width, roped, scaled), o_ref in zip(_IN_SECTIONS, out_refs):
        acc = jnp.dot(xb, w_ref[:, off:off + width], preferred_element_type=F32)
        for c in range(width // LANES):
            t = acc[:, c * LANES:(c + 1) * LANES]
            if roped:
                t = rope(t)
            if scaled:
                t = t * QK_SCALE
            o_ref[:, c * LANES:(c + 1) * LANES] = t.astype(BF16)


def _in_projection(x2, w_in_b, cos_t, sin_t, tm=512):
    n, d = x2.shape
    w_total = w_in_b.shape[1]
    widths = [s[1] for s in _IN_SECTIONS]
    return pl.pallas_call(
        _inproj_kernel,
        grid=(n // tm,),
        in_specs=[
            pl.BlockSpec((tm, d), lambda i: (i, 0)),
            pl.BlockSpec((d, w_total), lambda i: (0, 0)),
            pl.BlockSpec((tm, LANES), lambda i: (i, 0)),
            pl.BlockSpec((tm, LANES), lambda i: (i, 0)),
        ],
        out_specs=[pl.BlockSpec((tm, w), lambda i: (i, 0)) for w in widths],
        out_shape=[jax.ShapeDtypeStruct((n, w), BF16) for w in widths],
        compiler_params=_cparams("parallel"),
        name="in_proj_rope",
    )(x2, w_in_b, cos_t, sin_t)


def _rope_tables(positions):
    inv_freq = ROPE_THETA ** (-jnp.arange(ROPE_HALF, dtype=F32) / ROPE_HALF)
    ang = positions.reshape(-1).astype(F32)[:, None] * inv_freq
    cos8, sin8 = jnp.cos(ang), jnp.sin(ang)
    n = ang.shape[0]
    pad1 = jnp.ones((n, HEAD_DIM - ROPE_DIM), F32)
    pad0 = jnp.zeros((n, HEAD_DIM - ROPE_DIM), F32)
    cos_h = jnp.concatenate([cos8, cos8, pad1], axis=1)
    sin_h = jnp.concatenate([-sin8, sin8, pad0], axis=1)
    return jnp.concatenate([cos_h, cos_h], axis=1), jnp.concatenate([sin_h, sin_h], axis=1)


def _head_mask(shape, h):
    lane = lax.broadcasted_iota(jnp.int32, shape, 1)
    return (lane // HEAD_DIM) == h


def _dil_kernel(q_ref, kp_ref, kc_ref, vp_ref, vc_ref, o_ref, lse_ref):
    i = pl.program_id(2)
    q = q_ref[...]
    k = jnp.concatenate([kp_ref[...], kc_ref[...]], axis=0)
    v = jnp.concatenate([vp_ref[...], vc_ref[...]], axis=0)
    qi = lax.broadcasted_iota(jnp.int32, (BAND, 2 * BAND), 0)
    kj = lax.broadcasted_iota(jnp.int32, (BAND, 2 * BAND), 1)
    dist = qi + BAND - kj
    allowed = (dist >= 0) & (dist <= BAND) & ((kj >= BAND) | (i > 0))
    outs, lses = [], []
    for h in range(2):
        qh = jnp.where(_head_mask(q.shape, h), q, jnp.zeros_like(q))
        s = jnp.where(allowed, _dot_nt(qh, k), NEG)
        m = jnp.max(s, axis=1, keepdims=True)
        p = jnp.exp(s - m)
        l = jnp.sum(p, axis=1, keepdims=True)
        outs.append(jnp.dot(p.astype(BF16), v, preferred_element_type=F32) / l)
        lses.append(jnp.broadcast_to(m + jnp.log(l), (BAND, LANES)))
    first = _head_mask((BAND, LANES), 0)
    o_ref[...] = jnp.where(first, outs[0], outs[1]).astype(o_ref.dtype)
    lse_ref[...] = jnp.where(first, lses[0], lses[1])


def _dilated_attention(qd, kd, vd, batch, seq, g, dilation):
    steps = seq // dilation
    nblk = steps // BAND
    ng = W_DIL // LANES
    q3 = qd.reshape(batch, steps, dilation * W_DIL)
    k3 = kd.reshape(batch, steps, dilation * W_DIL)
    v3 = vd.reshape(batch, steps, dilation * W_DIL)
    cur = lambda b, r, i: (b, i, r * ng + g)
    prev = lambda b, r, i: (b, jnp.maximum(i - 1, 0), r * ng + g)
    blk = (None, BAND, LANES)
    o, lse = pl.pallas_call(
        _dil_kernel,
        grid=(batch, dilation, nblk),
        in_specs=[pl.BlockSpec(blk, cur), pl.BlockSpec(blk, prev), pl.BlockSpec(blk, cur),
                  pl.BlockSpec(blk, prev), pl.BlockSpec(blk, cur)],
        out_specs=[pl.BlockSpec(blk, lambda b, r, i: (b, i, r)),
                   pl.BlockSpec(blk, lambda b, r, i: (b, i, r))],
        out_shape=[jax.ShapeDtypeStruct((batch, steps, dilation * LANES), BF16),
                   jax.ShapeDtypeStruct((batch, steps, dilation * LANES), F32)],
        compiler_params=_cparams("parallel", "parallel", "arbitrary"),
        name=f"dilated_attn_d{dilation}",
    )(q3, k3, k3, v3, v3)
    return o.reshape(batch * seq, LANES), lse.reshape(batch * seq, LANES)


def _kmean_kernel(k_ref, hi_ref, lo_ref):
    k = k_ref[...].astype(F32)
    s, w = k.shape
    mean = jnp.sum(k.reshape(s // MOBA_BLOCK, MOBA_BLOCK, w), axis=1) / MOBA_BLOCK
    hi = mean.astype(BF16)
    hi_ref[...] = hi
    lo_ref[...] = (mean - hi.astype(F32)).astype(BF16)


def _moba_kmean(km3):
    batch, seq, w = km3.shape
    nb = seq // MOBA_BLOCK
    return pl.pallas_call(
        _kmean_kernel,
        grid=(batch,),
        in_specs=[pl.BlockSpec((None, seq, w), lambda b: (b, 0, 0))],
        out_specs=[pl.BlockSpec((None, nb, w), lambda b: (b, 0, 0))] * 2,
        out_shape=[jax.ShapeDtypeStruct((batch, nb, w), BF16)] * 2,
        compiler_params=_cparams("parallel"),
        name="moba_kmean",
    )(km3)


def _moba_kernel(q_ref, k_ref, v_ref, kmh_ref, kml_ref, o_ref, m_scr, l_scr, acc_scr):
    qi = pl.program_id(2)
    q = q_ref[...]
    tq = q.shape[0]
    nb = kmh_ref.shape[0]
    row = lax.broadcasted_iota(jnp.int32, (tq, MOBA_BLOCK), 0)
    col = lax.broadcasted_iota(jnp.int32, (tq, MOBA_BLOCK), 1)
    causal = col <= row
    blk = lax.broadcasted_iota(jnp.int32, (nb, tq), 0)
    own_start = pl.multiple_of(qi * MOBA_BLOCK, MOBA_BLOCK)
    outs = []
    for h in range(2):
        qh = jnp.where(_head_mask(q.shape, h), q, jnp.zeros_like(q))
        gate = _dot_nt(kmh_ref[...], qh) + _dot_nt(kml_ref[...], qh)
        cnt = jnp.zeros((nb, tq), jnp.int32)
        for jp in range(nb):
            g_jp = gate[jp:jp + 1, :]
            beats = (g_jp > gate) | ((g_jp == gate) & (blk > jp))
            cnt = cnt + jnp.where(beats & (qi > jp), 1, 0)
        sel = (blk < qi) & (cnt < MOBA_TOPK)
        bias_t = jnp.where(sel, 0.0, NEG).astype(F32)
        bias_t = jnp.concatenate([bias_t, jnp.zeros((LANES - nb, tq), F32)], axis=0)
        bias = bias_t.T

        k_own = k_ref[pl.ds(own_start, MOBA_BLOCK), :]
        v_own = v_ref[pl.ds(own_start, MOBA_BLOCK), :]
        s = jnp.where(causal, _dot_nt(qh, k_own), NEG)
        m = jnp.max(s, axis=1, keepdims=True)
        p = jnp.exp(s - m)
        m_scr[h] = jnp.broadcast_to(m, (tq, LANES))
        l_scr[h] = jnp.broadcast_to(jnp.sum(p, axis=1, keepdims=True), (tq, LANES))
        acc_scr[h] = jnp.dot(p.astype(BF16), v_own, preferred_element_type=F32)

        for j in range(nb - 1):
            @pl.when(j < qi)
            def _(j=j, h=h, qh=qh, bias=bias):
                kj = k_ref[j * MOBA_BLOCK:(j + 1) * MOBA_BLOCK, :]
                vj = v_ref[j * MOBA_BLOCK:(j + 1) * MOBA_BLOCK, :]
                s = _dot_nt(qh, kj) + bias[:, j:j + 1]
                m_old = m_scr[h]
                m_new = jnp.maximum(m_old, jnp.max(s, axis=1, keepdims=True))
                p = jnp.exp(s - jnp.concatenate([m_new, m_new], axis=1))
                alpha = jnp.exp(m_old - m_new)
                l_scr[h] = alpha * l_scr[h] + jnp.sum(p, axis=1, keepdims=True)
                acc_scr[h] = alpha * acc_scr[h] + jnp.dot(p.astype(BF16), vj, preferred_element_type=F32)
                m_scr[h] = m_new

        outs.append(acc_scr[h] / l_scr[h])
    o_ref[...] = jnp.where(_head_mask((tq, LANES), 0), outs[0], outs[1]).astype(o_ref.dtype)


def _moba_attention(qm3, km3, vm3, kmh, kml):
    batch, seq, w = qm3.shape
    nb = seq // MOBA_BLOCK
    npair = w // LANES
    tq = MOBA_BLOCK
    return pl.pallas_call(
        _moba_kernel,
        grid=(batch, npair, nb),
        in_specs=[
            pl.BlockSpec((None, tq, LANES), lambda b, p, i: (b, i, p)),
            pl.BlockSpec((None, seq, LANES), lambda b, p, i: (b, 0, p)),
            pl.BlockSpec((None, seq, LANES), lambda b, p, i: (b, 0, p)),
            pl.BlockSpec((None, nb, LANES), lambda b, p, i: (b, 0, p)),
            pl.BlockSpec((None, nb, LANES), lambda b, p, i: (b, 0, p)),
        ],
        out_specs=pl.BlockSpec((None, tq, LANES), lambda b, p, i: (b, i, p)),
        out_shape=jax.ShapeDtypeStruct((batch, seq, w), BF16),
        scratch_shapes=[pltpu.VMEM((2, tq, LANES), F32)] * 3,
        compiler_params=_cparams("parallel", "parallel", "arbitrary"),
        name="moba_attn",
    )(qm3, km3, vm3, kmh, kml)


def _memkv_kernel(mem_ref, w_ref, k_ref, v_ref):
    kv = jnp.dot(mem_ref[...].astype(BF16), w_ref[...], preferred_element_type=F32)
    k_ref[...] = kv[:, :W_MEM].astype(BF16)
    v_ref[...] = kv[:, W_MEM:].astype(BF16)


def _mem_kv(mem, w_kv_b):
    batch, m, d = mem.shape
    return pl.pallas_call(
        _memkv_kernel,
        grid=(batch,),
        in_specs=[pl.BlockSpec((None, m, d), lambda b: (b, 0, 0)),
                  pl.BlockSpec((d, 2 * W_MEM), lambda b: (0, 0))],
        out_specs=[pl.BlockSpec((None, m, W_MEM), lambda b: (b, 0, 0))] * 2,
        out_shape=[jax.ShapeDtypeStruct((batch, m, W_MEM), BF16)] * 2,
        compiler_params=_cparams("parallel"),
        name="mem_kv_proj",
    )(mem, w_kv_b)


def _memattn_kernel(q_ref, k_ref, v_ref, o_ref):
    q = q_ref[...]
    k = k_ref[...]
    v = v_ref[...]
    outs = []
    for h in range(2):
        qh = jnp.where(_head_mask(q.shape, h), q, jnp.zeros_like(q))
        s = _dot_nt(qh, k)
        m = jnp.max(s, axis=1, keepdims=True)
        p = jnp.exp(s - m)
        l = jnp.sum(p, axis=1, keepdims=True)
        outs.append(jnp.dot(p.astype(BF16), v, preferred_element_type=F32) / l)
    o_ref[...] = jnp.where(_head_mask(outs[0].shape, 0), outs[0], outs[1]).astype(o_ref.dtype)


def _mem_attention(qx3, k_mem, v_mem, tq=512):
    batch, seq, w = qx3.shape
    m = k_mem.shape[1]
    return pl.pallas_call(
        _memattn_kernel,
        grid=(batch, w // LANES, seq // tq),
        in_specs=[pl.BlockSpec((None, tq, LANES), lambda b, p, i: (b, i, p)),
                  pl.BlockSpec((None, m, LANES), lambda b, p, i: (b, 0, p)),
                  pl.BlockSpec((None, m, LANES), lambda b, p, i: (b, 0, p))],
        out_specs=pl.BlockSpec((None, tq, LANES), lambda b, p, i: (b, i, p)),
        out_shape=jax.ShapeDtypeStruct((batch, seq, w), BF16),
        compiler_params=_cparams("parallel", "parallel", "parallel"),
        name="mem_attn",
    )(qx3, k_mem, v_mem)


def _merge_kernel(alpha, x_ref, o1_ref, o2_ref, o3_ref, l1_ref, l2_ref, l3_ref, ym_ref, yx_ref,
                  wg_ref, bg_ref, wbd_ref, wbm_ref, wbx_ref, wo_ref, g_ref, b_ref, out_ref):
    x = x_ref[...]
    xb = x.astype(BF16)
    d = x.shape[1]
    l1, l2, l3 = l1_ref[...], l2_ref[...], l3_ref[...]
    mx = jnp.maximum(jnp.maximum(l1, l2), l3)
    e1, e2, e3 = jnp.exp(l1 - mx), jnp.exp(l2 - mx), jnp.exp(l3 - mx)
    y_dil = (e1 * o1_ref[...].astype(F32) + e2 * o2_ref[...].astype(F32)
             + e3 * o3_ref[...].astype(F32)) / (e1 + e2 + e3)
    branches = (
        jnp.dot(y_dil.astype(BF16), wbd_ref[...], preferred_element_type=F32),
        jnp.dot(ym_ref[...], wbm_ref[...], preferred_element_type=F32),
        jnp.dot(yx_ref[...], wbx_ref[...], preferred_element_type=F32),
    )
    merged = jnp.zeros_like(x)
    for i, br in enumerate(branches):
        logits = jnp.dot(xb, wg_ref[:, i * d:(i + 1) * d], preferred_element_type=F32) + bg_ref[:, i * d:(i + 1) * d]
        merged = merged + jax.nn.sigmoid(logits) * br
    mix = jnp.dot(merged.astype(BF16), wo_ref[...], preferred_element_type=F32)
    out_ref[...] = _layer_norm(alpha * x + mix, g_ref[...], b_ref[...])


def _merge(alpha, x2, o_dil, lse_dil, y_moba, y_mem, wg, bg, wbd, wbm, wbx, wo, g, b, tm=512):
    n, d = x2.shape
    row = lambda w: pl.BlockSpec((tm, w), lambda i: (i, 0))
    full = lambda a: pl.BlockSpec(a.shape, lambda i: (0, 0))
    weights = (wg, bg, wbd, wbm, wbx, wo, g, b)
    return pl.pallas_call(
        functools.partial(_merge_kernel, alpha),
        grid=(n // tm,),
        in_specs=[row(d)] + [row(LANES)] * 6 + [row(W_MOBA), row(W_MEM)] + [full(a) for a in weights],
        out_specs=row(d),
        out_shape=jax.ShapeDtypeStruct((n, d), F32),
        compiler_params=_cparams("parallel"),
        name="merge_outproj_ln1",
    )(x2, *o_dil, *lse_dil, y_moba, y_mem, *weights)


def _token_mixer(x2, mem, positions, w_in, w_mem_kv, w_gate, b_gate, w_br_dil, w_br_moba, w_br_mem,
                 w_out, ln1_g, ln1_b, batch, seq, alpha):
    cos_t, sin_t = _rope_tables(positions)
    qd, kd, vd, qm, km, vm, qx = _in_projection(x2, w_in.astype(BF16), cos_t, sin_t)
    o_dil, lse_dil = [], []
    for g, (_, dilation) in enumerate(DIL_PAIRS):
        o, lse = _dilated_attention(qd, kd, vd, batch, seq, g, dilation)
        o_dil.append(o)
        lse_dil.append(lse)
    qm3, km3, vm3 = (t.reshape(batch, seq, W_MOBA) for t in (qm, km, vm))
    kmh, kml = _moba_kmean(km3)
    y_moba = _moba_attention(qm3, km3, vm3, kmh, kml).reshape(batch * seq, W_MOBA)
    k_mem, v_mem = _mem_kv(mem, w_mem_kv.astype(BF16))
    y_mem = _mem_attention(qx.reshape(batch, seq, W_MEM), k_mem, v_mem).reshape(batch * seq, W_MEM)
    return _merge(alpha, x2, o_dil, lse_dil, y_moba, y_mem,
                  w_gate.astype(BF16), b_gate.reshape(1, -1), w_br_dil.astype(BF16),
                  w_br_moba.astype(BF16), w_br_mem.astype(BF16), w_out.astype(BF16),
                  ln1_g.reshape(1, -1), ln1_b.reshape(1, -1))


EXPERT_ROWS = 256
TOKEN_TILE = 256
PACKED = 512


def _first_index_of_max(v, iota_f, size):
    m = jnp.max(v, axis=0, keepdims=True)
    idx = jnp.min(jnp.where(v == m, iota_f, float(size)), axis=0, keepdims=True)
    return m, idx


def _route_kernel(x_ref, wh_ref, wl_ref, bias_ref, eidx_ref, gw_ref, rank_ref, cnt_ref, carry):
    step = pl.program_id(0)

    @pl.when(step == 0)
    def _():
        carry[...] = jnp.zeros_like(carry)

    x = x_ref[...]
    tm = x.shape[0]
    xh = x.astype(BF16)
    xl = (x - xh.astype(F32)).astype(BF16)
    wh = wh_ref[...]
    logits = _dot_nt(wh, xh) + _dot_nt(wh, xl) + _dot_nt(wl_ref[...], xh)
    scores = jax.nn.sigmoid(logits)
    biased = scores + bias_ref[...][:, :1]

    giota = lax.broadcasted_iota(jnp.int32, (GROUP_SIZE, tm), 0).astype(F32)
    group_scores = []
    for g in range(N_GROUPS):
        slab = biased[g * GROUP_SIZE:(g + 1) * GROUP_SIZE, :]
        m1, i1 = _first_index_of_max(slab, giota, GROUP_SIZE)
        m2 = jnp.max(jnp.where(giota == i1, -jnp.inf, slab), axis=0, keepdims=True)
        group_scores.append(m1 + m2)
    gs = jnp.concatenate(group_scores, axis=0)
    gidx = lax.broadcasted_iota(jnp.int32, (N_GROUPS, tm), 0)
    beaten = jnp.zeros((N_GROUPS, tm), jnp.int32)
    for gp in range(N_GROUPS):
        row = gs[gp:gp + 1, :]
        beaten = beaten + jnp.where((row > gs) | ((row == gs) & (gidx > gp)), 1, 0)
    keep = beaten < TOPK_GROUPS
    masked = jnp.concatenate(
        [jnp.where(keep[g:g + 1, :], biased[g * GROUP_SIZE:(g + 1) * GROUP_SIZE, :], -jnp.inf)
         for g in range(N_GROUPS)], axis=0)

    eiota = lax.broadcasted_iota(jnp.int32, (N_EXPERTS, tm), 0).astype(F32)
    idx_rows, gw_rows = [], []
    for _ in range(TOP_K):
        _, idx = _first_index_of_max(masked, eiota, N_EXPERTS)
        hit = eiota == idx
        gw_rows.append(jnp.sum(jnp.where(hit, scores, 0.0), axis=0, keepdims=True))
        masked = jnp.where(hit, -jnp.inf, masked)
        idx_rows.append(idx)
    idx8 = jnp.concatenate(idx_rows, axis=0)
    gw8 = jnp.concatenate(gw_rows, axis=0)
    gw_ref[...] = gw8 / jnp.sum(gw8, axis=0, keepdims=True) * ROUTED_SCALE
    eidx_ref[...] = idx8.astype(jnp.int32)

    onehot = jnp.zeros((N_EXPERTS, tm), F32)
    for k in range(TOP_K):
        onehot = onehot + jnp.where(eiota == idx8[k:k + 1, :], 1.0, 0.0)
    t_row = lax.broadcasted_iota(jnp.int32, (tm, tm), 0)
    t_col = lax.broadcasted_iota(jnp.int32, (tm, tm), 1)
    earlier = jnp.where(t_row < t_col, 1.0, 0.0).astype(BF16)
    prefix = jnp.dot(onehot.astype(BF16), earlier, preferred_element_type=F32)
    base = carry[...]
    prefix = prefix + jnp.concatenate([base] * (tm // LANES), axis=1)
    rank_rows = [jnp.sum(jnp.where(eiota == idx8[k:k + 1, :], prefix, 0.0), axis=0, keepdims=True)
                 for k in range(TOP_K)]
    rank_ref[...] = jnp.concatenate(rank_rows, axis=0).astype(jnp.int32)
    total = base + jnp.sum(onehot, axis=1, keepdims=True)
    carry[...] = total
    cnt_ref[...] = total.astype(jnp.int32)


def _route(x1, wr_hi_t, wr_lo_t, bias_b, tm=512):
    n, d = x1.shape
    tok = lambda dt: jax.ShapeDtypeStruct((TOP_K, n), dt)
    return pl.pallas_call(
        _route_kernel,
        grid=(n // tm,),
        in_specs=[pl.BlockSpec((tm, d), lambda i: (i, 0)),
                  pl.BlockSpec((N_EXPERTS, d), lambda i: (0, 0)),
                  pl.BlockSpec((N_EXPERTS, d), lambda i: (0, 0)),
                  pl.BlockSpec((N_EXPERTS, LANES), lambda i: (0, 0))],
        out_specs=[pl.BlockSpec((TOP_K, tm), lambda i: (0, i)),
                   pl.BlockSpec((TOP_K, tm), lambda i: (0, i)),
                   pl.BlockSpec((TOP_K, tm), lambda i: (0, i)),
                   pl.BlockSpec((N_EXPERTS, LANES), lambda i: (0, 0))],
        out_shape=[tok(jnp.int32), tok(F32), tok(jnp.int32),
                   jax.ShapeDtypeStruct((N_EXPERTS, LANES), jnp.int32)],
        scratch_shapes=[pltpu.VMEM((N_EXPERTS, LANES), F32)],
        compiler_params=_cparams("arbitrary"),
        name="moe_route",
    )(x1, wr_hi_t, wr_lo_t, bias_b)


def _dest_kernel(eidx_ref, rank_ref, start_ref, dest_ref):
    eidx = eidx_ref[...]
    tm = eidx.shape[1]
    eiota = lax.broadcasted_iota(jnp.int32, (N_EXPERTS, tm), 0)
    start = start_ref[...][:, :1]
    rows = [jnp.sum(jnp.where(eiota == eidx[k:k + 1, :], start, 0.0), axis=0, keepdims=True)
            for k in range(TOP_K)]
    dest_ref[...] = jnp.concatenate(rows, axis=0).astype(jnp.int32) + rank_ref[...]


def _dest_rows(eidx, rank, start_b):
    n = eidx.shape[1]
    nt = n // TOKEN_TILE
    return pl.pallas_call(
        _dest_kernel,
        grid=(nt,),
        in_specs=[pl.BlockSpec((TOP_K, TOKEN_TILE), lambda i: (0, i)),
                  pl.BlockSpec((TOP_K, TOKEN_TILE), lambda i: (0, i)),
                  pl.BlockSpec((N_EXPERTS, LANES), lambda i: (0, 0))],
        out_specs=pl.BlockSpec((None, TOP_K, TOKEN_TILE), lambda i: (i, 0, 0)),
        out_shape=jax.ShapeDtypeStruct((nt, TOP_K, TOKEN_TILE), jnp.int32),
        compiler_params=_cparams("parallel"),
        name="moe_dest_rows",
    )(eidx, rank, start_b)


def _pack_halves(v):
    return pltpu.pack_elementwise([v[:, :PACKED], v[:, PACKED:]], packed_dtype=BF16)


def _unpack_half(p, index):
    return pltpu.unpack_elementwise(p, index=index, packed_dtype=BF16, unpacked_dtype=F32)


def _dispatch_kernel(x_ref, dest_hbm, xs_hbm, packed, dest_smem, sem_idx, sem_rows):
    step = pl.program_id(0)
    idx_copy = pltpu.make_async_copy(dest_hbm.at[step], dest_smem, sem_idx)
    idx_copy.start()
    packed[...] = _pack_halves(x_ref[...])
    idx_copy.wait()

    def row_copy(t, k):
        return pltpu.make_async_copy(packed.at[pl.ds(t, 1)], xs_hbm.at[pl.ds(dest_smem[k, t], 1)], sem_rows)

    def issue(t, c):
        for k in range(TOP_K):
            row_copy(t, k).start()
        return c

    def drain(t, c):
        for k in range(TOP_K):
            row_copy(t, k).wait()
        return c

    lax.fori_loop(0, TOKEN_TILE, issue, 0)
    lax.fori_loop(0, TOKEN_TILE, drain, 0)


def _dispatch(x1, dest3, rows):
    n, d = x1.shape
    return pl.pallas_call(
        _dispatch_kernel,
        grid=(n // TOKEN_TILE,),
        in_specs=[pl.BlockSpec((TOKEN_TILE, d), lambda i: (i, 0)),
                  pl.BlockSpec(memory_space=pl.ANY)],
        out_specs=pl.BlockSpec(memory_space=pl.ANY),
        out_shape=jax.ShapeDtypeStruct((rows, PACKED), jnp.uint32),
        scratch_shapes=[pltpu.VMEM((TOKEN_TILE, PACKED), jnp.uint32),
                        pltpu.SMEM((TOP_K, TOKEN_TILE), jnp.int32),
                        pltpu.SemaphoreType.DMA, pltpu.SemaphoreType.DMA],
        compiler_params=_cparams("arbitrary"),
        name="moe_dispatch",
    )(x1, dest3)


def _expert_kernel(blk_e_ref, valid_ref, nused_ref, xs_ref, wg_ref, wu_ref, wd_ref, ys_ref):
    i = pl.program_id(0)

    @pl.when(i < nused_ref[0])
    def _():
        p = xs_ref[...]
        row = lax.broadcasted_iota(jnp.int32, p.shape, 0)
        p = jnp.where(row < valid_ref[i], p, jnp.zeros_like(p))
        lo = _unpack_half(p, 0).astype(BF16)
        hi = _unpack_half(p, 1).astype(BF16)

        def up(w_ref):
            return (jnp.dot(lo, w_ref[:PACKED, :].astype(BF16), preferred_element_type=F32)
                    + jnp.dot(hi, w_ref[PACKED:, :].astype(BF16), preferred_element_type=F32))

        hid = (jax.nn.silu(up(wg_ref)) * up(wu_ref)).astype(BF16)
        y = jnp.dot(hid, wd_ref[...].astype(BF16), preferred_element_type=F32)
        ys_ref[...] = _pack_halves(y)


def _expert_ffn(xs, blk_e, valid, nused, w_e_gate, w_e_up, w_e_down):
    rows = xs.shape[0]
    nblk = rows // EXPERT_ROWS
    d = w_e_gate.shape[1]
    row_map = lambda i, be, va, nu: (jnp.minimum(i, nu[0] - 1), 0)
    w_map = lambda i, be, va, nu: (be[i], 0, 0)
    return pl.pallas_call(
        _expert_kernel,
        grid_spec=pltpu.PrefetchScalarGridSpec(
            num_scalar_prefetch=3,
            grid=(nblk,),
            in_specs=[pl.BlockSpec((EXPERT_ROWS, PACKED), row_map),
                      pl.BlockSpec((None, d, D_EXPERT), w_map),
                      pl.BlockSpec((None, d, D_EXPERT), w_map),
                      pl.BlockSpec((None, D_EXPERT, d), w_map)],
            out_specs=pl.BlockSpec((EXPERT_ROWS, PACKED), row_map),
        ),
        out_shape=jax.ShapeDtypeStruct((rows, PACKED), jnp.uint32),
        compiler_params=_cparams("arbitrary"),
        name="moe_expert_ffn",
    )(blk_e, valid, nused, xs, w_e_gate, w_e_up, w_e_down)


def _combine_kernel(alpha, x_ref, gw_ref, dest_hbm, ys_hbm, wsg_ref, wsu_ref, wsd_ref, g_ref, b_ref,
                    out_ref, gathered, dest_smem, sem_idx, sem_rows):
    step = pl.program_id(0)
    idx_copy = pltpu.make_async_copy(dest_hbm.at[step], dest_smem, sem_idx)
    idx_copy.start()
    idx_copy.wait()

    def row_copy(t, k):
        return pltpu.make_async_copy(ys_hbm.at[pl.ds(dest_smem[k, t], 1)], gathered.at[k, pl.ds(t, 1)], sem_rows)

    def issue(t, c):
        for k in range(TOP_K):
            row_copy(t, k).start()
        return c

    def drain(t, c):
        for k in range(TOP_K):
            row_copy(t, k).wait()
        return c

    lax.fori_loop(0, TOKEN_TILE, issue, 0)
    x = x_ref[...]
    xb = x.astype(BF16)
    hid = (jax.nn.silu(jnp.dot(xb, wsg_ref[...], preferred_element_type=F32))
           * jnp.dot(xb, wsu_ref[...], preferred_element_type=F32)).astype(BF16)
    shared = jnp.dot(hid, wsd_ref[...], preferred_element_type=F32)
    lax.fori_loop(0, TOKEN_TILE, drain, 0)

    gw = gw_ref[...]
    lo = jnp.zeros((TOKEN_TILE, PACKED), F32)
    hi = jnp.zeros((TOKEN_TILE, PACKED), F32)
    for k in range(TOP_K):
        p = gathered[k]
        w = gw[:, k:k + 1]
        lo = lo + w * _unpack_half(p, 0)
        hi = hi + w * _unpack_half(p, 1)
    routed = jnp.concatenate([lo, hi], axis=1)
    out_ref[...] = _layer_norm(alpha * x + (routed + shared), g_ref[...], b_ref[...])


def _combine(alpha, x1, gw_t, dest3, ys, wsg, wsu, wsd, g, b):
    n, d = x1.shape
    full = lambda a: pl.BlockSpec(a.shape, lambda i: (0, 0))
    weights = (wsg, wsu, wsd, g, b)
    return pl.pallas_call(
        functools.partial(_combine_kernel, alpha),
        grid=(n // TOKEN_TILE,),
        in_specs=[pl.BlockSpec((TOKEN_TILE, d), lambda i: (i, 0)),
                  pl.BlockSpec((TOKEN_TILE, TOP_K), lambda i: (i, 0)),
                  pl.BlockSpec(memory_space=pl.ANY),
                  pl.BlockSpec(memory_space=pl.ANY)] + [full(a) for a in weights],
        out_specs=pl.BlockSpec((TOKEN_TILE, d), lambda i: (i, 0)),
        out_shape=jax.ShapeDtypeStruct((n, d), F32),
        scratch_shapes=[pltpu.VMEM((TOP_K, TOKEN_TILE, PACKED), jnp.uint32),
                        pltpu.SMEM((TOP_K, TOKEN_TILE), jnp.int32),
                        pltpu.SemaphoreType.DMA, pltpu.SemaphoreType.DMA],
        compiler_params=_cparams("arbitrary"),
        name="moe_combine_shared_ln2",
    )(x1, gw_t, dest3, ys, *weights)


def _moe_layer(x1, w_router, router_bias, w_e_gate, w_e_up, w_e_down, w_s_gate, w_s_up, w_s_down,
               ln2_g, ln2_b, alpha):
    n, d = x1.shape
    wr_t = w_router.T
    wr_hi = wr_t.astype(BF16)
    wr_lo = (wr_t - wr_hi.astype(F32)).astype(BF16)
    bias_b = jnp.broadcast_to(router_bias.astype(F32)[:, None], (N_EXPERTS, LANES))
    eidx, gw, rank, cnt = _route(x1, wr_hi, wr_lo, bias_b)

    counts = cnt[:, 0]
    padded = (counts + EXPERT_ROWS - 1) // EXPERT_ROWS * EXPERT_ROWS
    seg_end = jnp.cumsum(padded)
    seg_start = seg_end - padded
    rows = n * TOP_K + N_EXPERTS * EXPERT_ROWS
    blk_first_row = jnp.arange(rows // EXPERT_ROWS, dtype=jnp.int32) * EXPERT_ROWS
    blk_e = jnp.minimum(jnp.searchsorted(seg_end, blk_first_row, side="right"), N_EXPERTS - 1).astype(jnp.int32)
    valid = jnp.clip(counts[blk_e] - (blk_first_row - seg_start[blk_e]), 0, EXPERT_ROWS).astype(jnp.int32)
    nused = (seg_end[-1:] // EXPERT_ROWS).astype(jnp.int32)
    start_b = jnp.broadcast_to(seg_start.astype(F32)[:, None], (N_EXPERTS, LANES))

    dest3 = _dest_rows(eidx, rank, start_b)
    xs = _dispatch(x1, dest3, rows)
    ys = _expert_ffn(xs, blk_e, valid, nused, w_e_gate, w_e_up, w_e_down)
    return _combine(alpha, x1, gw.T, dest3, ys, w_s_gate.astype(BF16), w_s_up.astype(BF16),
                    w_s_down.astype(BF16), ln2_g.reshape(1, -1), ln2_b.reshape(1, -1))


def kernel(x, mem, positions, w_in, w_mem_kv, w_gate, b_gate, w_br_dil, w_br_moba, w_br_mem, w_out, ln1_g, ln1_b, w_router, router_bias, w_e_gate, w_e_up, w_e_down, w_s_gate, w_s_up, w_s_down, ln2_g, ln2_b):
    batch, seq, d = x.shape
    depth = w_in.shape[0]
    alpha = (2.0 * depth) ** 0.25
    h = x.reshape(batch * seq, d)
    for l in range(depth):
        h = _token_mixer(h, mem, positions, w_in[l], w_mem_kv[l], w_gate[l], b_gate[l], w_br_dil[l],
                         w_br_moba[l], w_br_mem[l], w_out[l], ln1_g[l], ln1_b[l], batch, seq, alpha)
        h = _moe_layer(h, w_router[l], router_bias[l], w_e_gate[l], w_e_up[l], w_e_down[l],
                       w_s_gate[l], w_s_up[l], w_s_down[l], ln2_g[l], ln2_b[l], alpha)
    return h.reshape(batch, seq, d)
```

```python
import functools

import jax
import jax.numpy as jnp
from jax import lax
from jax.experimental import pallas as pl
from jax.experimental.pallas import tpu as pltpu

F32 = jnp.float32
BF16 = jnp.bfloat16

LANES = 128
VMEM_LIMIT_BYTES = 48 * 1024 * 1024

HEAD_DIM = 64
ROPE_DIM = HEAD_DIM // 4
ROPE_HALF = ROPE_DIM // 2
ROPE_THETA = 500000.0
DIL_PAIRS = ((128, 1), (512, 4), (2048, 16))
BAND = 128
W_DIL = 384
W_MOBA = 384
W_MEM = 256
MOBA_BLOCK = 256
MOBA_TOPK = 3
N_EXPERTS = 256
N_GROUPS = 8
GROUP_SIZE = N_EXPERTS // N_GROUPS
TOPK_GROUPS = 4
TOP_K = 8
D_EXPERT = 256
ROUTED_SCALE = 2.5
LN_EPS = 1e-5
NEG = -1e30
QK_SCALE = HEAD_DIM ** -0.5


def _cparams(*sem):
    return pltpu.CompilerParams(dimension_semantics=sem, vmem_limit_bytes=VMEM_LIMIT_BYTES)


def _dot_nt(a, b):
    return lax.dot_general(a, b, (((1,), (1,)), ((), ())), preferred_element_type=F32)


def _layer_norm(v, g, b):
    mu = jnp.mean(v, axis=-1, keepdims=True)
    c = v - mu
    var = jnp.mean(c * c, axis=-1, keepdims=True)
    return c * lax.rsqrt(var + LN_EPS) * g + b


_IN_SECTIONS = (
    (0, W_DIL, True, True),
    (W_DIL, W_DIL, True, False),
    (2 * W_DIL, W_DIL, False, False),
    (3 * W_DIL, W_MOBA, True, True),
    (3 * W_DIL + W_MOBA, W_MOBA, True, False),
    (3 * W_DIL + 2 * W_MOBA, W_MOBA, False, False),
    (3 * W_DIL + 3 * W_MOBA, W_MEM, False, True),
)


def _inproj_kernel(x_ref, w_ref, cos_ref, sin_ref, *out_refs):
    xb = x_ref[...].astype(BF16)
    cos = cos_ref[...]
    sin = sin_ref[...]
    tm = xb.shape[0]
    lane = lax.broadcasted_iota(jnp.int32, (tm, LANES), 1)
    first_half = (lane % ROPE_DIM) < ROPE_HALF

    def rope(t):
        partner = jnp.where(first_half, pltpu.roll(t, LANES - ROPE_HALF, 1), pltpu.roll(t, ROPE_HALF, 1))
        return t * cos + partner * sin

    for (off, width, roped, scaled), o_ref in zip(_IN_SECTIONS, out_refs):
        acc = jnp.dot(xb, w_ref[:, off:off + width], preferred_element_type=F32)
        for c in range(width // LANES):
            t = acc[:, c * LANES:(c + 1) * LANES]
            if roped:
                t = rope(t)
            if scaled:
                t = t * QK_SCALE
            o_ref[:, c * LANES:(c + 1) * LANES] = t.astype(BF16)


def _in_projection(x2, w_in_b, cos_t, sin_t, tm=512):
    n, d = x2.shape
    w_total = w_in_b.shape[1]
    widths = [s[1] for s in _IN_SECTIONS]
    return pl.pallas_call(
        _inproj_kernel,
        grid=(n // tm,),
        in_specs=[
            pl.BlockSpec((tm, d), lambda i: (i, 0)),
            pl.BlockSpec((d, w_total), lambda i: (0, 0)),
            pl.BlockSpec((tm, LANES), lambda i: (i, 0)),
            pl.BlockSpec((tm, LANES), lambda i: (i, 0)),
        ],
        out_specs=[pl.BlockSpec((tm, w), lambda i: (i, 0)) for w in widths],
        out_shape=[jax.ShapeDtypeStruct((n, w), BF16) for w in widths],
        compiler_params=_cparams("parallel"),
        name="in_proj_rope",
    )(x2, w_in_b, cos_t, sin_t)


def _rope_tables(positions):
    inv_freq = ROPE_THETA ** (-jnp.arange(ROPE_HALF, dtype=F32) / ROPE_HALF)
    ang = positions.reshape(-1).astype(F32)[:, None] * inv_freq
    cos8, sin8 = jnp.cos(ang), jnp.sin(ang)
    n = ang.shape[0]
    pad1 = jnp.ones((n, HEAD_DIM - ROPE_DIM), F32)
    pad0 = jnp.zeros((n, HEAD_DIM - ROPE_DIM), F32)
    cos_h = jnp.concatenate([cos8, cos8, pad1], axis=1)
    sin_h = jnp.concatenate([-sin8, sin8, pad0], axis=1)
    return jnp.concatenate([cos_h, cos_h], axis=1), jnp.concatenate([sin_h, sin_h], axis=1)


def _head_mask(shape, h):
    lane = lax.broadcasted_iota(jnp.int32, shape, 1)
    return (lane // HEAD_DIM) == h


def _dil_kernel(q_ref, kp_ref, kc_ref, vp_ref, vc_ref, o_ref, lse_ref):
    i = pl.program_id(2)
    q = q_ref[...]
    k = jnp.concatenate([kp_ref[...], kc_ref[...]], axis=0)
    v = jnp.concatenate([vp_ref[...], vc_ref[...]], axis=0)
    qi = lax.broadcasted_iota(jnp.int32, (BAND, 2 * BAND), 0)
    kj = lax.broadcasted_iota(jnp.int32, (BAND, 2 * BAND), 1)
    dist = qi + BAND - kj
    allowed = (dist >= 0) & (dist <= BAND) & ((kj >= BAND) | (i > 0))
    outs, lses = [], []
    for h in range(2):
        qh = jnp.where(_head_mask(q.shape, h), q, jnp.zeros_like(q))
        s = jnp.where(allowed, _dot_nt(qh, k), NEG)
        m = jnp.max(s, axis=1, keepdims=True)
        p = jnp.exp(s - m)
        l = jnp.sum(p, axis=1, keepdims=True)
        outs.append(jnp.dot(p.astype(BF16), v, preferred_element_type=F32) / l)
        lses.append(jnp.broadcast_to(m + jnp.log(l), (BAND, LANES)))
    first = _head_mask((BAND, LANES), 0)
    o_ref[...] = jnp.where(first, outs[0], outs[1]).astype(o_ref.dtype)
    lse_ref[...] = jnp.where(first, lses[0], lses[1])


def _dilated_attention(qd, kd, vd, batch, seq, g, dilation):
    steps = seq // dilation
    nblk = steps // BAND
    ng = W_DIL // LANES
    q3 = qd.reshape(batch, steps, dilation * W_DIL)
    k3 = kd.reshape(batch, steps, dilation * W_DIL)
    v3 = vd.reshape(batch, steps, dilation * W_DIL)
    cur = lambda b, r, i: (b, i, r * ng + g)
    prev = lambda b, r, i: (b, jnp.maximum(i - 1, 0), r * ng + g)
    blk = (None, BAND, LANES)
    o, lse = pl.pallas_call(
        _dil_kernel,
        grid=(batch, dilation, nblk),
        in_specs=[pl.BlockSpec(blk, cur), pl.BlockSpec(blk, prev), pl.BlockSpec(blk, cur),
                  pl.BlockSpec(blk, prev), pl.BlockSpec(blk, cur)],
        out_specs=[pl.BlockSpec(blk, lambda b, r, i: (b, i, r)),
                   pl.BlockSpec(blk, lambda b, r, i: (b, i, r))],
        out_shape=[jax.ShapeDtypeStruct((batch, steps, dilation * LANES), BF16),
                   jax.ShapeDtypeStruct((batch, steps, dilation * LANES), F32)],
        compiler_params=_cparams("parallel", "parallel", "arbitrary"),
        name=f"dilated_attn_d{dilation}",
    )(q3, k3, k3, v3, v3)
    return o.reshape(batch * seq, LANES), lse.reshape(batch * seq, LANES)


def _kmean_kernel(k_ref, hi_ref, lo_ref):
    k = k_ref[...].astype(F32)
    s, w = k.shape
    mean = jnp.sum(k.reshape(s // MOBA_BLOCK, MOBA_BLOCK, w), axis=1) / MOBA_BLOCK
    hi = mean.astype(BF16)
    hi_ref[...] = hi
    lo_ref[...] = (mean - hi.astype(F32)).astype(BF16)


def _moba_kmean(km3):
    batch, seq, w = km3.shape
    nb = seq // MOBA_BLOCK
    return pl.pallas_call(
        _kmean_kernel,
        grid=(batch,),
        in_specs=[pl.BlockSpec((None, seq, w), lambda b: (b, 0, 0))],
        out_specs=[pl.BlockSpec((None, nb, w), lambda b: (b, 0, 0))] * 2,
        out_shape=[jax.ShapeDtypeStruct((batch, nb, w), BF16)] * 2,
        compiler_params=_cparams("parallel"),
        name="moba_kmean",
    )(km3)


def _moba_kernel(q_ref, k_ref, v_ref, kmh_ref, kml_ref, o_ref, qaug, kaug, m_scr, l_scr, acc_scr):
    s = pl.program_id(2)
    nb = kmh_ref.shape[0]
    tq = MOBA_BLOCK
    lane = lax.broadcasted_iota(jnp.int32, (tq, LANES), 1)

    @pl.when(s == 0)
    def _():
        for h in range(2):
            spare = HEAD_DIM * (1 - h)
            for j in range(nb):
                onehot = jnp.where(lane == spare + j, 1.0, 0.0).astype(BF16)
                kaug[h, j * tq:(j + 1) * tq, :] = jnp.where(
                    _head_mask((tq, LANES), h), k_ref[j * tq:(j + 1) * tq, :], onehot)

    row = lax.broadcasted_iota(jnp.int32, (tq, tq), 0)
    col = lax.broadcasted_iota(jnp.int32, (tq, tq), 1)
    causal = col <= row
    blk = lax.broadcasted_iota(jnp.int32, (nb, tq), 0)
    starts = []
    for t in range(2):
        qi = s if t == 0 else nb - 1 - s
        start = pl.multiple_of(qi * tq, tq)
        starts.append(start)
        q = q_ref[pl.ds(start, tq), :]
        v_own = v_ref[pl.ds(start, tq), :]
        for h in range(2):
            hm = _head_mask(q.shape, h)
            qh = jnp.where(hm, q, jnp.zeros_like(q))
            gate = _dot_nt(kmh_ref[...], qh) + _dot_nt(kml_ref[...], qh)
            cnt = jnp.zeros((nb, tq), jnp.int32)
            for jp in range(nb):
                g_jp = gate[jp:jp + 1, :]
                beats = (g_jp > gate) | ((g_jp == gate) & (blk > jp))
                cnt = cnt + jnp.where(beats & (qi > jp), 1, 0)
            sel = ((blk < qi) & (cnt < MOBA_TOPK)) | (blk == qi)
            bias_t = jnp.where(sel, 0.0, NEG).astype(F32)
            spare = HEAD_DIM * (1 - h)
            pieces = [jnp.zeros((spare, tq), F32)] if spare else []
            pieces += [bias_t, jnp.zeros((LANES - spare - nb, tq), F32)]
            qa = jnp.where(hm, q, jnp.concatenate(pieces, axis=0).T.astype(BF16))
            qaug[t, h] = qa

            sc = jnp.where(causal, _dot_nt(qa, kaug[h, pl.ds(start, tq), :]), NEG)
            m = jnp.max(sc, axis=1, keepdims=True)
            p = jnp.exp(sc - m)
            m_scr[t, h] = jnp.broadcast_to(m, (tq, LANES))
            l_scr[t, h] = jnp.broadcast_to(jnp.sum(p, axis=1, keepdims=True), (tq, LANES))
            acc_scr[t, h] = jnp.dot(p.astype(BF16), v_own, preferred_element_type=F32)

    for it in range(nb - 1):
        first = it < s
        t = jnp.where(first, 0, 1)
        kstart = pl.multiple_of(jnp.where(first, it, it - s) * tq, tq)
        vj = v_ref[pl.ds(kstart, tq), :]
        for h in range(2):
            sc = _dot_nt(qaug[t, h], kaug[h, pl.ds(kstart, tq), :])
            m_old = m_scr[t, h]
            m_new = jnp.maximum(m_old, jnp.max(sc, axis=1, keepdims=True))
            p = jnp.exp(sc - jnp.concatenate([m_new, m_new], axis=1))
            alpha = jnp.exp(m_old - m_new)
            l_scr[t, h] = alpha * l_scr[t, h] + jnp.sum(p, axis=1, keepdims=True)
            acc_scr[t, h] = alpha * acc_scr[t, h] + jnp.dot(p.astype(BF16), vj, preferred_element_type=F32)
            m_scr[t, h] = m_new

    for t in range(2):
        o = jnp.where(_head_mask((tq, LANES), 0), acc_scr[t, 0] / l_scr[t, 0], acc_scr[t, 1] / l_scr[t, 1])
        o_ref[pl.ds(starts[t], tq), :] = o.astype(o_ref.dtype)


def _moba_attention(qm3, km3, vm3, kmh, kml):
    batch, seq, w = qm3.shape
    nb = seq // MOBA_BLOCK
    assert nb % 2 == 0 and nb <= HEAD_DIM
    npair = w // LANES
    tq = MOBA_BLOCK
    seq_spec = pl.BlockSpec((None, seq, LANES), lambda b, p, i: (b, 0, p))
    km_spec = pl.BlockSpec((None, nb, LANES), lambda b, p, i: (b, 0, p))
    state = pltpu.VMEM((2, 2, tq, LANES), F32)
    return pl.pallas_call(
        _moba_kernel,
        grid=(batch, npair, nb // 2),
        in_specs=[seq_spec, seq_spec, seq_spec, km_spec, km_spec],
        out_specs=seq_spec,
        out_shape=jax.ShapeDtypeStruct((batch, seq, w), BF16),
        scratch_shapes=[pltpu.VMEM((2, 2, tq, LANES), BF16), pltpu.VMEM((2, seq, LANES), BF16),
                        state, state, state],
        compiler_params=_cparams("parallel", "parallel", "arbitrary"),
        name="moba_attn",
    )(qm3, km3, vm3, kmh, kml)


def _memkv_kernel(mem_ref, w_ref, k_ref, v_ref):
    kv = jnp.dot(mem_ref[...].astype(BF16), w_ref[...], preferred_element_type=F32)
    k_ref[...] = kv[:, :W_MEM].astype(BF16)
    v_ref[...] = kv[:, W_MEM:].astype(BF16)


def _mem_kv(mem, w_kv_b):
    batch, m, d = mem.shape
    return pl.pallas_call(
        _memkv_kernel,
        grid=(batch,),
        in_specs=[pl.BlockSpec((None, m, d), lambda b: (b, 0, 0)),
                  pl.BlockSpec((d, 2 * W_MEM), lambda b: (0, 0))],
        out_specs=[pl.BlockSpec((None, m, W_MEM), lambda b: (b, 0, 0))] * 2,
        out_shape=[jax.ShapeDtypeStruct((batch, m, W_MEM), BF16)] * 2,
        compiler_params=_cparams("parallel"),
        name="mem_kv_proj",
    )(mem, w_kv_b)


def _memattn_kernel(q_ref, k_ref, v_ref, o_ref):
    q = q_ref[...]
    k = k_ref[...]
    v = v_ref[...]
    outs = []
    for h in range(2):
        qh = jnp.where(_head_mask(q.shape, h), q, jnp.zeros_like(q))
        s = _dot_nt(qh, k)
        m = jnp.max(s, axis=1, keepdims=True)
        p = jnp.exp(s - m)
        l = jnp.sum(p, axis=1, keepdims=True)
        outs.append(jnp.dot(p.astype(BF16), v, preferred_element_type=F32) / l)
    o_ref[...] = jnp.where(_head_mask(outs[0].shape, 0), outs[0], outs[1]).astype(o_ref.dtype)


def _mem_attention(qx3, k_mem, v_mem, tq=512):
    batch, seq, w = qx3.shape
    m = k_mem.shape[1]
    return pl.pallas_call(
        _memattn_kernel,
        grid=(batch, w // LANES, seq // tq),
        in_specs=[pl.BlockSpec((None, tq, LANES), lambda b, p, i: (b, i, p)),
                  pl.BlockSpec((None, m, LANES), lambda b, p, i: (b, 0, p)),
                  pl.BlockSpec((None, m, LANES), lambda b, p, i: (b, 0, p))],
        out_specs=pl.BlockSpec((None, tq, LANES), lambda b, p, i: (b, i, p)),
        out_shape=jax.ShapeDtypeStruct((batch, seq, w), BF16),
        compiler_params=_cparams("parallel", "parallel", "parallel"),
        name="mem_attn",
    )(qx3, k_mem, v_mem)


def _merge_kernel(alpha, x_ref, o1_ref, o2_ref, o3_ref, l1_ref, l2_ref, l3_ref, ym_ref, yx_ref,
                  wg_ref, bg_ref, wbd_ref, wbm_ref, wbx_ref, wo_ref, g_ref, b_ref, out_ref):
    x = x_ref[...]
    xb = x.astype(BF16)
    d = x.shape[1]
    l1, l2, l3 = l1_ref[...], l2_ref[...], l3_ref[...]
    mx = jnp.maximum(jnp.maximum(l1, l2), l3)
    e1, e2, e3 = jnp.exp(l1 - mx), jnp.exp(l2 - mx), jnp.exp(l3 - mx)
    y_dil = (e1 * o1_ref[...].astype(F32) + e2 * o2_ref[...].astype(F32)
             + e3 * o3_ref[...].astype(F32)) / (e1 + e2 + e3)
    branches = (
        jnp.dot(y_dil.astype(BF16), wbd_ref[...], preferred_element_type=F32),
        jnp.dot(ym_ref[...], wbm_ref[...], preferred_element_type=F32),
        jnp.dot(yx_ref[...], wbx_ref[...], preferred_element_type=F32),
    )
    merged = jnp.zeros_like(x)
    for i, br in enumerate(branches):
        logits = jnp.dot(xb, wg_ref[:, i * d:(i + 1) * d], preferred_element_type=F32) + bg_ref[:, i * d:(i + 1) * d]
        merged = merged + jax.nn.sigmoid(logits) * br
    mix = jnp.dot(merged.astype(BF16), wo_ref[...], preferred_element_type=F32)
    out_ref[...] = _layer_norm(alpha * x + mix, g_ref[...], b_ref[...])


def _merge(alpha, x2, o_dil, lse_dil, y_moba, y_mem, wg, bg, wbd, wbm, wbx, wo, g, b, tm=512):
    n, d = x2.shape
    row = lambda w: pl.BlockSpec((tm, w), lambda i: (i, 0))
    full = lambda a: pl.BlockSpec(a.shape, lambda i: (0, 0))
    weights = (wg, bg, wbd, wbm, wbx, wo, g, b)
    return pl.pallas_call(
        functools.partial(_merge_kernel, alpha),
        grid=(n // tm,),
        in_specs=[row(d)] + [row(LANES)] * 6 + [row(W_MOBA), row(W_MEM)] + [full(a) for a in weights],
        out_specs=row(d),
        out_shape=jax.ShapeDtypeStruct((n, d), F32),
        compiler_params=_cparams("parallel"),
        name="merge_outproj_ln1",
    )(x2, *o_dil, *lse_dil, y_moba, y_mem, *weights)


def _token_mixer(x2, mem, positions, w_in, w_mem_kv, w_gate, b_gate, w_br_dil, w_br_moba, w_br_mem,
                 w_out, ln1_g, ln1_b, batch, seq, alpha):
    cos_t, sin_t = _rope_tables(positions)
    qd, kd, vd, qm, km, vm, qx = _in_projection(x2, w_in.astype(BF16), cos_t, sin_t)
    o_dil, lse_dil = [], []
    for g, (_, dilation) in enumerate(DIL_PAIRS):
        o, lse = _dilated_attention(qd, kd, vd, batch, seq, g, dilation)
        o_dil.append(o)
        lse_dil.append(lse)
    qm3, km3, vm3 = (t.reshape(batch, seq, W_MOBA) for t in (qm, km, vm))
    kmh, kml = _moba_kmean(km3)
    y_moba = _moba_attention(qm3, km3, vm3, kmh, kml).reshape(batch * seq, W_MOBA)
    k_mem, v_mem = _mem_kv(mem, w_mem_kv.astype(BF16))
    y_mem = _mem_attention(qx.reshape(batch, seq, W_MEM), k_mem, v_mem).reshape(batch * seq, W_MEM)
    return _merge(alpha, x2, o_dil, lse_dil, y_moba, y_mem,
                  w_gate.astype(BF16), b_gate.reshape(1, -1), w_br_dil.astype(BF16),
                  w_br_moba.astype(BF16), w_br_mem.astype(BF16), w_out.astype(BF16),
                  ln1_g.reshape(1, -1), ln1_b.reshape(1, -1))


EXPERT_ROWS = 256
TOKEN_TILE = 256
PACKED = 512


def _first_index_of_max(v, iota_f, size):
    m = jnp.max(v, axis=0, keepdims=True)
    idx = jnp.min(jnp.where(v == m, iota_f, float(size)), axis=0, keepdims=True)
    return m, idx


def _route_kernel(x_ref, wh_ref, wl_ref, bias_ref, eidx_ref, gw_ref, rank_ref, cnt_ref, carry):
    step = pl.program_id(0)

    @pl.when(step == 0)
    def _():
        carry[...] = jnp.zeros_like(carry)

    x = x_ref[...]
    tm = x.shape[0]
    xh = x.astype(BF16)
    xl = (x - xh.astype(F32)).astype(BF16)
    wh = wh_ref[...]
    logits = _dot_nt(wh, xh) + _dot_nt(wh, xl) + _dot_nt(wl_ref[...], xh)
    scores = jax.nn.sigmoid(logits)
    biased = scores + bias_ref[...][:, :1]

    giota = lax.broadcasted_iota(jnp.int32, (GROUP_SIZE, tm), 0).astype(F32)
    group_scores = []
    for g in range(N_GROUPS):
        slab = biased[g * GROUP_SIZE:(g + 1) * GROUP_SIZE, :]
        m1, i1 = _first_index_of_max(slab, giota, GROUP_SIZE)
        m2 = jnp.max(jnp.where(giota == i1, -jnp.inf, slab), axis=0, keepdims=True)
        group_scores.append(m1 + m2)
    gs = jnp.concatenate(group_scores, axis=0)
    gidx = lax.broadcasted_iota(jnp.int32, (N_GROUPS, tm), 0)
    beaten = jnp.zeros((N_GROUPS, tm), jnp.int32)
    for gp in range(N_GROUPS):
        row = gs[gp:gp + 1, :]
        beaten = beaten + jnp.where((row > gs) | ((row == gs) & (gidx > gp)), 1, 0)
    keep = beaten < TOPK_GROUPS
    masked = jnp.concatenate(
        [jnp.where(keep[g:g + 1, :], biased[g * GROUP_SIZE:(g + 1) * GROUP_SIZE, :], -jnp.inf)
         for g in range(N_GROUPS)], axis=0)

    eiota = lax.broadcasted_iota(jnp.int32, (N_EXPERTS, tm), 0).astype(F32)
    idx_rows, gw_rows = [], []
    for _ in range(TOP_K):
        _, idx = _first_index_of_max(masked, eiota, N_EXPERTS)
        hit = eiota == idx
        gw_rows.append(jnp.sum(jnp.where(hit, scores, 0.0), axis=0, keepdims=True))
        masked = jnp.where(hit, -jnp.inf, masked)
        idx_rows.append(idx)
    idx8 = jnp.concatenate(idx_rows, axis=0)
    gw8 = jnp.concatenate(gw_rows, axis=0)
    gw_ref[...] = gw8 / jnp.sum(gw8, axis=0, keepdims=True) * ROUTED_SCALE
    eidx_ref[...] = idx8.astype(jnp.int32)

    onehot = jnp.zeros((N_EXPERTS, tm), F32)
    for k in range(TOP_K):
        onehot = onehot + jnp.where(eiota == idx8[k:k + 1, :], 1.0, 0.0)
    t_row = lax.broadcasted_iota(jnp.int32, (tm, tm), 0)
    t_col = lax.broadcasted_iota(jnp.int32, (tm, tm), 1)
    earlier = jnp.where(t_row < t_col, 1.0, 0.0).astype(BF16)
    prefix = jnp.dot(onehot.astype(BF16), earlier, preferred_element_type=F32)
    base = carry[...]
    prefix = prefix + jnp.concatenate([base] * (tm // LANES), axis=1)
    rank_rows = [jnp.sum(jnp.where(eiota == idx8[k:k + 1, :], prefix, 0.0), axis=0, keepdims=True)
                 for k in range(TOP_K)]
    rank_ref[...] = jnp.concatenate(rank_rows, axis=0).astype(jnp.int32)
    total = base + jnp.sum(onehot, axis=1, keepdims=True)
    carry[...] = total
    cnt_ref[...] = total.astype(jnp.int32)


def _route(x1, wr_hi_t, wr_lo_t, bias_b, tm=512):
    n, d = x1.shape
    tok = lambda dt: jax.ShapeDtypeStruct((TOP_K, n), dt)
    return pl.pallas_call(
        _route_kernel,
        grid=(n // tm,),
        in_specs=[pl.BlockSpec((tm, d), lambda i: (i, 0)),
                  pl.BlockSpec((N_EXPERTS, d), lambda i: (0, 0)),
                  pl.BlockSpec((N_EXPERTS, d), lambda i: (0, 0)),
                  pl.BlockSpec((N_EXPERTS, LANES), lambda i: (0, 0))],
        out_specs=[pl.BlockSpec((TOP_K, tm), lambda i: (0, i)),
                   pl.BlockSpec((TOP_K, tm), lambda i: (0, i)),
                   pl.BlockSpec((TOP_K, tm), lambda i: (0, i)),
                   pl.BlockSpec((N_EXPERTS, LANES), lambda i: (0, 0))],
        out_shape=[tok(jnp.int32), tok(F32), tok(jnp.int32),
                   jax.ShapeDtypeStruct((N_EXPERTS, LANES), jnp.int32)],
        scratch_shapes=[pltpu.VMEM((N_EXPERTS, LANES), F32)],
        compiler_params=_cparams("arbitrary"),
        name="moe_route",
    )(x1, wr_hi_t, wr_lo_t, bias_b)


def _dest_kernel(eidx_ref, rank_ref, start_ref, dest_ref):
    eidx = eidx_ref[...]
    tm = eidx.shape[1]
    eiota = lax.broadcasted_iota(jnp.int32, (N_EXPERTS, tm), 0)
    start = start_ref[...][:, :1]
    rows = [jnp.sum(jnp.where(eiota == eidx[k:k + 1, :], start, 0.0), axis=0, keepdims=True)
            for k in range(TOP_K)]
    dest_ref[...] = jnp.concatenate(rows, axis=0).astype(jnp.int32) + rank_ref[...]


def _dest_rows(eidx, rank, start_b):
    n = eidx.shape[1]
    nt = n // TOKEN_TILE
    return pl.pallas_call(
        _dest_kernel,
        grid=(nt,),
        in_specs=[pl.BlockSpec((TOP_K, TOKEN_TILE), lambda i: (0, i)),
                  pl.BlockSpec((TOP_K, TOKEN_TILE), lambda i: (0, i)),
                  pl.BlockSpec((N_EXPERTS, LANES), lambda i: (0, 0))],
        out_specs=pl.BlockSpec((None, TOP_K, TOKEN_TILE), lambda i: (i, 0, 0)),
        out_shape=jax.ShapeDtypeStruct((nt, TOP_K, TOKEN_TILE), jnp.int32),
        compiler_params=_cparams("parallel"),
        name="moe_dest_rows",
    )(eidx, rank, start_b)


def _pack_halves(v):
    return pltpu.pack_elementwise([v[:, :PACKED], v[:, PACKED:]], packed_dtype=BF16)


def _unpack_half(p, index):
    return pltpu.unpack_elementwise(p, index=index, packed_dtype=BF16, unpacked_dtype=F32)


def _dispatch_kernel(x_ref, dest_hbm, xs_hbm, packed, dest_smem, sem_idx, sem_rows):
    step = pl.program_id(0)
    idx_copy = pltpu.make_async_copy(dest_hbm.at[step], dest_smem, sem_idx)
    idx_copy.start()
    packed[...] = _pack_halves(x_ref[...])
    idx_copy.wait()

    def row_copy(t, k):
        return pltpu.make_async_copy(packed.at[pl.ds(t, 1)], xs_hbm.at[pl.ds(dest_smem[k, t], 1)], sem_rows)

    def issue(t, c):
        for k in range(TOP_K):
            row_copy(t, k).start(priority=k % 2)
        return c

    def drain(t, c):
        for k in range(TOP_K):
            row_copy(t, k).wait()
        return c

    lax.fori_loop(0, TOKEN_TILE, issue, 0)
    lax.fori_loop(0, TOKEN_TILE, drain, 0)


def _dispatch(x1, dest3, rows):
    n, d = x1.shape
    return pl.pallas_call(
        _dispatch_kernel,
        grid=(n // TOKEN_TILE,),
        in_specs=[pl.BlockSpec((TOKEN_TILE, d), lambda i: (i, 0)),
                  pl.BlockSpec(memory_space=pl.ANY)],
        out_specs=pl.BlockSpec(memory_space=pl.ANY),
        out_shape=jax.ShapeDtypeStruct((rows, PACKED), jnp.uint32),
        scratch_shapes=[pltpu.VMEM((TOKEN_TILE, PACKED), jnp.uint32),
                        pltpu.SMEM((TOP_K, TOKEN_TILE), jnp.int32),
                        pltpu.SemaphoreType.DMA, pltpu.SemaphoreType.DMA],
        compiler_params=_cparams("arbitrary"),
        name="moe_dispatch",
    )(x1, dest3)


def _expert_kernel(blk_e_ref, valid_ref, nused_ref, xs_ref, wg_ref, wu_ref, wd_ref, ys_ref,
                   wg_b, wu_b, wd_b):
    i = pl.program_id(0)
    active = i < nused_ref[0]
    new_expert = (i == 0) | (blk_e_ref[i] != blk_e_ref[jnp.maximum(i - 1, 0)])

    @pl.when(active & new_expert)
    def _():
        wg_b[...] = wg_ref[...].astype(BF16)
        wu_b[...] = wu_ref[...].astype(BF16)
        wd_b[...] = wd_ref[...].astype(BF16)

    @pl.when(active)
    def _():
        p = xs_ref[...]
        row = lax.broadcasted_iota(jnp.int32, p.shape, 0)
        p = jnp.where(row < valid_ref[i], p, jnp.zeros_like(p))
        lo = _unpack_half(p, 0).astype(BF16)
        hi = _unpack_half(p, 1).astype(BF16)

        def up(w):
            return (jnp.dot(lo, w[:PACKED, :], preferred_element_type=F32)
                    + jnp.dot(hi, w[PACKED:, :], preferred_element_type=F32))

        hid = (jax.nn.silu(up(wg_b)) * up(wu_b)).astype(BF16)
        y = jnp.dot(hid, wd_b[...], preferred_element_type=F32)
        ys_ref[...] = _pack_halves(y)


def _expert_ffn(xs, blk_e, valid, nused, w_e_gate, w_e_up, w_e_down):
    rows = xs.shape[0]
    nblk = rows // EXPERT_ROWS
    d = w_e_gate.shape[1]
    row_map = lambda i, be, va, nu: (jnp.minimum(i, nu[0] - 1), 0)
    w_map = lambda i, be, va, nu: (be[i], 0, 0)
    return pl.pallas_call(
        _expert_kernel,
        grid_spec=pltpu.PrefetchScalarGridSpec(
            num_scalar_prefetch=3,
            grid=(nblk,),
            in_specs=[pl.BlockSpec((EXPERT_ROWS, PACKED), row_map),
                      pl.BlockSpec((None, d, D_EXPERT), w_map),
                      pl.BlockSpec((None, d, D_EXPERT), w_map),
                      pl.BlockSpec((None, D_EXPERT, d), w_map)],
            out_specs=pl.BlockSpec((EXPERT_ROWS, PACKED), row_map),
            scratch_shapes=[pltpu.VMEM((d, D_EXPERT), BF16), pltpu.VMEM((d, D_EXPERT), BF16),
                            pltpu.VMEM((D_EXPERT, d), BF16)],
        ),
        out_shape=jax.ShapeDtypeStruct((rows, PACKED), jnp.uint32),
        compiler_params=_cparams("arbitrary"),
        name="moe_expert_ffn",
    )(blk_e, valid, nused, xs, w_e_gate, w_e_up, w_e_down)


def _combine_kernel(alpha, x_ref, gw_ref, dest_hbm, ys_hbm, wsg_ref, wsu_ref, wsd_ref, g_ref, b_ref,
                    out_ref, gathered, dest_smem, sem_idx, sem_rows):
    step = pl.program_id(0)
    idx_copy = pltpu.make_async_copy(dest_hbm.at[step], dest_smem, sem_idx)
    idx_copy.start()
    idx_copy.wait()

    def row_copy(t, k):
        return pltpu.make_async_copy(ys_hbm.at[pl.ds(dest_smem[k, t], 1)], gathered.at[k, pl.ds(t, 1)], sem_rows)

    def issue(t, c):
        for k in range(TOP_K):
            row_copy(t, k).start(priority=k % 2)
        return c

    def drain(t, c):
        for k in range(TOP_K):
            row_copy(t, k).wait()
        return c

    lax.fori_loop(0, TOKEN_TILE, issue, 0)
    x = x_ref[...]
    xb = x.astype(BF16)
    hid = (jax.nn.silu(jnp.dot(xb, wsg_ref[...], preferred_element_type=F32))
           * jnp.dot(xb, wsu_ref[...], preferred_element_type=F32)).astype(BF16)
    shared = jnp.dot(hid, wsd_ref[...], preferred_element_type=F32)
    lax.fori_loop(0, TOKEN_TILE, drain, 0)

    gw = gw_ref[...]
    lo = jnp.zeros((TOKEN_TILE, PACKED), F32)
    hi = jnp.zeros((TOKEN_TILE, PACKED), F32)
    for k in range(TOP_K):
        p = gathered[k]
        w = gw[:, k:k + 1]
        lo = lo + w * _unpack_half(p, 0)
        hi = hi + w * _unpack_half(p, 1)
    routed = jnp.concatenate([lo, hi], axis=1)
    out_ref[...] = _layer_norm(alpha * x + (routed + shared), g_ref[...], b_ref[...])


def _combine(alpha, x1, gw_t, dest3, ys, wsg, wsu, wsd, g, b):
    n, d = x1.shape
    full = lambda a: pl.BlockSpec(a.shape, lambda i: (0, 0))
    weights = (wsg, wsu, wsd, g, b)
    return pl.pallas_call(
        functools.partial(_combine_kernel, alpha),
        grid=(n // TOKEN_TILE,),
        in_specs=[pl.BlockSpec((TOKEN_TILE, d), lambda i: (i, 0)),
                  pl.BlockSpec((TOKEN_TILE, TOP_K), lambda i: (i, 0)),
                  pl.BlockSpec(memory_space=pl.ANY),
                  pl.BlockSpec(memory_space=pl.ANY)] + [full(a) for a in weights],
        out_specs=pl.BlockSpec((TOKEN_TILE, d), lambda i: (i, 0)),
        out_shape=jax.ShapeDtypeStruct((n, d), F32),
        scratch_shapes=[pltpu.VMEM((TOP_K, TOKEN_TILE, PACKED), jnp.uint32),
                        pltpu.SMEM((TOP_K, TOKEN_TILE), jnp.int32),
                        pltpu.SemaphoreType.DMA, pltpu.SemaphoreType.DMA],
        compiler_params=_cparams("arbitrary"),
        name="moe_combine_shared_ln2",
    )(x1, gw_t, dest3, ys, *weights)


def _moe_layer(x1, w_router, router_bias, w_e_gate, w_e_up, w_e_down, w_s_gate, w_s_up, w_s_down,
               ln2_g, ln2_b, alpha):
    n, d = x1.shape
    wr_t = w_router.T
    wr_hi = wr_t.astype(BF16)
    wr_lo = (wr_t - wr_hi.astype(F32)).astype(BF16)
    bias_b = jnp.broadcast_to(router_bias.astype(F32)[:, None], (N_EXPERTS, LANES))
    eidx, gw, rank, cnt = _route(x1, wr_hi, wr_lo, bias_b)

    counts = cnt[:, 0]
    padded = (counts + EXPERT_ROWS - 1) // EXPERT_ROWS * EXPERT_ROWS
    seg_end = jnp.cumsum(padded)
    seg_start = seg_end - padded
    rows = n * TOP_K + N_EXPERTS * EXPERT_ROWS
    blk_first_row = jnp.arange(rows // EXPERT_ROWS, dtype=jnp.int32) * EXPERT_ROWS
    blk_e = jnp.minimum(jnp.sum((seg_end[None, :] <= blk_first_row[:, None]).astype(jnp.int32), axis=1),
                        N_EXPERTS - 1)
    valid = jnp.clip(counts[blk_e] - (blk_first_row - seg_start[blk_e]), 0, EXPERT_ROWS).astype(jnp.int32)
    nused = (seg_end[-1:] // EXPERT_ROWS).astype(jnp.int32)
    start_b = jnp.broadcast_to(seg_start.astype(F32)[:, None], (N_EXPERTS, LANES))

    dest3 = _dest_rows(eidx, rank, start_b)
    xs = _dispatch(x1, dest3, rows)
    ys = _expert_ffn(xs, blk_e, valid, nused, w_e_gate, w_e_up, w_e_down)
    return _combine(alpha, x1, gw.T, dest3, ys, w_s_gate.astype(BF16), w_s_up.astype(BF16),
                    w_s_down.astype(BF16), ln2_g.reshape(1, -1), ln2_b.reshape(1, -1))


def kernel(x, mem, positions, w_in, w_mem_kv, w_gate, b_gate, w_br_dil, w_br_moba, w_br_mem, w_out, ln1_g, ln1_b, w_router, router_bias, w_e_gate, w_e_up, w_e_down, w_s_gate, w_s_up, w_s_down, ln2_g, ln2_b):
    batch, seq, d = x.shape
    depth = w_in.shape[0]
    alpha = (2.0 * depth) ** 0.25
    h = x.reshape(batch * seq, d)
    for l in range(depth):
        h = _token_mixer(h, mem, positions, w_in[l], w_mem_kv[l], w_gate[l], b_gate[l], w_br_dil[l],
                         w_br_moba[l], w_br_mem[l], w_out[l], ln1_g[l], ln1_b[l], batch, seq, alpha)
        h = _moe_layer(h, w_router[l], router_bias[l], w_e_gate[l], w_e_up[l], w_e_down[l],
                       w_s_gate[l], w_s_up[l], w_s_down[l], ln2_g[l], ln2_b[l], alpha)
    return h.reshape(batch, seq, d)
```

```python
import functools

import jax
import jax.numpy as jnp
from jax import lax
from jax.experimental import pallas as pl
from jax.experimental.pallas import tpu as pltpu

F32 = jnp.float32
BF16 = jnp.bfloat16

LANES = 128
VMEM_LIMIT_BYTES = 48 * 1024 * 1024

HEAD_DIM = 64
ROPE_DIM = HEAD_DIM // 4
ROPE_HALF = ROPE_DIM // 2
ROPE_THETA = 500000.0
DIL_PAIRS = ((128, 1), (512, 4), (2048, 16))
BAND = 128
W_DIL = 384
W_MOBA = 384
W_MEM = 256
MOBA_BLOCK = 256
MOBA_TOPK = 3
N_EXPERTS = 256
N_GROUPS = 8
GROUP_SIZE = N_EXPERTS // N_GROUPS
TOPK_GROUPS = 4
TOP_K = 8
D_EXPERT = 256
ROUTED_SCALE = 2.5
LN_EPS = 1e-5
NEG = -1e30
QK_SCALE = HEAD_DIM ** -0.5


def _cparams(*sem):
    return pltpu.CompilerParams(dimension_semantics=sem, vmem_limit_bytes=VMEM_LIMIT_BYTES)


def _dot_nt(a, b):
    return lax.dot_general(a, b, (((1,), (1,)), ((), ())), preferred_element_type=F32)


def _layer_norm(v, g, b):
    mu = jnp.mean(v, axis=-1, keepdims=True)
    c = v - mu
    var = jnp.mean(c * c, axis=-1, keepdims=True)
    return c * lax.rsqrt(var + LN_EPS) * g + b


_IN_SECTIONS = (
    (0, W_DIL, True, True),
    (W_DIL, W_DIL, True, False),
    (2 * W_DIL, W_DIL, False, False),
    (3 * W_DIL, W_MOBA, True, True),
    (3 * W_DIL + W_MOBA, W_MOBA, True, False),
    (3 * W_DIL + 2 * W_MOBA, W_MOBA, False, False),
    (3 * W_DIL + 3 * W_MOBA, W_MEM, False, True),
)


def _inproj_kernel(x_ref, w_ref, cos_ref, sin_ref, *out_refs):
    xb = x_ref[...].astype(BF16)
    cos = cos_ref[...]
    sin = sin_ref[...]
    tm = xb.shape[0]
    lane = lax.broadcasted_iota(jnp.int32, (tm, LANES), 1)
    first_half = (lane % ROPE_DIM) < ROPE_HALF

    def rope(t):
        partner = jnp.where(first_half, pltpu.roll(t, LANES - ROPE_HALF, 1), pltpu.roll(t, ROPE_HALF, 1))
        return t * cos + partner * sin

    for (off, width, roped, scaled), o_ref in zip(_IN_SECTIONS, out_refs):
        acc = jnp.dot(xb, w_ref[:, off:off + width], preferred_element_type=F32)
        for c in range(width // LANES):
            t = acc[:, c * LANES:(c + 1) * LANES]
            if roped:
                t = rope(t)
            if scaled:
                t = t * QK_SCALE
            o_ref[:, c * LANES:(c + 1) * LANES] = t.astype(BF16)


def _in_projection(x2, w_in_b, cos_t, sin_t, tm=512):
    n, d = x2.shape
    w_total = w_in_b.shape[1]
    widths = [s[1] for s in _IN_SECTIONS]
    return pl.pallas_call(
        _inproj_kernel,
        grid=(n // tm,),
        in_specs=[
            pl.BlockSpec((tm, d), lambda i: (i, 0)),
            pl.BlockSpec((d, w_total), lambda i: (0, 0)),
            pl.BlockSpec((tm, LANES), lambda i: (i, 0)),
            pl.BlockSpec((tm, LANES), lambda i: (i, 0)),
        ],
        out_specs=[pl.BlockSpec((tm, w), lambda i: (i, 0)) for w in widths],
        out_shape=[jax.ShapeDtypeStruct((n, w), BF16) for w in widths],
        compiler_params=_cparams("parallel"),
        name="in_proj_rope",
    )(x2, w_in_b, cos_t, sin_t)


def _rope_tables(positions):
    inv_freq = ROPE_THETA ** (-jnp.arange(ROPE_HALF, dtype=F32) / ROPE_HALF)
    ang = positions.reshape(-1).astype(F32)[:, None] * inv_freq
    cos8, sin8 = jnp.cos(ang), jnp.sin(ang)
    n = ang.shape[0]
    pad1 = jnp.ones((n, HEAD_DIM - ROPE_DIM), F32)
    pad0 = jnp.zeros((n, HEAD_DIM - ROPE_DIM), F32)
    cos_h = jnp.concatenate([cos8, cos8, pad1], axis=1)
    sin_h = jnp.concatenate([-sin8, sin8, pad0], axis=1)
    return jnp.concatenate([cos_h, cos_h], axis=1), jnp.concatenate([sin_h, sin_h], axis=1)


def _head_mask(shape, h):
    lane = lax.broadcasted_iota(jnp.int32, shape, 1)
    return (lane // HEAD_DIM) == h


def _dil_kernel(q_ref, kp_ref, kc_ref, vp_ref, vc_ref, o_ref, lse_ref):
    i = pl.program_id(2)
    q = q_ref[...]
    k = jnp.concatenate([kp_ref[...], kc_ref[...]], axis=0)
    v = jnp.concatenate([vp_ref[...], vc_ref[...]], axis=0)
    qi = lax.broadcasted_iota(jnp.int32, (BAND, 2 * BAND), 0)
    kj = lax.broadcasted_iota(jnp.int32, (BAND, 2 * BAND), 1)
    dist = qi + BAND - kj
    allowed = (dist >= 0) & (dist <= BAND) & ((kj >= BAND) | (i > 0))
    outs, lses = [], []
    for h in range(2):
        qh = jnp.where(_head_mask(q.shape, h), q, jnp.zeros_like(q))
        s = jnp.where(allowed, _dot_nt(qh, k), NEG)
        m = jnp.max(s, axis=1, keepdims=True)
        p = jnp.exp(s - m)
        l = jnp.sum(p, axis=1, keepdims=True)
        outs.append(jnp.dot(p.astype(BF16), v, preferred_element_type=F32) / l)
        lses.append(jnp.broadcast_to(m + jnp.log(l), (BAND, LANES)))
    first = _head_mask((BAND, LANES), 0)
    o_ref[...] = jnp.where(first, outs[0], outs[1]).astype(o_ref.dtype)
    lse_ref[...] = jnp.where(first, lses[0], lses[1])


def _dilated_attention(qd, kd, vd, batch, seq, g, dilation):
    steps = seq // dilation
    nblk = steps // BAND
    ng = W_DIL // LANES
    q3 = qd.reshape(batch, steps, dilation * W_DIL)
    k3 = kd.reshape(batch, steps, dilation * W_DIL)
    v3 = vd.reshape(batch, steps, dilation * W_DIL)
    cur = lambda b, r, i: (b, i, r * ng + g)
    prev = lambda b, r, i: (b, jnp.maximum(i - 1, 0), r * ng + g)
    blk = (None, BAND, LANES)
    o, lse = pl.pallas_call(
        _dil_kernel,
        grid=(batch, dilation, nblk),
        in_specs=[pl.BlockSpec(blk, cur), pl.BlockSpec(blk, prev), pl.BlockSpec(blk, cur),
                  pl.BlockSpec(blk, prev), pl.BlockSpec(blk, cur)],
        out_specs=[pl.BlockSpec(blk, lambda b, r, i: (b, i, r)),
                   pl.BlockSpec(blk, lambda b, r, i: (b, i, r))],
        out_shape=[jax.ShapeDtypeStruct((batch, steps, dilation * LANES), BF16),
                   jax.ShapeDtypeStruct((batch, steps, dilation * LANES), F32)],
        compiler_params=_cparams("parallel", "parallel", "arbitrary"),
        name=f"dilated_attn_d{dilation}",
    )(q3, k3, k3, v3, v3)
    return o.reshape(batch * seq, LANES), lse.reshape(batch * seq, LANES)


def _kmean_kernel(k_ref, hi_ref, lo_ref):
    k = k_ref[...].astype(F32)
    s, w = k.shape
    mean = jnp.sum(k.reshape(s // MOBA_BLOCK, MOBA_BLOCK, w), axis=1) / MOBA_BLOCK
    hi = mean.astype(BF16)
    hi_ref[...] = hi
    lo_ref[...] = (mean - hi.astype(F32)).astype(BF16)


def _moba_kmean(km3):
    batch, seq, w = km3.shape
    nb = seq // MOBA_BLOCK
    return pl.pallas_call(
        _kmean_kernel,
        grid=(batch,),
        in_specs=[pl.BlockSpec((None, seq, w), lambda b: (b, 0, 0))],
        out_specs=[pl.BlockSpec((None, nb, w), lambda b: (b, 0, 0))] * 2,
        out_shape=[jax.ShapeDtypeStruct((batch, nb, w), BF16)] * 2,
        compiler_params=_cparams("parallel"),
        name="moba_kmean",
    )(km3)


def _moba_kernel(q_ref, k_ref, v_ref, kmh_ref, kml_ref, o_ref, qaug, kaug, m_scr, l_scr, acc_scr):
    s = pl.program_id(2)
    nb = kmh_ref.shape[0]
    tq = MOBA_BLOCK
    lane = lax.broadcasted_iota(jnp.int32, (tq, LANES), 1)

    @pl.when(s == 0)
    def _():
        for h in range(2):
            spare = HEAD_DIM * (1 - h)
            for j in range(nb):
                onehot = jnp.where(lane == spare + j, 1.0, 0.0).astype(BF16)
                kaug[h, j * tq:(j + 1) * tq, :] = jnp.where(
                    _head_mask((tq, LANES), h), k_ref[j * tq:(j + 1) * tq, :], onehot)

    row = lax.broadcasted_iota(jnp.int32, (tq, tq), 0)
    col = lax.broadcasted_iota(jnp.int32, (tq, tq), 1)
    causal = col <= row
    blk = lax.broadcasted_iota(jnp.int32, (nb, tq), 0)
    starts = []
    for t in range(2):
        qi = s if t == 0 else nb - 1 - s
        start = pl.multiple_of(qi * tq, tq)
        starts.append(start)
        q = q_ref[pl.ds(start, tq), :]
        v_own = v_ref[pl.ds(start, tq), :]
        for h in range(2):
            hm = _head_mask(q.shape, h)
            qh = jnp.where(hm, q, jnp.zeros_like(q))
            gate = _dot_nt(kmh_ref[...], qh) + _dot_nt(kml_ref[...], qh)
            cnt = jnp.zeros((nb, tq), jnp.int32)
            for jp in range(nb):
                g_jp = gate[jp:jp + 1, :]
                beats = (g_jp > gate) | ((g_jp == gate) & (blk > jp))
                cnt = cnt + jnp.where(beats & (qi > jp), 1, 0)
            sel = ((blk < qi) & (cnt < MOBA_TOPK)) | (blk == qi)
            bias_t = jnp.where(sel, 0.0, NEG).astype(F32)
            spare = HEAD_DIM * (1 - h)
            pieces = [jnp.zeros((spare, tq), F32)] if spare else []
            pieces += [bias_t, jnp.zeros((LANES - spare - nb, tq), F32)]
            qa = jnp.where(hm, q, jnp.concatenate(pieces, axis=0).T.astype(BF16))
            qaug[t, h] = qa

            sc = jnp.where(causal, _dot_nt(qa, kaug[h, pl.ds(start, tq), :]), NEG)
            m = jnp.max(sc, axis=1, keepdims=True)
            p = jnp.exp(sc - m)
            m_scr[t, h] = jnp.broadcast_to(m, (tq, LANES))
            l_scr[t, h] = jnp.broadcast_to(jnp.sum(p, axis=1, keepdims=True), (tq, LANES))
            acc_scr[t, h] = jnp.dot(p.astype(BF16), v_own, preferred_element_type=F32)

    for it in range(nb - 1):
        first = it < s
        t = jnp.where(first, 0, 1)
        kstart = pl.multiple_of(jnp.where(first, it, it - s) * tq, tq)
        vj = v_ref[pl.ds(kstart, tq), :]
        for h in range(2):
            sc = _dot_nt(qaug[t, h], kaug[h, pl.ds(kstart, tq), :])
            m_old = m_scr[t, h]
            m_new = jnp.maximum(m_old, jnp.max(sc, axis=1, keepdims=True))
            p = jnp.exp(sc - jnp.concatenate([m_new, m_new], axis=1))
            alpha = jnp.exp(m_old - m_new)
            l_scr[t, h] = alpha * l_scr[t, h] + jnp.sum(p, axis=1, keepdims=True)
            acc_scr[t, h] = alpha * acc_scr[t, h] + jnp.dot(p.astype(BF16), vj, preferred_element_type=F32)
            m_scr[t, h] = m_new

    for t in range(2):
        o = jnp.where(_head_mask((tq, LANES), 0), acc_scr[t, 0] / l_scr[t, 0], acc_scr[t, 1] / l_scr[t, 1])
        o_ref[pl.ds(starts[t], tq), :] = o.astype(o_ref.dtype)


def _moba_attention(qm3, km3, vm3, kmh, kml):
    batch, seq, w = qm3.shape
    nb = seq // MOBA_BLOCK
    assert nb % 2 == 0 and nb <= HEAD_DIM
    npair = w // LANES
    tq = MOBA_BLOCK
    seq_spec = pl.BlockSpec((None, seq, LANES), lambda b, p, i: (b, 0, p))
    km_spec = pl.BlockSpec((None, nb, LANES), lambda b, p, i: (b, 0, p))
    state = pltpu.VMEM((2, 2, tq, LANES), F32)
    return pl.pallas_call(
        _moba_kernel,
        grid=(batch, npair, nb // 2),
        in_specs=[seq_spec, seq_spec, seq_spec, km_spec, km_spec],
        out_specs=seq_spec,
        out_shape=jax.ShapeDtypeStruct((batch, seq, w), BF16),
        scratch_shapes=[pltpu.VMEM((2, 2, tq, LANES), BF16), pltpu.VMEM((2, seq, LANES), BF16),
                        state, state, state],
        compiler_params=_cparams("parallel", "parallel", "arbitrary"),
        name="moba_attn",
    )(qm3, km3, vm3, kmh, kml)


def _memkv_kernel(mem_ref, w_ref, k_ref, v_ref):
    kv = jnp.dot(mem_ref[...].astype(BF16), w_ref[...], preferred_element_type=F32)
    k_ref[...] = kv[:, :W_MEM].astype(BF16)
    v_ref[...] = kv[:, W_MEM:].astype(BF16)


def _mem_kv(mem, w_kv_b):
    batch, m, d = mem.shape
    return pl.pallas_call(
        _memkv_kernel,
        grid=(batch,),
        in_specs=[pl.BlockSpec((None, m, d), lambda b: (b, 0, 0)),
                  pl.BlockSpec((d, 2 * W_MEM), lambda b: (0, 0))],
        out_specs=[pl.BlockSpec((None, m, W_MEM), lambda b: (b, 0, 0))] * 2,
        out_shape=[jax.ShapeDtypeStruct((batch, m, W_MEM), BF16)] * 2,
        compiler_params=_cparams("parallel"),
        name="mem_kv_proj",
    )(mem, w_kv_b)


def _memattn_kernel(q_ref, k_ref, v_ref, o_ref):
    q = q_ref[...]
    k = k_ref[...]
    v = v_ref[...]
    outs = []
    for h in range(2):
        qh = jnp.where(_head_mask(q.shape, h), q, jnp.zeros_like(q))
        s = _dot_nt(qh, k)
        m = jnp.max(s, axis=1, keepdims=True)
        p = jnp.exp(s - m)
        l = jnp.sum(p, axis=1, keepdims=True)
        outs.append(jnp.dot(p.astype(BF16), v, preferred_element_type=F32) / l)
    o_ref[...] = jnp.where(_head_mask(outs[0].shape, 0), outs[0], outs[1]).astype(o_ref.dtype)


def _mem_attention(qx3, k_mem, v_mem, tq=512):
    batch, seq, w = qx3.shape
    m = k_mem.shape[1]
    return pl.pallas_call(
        _memattn_kernel,
        grid=(batch, w // LANES, seq // tq),
        in_specs=[pl.BlockSpec((None, tq, LANES), lambda b, p, i: (b, i, p)),
                  pl.BlockSpec((None, m, LANES), lambda b, p, i: (b, 0, p)),
                  pl.BlockSpec((None, m, LANES), lambda b, p, i: (b, 0, p))],
        out_specs=pl.BlockSpec((None, tq, LANES), lambda b, p, i: (b, i, p)),
        out_shape=jax.ShapeDtypeStruct((batch, seq, w), BF16),
        compiler_params=_cparams("parallel", "parallel", "parallel"),
        name="mem_attn",
    )(qx3, k_mem, v_mem)


def _merge_kernel(alpha, x_ref, o1_ref, o2_ref, o3_ref, l1_ref, l2_ref, l3_ref, ym_ref, yx_ref,
                  wg_ref, bg_ref, wbd_ref, wbm_ref, wbx_ref, wo_ref, g_ref, b_ref, out_ref):
    x = x_ref[...]
    xb = x.astype(BF16)
    d = x.shape[1]
    l1, l2, l3 = l1_ref[...], l2_ref[...], l3_ref[...]
    mx = jnp.maximum(jnp.maximum(l1, l2), l3)
    e1, e2, e3 = jnp.exp(l1 - mx), jnp.exp(l2 - mx), jnp.exp(l3 - mx)
    y_dil = (e1 * o1_ref[...].astype(F32) + e2 * o2_ref[...].astype(F32)
             + e3 * o3_ref[...].astype(F32)) / (e1 + e2 + e3)
    branches = (
        jnp.dot(y_dil.astype(BF16), wbd_ref[...], preferred_element_type=F32),
        jnp.dot(ym_ref[...], wbm_ref[...], preferred_element_type=F32),
        jnp.dot(yx_ref[...], wbx_ref[...], preferred_element_type=F32),
    )
    merged = jnp.zeros_like(x)
    for i, br in enumerate(branches):
        logits = jnp.dot(xb, wg_ref[:, i * d:(i + 1) * d], preferred_element_type=F32) + bg_ref[:, i * d:(i + 1) * d]
        merged = merged + jax.nn.sigmoid(logits) * br
    mix = jnp.dot(merged.astype(BF16), wo_ref[...], preferred_element_type=F32)
    out_ref[...] = _layer_norm(alpha * x + mix, g_ref[...], b_ref[...])


def _merge(alpha, x2, o_dil, lse_dil, y_moba, y_mem, wg, bg, wbd, wbm, wbx, wo, g, b, tm=512):
    n, d = x2.shape
    row = lambda w: pl.BlockSpec((tm, w), lambda i: (i, 0))
    full = lambda a: pl.BlockSpec(a.shape, lambda i: (0, 0))
    weights = (wg, bg, wbd, wbm, wbx, wo, g, b)
    return pl.pallas_call(
        functools.partial(_merge_kernel, alpha),
        grid=(n // tm,),
        in_specs=[row(d)] + [row(LANES)] * 6 + [row(W_MOBA), row(W_MEM)] + [full(a) for a in weights],
        out_specs=row(d),
        out_shape=jax.ShapeDtypeStruct((n, d), F32),
        compiler_params=_cparams("parallel"),
        name="merge_outproj_ln1",
    )(x2, *o_dil, *lse_dil, y_moba, y_mem, *weights)


def _token_mixer(x2, mem, positions, w_in, w_mem_kv, w_gate, b_gate, w_br_dil, w_br_moba, w_br_mem,
                 w_out, ln1_g, ln1_b, batch, seq, alpha):
    cos_t, sin_t = _rope_tables(positions)
    qd, kd, vd, qm, km, vm, qx = _in_projection(x2, w_in.astype(BF16), cos_t, sin_t)
    o_dil, lse_dil = [], []
    for g, (_, dilation) in enumerate(DIL_PAIRS):
        o, lse = _dilated_attention(qd, kd, vd, batch, seq, g, dilation)
        o_dil.append(o)
        lse_dil.append(lse)
    qm3, km3, vm3 = (t.reshape(batch, seq, W_MOBA) for t in (qm, km, vm))
    kmh, kml = _moba_kmean(km3)
    y_moba = _moba_attention(qm3, km3, vm3, kmh, kml).reshape(batch * seq, W_MOBA)
    k_mem, v_mem = _mem_kv(mem, w_mem_kv.astype(BF16))
    y_mem = _mem_attention(qx.reshape(batch, seq, W_MEM), k_mem, v_mem).reshape(batch * seq, W_MEM)
    return _merge(alpha, x2, o_dil, lse_dil, y_moba, y_mem,
                  w_gate.astype(BF16), b_gate.reshape(1, -1), w_br_dil.astype(BF16),
                  w_br_moba.astype(BF16), w_br_mem.astype(BF16), w_out.astype(BF16),
                  ln1_g.reshape(1, -1), ln1_b.reshape(1, -1))


EXPERT_ROWS = 512
EXPERT_SUB = 256
TOKEN_TILE = 256
PACKED = 512


def _first_index_of_max(v, iota_f, size):
    m = jnp.max(v, axis=0, keepdims=True)
    idx = jnp.min(jnp.where(v == m, iota_f, float(size)), axis=0, keepdims=True)
    return m, idx


def _route_kernel(x_ref, wh_ref, wl_ref, bias_ref, eidx_ref, gw_ref, rank_ref, cnt_ref, carry):
    step = pl.program_id(0)

    @pl.when(step == 0)
    def _():
        carry[...] = jnp.zeros_like(carry)

    x = x_ref[...]
    tm = x.shape[0]
    xh = x.astype(BF16)
    xl = (x - xh.astype(F32)).astype(BF16)
    wh = wh_ref[...]
    logits = _dot_nt(wh, xh) + _dot_nt(wh, xl) + _dot_nt(wl_ref[...], xh)
    scores = jax.nn.sigmoid(logits)
    biased = scores + bias_ref[...][:, :1]

    giota = lax.broadcasted_iota(jnp.int32, (GROUP_SIZE, tm), 0).astype(F32)
    group_scores = []
    for g in range(N_GROUPS):
        slab = biased[g * GROUP_SIZE:(g + 1) * GROUP_SIZE, :]
        m1, i1 = _first_index_of_max(slab, giota, GROUP_SIZE)
        m2 = jnp.max(jnp.where(giota == i1, -jnp.inf, slab), axis=0, keepdims=True)
        group_scores.append(m1 + m2)
    gs = jnp.concatenate(group_scores, axis=0)
    gidx = lax.broadcasted_iota(jnp.int32, (N_GROUPS, tm), 0)
    beaten = jnp.zeros((N_GROUPS, tm), jnp.int32)
    for gp in range(N_GROUPS):
        row = gs[gp:gp + 1, :]
        beaten = beaten + jnp.where((row > gs) | ((row == gs) & (gidx > gp)), 1, 0)
    keep = beaten < TOPK_GROUPS
    masked = jnp.concatenate(
        [jnp.where(keep[g:g + 1, :], biased[g * GROUP_SIZE:(g + 1) * GROUP_SIZE, :], -jnp.inf)
         for g in range(N_GROUPS)], axis=0)

    eiota = lax.broadcasted_iota(jnp.int32, (N_EXPERTS, tm), 0).astype(F32)
    idx_rows, gw_rows = [], []
    for _ in range(TOP_K):
        _, idx = _first_index_of_max(masked, eiota, N_EXPERTS)
        hit = eiota == idx
        gw_rows.append(jnp.sum(jnp.where(hit, scores, 0.0), axis=0, keepdims=True))
        masked = jnp.where(hit, -jnp.inf, masked)
        idx_rows.append(idx)
    idx8 = jnp.concatenate(idx_rows, axis=0)
    gw8 = jnp.concatenate(gw_rows, axis=0)
    gw_ref[...] = gw8 / jnp.sum(gw8, axis=0, keepdims=True) * ROUTED_SCALE
    eidx_ref[...] = idx8.astype(jnp.int32)

    onehot = jnp.zeros((N_EXPERTS, tm), F32)
    for k in range(TOP_K):
        onehot = onehot + jnp.where(eiota == idx8[k:k + 1, :], 1.0, 0.0)
    t_row = lax.broadcasted_iota(jnp.int32, (tm, tm), 0)
    t_col = lax.broadcasted_iota(jnp.int32, (tm, tm), 1)
    earlier = jnp.where(t_row < t_col, 1.0, 0.0).astype(BF16)
    prefix = jnp.dot(onehot.astype(BF16), earlier, preferred_element_type=F32)
    base = carry[...]
    prefix = prefix + jnp.concatenate([base] * (tm // LANES), axis=1)
    rank_rows = [jnp.sum(jnp.where(eiota == idx8[k:k + 1, :], prefix, 0.0), axis=0, keepdims=True)
                 for k in range(TOP_K)]
    rank_ref[...] = jnp.concatenate(rank_rows, axis=0).astype(jnp.int32)
    total = base + jnp.sum(onehot, axis=1, keepdims=True)
    carry[...] = total
    cnt_ref[...] = total.astype(jnp.int32)


def _route(x1, wr_hi_t, wr_lo_t, bias_b, tm=512):
    n, d = x1.shape
    tok = lambda dt: jax.ShapeDtypeStruct((TOP_K, n), dt)
    return pl.pallas_call(
        _route_kernel,
        grid=(n // tm,),
        in_specs=[pl.BlockSpec((tm, d), lambda i: (i, 0)),
                  pl.BlockSpec((N_EXPERTS, d), lambda i: (0, 0)),
                  pl.BlockSpec((N_EXPERTS, d), lambda i: (0, 0)),
                  pl.BlockSpec((N_EXPERTS, LANES), lambda i: (0, 0))],
        out_specs=[pl.BlockSpec((TOP_K, tm), lambda i: (0, i)),
                   pl.BlockSpec((TOP_K, tm), lambda i: (0, i)),
                   pl.BlockSpec((TOP_K, tm), lambda i: (0, i)),
                   pl.BlockSpec((N_EXPERTS, LANES), lambda i: (0, 0))],
        out_shape=[tok(jnp.int32), tok(F32), tok(jnp.int32),
                   jax.ShapeDtypeStruct((N_EXPERTS, LANES), jnp.int32)],
        scratch_shapes=[pltpu.VMEM((N_EXPERTS, LANES), F32)],
        compiler_params=_cparams("arbitrary"),
        name="moe_route",
    )(x1, wr_hi_t, wr_lo_t, bias_b)


def _dest_kernel(eidx_ref, rank_ref, start_ref, dest_ref):
    eidx = eidx_ref[...]
    tm = eidx.shape[1]
    eiota = lax.broadcasted_iota(jnp.int32, (N_EXPERTS, tm), 0)
    start = start_ref[...][:, :1]
    rows = [jnp.sum(jnp.where(eiota == eidx[k:k + 1, :], start, 0.0), axis=0, keepdims=True)
            for k in range(TOP_K)]
    dest_ref[...] = jnp.concatenate(rows, axis=0).astype(jnp.int32) + rank_ref[...]


def _dest_rows(eidx, rank, start_b):
    n = eidx.shape[1]
    nt = n // TOKEN_TILE
    return pl.pallas_call(
        _dest_kernel,
        grid=(nt,),
        in_specs=[pl.BlockSpec((TOP_K, TOKEN_TILE), lambda i: (0, i)),
                  pl.BlockSpec((TOP_K, TOKEN_TILE), lambda i: (0, i)),
                  pl.BlockSpec((N_EXPERTS, LANES), lambda i: (0, 0))],
        out_specs=pl.BlockSpec((None, TOP_K, TOKEN_TILE), lambda i: (i, 0, 0)),
        out_shape=jax.ShapeDtypeStruct((nt, TOP_K, TOKEN_TILE), jnp.int32),
        compiler_params=_cparams("parallel"),
        name="moe_dest_rows",
    )(eidx, rank, start_b)


def _pack_halves(v):
    return pltpu.pack_elementwise([v[:, :PACKED], v[:, PACKED:]], packed_dtype=BF16)


def _unpack_half(p, index):
    return pltpu.unpack_elementwise(p, index=index, packed_dtype=BF16, unpacked_dtype=F32)


def _wait_rows(src_row, dst_row, sem):
    def body(t, c):
        for _ in range(TOP_K):
            pltpu.make_async_copy(src_row, dst_row, sem).wait()
        return c

    lax.fori_loop(0, TOKEN_TILE, body, 0)


def _dispatch_kernel(x_ref, dest_hbm, xs_hbm, packed, dest_smem, sem_idx, sem_rows):
    step = pl.program_id(0)
    nsteps = pl.num_programs(0)
    slot = step % 2

    def idx_copy(s, sl):
        return pltpu.make_async_copy(dest_hbm.at[s], dest_smem.at[sl], sem_idx.at[sl])

    @pl.when(step == 0)
    def _():
        idx_copy(0, 0).start()

    @pl.when(step + 1 < nsteps)
    def _():
        idx_copy(step + 1, 1 - slot).start()

    packed[slot] = _pack_halves(x_ref[...])
    idx_copy(step, slot).wait()

    def issue(t, c):
        for k in range(TOP_K):
            pltpu.make_async_copy(packed.at[slot, pl.ds(t, 1)], xs_hbm.at[pl.ds(dest_smem[slot, k, t], 1)],
                                  sem_rows.at[slot]).start(priority=k % 2)
        return c

    lax.fori_loop(0, TOKEN_TILE, issue, 0)

    def drain(sl):
        _wait_rows(packed.at[sl, pl.ds(0, 1)], xs_hbm.at[pl.ds(0, 1)], sem_rows.at[sl])

    @pl.when(step > 0)
    def _():
        drain(1 - slot)

    @pl.when(step == nsteps - 1)
    def _():
        drain(slot)


def _dispatch(x1, dest3, rows):
    n, d = x1.shape
    return pl.pallas_call(
        _dispatch_kernel,
        grid=(n // TOKEN_TILE,),
        in_specs=[pl.BlockSpec((TOKEN_TILE, d), lambda i: (i, 0)),
                  pl.BlockSpec(memory_space=pl.ANY)],
        out_specs=pl.BlockSpec(memory_space=pl.ANY),
        out_shape=jax.ShapeDtypeStruct((rows, PACKED), jnp.uint32),
        scratch_shapes=[pltpu.VMEM((2, TOKEN_TILE, PACKED), jnp.uint32),
                        pltpu.SMEM((2, TOP_K, TOKEN_TILE), jnp.int32),
                        pltpu.SemaphoreType.DMA((2,)), pltpu.SemaphoreType.DMA((2,))],
        compiler_params=_cparams("arbitrary"),
        name="moe_dispatch",
    )(x1, dest3)


def _expert_kernel(blk_e_ref, valid_ref, nused_ref, xs_ref, wg_ref, wu_ref, wd_ref, ys_ref,
                   wg_b, wu_b, wd_b):
    i = pl.program_id(0)
    active = i < nused_ref[0]
    new_expert = (i == 0) | (blk_e_ref[i] != blk_e_ref[jnp.maximum(i - 1, 0)])

    @pl.when(active & new_expert)
    def _():
        wg_b[...] = wg_ref[...].astype(BF16)
        wu_b[...] = wu_ref[...].astype(BF16)
        wd_b[...] = wd_ref[...].astype(BF16)

    @pl.when(active)
    def _():
        for r0 in range(0, EXPERT_ROWS, EXPERT_SUB):
            p = xs_ref[r0:r0 + EXPERT_SUB, :]
            row = lax.broadcasted_iota(jnp.int32, p.shape, 0) + r0
            p = jnp.where(row < valid_ref[i], p, jnp.zeros_like(p))
            lo = _unpack_half(p, 0).astype(BF16)
            hi = _unpack_half(p, 1).astype(BF16)

            def up(w, lo=lo, hi=hi):
                return (jnp.dot(lo, w[:PACKED, :], preferred_element_type=F32)
                        + jnp.dot(hi, w[PACKED:, :], preferred_element_type=F32))

            hid = (jax.nn.silu(up(wg_b)) * up(wu_b)).astype(BF16)
            y = jnp.dot(hid, wd_b[...], preferred_element_type=F32)
            ys_ref[r0:r0 + EXPERT_SUB, :] = _pack_halves(y)


def _expert_ffn(xs, blk_e, valid, nused, w_e_gate, w_e_up, w_e_down):
    rows = xs.shape[0]
    nblk = rows // EXPERT_ROWS
    d = w_e_gate.shape[1]
    row_map = lambda i, be, va, nu: (jnp.minimum(i, nu[0] - 1), 0)
    w_map = lambda i, be, va, nu: (be[i], 0, 0)
    return pl.pallas_call(
        _expert_kernel,
        grid_spec=pltpu.PrefetchScalarGridSpec(
            num_scalar_prefetch=3,
            grid=(nblk,),
            in_specs=[pl.BlockSpec((EXPERT_ROWS, PACKED), row_map),
                      pl.BlockSpec((None, d, D_EXPERT), w_map),
                      pl.BlockSpec((None, d, D_EXPERT), w_map),
                      pl.BlockSpec((None, D_EXPERT, d), w_map)],
            out_specs=pl.BlockSpec((EXPERT_ROWS, PACKED), row_map),
            scratch_shapes=[pltpu.VMEM((d, D_EXPERT), BF16), pltpu.VMEM((d, D_EXPERT), BF16),
                            pltpu.VMEM((D_EXPERT, d), BF16)],
        ),
        out_shape=jax.ShapeDtypeStruct((rows, PACKED), jnp.uint32),
        compiler_params=_cparams("arbitrary"),
        name="moe_expert_ffn",
    )(blk_e, valid, nused, xs, w_e_gate, w_e_up, w_e_down)


def _combine_kernel(alpha, x_ref, gw_ref, dest_hbm, ys_hbm, wsg_ref, wsu_ref, wsd_ref, g_ref, b_ref,
                    out_ref, gathered, dest_smem, sem_idx, sem_rows):
    step = pl.program_id(0)
    nsteps = pl.num_programs(0)
    slot = step % 2

    def idx_copy(s, sl):
        return pltpu.make_async_copy(dest_hbm.at[s], dest_smem.at[sl], sem_idx.at[sl])

    def gather(sl):
        def issue(t, c):
            for k in range(TOP_K):
                pltpu.make_async_copy(ys_hbm.at[pl.ds(dest_smem[sl, k, t], 1)], gathered.at[sl, k, pl.ds(t, 1)],
                                      sem_rows.at[sl]).start(priority=k % 2)
            return c

        lax.fori_loop(0, TOKEN_TILE, issue, 0)

    @pl.when(step == 0)
    def _():
        idx_copy(0, 0).start()
        idx_copy(0, 0).wait()
        gather(0)

    @pl.when((step == 0) & (step + 1 < nsteps))
    def _():
        idx_copy(1, 1).start()

    @pl.when(step + 1 < nsteps)
    def _():
        idx_copy(step + 1, 1 - slot).wait()
        gather(1 - slot)

    @pl.when(step + 2 < nsteps)
    def _():
        idx_copy(step + 2, slot).start()

    x = x_ref[...]
    xb = x.astype(BF16)
    hid = (jax.nn.silu(jnp.dot(xb, wsg_ref[...], preferred_element_type=F32))
           * jnp.dot(xb, wsu_ref[...], preferred_element_type=F32)).astype(BF16)
    shared = jnp.dot(hid, wsd_ref[...], preferred_element_type=F32)
    _wait_rows(ys_hbm.at[pl.ds(0, 1)], gathered.at[slot, 0, pl.ds(0, 1)], sem_rows.at[slot])

    gw = gw_ref[...]
    lo = jnp.zeros((TOKEN_TILE, PACKED), F32)
    hi = jnp.zeros((TOKEN_TILE, PACKED), F32)
    for k in range(TOP_K):
        p = gathered[slot, k]
        w = gw[:, k:k + 1]
        lo = lo + w * _unpack_half(p, 0)
        hi = hi + w * _unpack_half(p, 1)
    routed = jnp.concatenate([lo, hi], axis=1)
    out_ref[...] = _layer_norm(alpha * x + (routed + shared), g_ref[...], b_ref[...])


def _combine(alpha, x1, gw_t, dest3, ys, wsg, wsu, wsd, g, b):
    n, d = x1.shape
    full = lambda a: pl.BlockSpec(a.shape, lambda i: (0, 0))
    weights = (wsg, wsu, wsd, g, b)
    return pl.pallas_call(
        functools.partial(_combine_kernel, alpha),
        grid=(n // TOKEN_TILE,),
        in_specs=[pl.BlockSpec((TOKEN_TILE, d), lambda i: (i, 0)),
                  pl.BlockSpec((TOKEN_TILE, TOP_K), lambda i: (i, 0)),
                  pl.BlockSpec(memory_space=pl.ANY),
                  pl.BlockSpec(memory_space=pl.ANY)] + [full(a) for a in weights],
        out_specs=pl.BlockSpec((TOKEN_TILE, d), lambda i: (i, 0)),
        out_shape=jax.ShapeDtypeStruct((n, d), F32),
        scratch_shapes=[pltpu.VMEM((2, TOP_K, TOKEN_TILE, PACKED), jnp.uint32),
                        pltpu.SMEM((2, TOP_K, TOKEN_TILE), jnp.int32),
                        pltpu.SemaphoreType.DMA((2,)), pltpu.SemaphoreType.DMA((2,))],
        compiler_params=_cparams("arbitrary"),
        name="moe_combine_shared_ln2",
    )(x1, gw_t, dest3, ys, *weights)


def _moe_layer(x1, w_router, router_bias, w_e_gate, w_e_up, w_e_down, w_s_gate, w_s_up, w_s_down,
               ln2_g, ln2_b, alpha):
    n, d = x1.shape
    wr_t = w_router.T
    wr_hi = wr_t.astype(BF16)
    wr_lo = (wr_t - wr_hi.astype(F32)).astype(BF16)
    bias_b = jnp.broadcast_to(router_bias.astype(F32)[:, None], (N_EXPERTS, LANES))
    eidx, gw, rank, cnt = _route(x1, wr_hi, wr_lo, bias_b)

    counts = cnt[:, 0]
    padded = (counts + EXPERT_ROWS - 1) // EXPERT_ROWS * EXPERT_ROWS
    seg_end = jnp.cumsum(padded)
    seg_start = seg_end - padded
    rows = n * TOP_K + N_EXPERTS * EXPERT_ROWS
    blk_first_row = jnp.arange(rows // EXPERT_ROWS, dtype=jnp.int32) * EXPERT_ROWS
    blk_e = jnp.minimum(jnp.sum((seg_end[None, :] <= blk_first_row[:, None]).astype(jnp.int32), axis=1),
                        N_EXPERTS - 1)
    valid = jnp.clip(counts[blk_e] - (blk_first_row - seg_start[blk_e]), 0, EXPERT_ROWS).astype(jnp.int32)
    nused = (seg_end[-1:] // EXPERT_ROWS).astype(jnp.int32)
    start_b = jnp.broadcast_to(seg_start.astype(F32)[:, None], (N_EXPERTS, LANES))

    dest3 = _dest_rows(eidx, rank, start_b)
    xs = _dispatch(x1, dest3, rows)
    ys = _expert_ffn(xs, blk_e, valid, nused, w_e_gate, w_e_up, w_e_down)
    return _combine(alpha, x1, gw.T, dest3, ys, w_s_gate.astype(BF16), w_s_up.astype(BF16),
                    w_s_down.astype(BF16), ln2_g.reshape(1, -1), ln2_b.reshape(1, -1))


def kernel(x, mem, positions, w_in, w_mem_kv, w_gate, b_gate, w_br_dil, w_br_moba, w_br_mem, w_out, ln1_g, ln1_b, w_router, router_bias, w_e_gate, w_e_up, w_e_down, w_s_gate, w_s_up, w_s_down, ln2_g, ln2_b):
    batch, seq, d = x.shape
    depth = w_in.shape[0]
    alpha = (2.0 * depth) ** 0.25
    h = x.reshape(batch * seq, d)
    for l in range(depth):
        h = _token_mixer(h, mem, positions, w_in[l], w_mem_kv[l], w_gate[l], b_gate[l], w_br_dil[l],
                         w_br_moba[l], w_br_mem[l], w_out[l], ln1_g[l], ln1_b[l], batch, seq, alpha)
        h = _moe_layer(h, w_router[l], router_bias[l], w_e_gate[l], w_e_up[l], w_e_down[l],
                       w_s_gate[l], w_s_up[l], w_s_down[l], ln2_g[l], ln2_b[l], alpha)
    return h.reshape(batch, seq, d)
```

```python
import functools

import jax
import jax.numpy as jnp
from jax import lax
from jax.experimental import pallas as pl
from jax.experimental.pallas import tpu as pltpu
from jax.experimental.pallas import tpu_sc as plsc

F32 = jnp.float32
BF16 = jnp.bfloat16

LANES = 128
VMEM_LIMIT_BYTES = 48 * 1024 * 1024

HEAD_DIM = 64
ROPE_DIM = HEAD_DIM // 4
ROPE_HALF = ROPE_DIM // 2
ROPE_THETA = 500000.0
DIL_PAIRS = ((128, 1), (512, 4), (2048, 16))
BAND = 128
W_DIL = 384
W_MOBA = 384
W_MEM = 256
MOBA_BLOCK = 256
MOBA_TOPK = 3
N_EXPERTS = 256
N_GROUPS = 8
GROUP_SIZE = N_EXPERTS // N_GROUPS
TOPK_GROUPS = 4
TOP_K = 8
D_EXPERT = 256
ROUTED_SCALE = 2.5
LN_EPS = 1e-5
NEG = -1e30
QK_SCALE = HEAD_DIM ** -0.5


def _cparams(*sem):
    return pltpu.CompilerParams(dimension_semantics=sem, vmem_limit_bytes=VMEM_LIMIT_BYTES)


def _dot_nt(a, b):
    return lax.dot_general(a, b, (((1,), (1,)), ((), ())), preferred_element_type=F32)


def _layer_norm(v, g, b):
    mu = jnp.mean(v, axis=-1, keepdims=True)
    c = v - mu
    var = jnp.mean(c * c, axis=-1, keepdims=True)
    return c * lax.rsqrt(var + LN_EPS) * g + b


_IN_SECTIONS = (
    (0, W_DIL, True, True),
    (W_DIL, W_DIL, True, False),
    (2 * W_DIL, W_DIL, False, False),
    (3 * W_DIL, W_MOBA, True, True),
    (3 * W_DIL + W_MOBA, W_MOBA, True, False),
    (3 * W_DIL + 2 * W_MOBA, W_MOBA, False, False),
    (3 * W_DIL + 3 * W_MOBA, W_MEM, False, True),
)


def _inproj_kernel(x_ref, w_ref, cos_ref, sin_ref, *out_refs):
    xb = x_ref[...].astype(BF16)
    cos = cos_ref[...]
    sin = sin_ref[...]
    tm = xb.shape[0]
    lane = lax.broadcasted_iota(jnp.int32, (tm, LANES), 1)
    first_half = (lane % ROPE_DIM) < ROPE_HALF

    def rope(t):
        partner = jnp.where(first_half, pltpu.roll(t, LANES - ROPE_HALF, 1), pltpu.roll(t, ROPE_HALF, 1))
        return t * cos + partner * sin

    for (off, width, roped, scaled), o_ref in zip(_IN_SECTIONS, out_refs):
        acc = jnp.dot(xb, w_ref[:, off:off + width], preferred_element_type=F32)
        for c in range(width // LANES):
            t = acc[:, c * LANES:(c + 1) * LANES]
            if roped:
                t = rope(t)
            if scaled:
                t = t * QK_SCALE
            o_ref[:, c * LANES:(c + 1) * LANES] = t.astype(BF16)


def _in_projection(x2, w_in_b, cos_t, sin_t, tm=512):
    n, d = x2.shape
    w_total = w_in_b.shape[1]
    widths = [s[1] for s in _IN_SECTIONS]
    return pl.pallas_call(
        _inproj_kernel,
        grid=(n // tm,),
        in_specs=[
            pl.BlockSpec((tm, d), lambda i: (i, 0)),
            pl.BlockSpec((d, w_total), lambda i: (0, 0)),
            pl.BlockSpec((tm, LANES), lambda i: (i, 0)),
            pl.BlockSpec((tm, LANES), lambda i: (i, 0)),
        ],
        out_specs=[pl.BlockSpec((tm, w), lambda i: (i, 0)) for w in widths],
        out_shape=[jax.ShapeDtypeStruct((n, w), BF16) for w in widths],
        compiler_params=_cparams("parallel"),
        name="in_proj_rope",
    )(x2, w_in_b, cos_t, sin_t)


def _rope_tables(positions):
    inv_freq = ROPE_THETA ** (-jnp.arange(ROPE_HALF, dtype=F32) / ROPE_HALF)
    ang = positions.reshape(-1).astype(F32)[:, None] * inv_freq
    cos8, sin8 = jnp.cos(ang), jnp.sin(ang)
    n = ang.shape[0]
    pad1 = jnp.ones((n, HEAD_DIM - ROPE_DIM), F32)
    pad0 = jnp.zeros((n, HEAD_DIM - ROPE_DIM), F32)
    cos_h = jnp.concatenate([cos8, cos8, pad1], axis=1)
    sin_h = jnp.concatenate([-sin8, sin8, pad0], axis=1)
    return jnp.concatenate([cos_h, cos_h], axis=1), jnp.concatenate([sin_h, sin_h], axis=1)


def _head_mask(shape, h):
    lane = lax.broadcasted_iota(jnp.int32, shape, 1)
    return (lane // HEAD_DIM) == h


def _dil_kernel(q_ref, kp_ref, kc_ref, vp_ref, vc_ref, o_ref, lse_ref):
    i = pl.program_id(2)
    q = q_ref[...]
    k = jnp.concatenate([kp_ref[...], kc_ref[...]], axis=0)
    v = jnp.concatenate([vp_ref[...], vc_ref[...]], axis=0)
    qi = lax.broadcasted_iota(jnp.int32, (BAND, 2 * BAND), 0)
    kj = lax.broadcasted_iota(jnp.int32, (BAND, 2 * BAND), 1)
    dist = qi + BAND - kj
    allowed = (dist >= 0) & (dist <= BAND) & ((kj >= BAND) | (i > 0))
    outs, lses = [], []
    for h in range(2):
        qh = jnp.where(_head_mask(q.shape, h), q, jnp.zeros_like(q))
        s = jnp.where(allowed, _dot_nt(qh, k), NEG)
        m = jnp.max(s, axis=1, keepdims=True)
        p = jnp.exp(s - m)
        l = jnp.sum(p, axis=1, keepdims=True)
        outs.append(jnp.dot(p.astype(BF16), v, preferred_element_type=F32) / l)
        lses.append(jnp.broadcast_to(m + jnp.log(l), (BAND, LANES)))
    first = _head_mask((BAND, LANES), 0)
    o_ref[...] = jnp.where(first, outs[0], outs[1]).astype(o_ref.dtype)
    lse_ref[...] = jnp.where(first, lses[0], lses[1])


def _dilated_attention(qd, kd, vd, batch, seq, g, dilation):
    steps = seq // dilation
    nblk = steps // BAND
    ng = W_DIL // LANES
    q3 = qd.reshape(batch, steps, dilation * W_DIL)
    k3 = kd.reshape(batch, steps, dilation * W_DIL)
    v3 = vd.reshape(batch, steps, dilation * W_DIL)
    cur = lambda b, r, i: (b, i, r * ng + g)
    prev = lambda b, r, i: (b, jnp.maximum(i - 1, 0), r * ng + g)
    blk = (None, BAND, LANES)
    o, lse = pl.pallas_call(
        _dil_kernel,
        grid=(batch, dilation, nblk),
        in_specs=[pl.BlockSpec(blk, cur), pl.BlockSpec(blk, prev), pl.BlockSpec(blk, cur),
                  pl.BlockSpec(blk, prev), pl.BlockSpec(blk, cur)],
        out_specs=[pl.BlockSpec(blk, lambda b, r, i: (b, i, r)),
                   pl.BlockSpec(blk, lambda b, r, i: (b, i, r))],
        out_shape=[jax.ShapeDtypeStruct((batch, steps, dilation * LANES), BF16),
                   jax.ShapeDtypeStruct((batch, steps, dilation * LANES), F32)],
        compiler_params=_cparams("parallel", "parallel", "arbitrary"),
        name=f"dilated_attn_d{dilation}",
    )(q3, k3, k3, v3, v3)
    return o.reshape(batch * seq, LANES), lse.reshape(batch * seq, LANES)


def _kmean_kernel(k_ref, hi_ref, lo_ref):
    k = k_ref[...].astype(F32)
    s, w = k.shape
    mean = jnp.sum(k.reshape(s // MOBA_BLOCK, MOBA_BLOCK, w), axis=1) / MOBA_BLOCK
    hi = mean.astype(BF16)
    hi_ref[...] = hi
    lo_ref[...] = (mean - hi.astype(F32)).astype(BF16)


def _moba_kmean(km3):
    batch, seq, w = km3.shape
    nb = seq // MOBA_BLOCK
    return pl.pallas_call(
        _kmean_kernel,
        grid=(batch,),
        in_specs=[pl.BlockSpec((None, seq, w), lambda b: (b, 0, 0))],
        out_specs=[pl.BlockSpec((None, nb, w), lambda b: (b, 0, 0))] * 2,
        out_shape=[jax.ShapeDtypeStruct((batch, nb, w), BF16)] * 2,
        compiler_params=_cparams("parallel"),
        name="moba_kmean",
    )(km3)


def _moba_kernel(q_ref, k_ref, v_ref, kmh_ref, kml_ref, o_ref, qaug, kaug, m_scr, l_scr, acc_scr):
    s = pl.program_id(2)
    nb = kmh_ref.shape[0]
    tq = MOBA_BLOCK
    lane = lax.broadcasted_iota(jnp.int32, (tq, LANES), 1)

    @pl.when(s == 0)
    def _():
        for h in range(2):
            spare = HEAD_DIM * (1 - h)
            for j in range(nb):
                onehot = jnp.where(lane == spare + j, 1.0, 0.0).astype(BF16)
                kaug[h, j * tq:(j + 1) * tq, :] = jnp.where(
                    _head_mask((tq, LANES), h), k_ref[j * tq:(j + 1) * tq, :], onehot)

    row = lax.broadcasted_iota(jnp.int32, (tq, tq), 0)
    col = lax.broadcasted_iota(jnp.int32, (tq, tq), 1)
    causal = col <= row
    blk = lax.broadcasted_iota(jnp.int32, (nb, tq), 0)
    starts = []
    for t in range(2):
        qi = s if t == 0 else nb - 1 - s
        start = pl.multiple_of(qi * tq, tq)
        starts.append(start)
        q = q_ref[pl.ds(start, tq), :]
        v_own = v_ref[pl.ds(start, tq), :]
        for h in range(2):
            hm = _head_mask(q.shape, h)
            qh = jnp.where(hm, q, jnp.zeros_like(q))
            gate = _dot_nt(kmh_ref[...], qh) + _dot_nt(kml_ref[...], qh)
            cnt = jnp.zeros((nb, tq), jnp.int32)
            for jp in range(nb):
                g_jp = gate[jp:jp + 1, :]
                beats = (g_jp > gate) | ((g_jp == gate) & (blk > jp))
                cnt = cnt + jnp.where(beats & (qi > jp), 1, 0)
            sel = ((blk < qi) & (cnt < MOBA_TOPK)) | (blk == qi)
            bias_t = jnp.where(sel, 0.0, NEG).astype(F32)
            spare = HEAD_DIM * (1 - h)
            pieces = [jnp.zeros((spare, tq), F32)] if spare else []
            pieces += [bias_t, jnp.zeros((LANES - spare - nb, tq), F32)]
            qa = jnp.where(hm, q, jnp.concatenate(pieces, axis=0).T.astype(BF16))
            qaug[t, h] = qa

            sc = jnp.where(causal, _dot_nt(qa, kaug[h, pl.ds(start, tq), :]), NEG)
            m = jnp.max(sc, axis=1, keepdims=True)
            p = jnp.exp(sc - m)
            m_scr[t, h] = jnp.broadcast_to(m, (tq, LANES))
            l_scr[t, h] = jnp.broadcast_to(jnp.sum(p, axis=1, keepdims=True), (tq, LANES))
            acc_scr[t, h] = jnp.dot(p.astype(BF16), v_own, preferred_element_type=F32)

    for it in range(nb - 1):
        first = it < s
        t = jnp.where(first, 0, 1)
        kstart = pl.multiple_of(jnp.where(first, it, it - s) * tq, tq)
        vj = v_ref[pl.ds(kstart, tq), :]
        for h in range(2):
            sc = _dot_nt(qaug[t, h], kaug[h, pl.ds(kstart, tq), :])
            m_old = m_scr[t, h]
            m_new = jnp.maximum(m_old, jnp.max(sc, axis=1, keepdims=True))
            p = jnp.exp(sc - jnp.concatenate([m_new, m_new], axis=1))
            alpha = jnp.exp(m_old - m_new)
            l_scr[t, h] = alpha * l_scr[t, h] + jnp.sum(p, axis=1, keepdims=True)
            acc_scr[t, h] = alpha * acc_scr[t, h] + jnp.dot(p.astype(BF16), vj, preferred_element_type=F32)
            m_scr[t, h] = m_new

    for t in range(2):
        o = jnp.where(_head_mask((tq, LANES), 0), acc_scr[t, 0] / l_scr[t, 0], acc_scr[t, 1] / l_scr[t, 1])
        o_ref[pl.ds(starts[t], tq), :] = o.astype(o_ref.dtype)


def _moba_attention(qm3, km3, vm3, kmh, kml):
    batch, seq, w = qm3.shape
    nb = seq // MOBA_BLOCK
    assert nb % 2 == 0 and nb <= HEAD_DIM
    npair = w // LANES
    tq = MOBA_BLOCK
    seq_spec = pl.BlockSpec((None, seq, LANES), lambda b, p, i: (b, 0, p))
    km_spec = pl.BlockSpec((None, nb, LANES), lambda b, p, i: (b, 0, p))
    state = pltpu.VMEM((2, 2, tq, LANES), F32)
    return pl.pallas_call(
        _moba_kernel,
        grid=(batch, npair, nb // 2),
        in_specs=[seq_spec, seq_spec, seq_spec, km_spec, km_spec],
        out_specs=seq_spec,
        out_shape=jax.ShapeDtypeStruct((batch, seq, w), BF16),
        scratch_shapes=[pltpu.VMEM((2, 2, tq, LANES), BF16), pltpu.VMEM((2, seq, LANES), BF16),
                        state, state, state],
        compiler_params=_cparams("parallel", "parallel", "arbitrary"),
        name="moba_attn",
    )(qm3, km3, vm3, kmh, kml)


def _memkv_kernel(mem_ref, w_ref, k_ref, v_ref):
    kv = jnp.dot(mem_ref[...].astype(BF16), w_ref[...], preferred_element_type=F32)
    k_ref[...] = kv[:, :W_MEM].astype(BF16)
    v_ref[...] = kv[:, W_MEM:].astype(BF16)


def _mem_kv(mem, w_kv_b):
    batch, m, d = mem.shape
    return pl.pallas_call(
        _memkv_kernel,
        grid=(batch,),
        in_specs=[pl.BlockSpec((None, m, d), lambda b: (b, 0, 0)),
                  pl.BlockSpec((d, 2 * W_MEM), lambda b: (0, 0))],
        out_specs=[pl.BlockSpec((None, m, W_MEM), lambda b: (b, 0, 0))] * 2,
        out_shape=[jax.ShapeDtypeStruct((batch, m, W_MEM), BF16)] * 2,
        compiler_params=_cparams("parallel"),
        name="mem_kv_proj",
    )(mem, w_kv_b)


def _memattn_kernel(q_ref, k_ref, v_ref, o_ref):
    q = q_ref[...]
    k = k_ref[...]
    v = v_ref[...]
    outs = []
    for h in range(2):
        qh = jnp.where(_head_mask(q.shape, h), q, jnp.zeros_like(q))
        s = _dot_nt(qh, k)
        m = jnp.max(s, axis=1, keepdims=True)
        p = jnp.exp(s - m)
        l = jnp.sum(p, axis=1, keepdims=True)
        outs.append(jnp.dot(p.astype(BF16), v, preferred_element_type=F32) / l)
    o_ref[...] = jnp.where(_head_mask(outs[0].shape, 0), outs[0], outs[1]).astype(o_ref.dtype)


def _mem_attention(qx3, k_mem, v_mem, tq=512):
    batch, seq, w = qx3.shape
    m = k_mem.shape[1]
    return pl.pallas_call(
        _memattn_kernel,
        grid=(batch, w // LANES, seq // tq),
        in_specs=[pl.BlockSpec((None, tq, LANES), lambda b, p, i: (b, i, p)),
                  pl.BlockSpec((None, m, LANES), lambda b, p, i: (b, 0, p)),
                  pl.BlockSpec((None, m, LANES), lambda b, p, i: (b, 0, p))],
        out_specs=pl.BlockSpec((None, tq, LANES), lambda b, p, i: (b, i, p)),
        out_shape=jax.ShapeDtypeStruct((batch, seq, w), BF16),
        compiler_params=_cparams("parallel", "parallel", "parallel"),
        name="mem_attn",
    )(qx3, k_mem, v_mem)


def _merge_kernel(alpha, x_ref, o1_ref, o2_ref, o3_ref, l1_ref, l2_ref, l3_ref, ym_ref, yx_ref,
                  wg_ref, bg_ref, wbd_ref, wbm_ref, wbx_ref, wo_ref, g_ref, b_ref, out_ref, packed_ref):
    x = x_ref[...]
    xb = x.astype(BF16)
    d = x.shape[1]
    l1, l2, l3 = l1_ref[...], l2_ref[...], l3_ref[...]
    mx = jnp.maximum(jnp.maximum(l1, l2), l3)
    e1, e2, e3 = jnp.exp(l1 - mx), jnp.exp(l2 - mx), jnp.exp(l3 - mx)
    y_dil = (e1 * o1_ref[...].astype(F32) + e2 * o2_ref[...].astype(F32)
             + e3 * o3_ref[...].astype(F32)) / (e1 + e2 + e3)
    branches = (
        jnp.dot(y_dil.astype(BF16), wbd_ref[...], preferred_element_type=F32),
        jnp.dot(ym_ref[...], wbm_ref[...], preferred_element_type=F32),
        jnp.dot(yx_ref[...], wbx_ref[...], preferred_element_type=F32),
    )
    merged = jnp.zeros_like(x)
    for i, br in enumerate(branches):
        logits = jnp.dot(xb, wg_ref[:, i * d:(i + 1) * d], preferred_element_type=F32) + bg_ref[:, i * d:(i + 1) * d]
        merged = merged + jax.nn.sigmoid(logits) * br
    mix = jnp.dot(merged.astype(BF16), wo_ref[...], preferred_element_type=F32)
    x1 = _layer_norm(alpha * x + mix, g_ref[...], b_ref[...])
    out_ref[...] = x1
    packed_ref[...] = _pack_halves(x1)


def _merge(alpha, x2, o_dil, lse_dil, y_moba, y_mem, wg, bg, wbd, wbm, wbx, wo, g, b, tm=512):
    n, d = x2.shape
    row = lambda w: pl.BlockSpec((tm, w), lambda i: (i, 0))
    full = lambda a: pl.BlockSpec(a.shape, lambda i: (0, 0))
    weights = (wg, bg, wbd, wbm, wbx, wo, g, b)
    return pl.pallas_call(
        functools.partial(_merge_kernel, alpha),
        grid=(n // tm,),
        in_specs=[row(d)] + [row(LANES)] * 6 + [row(W_MOBA), row(W_MEM)] + [full(a) for a in weights],
        out_specs=[row(d), row(PACKED)],
        out_shape=[jax.ShapeDtypeStruct((n, d), F32), jax.ShapeDtypeStruct((n, PACKED), jnp.uint32)],
        compiler_params=_cparams("parallel"),
        name="merge_outproj_ln1",
    )(x2, *o_dil, *lse_dil, y_moba, y_mem, *weights)


def _token_mixer(x2, mem, positions, w_in, w_mem_kv, w_gate, b_gate, w_br_dil, w_br_moba, w_br_mem,
                 w_out, ln1_g, ln1_b, batch, seq, alpha):
    cos_t, sin_t = _rope_tables(positions)
    qd, kd, vd, qm, km, vm, qx = _in_projection(x2, w_in.astype(BF16), cos_t, sin_t)
    o_dil, lse_dil = [], []
    for g, (_, dilation) in enumerate(DIL_PAIRS):
        o, lse = _dilated_attention(qd, kd, vd, batch, seq, g, dilation)
        o_dil.append(o)
        lse_dil.append(lse)
    qm3, km3, vm3 = (t.reshape(batch, seq, W_MOBA) for t in (qm, km, vm))
    kmh, kml = _moba_kmean(km3)
    y_moba = _moba_attention(qm3, km3, vm3, kmh, kml).reshape(batch * seq, W_MOBA)
    k_mem, v_mem = _mem_kv(mem, w_mem_kv.astype(BF16))
    y_mem = _mem_attention(qx.reshape(batch, seq, W_MEM), k_mem, v_mem).reshape(batch * seq, W_MEM)
    return _merge(alpha, x2, o_dil, lse_dil, y_moba, y_mem,
                  w_gate.astype(BF16), b_gate.reshape(1, -1), w_br_dil.astype(BF16),
                  w_br_moba.astype(BF16), w_br_mem.astype(BF16), w_out.astype(BF16),
                  ln1_g.reshape(1, -1), ln1_b.reshape(1, -1))


EXPERT_ROWS = 512
EXPERT_SUB = 256
TOKEN_TILE = 256
PACKED = 512
SC_WINDOW = 128


def _first_index_of_max(v, iota_f, size):
    m = jnp.max(v, axis=0, keepdims=True)
    idx = jnp.min(jnp.where(v == m, iota_f, float(size)), axis=0, keepdims=True)
    return m, idx


def _route_kernel(x_ref, wh_ref, wl_ref, bias_ref, eidx_ref, gw_ref, rank_ref, cnt_ref, carry):
    step = pl.program_id(0)

    @pl.when(step == 0)
    def _():
        carry[...] = jnp.zeros_like(carry)

    x = x_ref[...]
    tm = x.shape[0]
    xh = x.astype(BF16)
    xl = (x - xh.astype(F32)).astype(BF16)
    wh = wh_ref[...]
    logits = _dot_nt(wh, xh) + _dot_nt(wh, xl) + _dot_nt(wl_ref[...], xh)
    scores = jax.nn.sigmoid(logits)
    biased = scores + bias_ref[...][:, :1]

    giota = lax.broadcasted_iota(jnp.int32, (GROUP_SIZE, tm), 0).astype(F32)
    group_scores = []
    for g in range(N_GROUPS):
        slab = biased[g * GROUP_SIZE:(g + 1) * GROUP_SIZE, :]
        m1, i1 = _first_index_of_max(slab, giota, GROUP_SIZE)
        m2 = jnp.max(jnp.where(giota == i1, -jnp.inf, slab), axis=0, keepdims=True)
        group_scores.append(m1 + m2)
    gs = jnp.concatenate(group_scores, axis=0)
    gidx = lax.broadcasted_iota(jnp.int32, (N_GROUPS, tm), 0)
    beaten = jnp.zeros((N_GROUPS, tm), jnp.int32)
    for gp in range(N_GROUPS):
        row = gs[gp:gp + 1, :]
        beaten = beaten + jnp.where((row > gs) | ((row == gs) & (gidx > gp)), 1, 0)
    keep = beaten < TOPK_GROUPS
    masked = jnp.concatenate(
        [jnp.where(keep[g:g + 1, :], biased[g * GROUP_SIZE:(g + 1) * GROUP_SIZE, :], -jnp.inf)
         for g in range(N_GROUPS)], axis=0)

    eiota = lax.broadcasted_iota(jnp.int32, (N_EXPERTS, tm), 0).astype(F32)
    idx_rows, gw_rows = [], []
    for _ in range(TOP_K):
        _, idx = _first_index_of_max(masked, eiota, N_EXPERTS)
        hit = eiota == idx
        gw_rows.append(jnp.sum(jnp.where(hit, scores, 0.0), axis=0, keepdims=True))
        masked = jnp.where(hit, -jnp.inf, masked)
        idx_rows.append(idx)
    idx8 = jnp.concatenate(idx_rows, axis=0)
    gw8 = jnp.concatenate(gw_rows, axis=0)
    gw_ref[...] = gw8 / jnp.sum(gw8, axis=0, keepdims=True) * ROUTED_SCALE
    eidx_ref[...] = idx8.astype(jnp.int32)

    onehot = jnp.zeros((N_EXPERTS, tm), F32)
    for k in range(TOP_K):
        onehot = onehot + jnp.where(eiota == idx8[k:k + 1, :], 1.0, 0.0)
    t_row = lax.broadcasted_iota(jnp.int32, (tm, tm), 0)
    t_col = lax.broadcasted_iota(jnp.int32, (tm, tm), 1)
    earlier = jnp.where(t_row < t_col, 1.0, 0.0).astype(BF16)
    prefix = jnp.dot(onehot.astype(BF16), earlier, preferred_element_type=F32)
    base = carry[...]
    prefix = prefix + jnp.concatenate([base] * (tm // LANES), axis=1)
    rank_rows = [jnp.sum(jnp.where(eiota == idx8[k:k + 1, :], prefix, 0.0), axis=0, keepdims=True)
                 for k in range(TOP_K)]
    rank_ref[...] = jnp.concatenate(rank_rows, axis=0).astype(jnp.int32)
    total = base + jnp.sum(onehot, axis=1, keepdims=True)
    carry[...] = total
    cnt_ref[...] = total.astype(jnp.int32)


def _route(x1, wr_hi_t, wr_lo_t, bias_b, tm=512):
    n, d = x1.shape
    tok = lambda dt: jax.ShapeDtypeStruct((TOP_K, n), dt)
    return pl.pallas_call(
        _route_kernel,
        grid=(n // tm,),
        in_specs=[pl.BlockSpec((tm, d), lambda i: (i, 0)),
                  pl.BlockSpec((N_EXPERTS, d), lambda i: (0, 0)),
                  pl.BlockSpec((N_EXPERTS, d), lambda i: (0, 0)),
                  pl.BlockSpec((N_EXPERTS, LANES), lambda i: (0, 0))],
        out_specs=[pl.BlockSpec((TOP_K, tm), lambda i: (0, i)),
                   pl.BlockSpec((TOP_K, tm), lambda i: (0, i)),
                   pl.BlockSpec((TOP_K, tm), lambda i: (0, i)),
                   pl.BlockSpec((N_EXPERTS, LANES), lambda i: (0, 0))],
        out_shape=[tok(jnp.int32), tok(F32), tok(jnp.int32),
                   jax.ShapeDtypeStruct((N_EXPERTS, LANES), jnp.int32)],
        scratch_shapes=[pltpu.VMEM((N_EXPERTS, LANES), F32)],
        compiler_params=_cparams("arbitrary"),
        name="moe_route",
    )(x1, wr_hi_t, wr_lo_t, bias_b)


def _dest_kernel(eidx_ref, rank_ref, start_ref, dest_ref):
    eidx = eidx_ref[...]
    tm = eidx.shape[1]
    eiota = lax.broadcasted_iota(jnp.int32, (N_EXPERTS, tm), 0)
    start = start_ref[...][:, :1]
    rows = [jnp.sum(jnp.where(eiota == eidx[k:k + 1, :], start, 0.0), axis=0, keepdims=True)
            for k in range(TOP_K)]
    dest_ref[...] = jnp.concatenate(rows, axis=0).astype(jnp.int32) + rank_ref[...]


def _dest_rows(eidx, rank, start_b):
    n = eidx.shape[1]
    nt = n // TOKEN_TILE
    return pl.pallas_call(
        _dest_kernel,
        grid=(nt,),
        in_specs=[pl.BlockSpec((TOP_K, TOKEN_TILE), lambda i: (0, i)),
                  pl.BlockSpec((TOP_K, TOKEN_TILE), lambda i: (0, i)),
                  pl.BlockSpec((N_EXPERTS, LANES), lambda i: (0, 0))],
        out_specs=pl.BlockSpec((TOP_K, TOKEN_TILE), lambda i: (0, i)),
        out_shape=jax.ShapeDtypeStruct((TOP_K, n), jnp.int32),
        compiler_params=_cparams("parallel"),
        name="moe_dest_rows",
    )(eidx, rank, start_b)


def _pack_halves(v):
    return pltpu.pack_elementwise([v[:, :PACKED], v[:, PACKED:]], packed_dtype=BF16)


def _unpack_half(p, index):
    return pltpu.unpack_elementwise(p, index=index, packed_dtype=BF16, unpacked_dtype=F32)


def _sc_mesh():
    return plsc.VectorSubcoreMesh(core_axis_name="core", subcore_axis_name="subcore")


def _sc_move_rows(src, src_idx, dst_idx, out_rows, name):
    width = src.shape[1]
    count = src_idx.shape[1]

    @functools.partial(pl.kernel, out_type=jax.ShapeDtypeStruct((out_rows, width), src.dtype),
                       mesh=_sc_mesh(), scratch_types=[pltpu.VMEM((SC_WINDOW, width), src.dtype)], name=name)
    def move(src_hbm, sidx_hbm, didx_hbm, out_hbm, rows_vmem):
        def body(sidx_vmem, didx_vmem):
            pltpu.sync_copy(src_hbm.at[sidx_vmem.at[0]], rows_vmem)
            pltpu.sync_copy(rows_vmem, out_hbm.at[didx_vmem.at[0]])

        idx_spec = pl.BlockSpec((1, SC_WINDOW), index_map=lambda i: (0, i))
        pltpu.emit_pipeline(
            body,
            grid=(count // SC_WINDOW,),
            in_specs=[idx_spec, idx_spec],
            out_specs=[],
            core_axis_name=("core", "subcore"),
            dimension_semantics=(pltpu.PARALLEL,),
        )(sidx_hbm, didx_hbm)

    return move(src, src_idx, dst_idx)


def _expert_kernel(blk_e_ref, valid_ref, nused_ref, xs_ref, wg_ref, wu_ref, wd_ref, ys_ref,
                   wg_b, wu_b, wd_b):
    i = pl.program_id(0)
    active = i < nused_ref[0]
    new_expert = (i == 0) | (blk_e_ref[i] != blk_e_ref[jnp.maximum(i - 1, 0)])

    @pl.when(active & new_expert)
    def _():
        wg_b[...] = wg_ref[...].astype(BF16)
        wu_b[...] = wu_ref[...].astype(BF16)
        wd_b[...] = wd_ref[...].astype(BF16)

    @pl.when(active)
    def _():
        for r0 in range(0, EXPERT_ROWS, EXPERT_SUB):
            p = xs_ref[r0:r0 + EXPERT_SUB, :]
            row = lax.broadcasted_iota(jnp.int32, p.shape, 0) + r0
            p = jnp.where(row < valid_ref[i], p, jnp.zeros_like(p))
            lo = _unpack_half(p, 0).astype(BF16)
            hi = _unpack_half(p, 1).astype(BF16)

            def up(w, lo=lo, hi=hi):
                return (jnp.dot(lo, w[:PACKED, :], preferred_element_type=F32)
                        + jnp.dot(hi, w[PACKED:, :], preferred_element_type=F32))

            hid = (jax.nn.silu(up(wg_b)) * up(wu_b)).astype(BF16)
            y = jnp.dot(hid, wd_b[...], preferred_element_type=F32)
            ys_ref[r0:r0 + EXPERT_SUB, :] = _pack_halves(y)


def _expert_ffn(xs, blk_e, valid, nused, w_e_gate, w_e_up, w_e_down):
    rows = xs.shape[0]
    nblk = rows // EXPERT_ROWS
    d = w_e_gate.shape[1]
    row_map = lambda i, be, va, nu: (jnp.minimum(i, nu[0] - 1), 0)
    w_map = lambda i, be, va, nu: (be[i], 0, 0)
    return pl.pallas_call(
        _expert_kernel,
        grid_spec=pltpu.PrefetchScalarGridSpec(
            num_scalar_prefetch=3,
            grid=(nblk,),
            in_specs=[pl.BlockSpec((EXPERT_ROWS, PACKED), row_map),
                      pl.BlockSpec((None, d, D_EXPERT), w_map),
                      pl.BlockSpec((None, d, D_EXPERT), w_map),
                      pl.BlockSpec((None, D_EXPERT, d), w_map)],
            out_specs=pl.BlockSpec((EXPERT_ROWS, PACKED), row_map),
            scratch_shapes=[pltpu.VMEM((d, D_EXPERT), BF16), pltpu.VMEM((d, D_EXPERT), BF16),
                            pltpu.VMEM((D_EXPERT, d), BF16)],
        ),
        out_shape=jax.ShapeDtypeStruct((rows, PACKED), jnp.uint32),
        compiler_params=_cparams("arbitrary"),
        name="moe_expert_ffn",
    )(blk_e, valid, nused, xs, w_e_gate, w_e_up, w_e_down)


def _combine_kernel(alpha, x_ref, gw_ref, ys_ref, wsg_ref, wsu_ref, wsd_ref, g_ref, b_ref, out_ref):
    x = x_ref[...]
    xb = x.astype(BF16)
    hid = (jax.nn.silu(jnp.dot(xb, wsg_ref[...], preferred_element_type=F32))
           * jnp.dot(xb, wsu_ref[...], preferred_element_type=F32)).astype(BF16)
    shared = jnp.dot(hid, wsd_ref[...], preferred_element_type=F32)
    gw = gw_ref[...]
    lo = jnp.zeros((TOKEN_TILE, PACKED), F32)
    hi = jnp.zeros((TOKEN_TILE, PACKED), F32)
    for k in range(TOP_K):
        p = ys_ref[k]
        w = gw[:, k:k + 1]
        lo = lo + w * _unpack_half(p, 0)
        hi = hi + w * _unpack_half(p, 1)
    routed = jnp.concatenate([lo, hi], axis=1)
    out_ref[...] = _layer_norm(alpha * x + (routed + shared), g_ref[...], b_ref[...])


def _combine(alpha, x1, gw_t, ys_tok, wsg, wsu, wsd, g, b):
    n, d = x1.shape
    full = lambda a: pl.BlockSpec(a.shape, lambda i: (0, 0))
    weights = (wsg, wsu, wsd, g, b)
    return pl.pallas_call(
        functools.partial(_combine_kernel, alpha),
        grid=(n // TOKEN_TILE,),
        in_specs=[pl.BlockSpec((TOKEN_TILE, d), lambda i: (i, 0)),
                  pl.BlockSpec((TOKEN_TILE, TOP_K), lambda i: (i, 0)),
                  pl.BlockSpec((TOP_K, TOKEN_TILE, PACKED), lambda i: (0, i, 0))] + [full(a) for a in weights],
        out_specs=pl.BlockSpec((TOKEN_TILE, d), lambda i: (i, 0)),
        out_shape=jax.ShapeDtypeStruct((n, d), F32),
        compiler_params=_cparams("parallel"),
        name="moe_combine_shared_ln2",
    )(x1, gw_t, ys_tok, *weights)


def _moe_layer(x1, x1_packed, w_router, router_bias, w_e_gate, w_e_up, w_e_down, w_s_gate, w_s_up, w_s_down,
               ln2_g, ln2_b, alpha):
    n, d = x1.shape
    wr_t = w_router.T
    wr_hi = wr_t.astype(BF16)
    wr_lo = (wr_t - wr_hi.astype(F32)).astype(BF16)
    bias_b = jnp.broadcast_to(router_bias.astype(F32)[:, None], (N_EXPERTS, LANES))
    eidx, gw, rank, cnt = _route(x1, wr_hi, wr_lo, bias_b)

    counts = cnt[:, 0]
    padded = (counts + EXPERT_ROWS - 1) // EXPERT_ROWS * EXPERT_ROWS
    seg_end = jnp.cumsum(padded)
    seg_start = seg_end - padded
    rows = n * TOP_K + N_EXPERTS * EXPERT_ROWS
    blk_first_row = jnp.arange(rows // EXPERT_ROWS, dtype=jnp.int32) * EXPERT_ROWS
    blk_e = jnp.minimum(jnp.sum((seg_end[None, :] <= blk_first_row[:, None]).astype(jnp.int32), axis=1),
                        N_EXPERTS - 1)
    valid = jnp.clip(counts[blk_e] - (blk_first_row - seg_start[blk_e]), 0, EXPERT_ROWS).astype(jnp.int32)
    nused = (seg_end[-1:] // EXPERT_ROWS).astype(jnp.int32)
    start_b = jnp.broadcast_to(seg_start.astype(F32)[:, None], (N_EXPERTS, LANES))

    dest_flat = _dest_rows(eidx, rank, start_b).reshape(1, TOP_K * n)
    assign = jnp.arange(TOP_K * n, dtype=jnp.int32).reshape(1, TOP_K * n)
    xs = _sc_move_rows(x1_packed, assign % n, dest_flat, rows, "moe_dispatch_sc")
    ys = _expert_ffn(xs, blk_e, valid, nused, w_e_gate, w_e_up, w_e_down)
    ys_tok = _sc_move_rows(ys, dest_flat, assign, TOP_K * n, "moe_gather_sc").reshape(TOP_K, n, PACKED)
    return _combine(alpha, x1, gw.T, ys_tok, w_s_gate.astype(BF16), w_s_up.astype(BF16),
                    w_s_down.astype(BF16), ln2_g.reshape(1, -1), ln2_b.reshape(1, -1))


def kernel(x, mem, positions, w_in, w_mem_kv, w_gate, b_gate, w_br_dil, w_br_moba, w_br_mem, w_out, ln1_g, ln1_b, w_router, router_bias, w_e_gate, w_e_up, w_e_down, w_s_gate, w_s_up, w_s_down, ln2_g, ln2_b):
    batch, seq, d = x.shape
    depth = w_in.shape[0]
    alpha = (2.0 * depth) ** 0.25
    h = x.reshape(batch * seq, d)
    for l in range(depth):
        h, h_packed = _token_mixer(h, mem, positions, w_in[l], w_mem_kv[l], w_gate[l], b_gate[l], w_br_dil[l],
                                   w_br_moba[l], w_br_mem[l], w_out[l], ln1_g[l], ln1_b[l], batch, seq, alpha)
        h = _moe_layer(h, h_packed, w_router[l], router_bias[l], w_e_gate[l], w_e_up[l], w_e_down[l],
                       w_s_gate[l], w_s_up[l], w_s_down[l], ln2_g[l], ln2_b[l], alpha)
    return h.reshape(batch, seq, d)
```

```python
import functools

import jax
import jax.numpy as jnp
from jax import lax
from jax.experimental import pallas as pl
from jax.experimental.pallas import tpu as pltpu
from jax.experimental.pallas import tpu_sc as plsc

F32 = jnp.float32
BF16 = jnp.bfloat16

LANES = 128
VMEM_LIMIT_BYTES = 48 * 1024 * 1024

HEAD_DIM = 64
ROPE_DIM = HEAD_DIM // 4
ROPE_HALF = ROPE_DIM // 2
ROPE_THETA = 500000.0
DIL_PAIRS = ((128, 1), (512, 4), (2048, 16))
BAND = 128
DIL_TILE = 512
W_DIL = 384
W_MOBA = 384
W_MEM = 256
MOBA_BLOCK = 256
MOBA_TOPK = 3
N_EXPERTS = 256
N_GROUPS = 8
GROUP_SIZE = N_EXPERTS // N_GROUPS
TOPK_GROUPS = 4
TOP_K = 8
D_EXPERT = 256
ROUTED_SCALE = 2.5
LN_EPS = 1e-5
NEG = -1e30
QK_SCALE = HEAD_DIM ** -0.5


def _cparams(*sem):
    return pltpu.CompilerParams(dimension_semantics=sem, vmem_limit_bytes=VMEM_LIMIT_BYTES)


def _dot_nt(a, b):
    return lax.dot_general(a, b, (((1,), (1,)), ((), ())), preferred_element_type=F32)


def _layer_norm(v, g, b):
    mu = jnp.mean(v, axis=-1, keepdims=True)
    c = v - mu
    var = jnp.mean(c * c, axis=-1, keepdims=True)
    return c * lax.rsqrt(var + LN_EPS) * g + b


_IN_SECTIONS = (
    (0, W_DIL, True, True),
    (W_DIL, W_DIL, True, False),
    (2 * W_DIL, W_DIL, False, False),
    (3 * W_DIL, W_MOBA, True, True),
    (3 * W_DIL + W_MOBA, W_MOBA, True, False),
    (3 * W_DIL + 2 * W_MOBA, W_MOBA, False, False),
    (3 * W_DIL + 3 * W_MOBA, W_MEM, False, True),
)


N_DIL_SECTIONS = 3


def _inproj_kernel(x_ref, w_ref, cos_ref, sin_ref, *refs):
    n_dil = N_DIL_SECTIONS * len(DIL_PAIRS)
    dil_refs, flat_refs, stage = refs[:n_dil], refs[n_dil:-1], refs[-1]
    xb = x_ref[...].astype(BF16)
    cos = cos_ref[...]
    sin = sin_ref[...]
    tm = xb.shape[0]
    lane = lax.broadcasted_iota(jnp.int32, (tm, LANES), 1)
    first_half = (lane % ROPE_DIM) < ROPE_HALF

    def rope(t):
        partner = jnp.where(first_half, pltpu.roll(t, LANES - ROPE_HALF, 1), pltpu.roll(t, ROPE_HALF, 1))
        return t * cos + partner * sin

    for sec, (off, width, roped, scaled) in enumerate(_IN_SECTIONS):
        acc = jnp.dot(xb, w_ref[:, off:off + width], preferred_element_type=F32)
        for c in range(width // LANES):
            t = acc[:, c * LANES:(c + 1) * LANES]
            if roped:
                t = rope(t)
            if scaled:
                t = t * QK_SCALE
            if sec >= N_DIL_SECTIONS:
                flat_refs[sec - N_DIL_SECTIONS][:, c * LANES:(c + 1) * LANES] = t.astype(BF16)
                continue
            o_ref = dil_refs[sec * len(DIL_PAIRS) + c]
            dilation = DIL_PAIRS[c][1]
            if dilation == 1:
                o_ref[0] = t.astype(BF16)
                continue
            slot = sec * len(DIL_PAIRS) + c
            stage[slot] = t
            for r in range(dilation):
                o_ref[r] = stage[slot, pl.ds(r, tm // dilation, stride=dilation), :].astype(BF16)


def _in_projection(x2, w_in_b, cos_t, sin_t, batch, seq, tm=512):
    n, d = x2.shape
    w_total = w_in_b.shape[1]
    tiles = seq // tm
    dil_specs, dil_shapes = [], []
    for _ in range(N_DIL_SECTIONS):
        for _, dilation in DIL_PAIRS:
            dil_specs.append(pl.BlockSpec((None, dilation, tm // dilation, LANES),
                                          lambda i: (i // tiles, 0, i % tiles, 0)))
            dil_shapes.append(jax.ShapeDtypeStruct((batch, dilation, seq // dilation, LANES), BF16))
    widths = [s[1] for s in _IN_SECTIONS[N_DIL_SECTIONS:]]
    outs = pl.pallas_call(
        _inproj_kernel,
        grid=(n // tm,),
        in_specs=[
            pl.BlockSpec((tm, d), lambda i: (i, 0)),
            pl.BlockSpec((d, w_total), lambda i: (0, 0)),
            pl.BlockSpec((tm, LANES), lambda i: (i, 0)),
            pl.BlockSpec((tm, LANES), lambda i: (i, 0)),
        ],
        out_specs=dil_specs + [pl.BlockSpec((tm, w), lambda i: (i, 0)) for w in widths],
        out_shape=dil_shapes + [jax.ShapeDtypeStruct((n, w), BF16) for w in widths],
        scratch_shapes=[pltpu.VMEM((N_DIL_SECTIONS * len(DIL_PAIRS), tm, LANES), F32)],
        compiler_params=_cparams("parallel"),
        name="in_proj_rope",
    )(x2, w_in_b, cos_t, sin_t)
    n_dil = len(dil_specs)
    ng = len(DIL_PAIRS)
    qkv_dil = [outs[s * ng:(s + 1) * ng] for s in range(N_DIL_SECTIONS)]
    return qkv_dil, outs[n_dil:]


def _rope_tables(positions):
    lane = jnp.arange(LANES)
    rotary = (lane % HEAD_DIM) < ROPE_DIM
    inv_freq = ROPE_THETA ** (-(lane % ROPE_HALF).astype(F32) / ROPE_HALF)
    ang = positions.reshape(-1).astype(F32)[:, None] * jnp.where(rotary, inv_freq, 0.0)
    sign = jnp.where((lane % ROPE_DIM) < ROPE_HALF, -1.0, 1.0)
    return jnp.cos(ang), jnp.sin(ang) * sign


def _head_mask(shape, h):
    lane = lax.broadcasted_iota(jnp.int32, shape, 1)
    return (lane // HEAD_DIM) == h


def _dil_kernel(q_ref, kp_ref, k_ref, vp_ref, v_ref, o_ref, lse_ref):
    i = pl.program_id(2)
    tq = q_ref.shape[0]
    qi = lax.broadcasted_iota(jnp.int32, (BAND, 2 * BAND), 0)
    kj = lax.broadcasted_iota(jnp.int32, (BAND, 2 * BAND), 1)
    dist = qi + BAND - kj
    band = (dist >= 0) & (dist <= BAND)
    first = _head_mask((BAND, LANES), 0)
    for j in range(tq // BAND):
        rows = slice(j * BAND, (j + 1) * BAND)
        q = q_ref[rows, :]
        if j == 0:
            k_prev, v_prev = kp_ref[...], vp_ref[...]
            allowed = band & ((kj >= BAND) | (i > 0))
        else:
            prev_rows = slice((j - 1) * BAND, j * BAND)
            k_prev, v_prev = k_ref[prev_rows, :], v_ref[prev_rows, :]
            allowed = band
        k = jnp.concatenate([k_prev, k_ref[rows, :]], axis=0)
        v = jnp.concatenate([v_prev, v_ref[rows, :]], axis=0)
        outs, lses = [], []
        for h in range(2):
            qh = jnp.where(_head_mask(q.shape, h), q, jnp.zeros_like(q))
            s = jnp.where(allowed, _dot_nt(qh, k), NEG)
            m = jnp.max(s, axis=1, keepdims=True)
            p = jnp.exp(s - m)
            l = jnp.sum(p, axis=1, keepdims=True)
            outs.append(jnp.dot(p.astype(BF16), v, preferred_element_type=F32) / l)
            lses.append(jnp.broadcast_to(m + jnp.log(l), (BAND, LANES)))
        o_ref[rows, :] = jnp.where(first, outs[0], outs[1]).astype(o_ref.dtype)
        lse_ref[rows, :] = jnp.where(first, lses[0], lses[1])


def _dilated_attention(q4, k4, v4):
    batch, dilation, steps, _ = q4.shape
    tq = min(steps, DIL_TILE)
    per_tile = tq // BAND
    cur = pl.BlockSpec((None, None, tq, LANES), lambda b, r, i: (b, r, i, 0))
    prev = pl.BlockSpec((None, None, BAND, LANES), lambda b, r, i: (b, r, jnp.maximum(i * per_tile - 1, 0), 0))
    return pl.pallas_call(
        _dil_kernel,
        grid=(batch, dilation, steps // tq),
        in_specs=[cur, prev, cur, prev, cur],
        out_specs=[cur, cur],
        out_shape=[jax.ShapeDtypeStruct(q4.shape, BF16), jax.ShapeDtypeStruct(q4.shape, F32)],
        compiler_params=_cparams("parallel", "parallel", "arbitrary"),
        name=f"dilated_attn_d{dilation}",
    )(q4, k4, k4, v4, v4)


def _kmean_kernel(k_ref, hi_ref, lo_ref):
    k = k_ref[...].astype(F32)
    s, w = k.shape
    mean = jnp.sum(k.reshape(s // MOBA_BLOCK, MOBA_BLOCK, w), axis=1) / MOBA_BLOCK
    hi = mean.astype(BF16)
    hi_ref[...] = hi
    lo_ref[...] = (mean - hi.astype(F32)).astype(BF16)


def _moba_kmean(km3):
    batch, seq, w = km3.shape
    nb = seq // MOBA_BLOCK
    return pl.pallas_call(
        _kmean_kernel,
        grid=(batch,),
        in_specs=[pl.BlockSpec((None, seq, w), lambda b: (b, 0, 0))],
        out_specs=[pl.BlockSpec((None, nb, w), lambda b: (b, 0, 0))] * 2,
        out_shape=[jax.ShapeDtypeStruct((batch, nb, w), BF16)] * 2,
        compiler_params=_cparams("parallel"),
        name="moba_kmean",
    )(km3)


def _moba_kernel(q_ref, k_ref, v_ref, kmh_ref, kml_ref, o_ref, qaug, kaug, m_scr, l_scr, acc_scr):
    s = pl.program_id(2)
    nb = kmh_ref.shape[0]
    tq = MOBA_BLOCK
    lane = lax.broadcasted_iota(jnp.int32, (tq, LANES), 1)

    @pl.when(s == 0)
    def _():
        for h in range(2):
            spare = HEAD_DIM * (1 - h)
            for j in range(nb):
                onehot = jnp.where(lane == spare + j, 1.0, 0.0).astype(BF16)
                kaug[h, j * tq:(j + 1) * tq, :] = jnp.where(
                    _head_mask((tq, LANES), h), k_ref[j * tq:(j + 1) * tq, :], onehot)

    row = lax.broadcasted_iota(jnp.int32, (tq, tq), 0)
    col = lax.broadcasted_iota(jnp.int32, (tq, tq), 1)
    causal = col <= row
    blk = lax.broadcasted_iota(jnp.int32, (nb, tq), 0)
    starts = []
    for t in range(2):
        qi = s if t == 0 else nb - 1 - s
        start = pl.multiple_of(qi * tq, tq)
        starts.append(start)
        q = q_ref[pl.ds(start, tq), :]
        v_own = v_ref[pl.ds(start, tq), :]
        for h in range(2):
            hm = _head_mask(q.shape, h)
            qh = jnp.where(hm, q, jnp.zeros_like(q))
            gate = _dot_nt(kmh_ref[...], qh) + _dot_nt(kml_ref[...], qh)
            cnt = jnp.zeros((nb, tq), jnp.int32)
            for jp in range(nb):
                g_jp = gate[jp:jp + 1, :]
                beats = (g_jp > gate) | ((g_jp == gate) & (blk > jp))
                cnt = cnt + jnp.where(beats & (qi > jp), 1, 0)
            sel = ((blk < qi) & (cnt < MOBA_TOPK)) | (blk == qi)
            bias_t = jnp.where(sel, 0.0, NEG).astype(F32)
            spare = HEAD_DIM * (1 - h)
            pieces = [jnp.zeros((spare, tq), F32)] if spare else []
            pieces += [bias_t, jnp.zeros((LANES - spare - nb, tq), F32)]
            qa = jnp.where(hm, q, jnp.concatenate(pieces, axis=0).T.astype(BF16))
            qaug[t, h] = qa

            sc = jnp.where(causal, _dot_nt(qa, kaug[h, pl.ds(start, tq), :]), NEG)
            m = jnp.max(sc, axis=1, keepdims=True)
            p = jnp.exp(sc - m)
            m_scr[t, h] = jnp.broadcast_to(m, (tq, LANES))
            l_scr[t, h] = jnp.broadcast_to(jnp.sum(p, axis=1, keepdims=True), (tq, LANES))
            acc_scr[t, h] = jnp.dot(p.astype(BF16), v_own, preferred_element_type=F32)

    for it in range(nb - 1):
        first = it < s
        t = jnp.where(first, 0, 1)
        kstart = pl.multiple_of(jnp.where(first, it, it - s) * tq, tq)
        vj = v_ref[pl.ds(kstart, tq), :]
        for h in range(2):
            sc = _dot_nt(qaug[t, h], kaug[h, pl.ds(kstart, tq), :])
            m_old = m_scr[t, h]
            m_new = jnp.maximum(m_old, jnp.max(sc, axis=1, keepdims=True))
            p = jnp.exp(sc - jnp.concatenate([m_new, m_new], axis=1))
            alpha = jnp.exp(m_old - m_new)
            l_scr[t, h] = alpha * l_scr[t, h] + jnp.sum(p, axis=1, keepdims=True)
            acc_scr[t, h] = alpha * acc_scr[t, h] + jnp.dot(p.astype(BF16), vj, preferred_element_type=F32)
            m_scr[t, h] = m_new

    for t in range(2):
        o = jnp.where(_head_mask((tq, LANES), 0), acc_scr[t, 0] / l_scr[t, 0], acc_scr[t, 1] / l_scr[t, 1])
        o_ref[pl.ds(starts[t], tq), :] = o.astype(o_ref.dtype)


def _moba_attention(qm3, km3, vm3, kmh, kml):
    batch, seq, w = qm3.shape
    nb = seq // MOBA_BLOCK
    assert nb % 2 == 0 and nb <= HEAD_DIM
    npair = w // LANES
    tq = MOBA_BLOCK
    seq_spec = pl.BlockSpec((None, seq, LANES), lambda b, p, i: (b, 0, p))
    km_spec = pl.BlockSpec((None, nb, LANES), lambda b, p, i: (b, 0, p))
    state = pltpu.VMEM((2, 2, tq, LANES), F32)
    return pl.pallas_call(
        _moba_kernel,
        grid=(batch, npair, nb // 2),
        in_specs=[seq_spec, seq_spec, seq_spec, km_spec, km_spec],
        out_specs=seq_spec,
        out_shape=jax.ShapeDtypeStruct((batch, seq, w), BF16),
        scratch_shapes=[pltpu.VMEM((2, 2, tq, LANES), BF16), pltpu.VMEM((2, seq, LANES), BF16),
                        state, state, state],
        compiler_params=_cparams("parallel", "parallel", "arbitrary"),
        name="moba_attn",
    )(qm3, km3, vm3, kmh, kml)


def _memkv_kernel(mem_ref, w_ref, k_ref, v_ref):
    kv = jnp.dot(mem_ref[...].astype(BF16), w_ref[...], preferred_element_type=F32)
    k_ref[...] = kv[:, :W_MEM].astype(BF16)
    v_ref[...] = kv[:, W_MEM:].astype(BF16)


def _mem_kv(mem, w_kv_b):
    batch, m, d = mem.shape
    return pl.pallas_call(
        _memkv_kernel,
        grid=(batch,),
        in_specs=[pl.BlockSpec((None, m, d), lambda b: (b, 0, 0)),
                  pl.BlockSpec((d, 2 * W_MEM), lambda b: (0, 0))],
        out_specs=[pl.BlockSpec((None, m, W_MEM), lambda b: (b, 0, 0))] * 2,
        out_shape=[jax.ShapeDtypeStruct((batch, m, W_MEM), BF16)] * 2,
        compiler_params=_cparams("parallel"),
        name="mem_kv_proj",
    )(mem, w_kv_b)


def _memattn_kernel(q_ref, k_ref, v_ref, o_ref):
    q = q_ref[...]
    k = k_ref[...]
    v = v_ref[...]
    outs = []
    for h in range(2):
        qh = jnp.where(_head_mask(q.shape, h), q, jnp.zeros_like(q))
        s = _dot_nt(qh, k)
        m = jnp.max(s, axis=1, keepdims=True)
        p = jnp.exp(s - m)
        l = jnp.sum(p, axis=1, keepdims=True)
        outs.append(jnp.dot(p.astype(BF16), v, preferred_element_type=F32) / l)
    o_ref[...] = jnp.where(_head_mask(outs[0].shape, 0), outs[0], outs[1]).astype(o_ref.dtype)


def _mem_attention(qx3, k_mem, v_mem, tq=512):
    batch, seq, w = qx3.shape
    m = k_mem.shape[1]
    return pl.pallas_call(
        _memattn_kernel,
        grid=(batch, w // LANES, seq // tq),
        in_specs=[pl.BlockSpec((None, tq, LANES), lambda b, p, i: (b, i, p)),
                  pl.BlockSpec((None, m, LANES), lambda b, p, i: (b, 0, p)),
                  pl.BlockSpec((None, m, LANES), lambda b, p, i: (b, 0, p))],
        out_specs=pl.BlockSpec((None, tq, LANES), lambda b, p, i: (b, i, p)),
        out_shape=jax.ShapeDtypeStruct((batch, seq, w), BF16),
        compiler_params=_cparams("parallel", "parallel", "parallel"),
        name="mem_attn",
    )(qx3, k_mem, v_mem)


def _merge_kernel(alpha, x_ref, o1_ref, o2_ref, o3_ref, l1_ref, l2_ref, l3_ref, ym_ref, yx_ref,
                  wg_ref, bg_ref, wbd_ref, wbm_ref, wbx_ref, wo_ref, g_ref, b_ref, out_ref, packed_ref, stage):
    x = x_ref[...]
    xb = x.astype(BF16)
    tm, d = x.shape

    def token_major(ref, slot):
        dilation = ref.shape[0]
        if dilation == 1:
            return ref[0].astype(F32)
        for r in range(dilation):
            stage[slot, pl.ds(r, tm // dilation, stride=dilation), :] = ref[r].astype(F32)
        return stage[slot]

    o1, o2, o3 = (token_major(r, s) for s, r in enumerate((o1_ref, o2_ref, o3_ref)))
    l1, l2, l3 = (token_major(r, s + 3) for s, r in enumerate((l1_ref, l2_ref, l3_ref)))
    mx = jnp.maximum(jnp.maximum(l1, l2), l3)
    e1, e2, e3 = jnp.exp(l1 - mx), jnp.exp(l2 - mx), jnp.exp(l3 - mx)
    y_dil = (e1 * o1 + e2 * o2 + e3 * o3) / (e1 + e2 + e3)
    branches = (
        jnp.dot(y_dil.astype(BF16), wbd_ref[...], preferred_element_type=F32),
        jnp.dot(ym_ref[...], wbm_ref[...], preferred_element_type=F32),
        jnp.dot(yx_ref[...], wbx_ref[...], preferred_element_type=F32),
    )
    merged = jnp.zeros_like(x)
    for i, br in enumerate(branches):
        logits = jnp.dot(xb, wg_ref[:, i * d:(i + 1) * d], preferred_element_type=F32) + bg_ref[:, i * d:(i + 1) * d]
        merged = merged + jax.nn.sigmoid(logits) * br
    mix = jnp.dot(merged.astype(BF16), wo_ref[...], preferred_element_type=F32)
    x1 = _layer_norm(alpha * x + mix, g_ref[...], b_ref[...])
    out_ref[...] = x1
    packed_ref[...] = _pack_halves(x1)


def _merge(alpha, x2, o_dil, lse_dil, y_moba, y_mem, wg, bg, wbd, wbm, wbx, wo, g, b, seq, tm=512):
    n, d = x2.shape
    tiles = seq // tm
    row = lambda w: pl.BlockSpec((tm, w), lambda i: (i, 0))
    full = lambda a: pl.BlockSpec(a.shape, lambda i: (0, 0))
    residue_major = lambda a: pl.BlockSpec((None, a.shape[1], tm // a.shape[1], LANES),
                                           lambda i: (i // tiles, 0, i % tiles, 0))
    weights = (wg, bg, wbd, wbm, wbx, wo, g, b)
    return pl.pallas_call(
        functools.partial(_merge_kernel, alpha),
        grid=(n // tm,),
        in_specs=([row(d)] + [residue_major(a) for a in (*o_dil, *lse_dil)] + [row(W_MOBA), row(W_MEM)]
                  + [full(a) for a in weights]),
        out_specs=[row(d), row(PACKED)],
        out_shape=[jax.ShapeDtypeStruct((n, d), F32), jax.ShapeDtypeStruct((n, PACKED), jnp.uint32)],
        scratch_shapes=[pltpu.VMEM((2 * len(DIL_PAIRS), tm, LANES), F32)],
        compiler_params=_cparams("parallel"),
        name="merge_outproj_ln1",
    )(x2, *o_dil, *lse_dil, y_moba, y_mem, *weights)


def _token_mixer(x2, mem, positions, w_in, w_mem_kv, w_gate, b_gate, w_br_dil, w_br_moba, w_br_mem,
                 w_out, ln1_g, ln1_b, batch, seq, alpha):
    cos_t, sin_t = _rope_tables(positions)
    (q_dil, k_dil, v_dil), (qm, km, vm, qx) = _in_projection(x2, w_in.astype(BF16), cos_t, sin_t, batch, seq)
    o_dil, lse_dil = [], []
    for g in range(len(DIL_PAIRS)):
        o, lse = _dilated_attention(q_dil[g], k_dil[g], v_dil[g])
        o_dil.append(o)
        lse_dil.append(lse)
    qm3, km3, vm3 = (t.reshape(batch, seq, W_MOBA) for t in (qm, km, vm))
    kmh, kml = _moba_kmean(km3)
    y_moba = _moba_attention(qm3, km3, vm3, kmh, kml).reshape(batch * seq, W_MOBA)
    k_mem, v_mem = _mem_kv(mem, w_mem_kv.astype(BF16))
    y_mem = _mem_attention(qx.reshape(batch, seq, W_MEM), k_mem, v_mem).reshape(batch * seq, W_MEM)
    return _merge(alpha, x2, o_dil, lse_dil, y_moba, y_mem,
                  w_gate.astype(BF16), b_gate.reshape(1, -1), w_br_dil.astype(BF16),
                  w_br_moba.astype(BF16), w_br_mem.astype(BF16), w_out.astype(BF16),
                  ln1_g.reshape(1, -1), ln1_b.reshape(1, -1), seq)


EXPERT_ROWS = 512
EXPERT_SUB = 256
TOKEN_TILE = 256
PACKED = 512
SC_WINDOW = 128


def _first_index_of_max(v, iota_f, size):
    m = jnp.max(v, axis=0, keepdims=True)
    idx = jnp.min(jnp.where(v == m, iota_f, float(size)), axis=0, keepdims=True)
    return m, idx


def _route_kernel(x_ref, wh_ref, wl_ref, bias_ref, eidx_ref, gw_ref, rank_ref, cnt_ref, carry):
    step = pl.program_id(0)

    @pl.when(step == 0)
    def _():
        carry[...] = jnp.zeros_like(carry)

    x = x_ref[...]
    tm = x.shape[0]
    xh = x.astype(BF16)
    xl = (x - xh.astype(F32)).astype(BF16)
    wh = wh_ref[...]
    logits = _dot_nt(wh, xh) + _dot_nt(wh, xl) + _dot_nt(wl_ref[...], xh)
    scores = jax.nn.sigmoid(logits)
    biased = scores + bias_ref[...][:, :1]

    giota = lax.broadcasted_iota(jnp.int32, (GROUP_SIZE, tm), 0).astype(F32)
    group_scores = []
    for g in range(N_GROUPS):
        slab = biased[g * GROUP_SIZE:(g + 1) * GROUP_SIZE, :]
        m1, i1 = _first_index_of_max(slab, giota, GROUP_SIZE)
        m2 = jnp.max(jnp.where(giota == i1, -jnp.inf, slab), axis=0, keepdims=True)
        group_scores.append(m1 + m2)
    gs = jnp.concatenate(group_scores, axis=0)
    gidx = lax.broadcasted_iota(jnp.int32, (N_GROUPS, tm), 0)
    beaten = jnp.zeros((N_GROUPS, tm), jnp.int32)
    for gp in range(N_GROUPS):
        row = gs[gp:gp + 1, :]
        beaten = beaten + jnp.where((row > gs) | ((row == gs) & (gidx > gp)), 1, 0)
    keep = beaten < TOPK_GROUPS
    masked = jnp.concatenate(
        [jnp.where(keep[g:g + 1, :], biased[g * GROUP_SIZE:(g + 1) * GROUP_SIZE, :], -jnp.inf)
         for g in range(N_GROUPS)], axis=0)

    eiota = lax.broadcasted_iota(jnp.int32, (N_EXPERTS, tm), 0).astype(F32)
    idx_rows, gw_rows = [], []
    for _ in range(TOP_K):
        _, idx = _first_index_of_max(masked, eiota, N_EXPERTS)
        hit = eiota == idx
        gw_rows.append(jnp.sum(jnp.where(hit, scores, 0.0), axis=0, keepdims=True))
        masked = jnp.where(hit, -jnp.inf, masked)
        idx_rows.append(idx)
    idx8 = jnp.concatenate(idx_rows, axis=0)
    gw8 = jnp.concatenate(gw_rows, axis=0)
    gw_ref[...] = gw8 / jnp.sum(gw8, axis=0, keepdims=True) * ROUTED_SCALE
    eidx_ref[...] = idx8.astype(jnp.int32)

    onehot = jnp.zeros((N_EXPERTS, tm), F32)
    for k in range(TOP_K):
        onehot = onehot + jnp.where(eiota == idx8[k:k + 1, :], 1.0, 0.0)
    t_row = lax.broadcasted_iota(jnp.int32, (tm, tm), 0)
    t_col = lax.broadcasted_iota(jnp.int32, (tm, tm), 1)
    earlier = jnp.where(t_row < t_col, 1.0, 0.0).astype(BF16)
    prefix = jnp.dot(onehot.astype(BF16), earlier, preferred_element_type=F32)
    base = carry[...]
    prefix = prefix + jnp.concatenate([base] * (tm // LANES), axis=1)
    rank_rows = [jnp.sum(jnp.where(eiota == idx8[k:k + 1, :], prefix, 0.0), axis=0, keepdims=True)
                 for k in range(TOP_K)]
    rank_ref[...] = jnp.concatenate(rank_rows, axis=0).astype(jnp.int32)
    total = base + jnp.sum(onehot, axis=1, keepdims=True)
    carry[...] = total
    cnt_ref[...] = total.astype(jnp.int32)


def _route(x1, wr_hi_t, wr_lo_t, bias_b, tm=512):
    n, d = x1.shape
    tok = lambda dt: jax.ShapeDtypeStruct((TOP_K, n), dt)
    return pl.pallas_call(
        _route_kernel,
        grid=(n // tm,),
        in_specs=[pl.BlockSpec((tm, d), lambda i: (i, 0)),
                  pl.BlockSpec((N_EXPERTS, d), lambda i: (0, 0)),
                  pl.BlockSpec((N_EXPERTS, d), lambda i: (0, 0)),
                  pl.BlockSpec((N_EXPERTS, LANES), lambda i: (0, 0))],
        out_specs=[pl.BlockSpec((TOP_K, tm), lambda i: (0, i)),
                   pl.BlockSpec((TOP_K, tm), lambda i: (0, i)),
                   pl.BlockSpec((TOP_K, tm), lambda i: (0, i)),
                   pl.BlockSpec((N_EXPERTS, LANES), lambda i: (0, 0))],
        out_shape=[tok(jnp.int32), tok(F32), tok(jnp.int32),
                   jax.ShapeDtypeStruct((N_EXPERTS, LANES), jnp.int32)],
        scratch_shapes=[pltpu.VMEM((N_EXPERTS, LANES), F32)],
        compiler_params=_cparams("arbitrary"),
        name="moe_route",
    )(x1, wr_hi_t, wr_lo_t, bias_b)


def _dest_kernel(eidx_ref, rank_ref, start_ref, dest_ref):
    eidx = eidx_ref[...]
    tm = eidx.shape[1]
    eiota = lax.broadcasted_iota(jnp.int32, (N_EXPERTS, tm), 0)
    start = start_ref[...][:, :1]
    rows = [jnp.sum(jnp.where(eiota == eidx[k:k + 1, :], start, 0.0), axis=0, keepdims=True)
            for k in range(TOP_K)]
    dest_ref[...] = jnp.concatenate(rows, axis=0).astype(jnp.int32) + rank_ref[...]


def _dest_rows(eidx, rank, start_b):
    n = eidx.shape[1]
    nt = n // TOKEN_TILE
    return pl.pallas_call(
        _dest_kernel,
        grid=(nt,),
        in_specs=[pl.BlockSpec((TOP_K, TOKEN_TILE), lambda i: (0, i)),
                  pl.BlockSpec((TOP_K, TOKEN_TILE), lambda i: (0, i)),
                  pl.BlockSpec((N_EXPERTS, LANES), lambda i: (0, 0))],
        out_specs=pl.BlockSpec((TOP_K, TOKEN_TILE), lambda i: (0, i)),
        out_shape=jax.ShapeDtypeStruct((TOP_K, n), jnp.int32),
        compiler_params=_cparams("parallel"),
        name="moe_dest_rows",
    )(eidx, rank, start_b)


def _pack_halves(v):
    return pltpu.pack_elementwise([v[:, :PACKED], v[:, PACKED:]], packed_dtype=BF16)


def _unpack_half(p, index):
    return pltpu.unpack_elementwise(p, index=index, packed_dtype=BF16, unpacked_dtype=F32)


def _sc_mesh():
    return plsc.VectorSubcoreMesh(core_axis_name="core", subcore_axis_name="subcore")


def _sc_move_rows(src, src_idx, dst_idx, out_rows, name):
    width = src.shape[1]
    count = src_idx.shape[1]

    @functools.partial(pl.kernel, out_type=jax.ShapeDtypeStruct((out_rows, width), src.dtype),
                       mesh=_sc_mesh(), scratch_types=[pltpu.VMEM((SC_WINDOW, width), src.dtype)], name=name)
    def move(src_hbm, sidx_hbm, didx_hbm, out_hbm, rows_vmem):
        def body(sidx_vmem, didx_vmem):
            pltpu.sync_copy(src_hbm.at[sidx_vmem.at[0]], rows_vmem)
            pltpu.sync_copy(rows_vmem, out_hbm.at[didx_vmem.at[0]])

        idx_spec = pl.BlockSpec((1, SC_WINDOW), index_map=lambda i: (0, i))
        pltpu.emit_pipeline(
            body,
            grid=(count // SC_WINDOW,),
            in_specs=[idx_spec, idx_spec],
            out_specs=[],
            core_axis_name=("core", "subcore"),
            dimension_semantics=(pltpu.PARALLEL,),
        )(sidx_hbm, didx_hbm)

    return move(src, src_idx, dst_idx)


def _expert_kernel(blk_e_ref, valid_ref, nused_ref, xs_ref, wg_ref, wu_ref, wd_ref, ys_ref,
                   wg_b, wu_b, wd_b):
    i = pl.program_id(0)
    active = i < nused_ref[0]
    new_expert = (i == 0) | (blk_e_ref[i] != blk_e_ref[jnp.maximum(i - 1, 0)])

    @pl.when(active & new_expert)
    def _():
        wg_b[...] = wg_ref[...].astype(BF16)
        wu_b[...] = wu_ref[...].astype(BF16)
        wd_b[...] = wd_ref[...].astype(BF16)

    @pl.when(active)
    def _():
        for r0 in range(0, EXPERT_ROWS, EXPERT_SUB):
            p = xs_ref[r0:r0 + EXPERT_SUB, :]
            row = lax.broadcasted_iota(jnp.int32, p.shape, 0) + r0
            p = jnp.where(row < valid_ref[i], p, jnp.zeros_like(p))
            lo = _unpack_half(p, 0).astype(BF16)
            hi = _unpack_half(p, 1).astype(BF16)

            def up(w, lo=lo, hi=hi):
                return (jnp.dot(lo, w[:PACKED, :], preferred_element_type=F32)
                        + jnp.dot(hi, w[PACKED:, :], preferred_element_type=F32))

            hid = (jax.nn.silu(up(wg_b)) * up(wu_b)).astype(BF16)
            y = jnp.dot(hid, wd_b[...], preferred_element_type=F32)
            ys_ref[r0:r0 + EXPERT_SUB, :] = _pack_halves(y)


def _expert_ffn(xs, blk_e, valid, nused, w_e_gate, w_e_up, w_e_down):
    rows = xs.shape[0]
    nblk = rows // EXPERT_ROWS
    d = w_e_gate.shape[1]
    row_map = lambda i, be, va, nu: (jnp.minimum(i, nu[0] - 1), 0)
    w_map = lambda i, be, va, nu: (be[i], 0, 0)
    return pl.pallas_call(
        _expert_kernel,
        grid_spec=pltpu.PrefetchScalarGridSpec(
            num_scalar_prefetch=3,
            grid=(nblk,),
            in_specs=[pl.BlockSpec((EXPERT_ROWS, PACKED), row_map),
                      pl.BlockSpec((None, d, D_EXPERT), w_map),
                      pl.BlockSpec((None, d, D_EXPERT), w_map),
                      pl.BlockSpec((None, D_EXPERT, d), w_map)],
            out_specs=pl.BlockSpec((EXPERT_ROWS, PACKED), row_map),
            scratch_shapes=[pltpu.VMEM((d, D_EXPERT), BF16), pltpu.VMEM((d, D_EXPERT), BF16),
                            pltpu.VMEM((D_EXPERT, d), BF16)],
        ),
        out_shape=jax.ShapeDtypeStruct((rows, PACKED), jnp.uint32),
        compiler_params=_cparams("arbitrary"),
        name="moe_expert_ffn",
    )(blk_e, valid, nused, xs, w_e_gate, w_e_up, w_e_down)


def _combine_kernel(alpha, x_ref, gw_ref, ys_ref, wsg_ref, wsu_ref, wsd_ref, g_ref, b_ref, out_ref):
    x = x_ref[...]
    xb = x.astype(BF16)
    hid = (jax.nn.silu(jnp.dot(xb, wsg_ref[...], preferred_element_type=F32))
           * jnp.dot(xb, wsu_ref[...], preferred_element_type=F32)).astype(BF16)
    shared = jnp.dot(hid, wsd_ref[...], preferred_element_type=F32)
    gw = gw_ref[...]
    lo = jnp.zeros((TOKEN_TILE, PACKED), F32)
    hi = jnp.zeros((TOKEN_TILE, PACKED), F32)
    for k in range(TOP_K):
        p = ys_ref[k]
        w = gw[:, k:k + 1]
        lo = lo + w * _unpack_half(p, 0)
        hi = hi + w * _unpack_half(p, 1)
    routed = jnp.concatenate([lo, hi], axis=1)
    out_ref[...] = _layer_norm(alpha * x + (routed + shared), g_ref[...], b_ref[...])


def _combine(alpha, x1, gw_t, ys_tok, wsg, wsu, wsd, g, b):
    n, d = x1.shape
    full = lambda a: pl.BlockSpec(a.shape, lambda i: (0, 0))
    weights = (wsg, wsu, wsd, g, b)
    return pl.pallas_call(
        functools.partial(_combine_kernel, alpha),
        grid=(n // TOKEN_TILE,),
        in_specs=[pl.BlockSpec((TOKEN_TILE, d), lambda i: (i, 0)),
                  pl.BlockSpec((TOKEN_TILE, TOP_K), lambda i: (i, 0)),
                  pl.BlockSpec((TOP_K, TOKEN_TILE, PACKED), lambda i: (0, i, 0))] + [full(a) for a in weights],
        out_specs=pl.BlockSpec((TOKEN_TILE, d), lambda i: (i, 0)),
        out_shape=jax.ShapeDtypeStruct((n, d), F32),
        compiler_params=_cparams("parallel"),
        name="moe_combine_shared_ln2",
    )(x1, gw_t, ys_tok, *weights)


def _moe_layer(x1, x1_packed, w_router, router_bias, w_e_gate, w_e_up, w_e_down, w_s_gate, w_s_up, w_s_down,
               ln2_g, ln2_b, alpha):
    n, d = x1.shape
    wr_t = w_router.T
    wr_hi = wr_t.astype(BF16)
    wr_lo = (wr_t - wr_hi.astype(F32)).astype(BF16)
    bias_b = jnp.broadcast_to(router_bias.astype(F32)[:, None], (N_EXPERTS, LANES))
    eidx, gw, rank, cnt = _route(x1, wr_hi, wr_lo, bias_b)

    counts = cnt[:, 0]
    padded = (counts + EXPERT_ROWS - 1) // EXPERT_ROWS * EXPERT_ROWS
    seg_end = jnp.cumsum(padded)
    seg_start = seg_end - padded
    rows = n * TOP_K + N_EXPERTS * EXPERT_ROWS
    blk_first_row = jnp.arange(rows // EXPERT_ROWS, dtype=jnp.int32) * EXPERT_ROWS
    blk_e = jnp.minimum(jnp.sum((seg_end[None, :] <= blk_first_row[:, None]).astype(jnp.int32), axis=1),
                        N_EXPERTS - 1)
    valid = jnp.clip(counts[blk_e] - (blk_first_row - seg_start[blk_e]), 0, EXPERT_ROWS).astype(jnp.int32)
    nused = (seg_end[-1:] // EXPERT_ROWS).astype(jnp.int32)
    start_b = jnp.broadcast_to(seg_start.astype(F32)[:, None], (N_EXPERTS, LANES))

    dest_flat = _dest_rows(eidx, rank, start_b).reshape(1, TOP_K * n)
    assign = jnp.arange(TOP_K * n, dtype=jnp.int32).reshape(1, TOP_K * n)
    xs = _sc_move_rows(x1_packed, assign % n, dest_flat, rows, "moe_dispatch_sc")
    ys = _expert_ffn(xs, blk_e, valid, nused, w_e_gate, w_e_up, w_e_down)
    ys_tok = _sc_move_rows(ys, dest_flat, assign, TOP_K * n, "moe_gather_sc").reshape(TOP_K, n, PACKED)
    return _combine(alpha, x1, gw.T, ys_tok, w_s_gate.astype(BF16), w_s_up.astype(BF16),
                    w_s_down.astype(BF16), ln2_g.reshape(1, -1), ln2_b.reshape(1, -1))


def kernel(x, mem, positions, w_in, w_mem_kv, w_gate, b_gate, w_br_dil, w_br_moba, w_br_mem, w_out, ln1_g, ln1_b, w_router, router_bias, w_e_gate, w_e_up, w_e_down, w_s_gate, w_s_up, w_s_down, ln2_g, ln2_b):
    batch, seq, d = x.shape
    depth = w_in.shape[0]
    alpha = (2.0 * depth) ** 0.25
    h = x.reshape(batch * seq, d)
    for l in range(depth):
        h, h_packed = _token_mixer(h, mem, positions, w_in[l], w_mem_kv[l], w_gate[l], b_gate[l], w_br_dil[l],
                                   w_br_moba[l], w_br_mem[l], w_out[l], ln1_g[l], ln1_b[l], batch, seq, alpha)
        h = _moe_layer(h, h_packed, w_router[l], router_bias[l], w_e_gate[l], w_e_up[l], w_e_down[l],
                       w_s_gate[l], w_s_up[l], w_s_down[l], ln2_g[l], ln2_b[l], alpha)
    return h.reshape(batch, seq, d)
```

```python
import functools

import jax
import jax.numpy as jnp
from jax import lax
from jax.experimental import pallas as pl
from jax.experimental.pallas import tpu as pltpu
from jax.experimental.pallas import tpu_sc as plsc

F32 = jnp.float32
BF16 = jnp.bfloat16

LANES = 128
VMEM_LIMIT_BYTES = 48 * 1024 * 1024

HEAD_DIM = 64
ROPE_DIM = HEAD_DIM // 4
ROPE_HALF = ROPE_DIM // 2
ROPE_THETA = 500000.0
DIL_PAIRS = ((128, 1), (512, 4), (2048, 16))
BAND = 128
DIL_TILE = 512
W_DIL = 384
W_MOBA = 384
W_MEM = 256
MOBA_BLOCK = 256
MOBA_TOPK = 3
N_EXPERTS = 256
N_GROUPS = 8
GROUP_SIZE = N_EXPERTS // N_GROUPS
TOPK_GROUPS = 4
TOP_K = 8
D_EXPERT = 256
ROUTED_SCALE = 2.5
LN_EPS = 1e-5
NEG = -1e30
QK_SCALE = HEAD_DIM ** -0.5


def _cparams(*sem):
    return pltpu.CompilerParams(dimension_semantics=sem, vmem_limit_bytes=VMEM_LIMIT_BYTES)


def _dot_nt(a, b):
    return lax.dot_general(a, b, (((1,), (1,)), ((), ())), preferred_element_type=F32)


def _layer_norm(v, g, b):
    mu = jnp.mean(v, axis=-1, keepdims=True)
    c = v - mu
    var = jnp.mean(c * c, axis=-1, keepdims=True)
    return c * lax.rsqrt(var + LN_EPS) * g + b


_IN_SECTIONS = (
    (0, W_DIL, True, True),
    (W_DIL, W_DIL, True, False),
    (2 * W_DIL, W_DIL, False, False),
    (3 * W_DIL, W_MOBA, True, True),
    (3 * W_DIL + W_MOBA, W_MOBA, True, False),
    (3 * W_DIL + 2 * W_MOBA, W_MOBA, False, False),
    (3 * W_DIL + 3 * W_MOBA, W_MEM, False, True),
)


N_DIL_SECTIONS = 3


def _inproj_kernel(x_ref, w_ref, cos_ref, sin_ref, *refs):
    n_dil = N_DIL_SECTIONS * len(DIL_PAIRS)
    dil_refs, flat_refs, stage = refs[:n_dil], refs[n_dil:-1], refs[-1]
    xb = x_ref[...].astype(BF16)
    cos = cos_ref[...]
    sin = sin_ref[...]
    tm = xb.shape[0]
    lane = lax.broadcasted_iota(jnp.int32, (tm, LANES), 1)
    first_half = (lane % ROPE_DIM) < ROPE_HALF

    def rope(t):
        partner = jnp.where(first_half, pltpu.roll(t, LANES - ROPE_HALF, 1), pltpu.roll(t, ROPE_HALF, 1))
        return t * cos + partner * sin

    for sec, (off, width, roped, scaled) in enumerate(_IN_SECTIONS):
        acc = jnp.dot(xb, w_ref[:, off:off + width], preferred_element_type=F32)
        for c in range(width // LANES):
            t = acc[:, c * LANES:(c + 1) * LANES]
            if roped:
                t = rope(t)
            if scaled:
                t = t * QK_SCALE
            if sec >= N_DIL_SECTIONS:
                flat_refs[sec - N_DIL_SECTIONS][:, c * LANES:(c + 1) * LANES] = t.astype(BF16)
                continue
            o_ref = dil_refs[sec * len(DIL_PAIRS) + c]
            dilation = DIL_PAIRS[c][1]
            if dilation == 1:
                o_ref[0] = t.astype(BF16)
                continue
            slot = sec * len(DIL_PAIRS) + c
            stage[slot] = t
            for r in range(dilation):
                o_ref[r] = stage[slot, pl.ds(r, tm // dilation, stride=dilation), :].astype(BF16)


def _in_projection(x2, w_in_b, cos_t, sin_t, batch, seq, tm=512):
    n, d = x2.shape
    w_total = w_in_b.shape[1]
    tiles = seq // tm
    dil_specs, dil_shapes = [], []
    for _ in range(N_DIL_SECTIONS):
        for _, dilation in DIL_PAIRS:
            dil_specs.append(pl.BlockSpec((None, dilation, tm // dilation, LANES),
                                          lambda i: (i // tiles, 0, i % tiles, 0)))
            dil_shapes.append(jax.ShapeDtypeStruct((batch, dilation, seq // dilation, LANES), BF16))
    widths = [s[1] for s in _IN_SECTIONS[N_DIL_SECTIONS:]]
    outs = pl.pallas_call(
        _inproj_kernel,
        grid=(n // tm,),
        in_specs=[
            pl.BlockSpec((tm, d), lambda i: (i, 0)),
            pl.BlockSpec((d, w_total), lambda i: (0, 0)),
            pl.BlockSpec((tm, LANES), lambda i: (i, 0)),
            pl.BlockSpec((tm, LANES), lambda i: (i, 0)),
        ],
        out_specs=dil_specs + [pl.BlockSpec((tm, w), lambda i: (i, 0)) for w in widths],
        out_shape=dil_shapes + [jax.ShapeDtypeStruct((n, w), BF16) for w in widths],
        scratch_shapes=[pltpu.VMEM((N_DIL_SECTIONS * len(DIL_PAIRS), tm, LANES), F32)],
        compiler_params=_cparams("parallel"),
        name="in_proj_rope",
    )(x2, w_in_b, cos_t, sin_t)
    n_dil = len(dil_specs)
    ng = len(DIL_PAIRS)
    qkv_dil = [outs[s * ng:(s + 1) * ng] for s in range(N_DIL_SECTIONS)]
    return qkv_dil, outs[n_dil:]


def _rope_tables(positions):
    lane = jnp.arange(LANES)
    rotary = (lane % HEAD_DIM) < ROPE_DIM
    inv_freq = ROPE_THETA ** (-(lane % ROPE_HALF).astype(F32) / ROPE_HALF)
    ang = positions.reshape(-1).astype(F32)[:, None] * jnp.where(rotary, inv_freq, 0.0)
    sign = jnp.where((lane % ROPE_DIM) < ROPE_HALF, -1.0, 1.0)
    return jnp.cos(ang), jnp.sin(ang) * sign


def _head_mask(shape, h):
    lane = lax.broadcasted_iota(jnp.int32, shape, 1)
    return (lane // HEAD_DIM) == h


def _dil_kernel(q_ref, kp_ref, k_ref, vp_ref, v_ref, o_ref, lse_ref):
    i = pl.program_id(2)
    tq = q_ref.shape[0]
    qi = lax.broadcasted_iota(jnp.int32, (BAND, 2 * BAND), 0)
    kj = lax.broadcasted_iota(jnp.int32, (BAND, 2 * BAND), 1)
    dist = qi + BAND - kj
    band = (dist >= 0) & (dist <= BAND)
    first = _head_mask((BAND, LANES), 0)
    for j in range(tq // BAND):
        rows = slice(j * BAND, (j + 1) * BAND)
        q = q_ref[rows, :]
        if j == 0:
            k_prev, v_prev = kp_ref[...], vp_ref[...]
            allowed = band & ((kj >= BAND) | (i > 0))
        else:
            prev_rows = slice((j - 1) * BAND, j * BAND)
            k_prev, v_prev = k_ref[prev_rows, :], v_ref[prev_rows, :]
            allowed = band
        k = jnp.concatenate([k_prev, k_ref[rows, :]], axis=0)
        v = jnp.concatenate([v_prev, v_ref[rows, :]], axis=0)
        outs, lses = [], []
        for h in range(2):
            qh = jnp.where(_head_mask(q.shape, h), q, jnp.zeros_like(q))
            s = jnp.where(allowed, _dot_nt(qh, k), NEG)
            m = jnp.max(s, axis=1, keepdims=True)
            p = jnp.exp(s - m)
            l = jnp.sum(p, axis=1, keepdims=True)
            outs.append(jnp.dot(p.astype(BF16), v, preferred_element_type=F32) / l)
            lses.append(jnp.broadcast_to(m + jnp.log(l), (BAND, LANES)))
        o_ref[rows, :] = jnp.where(first, outs[0], outs[1]).astype(o_ref.dtype)
        lse_ref[rows, :] = jnp.where(first, lses[0], lses[1])


def _dilated_attention(q4, k4, v4):
    batch, dilation, steps, _ = q4.shape
    tq = min(steps, DIL_TILE)
    per_tile = tq // BAND
    cur = pl.BlockSpec((None, None, tq, LANES), lambda b, r, i: (b, r, i, 0))
    prev = pl.BlockSpec((None, None, BAND, LANES), lambda b, r, i: (b, r, jnp.maximum(i * per_tile - 1, 0), 0))
    return pl.pallas_call(
        _dil_kernel,
        grid=(batch, dilation, steps // tq),
        in_specs=[cur, prev, cur, prev, cur],
        out_specs=[cur, cur],
        out_shape=[jax.ShapeDtypeStruct(q4.shape, BF16), jax.ShapeDtypeStruct(q4.shape, F32)],
        compiler_params=_cparams("parallel", "parallel", "arbitrary"),
        name=f"dilated_attn_d{dilation}",
    )(q4, k4, k4, v4, v4)


def _kmean_kernel(k_ref, hi_ref, lo_ref):
    k = k_ref[...].astype(F32)
    s, w = k.shape
    mean = jnp.sum(k.reshape(s // MOBA_BLOCK, MOBA_BLOCK, w), axis=1) / MOBA_BLOCK
    hi = mean.astype(BF16)
    hi_ref[...] = hi
    lo_ref[...] = (mean - hi.astype(F32)).astype(BF16)


def _moba_kmean(km3):
    batch, seq, w = km3.shape
    nb = seq // MOBA_BLOCK
    return pl.pallas_call(
        _kmean_kernel,
        grid=(batch,),
        in_specs=[pl.BlockSpec((None, seq, w), lambda b: (b, 0, 0))],
        out_specs=[pl.BlockSpec((None, nb, w), lambda b: (b, 0, 0))] * 2,
        out_shape=[jax.ShapeDtypeStruct((batch, nb, w), BF16)] * 2,
        compiler_params=_cparams("parallel"),
        name="moba_kmean",
    )(km3)


def _moba_kernel(q_ref, k_ref, v_ref, kmh_ref, kml_ref, o_ref, qaug, kaug, m_scr, l_scr, acc_scr):
    s = pl.program_id(2)
    nb = kmh_ref.shape[0]
    tq = MOBA_BLOCK
    lane = lax.broadcasted_iota(jnp.int32, (tq, LANES), 1)

    @pl.when(s == 0)
    def _():
        for h in range(2):
            spare = HEAD_DIM * (1 - h)
            for j in range(nb):
                onehot = jnp.where(lane == spare + j, 1.0, 0.0).astype(BF16)
                kaug[h, j * tq:(j + 1) * tq, :] = jnp.where(
                    _head_mask((tq, LANES), h), k_ref[j * tq:(j + 1) * tq, :], onehot)

    row = lax.broadcasted_iota(jnp.int32, (tq, tq), 0)
    col = lax.broadcasted_iota(jnp.int32, (tq, tq), 1)
    causal = col <= row
    blk = lax.broadcasted_iota(jnp.int32, (nb, tq), 0)
    starts = []
    for t in range(2):
        qi = s if t == 0 else nb - 1 - s
        start = pl.multiple_of(qi * tq, tq)
        starts.append(start)
        q = q_ref[pl.ds(start, tq), :]
        v_own = v_ref[pl.ds(start, tq), :]
        for h in range(2):
            hm = _head_mask(q.shape, h)
            qh = jnp.where(hm, q, jnp.zeros_like(q))
            gate = _dot_nt(kmh_ref[...], qh) + _dot_nt(kml_ref[...], qh)
            cnt = jnp.zeros((nb, tq), jnp.int32)
            for jp in range(nb):
                g_jp = gate[jp:jp + 1, :]
                beats = (g_jp > gate) | ((g_jp == gate) & (blk > jp))
                cnt = cnt + jnp.where(beats & (qi > jp), 1, 0)
            sel = ((blk < qi) & (cnt < MOBA_TOPK)) | (blk == qi)
            bias_t = jnp.where(sel, 0.0, NEG).astype(F32)
            spare = HEAD_DIM * (1 - h)
            pieces = [jnp.zeros((spare, tq), F32)] if spare else []
            pieces += [bias_t, jnp.zeros((LANES - spare - nb, tq), F32)]
            qa = jnp.where(hm, q, jnp.concatenate(pieces, axis=0).T.astype(BF16))
            qaug[t, h] = qa

            sc = jnp.where(causal, _dot_nt(qa, kaug[h, pl.ds(start, tq), :]), NEG)
            m = jnp.max(sc, axis=1, keepdims=True)
            p = jnp.exp(sc - m)
            m_scr[t, h] = jnp.broadcast_to(m, (tq, LANES))
            l_scr[t, h] = jnp.broadcast_to(jnp.sum(p, axis=1, keepdims=True), (tq, LANES))
            acc_scr[t, h] = jnp.dot(p.astype(BF16), v_own, preferred_element_type=F32)

    for it in range(nb - 1):
        first = it < s
        t = jnp.where(first, 0, 1)
        kstart = pl.multiple_of(jnp.where(first, it, it - s) * tq, tq)
        vj = v_ref[pl.ds(kstart, tq), :]
        for h in range(2):
            sc = _dot_nt(qaug[t, h], kaug[h, pl.ds(kstart, tq), :])
            m_old = m_scr[t, h]
            m_new = jnp.maximum(m_old, jnp.max(sc, axis=1, keepdims=True))
            p = jnp.exp(sc - jnp.concatenate([m_new, m_new], axis=1))
            alpha = jnp.exp(m_old - m_new)
            l_scr[t, h] = alpha * l_scr[t, h] + jnp.sum(p, axis=1, keepdims=True)
            acc_scr[t, h] = alpha * acc_scr[t, h] + jnp.dot(p.astype(BF16), vj, preferred_element_type=F32)
            m_scr[t, h] = m_new

    for t in range(2):
        o = jnp.where(_head_mask((tq, LANES), 0), acc_scr[t, 0] / l_scr[t, 0], acc_scr[t, 1] / l_scr[t, 1])
        o_ref[pl.ds(starts[t], tq), :] = o.astype(o_ref.dtype)


def _moba_attention(qm3, km3, vm3, kmh, kml):
    batch, seq, w = qm3.shape
    nb = seq // MOBA_BLOCK
    assert nb % 2 == 0 and nb <= HEAD_DIM
    npair = w // LANES
    tq = MOBA_BLOCK
    seq_spec = pl.BlockSpec((None, seq, LANES), lambda b, p, i: (b, 0, p))
    km_spec = pl.BlockSpec((None, nb, LANES), lambda b, p, i: (b, 0, p))
    state = pltpu.VMEM((2, 2, tq, LANES), F32)
    return pl.pallas_call(
        _moba_kernel,
        grid=(batch, npair, nb // 2),
        in_specs=[seq_spec, seq_spec, seq_spec, km_spec, km_spec],
        out_specs=seq_spec,
        out_shape=jax.ShapeDtypeStruct((batch, seq, w), BF16),
        scratch_shapes=[pltpu.VMEM((2, 2, tq, LANES), BF16), pltpu.VMEM((2, seq, LANES), BF16),
                        state, state, state],
        compiler_params=_cparams("parallel", "parallel", "arbitrary"),
        name="moba_attn",
    )(qm3, km3, vm3, kmh, kml)


def _memkv_kernel(mem_ref, w_ref, k_ref, v_ref):
    kv = jnp.dot(mem_ref[...].astype(BF16), w_ref[...], preferred_element_type=F32)
    k_ref[...] = kv[:, :W_MEM].astype(BF16)
    v_ref[...] = kv[:, W_MEM:].astype(BF16)


def _mem_kv(mem, w_kv_b):
    batch, m, d = mem.shape
    return pl.pallas_call(
        _memkv_kernel,
        grid=(batch,),
        in_specs=[pl.BlockSpec((None, m, d), lambda b: (b, 0, 0)),
                  pl.BlockSpec((d, 2 * W_MEM), lambda b: (0, 0))],
        out_specs=[pl.BlockSpec((None, m, W_MEM), lambda b: (b, 0, 0))] * 2,
        out_shape=[jax.ShapeDtypeStruct((batch, m, W_MEM), BF16)] * 2,
        compiler_params=_cparams("parallel"),
        name="mem_kv_proj",
    )(mem, w_kv_b)


def _memattn_kernel(q_ref, k_ref, v_ref, o_ref):
    q = q_ref[...]
    k = k_ref[...]
    v = v_ref[...]
    outs = []
    for h in range(2):
        qh = jnp.where(_head_mask(q.shape, h), q, jnp.zeros_like(q))
        s = _dot_nt(qh, k)
        m = jnp.max(s, axis=1, keepdims=True)
        p = jnp.exp(s - m)
        l = jnp.sum(p, axis=1, keepdims=True)
        outs.append(jnp.dot(p.astype(BF16), v, preferred_element_type=F32) / l)
    o_ref[...] = jnp.where(_head_mask(outs[0].shape, 0), outs[0], outs[1]).astype(o_ref.dtype)


def _mem_attention(qx3, k_mem, v_mem, tq=512):
    batch, seq, w = qx3.shape
    m = k_mem.shape[1]
    return pl.pallas_call(
        _memattn_kernel,
        grid=(batch, w // LANES, seq // tq),
        in_specs=[pl.BlockSpec((None, tq, LANES), lambda b, p, i: (b, i, p)),
                  pl.BlockSpec((None, m, LANES), lambda b, p, i: (b, 0, p)),
                  pl.BlockSpec((None, m, LANES), lambda b, p, i: (b, 0, p))],
        out_specs=pl.BlockSpec((None, tq, LANES), lambda b, p, i: (b, i, p)),
        out_shape=jax.ShapeDtypeStruct((batch, seq, w), BF16),
        compiler_params=_cparams("parallel", "parallel", "parallel"),
        name="mem_attn",
    )(qx3, k_mem, v_mem)


def _merge_kernel(alpha, x_ref, o1_ref, o2_ref, o3_ref, l1_ref, l2_ref, l3_ref, ym_ref, yx_ref,
                  wg_ref, bg_ref, wbd_ref, wbm_ref, wbx_ref, wo_ref, g_ref, b_ref, out_ref, packed_ref, stage):
    x = x_ref[...]
    xb = x.astype(BF16)
    tm, d = x.shape

    def token_major(ref, slot):
        dilation = ref.shape[0]
        if dilation == 1:
            return ref[0].astype(F32)
        for r in range(dilation):
            stage[slot, pl.ds(r, tm // dilation, stride=dilation), :] = ref[r].astype(F32)
        return stage[slot]

    o1, o2, o3 = (token_major(r, s) for s, r in enumerate((o1_ref, o2_ref, o3_ref)))
    l1, l2, l3 = (token_major(r, s + 3) for s, r in enumerate((l1_ref, l2_ref, l3_ref)))
    mx = jnp.maximum(jnp.maximum(l1, l2), l3)
    e1, e2, e3 = jnp.exp(l1 - mx), jnp.exp(l2 - mx), jnp.exp(l3 - mx)
    y_dil = (e1 * o1 + e2 * o2 + e3 * o3) / (e1 + e2 + e3)
    branches = (
        jnp.dot(y_dil.astype(BF16), wbd_ref[...], preferred_element_type=F32),
        jnp.dot(ym_ref[...], wbm_ref[...], preferred_element_type=F32),
        jnp.dot(yx_ref[...], wbx_ref[...], preferred_element_type=F32),
    )
    merged = jnp.zeros_like(x)
    for i, br in enumerate(branches):
        logits = jnp.dot(xb, wg_ref[:, i * d:(i + 1) * d], preferred_element_type=F32) + bg_ref[:, i * d:(i + 1) * d]
        merged = merged + jax.nn.sigmoid(logits) * br
    mix = jnp.dot(merged.astype(BF16), wo_ref[...], preferred_element_type=F32)
    x1 = _layer_norm(alpha * x + mix, g_ref[...], b_ref[...])
    out_ref[...] = x1
    packed_ref[...] = _pack_halves(x1)


def _merge(alpha, x2, o_dil, lse_dil, y_moba, y_mem, wg, bg, wbd, wbm, wbx, wo, g, b, seq, tm=512):
    n, d = x2.shape
    tiles = seq // tm
    row = lambda w: pl.BlockSpec((tm, w), lambda i: (i, 0))
    full = lambda a: pl.BlockSpec(a.shape, lambda i: (0, 0))
    residue_major = lambda a: pl.BlockSpec((None, a.shape[1], tm // a.shape[1], LANES),
                                           lambda i: (i // tiles, 0, i % tiles, 0))
    weights = (wg, bg, wbd, wbm, wbx, wo, g, b)
    return pl.pallas_call(
        functools.partial(_merge_kernel, alpha),
        grid=(n // tm,),
        in_specs=([row(d)] + [residue_major(a) for a in (*o_dil, *lse_dil)] + [row(W_MOBA), row(W_MEM)]
                  + [full(a) for a in weights]),
        out_specs=[row(d), row(PACKED)],
        out_shape=[jax.ShapeDtypeStruct((n, d), F32), jax.ShapeDtypeStruct((n, PACKED), jnp.uint32)],
        scratch_shapes=[pltpu.VMEM((2 * len(DIL_PAIRS), tm, LANES), F32)],
        compiler_params=_cparams("parallel"),
        name="merge_outproj_ln1",
    )(x2, *o_dil, *lse_dil, y_moba, y_mem, *weights)


def _token_mixer(x2, mem, positions, w_in, w_mem_kv, w_gate, b_gate, w_br_dil, w_br_moba, w_br_mem,
                 w_out, ln1_g, ln1_b, batch, seq, alpha):
    cos_t, sin_t = _rope_tables(positions)
    (q_dil, k_dil, v_dil), (qm, km, vm, qx) = _in_projection(x2, w_in.astype(BF16), cos_t, sin_t, batch, seq)
    o_dil, lse_dil = [], []
    for g in range(len(DIL_PAIRS)):
        o, lse = _dilated_attention(q_dil[g], k_dil[g], v_dil[g])
        o_dil.append(o)
        lse_dil.append(lse)
    qm3, km3, vm3 = (t.reshape(batch, seq, W_MOBA) for t in (qm, km, vm))
    kmh, kml = _moba_kmean(km3)
    y_moba = _moba_attention(qm3, km3, vm3, kmh, kml).reshape(batch * seq, W_MOBA)
    k_mem, v_mem = _mem_kv(mem, w_mem_kv.astype(BF16))
    y_mem = _mem_attention(qx.reshape(batch, seq, W_MEM), k_mem, v_mem).reshape(batch * seq, W_MEM)
    return _merge(alpha, x2, o_dil, lse_dil, y_moba, y_mem,
                  w_gate.astype(BF16), b_gate.reshape(1, -1), w_br_dil.astype(BF16),
                  w_br_moba.astype(BF16), w_br_mem.astype(BF16), w_out.astype(BF16),
                  ln1_g.reshape(1, -1), ln1_b.reshape(1, -1), seq)


EXPERT_ROWS = 256
EXPERT_RING = 4
TOKEN_TILE = 256
PACKED = 512
SC_WINDOW = 128


def _first_index_of_max(v, iota_f, size):
    m = jnp.max(v, axis=0, keepdims=True)
    idx = jnp.min(jnp.where(v == m, iota_f, float(size)), axis=0, keepdims=True)
    return m, idx


def _route_kernel(x_ref, wh_ref, wl_ref, bias_ref, eidx_ref, gw_ref, rank_ref, cnt_ref, carry):
    step = pl.program_id(0)

    @pl.when(step == 0)
    def _():
        carry[...] = jnp.zeros_like(carry)

    x = x_ref[...]
    tm = x.shape[0]
    xh = x.astype(BF16)
    xl = (x - xh.astype(F32)).astype(BF16)
    wh = wh_ref[...]
    logits = _dot_nt(wh, xh) + _dot_nt(wh, xl) + _dot_nt(wl_ref[...], xh)
    scores = jax.nn.sigmoid(logits)
    biased = scores + bias_ref[...][:, :1]

    giota = lax.broadcasted_iota(jnp.int32, (GROUP_SIZE, tm), 0).astype(F32)
    group_scores = []
    for g in range(N_GROUPS):
        slab = biased[g * GROUP_SIZE:(g + 1) * GROUP_SIZE, :]
        m1, i1 = _first_index_of_max(slab, giota, GROUP_SIZE)
        m2 = jnp.max(jnp.where(giota == i1, -jnp.inf, slab), axis=0, keepdims=True)
        group_scores.append(m1 + m2)
    gs = jnp.concatenate(group_scores, axis=0)
    gidx = lax.broadcasted_iota(jnp.int32, (N_GROUPS, tm), 0)
    beaten = jnp.zeros((N_GROUPS, tm), jnp.int32)
    for gp in range(N_GROUPS):
        row = gs[gp:gp + 1, :]
        beaten = beaten + jnp.where((row > gs) | ((row == gs) & (gidx > gp)), 1, 0)
    keep = beaten < TOPK_GROUPS
    masked = jnp.concatenate(
        [jnp.where(keep[g:g + 1, :], biased[g * GROUP_SIZE:(g + 1) * GROUP_SIZE, :], -jnp.inf)
         for g in range(N_GROUPS)], axis=0)

    eiota = lax.broadcasted_iota(jnp.int32, (N_EXPERTS, tm), 0).astype(F32)
    idx_rows, gw_rows = [], []
    for _ in range(TOP_K):
        _, idx = _first_index_of_max(masked, eiota, N_EXPERTS)
        hit = eiota == idx
        gw_rows.append(jnp.sum(jnp.where(hit, scores, 0.0), axis=0, keepdims=True))
        masked = jnp.where(hit, -jnp.inf, masked)
        idx_rows.append(idx)
    idx8 = jnp.concatenate(idx_rows, axis=0)
    gw8 = jnp.concatenate(gw_rows, axis=0)
    gw_ref[...] = gw8 / jnp.sum(gw8, axis=0, keepdims=True) * ROUTED_SCALE
    eidx_ref[...] = idx8.astype(jnp.int32)

    onehot = jnp.zeros((N_EXPERTS, tm), F32)
    for k in range(TOP_K):
        onehot = onehot + jnp.where(eiota == idx8[k:k + 1, :], 1.0, 0.0)
    t_row = lax.broadcasted_iota(jnp.int32, (tm, tm), 0)
    t_col = lax.broadcasted_iota(jnp.int32, (tm, tm), 1)
    earlier = jnp.where(t_row < t_col, 1.0, 0.0).astype(BF16)
    prefix = jnp.dot(onehot.astype(BF16), earlier, preferred_element_type=F32)
    base = carry[...]
    prefix = prefix + jnp.concatenate([base] * (tm // LANES), axis=1)
    rank_rows = [jnp.sum(jnp.where(eiota == idx8[k:k + 1, :], prefix, 0.0), axis=0, keepdims=True)
                 for k in range(TOP_K)]
    rank_ref[...] = jnp.concatenate(rank_rows, axis=0).astype(jnp.int32)
    total = base + jnp.sum(onehot, axis=1, keepdims=True)
    carry[...] = total
    cnt_ref[...] = total.astype(jnp.int32)


def _route(x1, wr_hi_t, wr_lo_t, bias_b, tm=512):
    n, d = x1.shape
    tok = lambda dt: jax.ShapeDtypeStruct((TOP_K, n), dt)
    return pl.pallas_call(
        _route_kernel,
        grid=(n // tm,),
        in_specs=[pl.BlockSpec((tm, d), lambda i: (i, 0)),
                  pl.BlockSpec((N_EXPERTS, d), lambda i: (0, 0)),
                  pl.BlockSpec((N_EXPERTS, d), lambda i: (0, 0)),
                  pl.BlockSpec((N_EXPERTS, LANES), lambda i: (0, 0))],
        out_specs=[pl.BlockSpec((TOP_K, tm), lambda i: (0, i)),
                   pl.BlockSpec((TOP_K, tm), lambda i: (0, i)),
                   pl.BlockSpec((TOP_K, tm), lambda i: (0, i)),
                   pl.BlockSpec((N_EXPERTS, LANES), lambda i: (0, 0))],
        out_shape=[tok(jnp.int32), tok(F32), tok(jnp.int32),
                   jax.ShapeDtypeStruct((N_EXPERTS, LANES), jnp.int32)],
        scratch_shapes=[pltpu.VMEM((N_EXPERTS, LANES), F32)],
        compiler_params=_cparams("arbitrary"),
        name="moe_route",
    )(x1, wr_hi_t, wr_lo_t, bias_b)


def _dest_kernel(eidx_ref, rank_ref, start_ref, dest_ref):
    eidx = eidx_ref[...]
    tm = eidx.shape[1]
    eiota = lax.broadcasted_iota(jnp.int32, (N_EXPERTS, tm), 0)
    start = start_ref[...][:, :1]
    rows = [jnp.sum(jnp.where(eiota == eidx[k:k + 1, :], start, 0.0), axis=0, keepdims=True)
            for k in range(TOP_K)]
    dest_ref[...] = jnp.concatenate(rows, axis=0).astype(jnp.int32) + rank_ref[...]


def _dest_rows(eidx, rank, start_b):
    n = eidx.shape[1]
    nt = n // TOKEN_TILE
    return pl.pallas_call(
        _dest_kernel,
        grid=(nt,),
        in_specs=[pl.BlockSpec((TOP_K, TOKEN_TILE), lambda i: (0, i)),
                  pl.BlockSpec((TOP_K, TOKEN_TILE), lambda i: (0, i)),
                  pl.BlockSpec((N_EXPERTS, LANES), lambda i: (0, 0))],
        out_specs=pl.BlockSpec((TOP_K, TOKEN_TILE), lambda i: (0, i)),
        out_shape=jax.ShapeDtypeStruct((TOP_K, n), jnp.int32),
        compiler_params=_cparams("parallel"),
        name="moe_dest_rows",
    )(eidx, rank, start_b)


def _pack_halves(v):
    return pltpu.pack_elementwise([v[:, :PACKED], v[:, PACKED:]], packed_dtype=BF16)


def _unpack_half(p, index):
    return pltpu.unpack_elementwise(p, index=index, packed_dtype=BF16, unpacked_dtype=F32)


def _sc_mesh():
    return plsc.VectorSubcoreMesh(core_axis_name="core", subcore_axis_name="subcore")


def _sc_move_rows(src, src_idx, dst_idx, out_rows, name):
    width = src.shape[1]
    count = src_idx.shape[1]

    @functools.partial(pl.kernel, out_type=jax.ShapeDtypeStruct((out_rows, width), src.dtype),
                       mesh=_sc_mesh(), scratch_types=[pltpu.VMEM((SC_WINDOW, width), src.dtype)], name=name)
    def move(src_hbm, sidx_hbm, didx_hbm, out_hbm, rows_vmem):
        def body(sidx_vmem, didx_vmem):
            pltpu.sync_copy(src_hbm.at[sidx_vmem.at[0]], rows_vmem)
            pltpu.sync_copy(rows_vmem, out_hbm.at[didx_vmem.at[0]])

        idx_spec = pl.BlockSpec((1, SC_WINDOW), index_map=lambda i: (0, i))
        pltpu.emit_pipeline(
            body,
            grid=(count // SC_WINDOW,),
            in_specs=[idx_spec, idx_spec],
            out_specs=[],
            core_axis_name=("core", "subcore"),
            dimension_semantics=(pltpu.PARALLEL,),
        )(sidx_hbm, didx_hbm)

    return move(src, src_idx, dst_idx)


def _expert_kernel(first_blk_ref, count_ref, total_ref, xs_hbm, wg_ref, wu_ref, wd_ref, ys_hbm,
                   wg_b, wu_b, wd_b, x_ring, y_ring, in_sem, out_sem):
    e = pl.program_id(0)
    total = total_ref[0]
    first = first_blk_ref[e]
    count = count_ref[e]
    nblk = (count + EXPERT_ROWS - 1) // EXPERT_ROWS

    def in_copy(g):
        slot = g % EXPERT_RING
        return pltpu.make_async_copy(xs_hbm.at[pl.ds(pl.multiple_of(g * EXPERT_ROWS, EXPERT_ROWS), EXPERT_ROWS)],
                                     x_ring.at[slot], in_sem.at[slot])

    def out_copy(g):
        slot = g % EXPERT_RING
        return pltpu.make_async_copy(y_ring.at[slot],
                                     ys_hbm.at[pl.ds(pl.multiple_of(g * EXPERT_ROWS, EXPERT_ROWS), EXPERT_ROWS)],
                                     out_sem.at[slot])

    @pl.when(e == 0)
    def _():
        for g in range(EXPERT_RING - 1):
            @pl.when(g < total)
            def _(g=g):
                in_copy(g).start()

    @pl.when(nblk > 0)
    def _():
        wg_b[...] = wg_ref[...].astype(BF16)
        wu_b[...] = wu_ref[...].astype(BF16)
        wd_b[...] = wd_ref[...].astype(BF16)

    def block(j, carry):
        g = first + j
        slot = g % EXPERT_RING
        in_copy(g).wait()

        @pl.when(g + EXPERT_RING - 1 < total)
        def _():
            in_copy(g + EXPERT_RING - 1).start()

        p = x_ring[slot]
        row = lax.broadcasted_iota(jnp.int32, p.shape, 0) + j * EXPERT_ROWS
        p = jnp.where(row < count, p, jnp.zeros_like(p))
        lo = _unpack_half(p, 0).astype(BF16)
        hi = _unpack_half(p, 1).astype(BF16)

        def up(w):
            return (jnp.dot(lo, w[:PACKED, :], preferred_element_type=F32)
                    + jnp.dot(hi, w[PACKED:, :], preferred_element_type=F32))

        hid = (jax.nn.silu(up(wg_b)) * up(wu_b)).astype(BF16)
        y = jnp.dot(hid, wd_b[...], preferred_element_type=F32)

        @pl.when(g >= EXPERT_RING)
        def _():
            out_copy(g - EXPERT_RING).wait()
        y_ring[slot] = _pack_halves(y)
        out_copy(g).start()
        return carry

    lax.fori_loop(0, nblk, block, 0)

    @pl.when(e == pl.num_programs(0) - 1)
    def _():
        def drain(g, carry):
            out_copy(g).wait()
            return carry

        lax.fori_loop(jnp.maximum(total - EXPERT_RING, 0), total, drain, 0)


def _expert_ffn(xs, first_blk, counts, total, w_e_gate, w_e_up, w_e_down):
    rows = xs.shape[0]
    n_experts, d = w_e_gate.shape[0], w_e_gate.shape[1]
    w_map = lambda e, fb, ct, tt: (e, 0, 0)
    ring = pltpu.VMEM((EXPERT_RING, EXPERT_ROWS, PACKED), jnp.uint32)
    return pl.pallas_call(
        _expert_kernel,
        grid_spec=pltpu.PrefetchScalarGridSpec(
            num_scalar_prefetch=3,
            grid=(n_experts,),
            in_specs=[pl.BlockSpec(memory_space=pl.ANY),
                      pl.BlockSpec((None, d, D_EXPERT), w_map),
                      pl.BlockSpec((None, d, D_EXPERT), w_map),
                      pl.BlockSpec((None, D_EXPERT, d), w_map)],
            out_specs=pl.BlockSpec(memory_space=pl.ANY),
            scratch_shapes=[pltpu.VMEM((d, D_EXPERT), BF16), pltpu.VMEM((d, D_EXPERT), BF16),
                            pltpu.VMEM((D_EXPERT, d), BF16), ring, ring,
                            pltpu.SemaphoreType.DMA((EXPERT_RING,)), pltpu.SemaphoreType.DMA((EXPERT_RING,))],
        ),
        out_shape=jax.ShapeDtypeStruct((rows, PACKED), jnp.uint32),
        compiler_params=_cparams("arbitrary"),
        name="moe_expert_ffn",
    )(first_blk, counts, total, xs, w_e_gate, w_e_up, w_e_down)


def _combine_kernel(alpha, x_ref, gw_ref, ys_ref, wsg_ref, wsu_ref, wsd_ref, g_ref, b_ref, out_ref):
    x = x_ref[...]
    xb = x.astype(BF16)
    hid = (jax.nn.silu(jnp.dot(xb, wsg_ref[...], preferred_element_type=F32))
           * jnp.dot(xb, wsu_ref[...], preferred_element_type=F32)).astype(BF16)
    shared = jnp.dot(hid, wsd_ref[...], preferred_element_type=F32)
    gw = gw_ref[...]
    lo = jnp.zeros((TOKEN_TILE, PACKED), F32)
    hi = jnp.zeros((TOKEN_TILE, PACKED), F32)
    for k in range(TOP_K):
        p = ys_ref[k]
        w = gw[:, k:k + 1]
        lo = lo + w * _unpack_half(p, 0)
        hi = hi + w * _unpack_half(p, 1)
    routed = jnp.concatenate([lo, hi], axis=1)
    out_ref[...] = _layer_norm(alpha * x + (routed + shared), g_ref[...], b_ref[...])


def _combine(alpha, x1, gw_t, ys_tok, wsg, wsu, wsd, g, b):
    n, d = x1.shape
    full = lambda a: pl.BlockSpec(a.shape, lambda i: (0, 0))
    weights = (wsg, wsu, wsd, g, b)
    return pl.pallas_call(
        functools.partial(_combine_kernel, alpha),
        grid=(n // TOKEN_TILE,),
        in_specs=[pl.BlockSpec((TOKEN_TILE, d), lambda i: (i, 0)),
                  pl.BlockSpec((TOKEN_TILE, TOP_K), lambda i: (i, 0)),
                  pl.BlockSpec((TOP_K, TOKEN_TILE, PACKED), lambda i: (0, i, 0))] + [full(a) for a in weights],
        out_specs=pl.BlockSpec((TOKEN_TILE, d), lambda i: (i, 0)),
        out_shape=jax.ShapeDtypeStruct((n, d), F32),
        compiler_params=_cparams("parallel"),
        name="moe_combine_shared_ln2",
    )(x1, gw_t, ys_tok, *weights)


def _moe_layer(x1, x1_packed, w_router, router_bias, w_e_gate, w_e_up, w_e_down, w_s_gate, w_s_up, w_s_down,
               ln2_g, ln2_b, alpha):
    n, d = x1.shape
    wr_t = w_router.T
    wr_hi = wr_t.astype(BF16)
    wr_lo = (wr_t - wr_hi.astype(F32)).astype(BF16)
    bias_b = jnp.broadcast_to(router_bias.astype(F32)[:, None], (N_EXPERTS, LANES))
    eidx, gw, rank, cnt = _route(x1, wr_hi, wr_lo, bias_b)

    counts = cnt[:, 0]
    padded = (counts + EXPERT_ROWS - 1) // EXPERT_ROWS * EXPERT_ROWS
    seg_end = jnp.cumsum(padded)
    seg_start = seg_end - padded
    rows = n * TOP_K + N_EXPERTS * EXPERT_ROWS
    first_blk = (seg_start // EXPERT_ROWS).astype(jnp.int32)
    total_blk = (seg_end[-1:] // EXPERT_ROWS).astype(jnp.int32)
    start_b = jnp.broadcast_to(seg_start.astype(F32)[:, None], (N_EXPERTS, LANES))

    dest_flat = _dest_rows(eidx, rank, start_b).reshape(1, TOP_K * n)
    assign = jnp.arange(TOP_K * n, dtype=jnp.int32).reshape(1, TOP_K * n)
    xs = _sc_move_rows(x1_packed, assign % n, dest_flat, rows, "moe_dispatch_sc")
    ys = _expert_ffn(xs, first_blk, counts, total_blk, w_e_gate, w_e_up, w_e_down)
    ys_tok = _sc_move_rows(ys, dest_flat, assign, TOP_K * n, "moe_gather_sc").reshape(TOP_K, n, PACKED)
    return _combine(alpha, x1, gw.T, ys_tok, w_s_gate.astype(BF16), w_s_up.astype(BF16),
                    w_s_down.astype(BF16), ln2_g.reshape(1, -1), ln2_b.reshape(1, -1))


def kernel(x, mem, positions, w_in, w_mem_kv, w_gate, b_gate, w_br_dil, w_br_moba, w_br_mem, w_out, ln1_g, ln1_b, w_router, router_bias, w_e_gate, w_e_up, w_e_down, w_s_gate, w_s_up, w_s_down, ln2_g, ln2_b):
    batch, seq, d = x.shape
    depth = w_in.shape[0]
    alpha = (2.0 * depth) ** 0.25
    h = x.reshape(batch * seq, d)
    for l in range(depth):
        h, h_packed = _token_mixer(h, mem, positions, w_in[l], w_mem_kv[l], w_gate[l], b_gate[l], w_br_dil[l],
                                   w_br_moba[l], w_br_mem[l], w_out[l], ln1_g[l], ln1_b[l], batch, seq, alpha)
        h = _moe_layer(h, h_packed, w_router[l], router_bias[l], w_e_gate[l], w_e_up[l], w_e_down[l],
                       w_s_gate[l], w_s_up[l], w_s_down[l], ln2_g[l], ln2_b[l], alpha)
    return h.reshape(batch, seq, d)
```

```python
import functools

import jax
import jax.numpy as jnp
from jax import lax
from jax.experimental import pallas as pl
from jax.experimental.pallas import tpu as pltpu
from jax.experimental.pallas import tpu_sc as plsc

F32 = jnp.float32
BF16 = jnp.bfloat16

LANES = 128
VMEM_LIMIT_BYTES = 48 * 1024 * 1024

HEAD_DIM = 64
ROPE_DIM = HEAD_DIM // 4
ROPE_HALF = ROPE_DIM // 2
ROPE_THETA = 500000.0
DIL_PAIRS = ((128, 1), (512, 4), (2048, 16))
BAND = 128
DIL_TILE = 512
W_DIL = 384
W_MOBA = 384
W_MEM = 256
MOBA_BLOCK = 256
MOBA_TOPK = 3
N_EXPERTS = 256
N_GROUPS = 8
GROUP_SIZE = N_EXPERTS // N_GROUPS
TOPK_GROUPS = 4
TOP_K = 8
D_EXPERT = 256
ROUTED_SCALE = 2.5
LN_EPS = 1e-5
NEG = -1e30
QK_SCALE = HEAD_DIM ** -0.5


def _cparams(*sem):
    return pltpu.CompilerParams(dimension_semantics=sem, vmem_limit_bytes=VMEM_LIMIT_BYTES)


def _dot_nt(a, b):
    return lax.dot_general(a, b, (((1,), (1,)), ((), ())), preferred_element_type=F32)


def _layer_norm(v, g, b):
    mu = jnp.mean(v, axis=-1, keepdims=True)
    c = v - mu
    var = jnp.mean(c * c, axis=-1, keepdims=True)
    return c * lax.rsqrt(var + LN_EPS) * g + b


_IN_SECTIONS = (
    (0, W_DIL, True, True),
    (W_DIL, W_DIL, True, False),
    (2 * W_DIL, W_DIL, False, False),
    (3 * W_DIL, W_MOBA, True, True),
    (3 * W_DIL + W_MOBA, W_MOBA, True, False),
    (3 * W_DIL + 2 * W_MOBA, W_MOBA, False, False),
    (3 * W_DIL + 3 * W_MOBA, W_MEM, False, True),
)


N_DIL_SECTIONS = 3


def _inproj_kernel(x_ref, w_ref, cos_ref, sin_ref, *refs):
    n_dil = N_DIL_SECTIONS * len(DIL_PAIRS)
    dil_refs, flat_refs, stage = refs[:n_dil], refs[n_dil:-1], refs[-1]
    xb = x_ref[...].astype(BF16)
    cos = cos_ref[...]
    sin = sin_ref[...]
    tm = xb.shape[0]
    lane = lax.broadcasted_iota(jnp.int32, (tm, LANES), 1)
    first_half = (lane % ROPE_DIM) < ROPE_HALF

    def rope(t):
        partner = jnp.where(first_half, pltpu.roll(t, LANES - ROPE_HALF, 1), pltpu.roll(t, ROPE_HALF, 1))
        return t * cos + partner * sin

    for sec, (off, width, roped, scaled) in enumerate(_IN_SECTIONS):
        acc = jnp.dot(xb, w_ref[:, off:off + width], preferred_element_type=F32)
        for c in range(width // LANES):
            t = acc[:, c * LANES:(c + 1) * LANES]
            if roped:
                t = rope(t)
            if scaled:
                t = t * QK_SCALE
            if sec >= N_DIL_SECTIONS:
                flat_refs[sec - N_DIL_SECTIONS][:, c * LANES:(c + 1) * LANES] = t.astype(BF16)
                continue
            o_ref = dil_refs[sec * len(DIL_PAIRS) + c]
            dilation = DIL_PAIRS[c][1]
            if dilation == 1:
                o_ref[0] = t.astype(BF16)
                continue
            slot = sec * len(DIL_PAIRS) + c
            stage[slot] = t
            for r in range(dilation):
                o_ref[r] = stage[slot, pl.ds(r, tm // dilation, stride=dilation), :].astype(BF16)


def _in_projection(x2, w_in_b, cos_t, sin_t, batch, seq, tm=512):
    n, d = x2.shape
    w_total = w_in_b.shape[1]
    tiles = seq // tm
    dil_specs, dil_shapes = [], []
    for _ in range(N_DIL_SECTIONS):
        for _, dilation in DIL_PAIRS:
            dil_specs.append(pl.BlockSpec((None, dilation, tm // dilation, LANES),
                                          lambda i: (i // tiles, 0, i % tiles, 0)))
            dil_shapes.append(jax.ShapeDtypeStruct((batch, dilation, seq // dilation, LANES), BF16))
    widths = [s[1] for s in _IN_SECTIONS[N_DIL_SECTIONS:]]
    outs = pl.pallas_call(
        _inproj_kernel,
        grid=(n // tm,),
        in_specs=[
            pl.BlockSpec((tm, d), lambda i: (i, 0)),
            pl.BlockSpec((d, w_total), lambda i: (0, 0)),
            pl.BlockSpec((tm, LANES), lambda i: (i, 0)),
            pl.BlockSpec((tm, LANES), lambda i: (i, 0)),
        ],
        out_specs=dil_specs + [pl.BlockSpec((tm, w), lambda i: (i, 0)) for w in widths],
        out_shape=dil_shapes + [jax.ShapeDtypeStruct((n, w), BF16) for w in widths],
        scratch_shapes=[pltpu.VMEM((N_DIL_SECTIONS * len(DIL_PAIRS), tm, LANES), F32)],
        compiler_params=_cparams("parallel"),
        name="in_proj_rope",
    )(x2, w_in_b, cos_t, sin_t)
    n_dil = len(dil_specs)
    ng = len(DIL_PAIRS)
    qkv_dil = [outs[s * ng:(s + 1) * ng] for s in range(N_DIL_SECTIONS)]
    return qkv_dil, outs[n_dil:]


def _rope_tables(positions):
    lane = jnp.arange(LANES)
    rotary = (lane % HEAD_DIM) < ROPE_DIM
    inv_freq = ROPE_THETA ** (-(lane % ROPE_HALF).astype(F32) / ROPE_HALF)
    ang = positions.reshape(-1).astype(F32)[:, None] * jnp.where(rotary, inv_freq, 0.0)
    sign = jnp.where((lane % ROPE_DIM) < ROPE_HALF, -1.0, 1.0)
    return jnp.cos(ang), jnp.sin(ang) * sign


def _head_mask(shape, h):
    lane = lax.broadcasted_iota(jnp.int32, shape, 1)
    return (lane // HEAD_DIM) == h


def _dil_kernel(q_ref, kp_ref, k_ref, vp_ref, v_ref, o_ref, lse_ref):
    i = pl.program_id(2)
    tq = q_ref.shape[0]
    qi = lax.broadcasted_iota(jnp.int32, (BAND, 2 * BAND), 0)
    kj = lax.broadcasted_iota(jnp.int32, (BAND, 2 * BAND), 1)
    dist = qi + BAND - kj
    band = (dist >= 0) & (dist <= BAND)
    first = _head_mask((BAND, LANES), 0)
    for j in range(tq // BAND):
        rows = slice(j * BAND, (j + 1) * BAND)
        q = q_ref[rows, :]
        if j == 0:
            k_prev, v_prev = kp_ref[...], vp_ref[...]
            allowed = band & ((kj >= BAND) | (i > 0))
        else:
            prev_rows = slice((j - 1) * BAND, j * BAND)
            k_prev, v_prev = k_ref[prev_rows, :], v_ref[prev_rows, :]
            allowed = band
        k = jnp.concatenate([k_prev, k_ref[rows, :]], axis=0)
        v = jnp.concatenate([v_prev, v_ref[rows, :]], axis=0)
        outs, lses = [], []
        for h in range(2):
            qh = jnp.where(_head_mask(q.shape, h), q, jnp.zeros_like(q))
            s = jnp.where(allowed, _dot_nt(qh, k), NEG)
            m = jnp.max(s, axis=1, keepdims=True)
            p = jnp.exp(s - m)
            l = jnp.sum(p, axis=1, keepdims=True)
            outs.append(jnp.dot(p.astype(BF16), v, preferred_element_type=F32) / l)
            lses.append(jnp.broadcast_to(m + jnp.log(l), (BAND, LANES)))
        o_ref[rows, :] = jnp.where(first, outs[0], outs[1]).astype(o_ref.dtype)
        lse_ref[rows, :] = jnp.where(first, lses[0], lses[1])


def _dilated_attention(q4, k4, v4):
    batch, dilation, steps, _ = q4.shape
    tq = min(steps, DIL_TILE)
    per_tile = tq // BAND
    cur = pl.BlockSpec((None, None, tq, LANES), lambda b, r, i: (b, r, i, 0))
    prev = pl.BlockSpec((None, None, BAND, LANES), lambda b, r, i: (b, r, jnp.maximum(i * per_tile - 1, 0), 0))
    return pl.pallas_call(
        _dil_kernel,
        grid=(batch, dilation, steps // tq),
        in_specs=[cur, prev, cur, prev, cur],
        out_specs=[cur, cur],
        out_shape=[jax.ShapeDtypeStruct(q4.shape, BF16), jax.ShapeDtypeStruct(q4.shape, F32)],
        compiler_params=_cparams("parallel", "parallel", "arbitrary"),
        name=f"dilated_attn_d{dilation}",
    )(q4, k4, k4, v4, v4)


def _kmean_kernel(k_ref, hi_ref, lo_ref):
    k = k_ref[...].astype(F32)
    s, w = k.shape
    mean = jnp.sum(k.reshape(s // MOBA_BLOCK, MOBA_BLOCK, w), axis=1) / MOBA_BLOCK
    hi = mean.astype(BF16)
    hi_ref[...] = hi
    lo_ref[...] = (mean - hi.astype(F32)).astype(BF16)


def _moba_kmean(km3):
    batch, seq, w = km3.shape
    nb = seq // MOBA_BLOCK
    return pl.pallas_call(
        _kmean_kernel,
        grid=(batch,),
        in_specs=[pl.BlockSpec((None, seq, w), lambda b: (b, 0, 0))],
        out_specs=[pl.BlockSpec((None, nb, w), lambda b: (b, 0, 0))] * 2,
        out_shape=[jax.ShapeDtypeStruct((batch, nb, w), BF16)] * 2,
        compiler_params=_cparams("parallel"),
        name="moba_kmean",
    )(km3)


def _moba_kernel(q_ref, k_ref, v_ref, kmh_ref, kml_ref, o_ref, qaug, kaug, m_scr, l_scr, acc_scr):
    s = pl.program_id(2)
    nsteps = pl.num_programs(2)
    nb = kmh_ref.shape[0]
    tq = MOBA_BLOCK
    lane = lax.broadcasted_iota(jnp.int32, (tq, LANES), 1)
    blk = lax.broadcasted_iota(jnp.int32, (nb, tq), 0)
    slot = s % 2

    def select(step, dst_slot):
        for t in range(2):
            qi = step if t == 0 else nb - 1 - step
            q = q_ref[pl.ds(pl.multiple_of(qi * tq, tq), tq), :]
            for h in range(2):
                hm = _head_mask(q.shape, h)
                qh = jnp.where(hm, q, jnp.zeros_like(q))
                gate = _dot_nt(kmh_ref[...], qh) + _dot_nt(kml_ref[...], qh)
                cnt = jnp.zeros((nb, tq), jnp.int32)
                for jp in range(nb):
                    g_jp = gate[jp:jp + 1, :]
                    beats = (g_jp > gate) | ((g_jp == gate) & (blk > jp))
                    cnt = cnt + jnp.where(beats & (qi > jp), 1, 0)
                sel = ((blk < qi) & (cnt < MOBA_TOPK)) | (blk == qi)
                bias_t = jnp.where(sel, 0.0, NEG).astype(F32)
                spare = HEAD_DIM * (1 - h)
                pieces = [jnp.zeros((spare, tq), F32)] if spare else []
                pieces += [bias_t, jnp.zeros((LANES - spare - nb, tq), F32)]
                qaug[dst_slot, t, h] = jnp.where(hm, q, jnp.concatenate(pieces, axis=0).T.astype(BF16))

    @pl.when(s == 0)
    def _():
        for h in range(2):
            spare = HEAD_DIM * (1 - h)
            for j in range(nb):
                onehot = jnp.where(lane == spare + j, 1.0, 0.0).astype(BF16)
                kaug[h, j * tq:(j + 1) * tq, :] = jnp.where(
                    _head_mask((tq, LANES), h), k_ref[j * tq:(j + 1) * tq, :], onehot)
        select(0, 0)

    for t in range(2):
        for h in range(2):
            m_scr[t, h] = jnp.full((tq, LANES), NEG, F32)
            l_scr[t, h] = jnp.zeros((tq, LANES), F32)
            acc_scr[t, h] = jnp.zeros((tq, LANES), F32)

    row = lax.broadcasted_iota(jnp.int32, (tq, tq), 0)
    col = lax.broadcasted_iota(jnp.int32, (tq, tq), 1)
    causal_bias = jnp.where(col <= row, 0.0, NEG).astype(F32)
    starts = [pl.multiple_of(s * tq, tq), pl.multiple_of((nb - 1 - s) * tq, tq)]

    for it in range(nb + 1):
        if it < 2:
            t, kstart = it, starts[it]
        else:
            first = it - 2 < s
            t = jnp.where(first, 0, 1)
            kstart = pl.multiple_of(jnp.where(first, it - 2, it - 2 - s) * tq, tq)
        vj = v_ref[pl.ds(kstart, tq), :]
        for h in range(2):
            sc = _dot_nt(qaug[slot, t, h], kaug[h, pl.ds(kstart, tq), :])
            if it < 2:
                sc = sc + causal_bias
            m_old = m_scr[t, h]
            m_new = jnp.maximum(m_old, jnp.max(sc, axis=1, keepdims=True))
            p = jnp.exp(sc - jnp.concatenate([m_new, m_new], axis=1))
            alpha = jnp.exp(m_old - m_new)
            l_scr[t, h] = alpha * l_scr[t, h] + jnp.sum(p, axis=1, keepdims=True)
            acc_scr[t, h] = alpha * acc_scr[t, h] + jnp.dot(p.astype(BF16), vj, preferred_element_type=F32)
            m_scr[t, h] = m_new

    for t in range(2):
        o = jnp.where(_head_mask((tq, LANES), 0), acc_scr[t, 0] / l_scr[t, 0], acc_scr[t, 1] / l_scr[t, 1])
        o_ref[pl.ds(starts[t], tq), :] = o.astype(o_ref.dtype)

    select(jnp.minimum(s + 1, nsteps - 1), 1 - slot)


def _moba_attention(qm3, km3, vm3, kmh, kml):
    batch, seq, w = qm3.shape
    nb = seq // MOBA_BLOCK
    assert nb % 2 == 0 and nb <= HEAD_DIM
    npair = w // LANES
    tq = MOBA_BLOCK
    seq_spec = pl.BlockSpec((None, seq, LANES), lambda b, p, i: (b, 0, p))
    km_spec = pl.BlockSpec((None, nb, LANES), lambda b, p, i: (b, 0, p))
    state = pltpu.VMEM((2, 2, tq, LANES), F32)
    return pl.pallas_call(
        _moba_kernel,
        grid=(batch, npair, nb // 2),
        in_specs=[seq_spec, seq_spec, seq_spec, km_spec, km_spec],
        out_specs=seq_spec,
        out_shape=jax.ShapeDtypeStruct((batch, seq, w), BF16),
        scratch_shapes=[pltpu.VMEM((2, 2, 2, tq, LANES), BF16), pltpu.VMEM((2, seq, LANES), BF16),
                        state, state, state],
        compiler_params=_cparams("parallel", "parallel", "arbitrary"),
        name="moba_attn",
    )(qm3, km3, vm3, kmh, kml)


def _memkv_kernel(mem_ref, w_ref, k_ref, v_ref):
    kv = jnp.dot(mem_ref[...].astype(BF16), w_ref[...], preferred_element_type=F32)
    k_ref[...] = kv[:, :W_MEM].astype(BF16)
    v_ref[...] = kv[:, W_MEM:].astype(BF16)


def _mem_kv(mem, w_kv_b):
    batch, m, d = mem.shape
    return pl.pallas_call(
        _memkv_kernel,
        grid=(batch,),
        in_specs=[pl.BlockSpec((None, m, d), lambda b: (b, 0, 0)),
                  pl.BlockSpec((d, 2 * W_MEM), lambda b: (0, 0))],
        out_specs=[pl.BlockSpec((None, m, W_MEM), lambda b: (b, 0, 0))] * 2,
        out_shape=[jax.ShapeDtypeStruct((batch, m, W_MEM), BF16)] * 2,
        compiler_params=_cparams("parallel"),
        name="mem_kv_proj",
    )(mem, w_kv_b)


def _memattn_kernel(q_ref, k_ref, v_ref, o_ref):
    q = q_ref[...]
    k = k_ref[...]
    v = v_ref[...]
    outs = []
    for h in range(2):
        qh = jnp.where(_head_mask(q.shape, h), q, jnp.zeros_like(q))
        s = _dot_nt(qh, k)
        m = jnp.max(s, axis=1, keepdims=True)
        p = jnp.exp(s - m)
        l = jnp.sum(p, axis=1, keepdims=True)
        outs.append(jnp.dot(p.astype(BF16), v, preferred_element_type=F32) / l)
    o_ref[...] = jnp.where(_head_mask(outs[0].shape, 0), outs[0], outs[1]).astype(o_ref.dtype)


def _mem_attention(qx3, k_mem, v_mem, tq=512):
    batch, seq, w = qx3.shape
    m = k_mem.shape[1]
    return pl.pallas_call(
        _memattn_kernel,
        grid=(batch, w // LANES, seq // tq),
        in_specs=[pl.BlockSpec((None, tq, LANES), lambda b, p, i: (b, i, p)),
                  pl.BlockSpec((None, m, LANES), lambda b, p, i: (b, 0, p)),
                  pl.BlockSpec((None, m, LANES), lambda b, p, i: (b, 0, p))],
        out_specs=pl.BlockSpec((None, tq, LANES), lambda b, p, i: (b, i, p)),
        out_shape=jax.ShapeDtypeStruct((batch, seq, w), BF16),
        compiler_params=_cparams("parallel", "parallel", "parallel"),
        name="mem_attn",
    )(qx3, k_mem, v_mem)


def _merge_kernel(alpha, x_ref, o1_ref, o2_ref, o3_ref, l1_ref, l2_ref, l3_ref, ym_ref, yx_ref,
                  wg_ref, bg_ref, wbd_ref, wbm_ref, wbx_ref, wo_ref, g_ref, b_ref, out_ref, packed_ref, stage):
    x = x_ref[...]
    xb = x.astype(BF16)
    tm, d = x.shape

    def token_major(ref, slot):
        dilation = ref.shape[0]
        if dilation == 1:
            return ref[0].astype(F32)
        for r in range(dilation):
            stage[slot, pl.ds(r, tm // dilation, stride=dilation), :] = ref[r].astype(F32)
        return stage[slot]

    o1, o2, o3 = (token_major(r, s) for s, r in enumerate((o1_ref, o2_ref, o3_ref)))
    l1, l2, l3 = (token_major(r, s + 3) for s, r in enumerate((l1_ref, l2_ref, l3_ref)))
    mx = jnp.maximum(jnp.maximum(l1, l2), l3)
    e1, e2, e3 = jnp.exp(l1 - mx), jnp.exp(l2 - mx), jnp.exp(l3 - mx)
    y_dil = (e1 * o1 + e2 * o2 + e3 * o3) / (e1 + e2 + e3)
    branches = (
        jnp.dot(y_dil.astype(BF16), wbd_ref[...], preferred_element_type=F32),
        jnp.dot(ym_ref[...], wbm_ref[...], preferred_element_type=F32),
        jnp.dot(yx_ref[...], wbx_ref[...], preferred_element_type=F32),
    )
    merged = jnp.zeros_like(x)
    for i, br in enumerate(branches):
        logits = jnp.dot(xb, wg_ref[:, i * d:(i + 1) * d], preferred_element_type=F32) + bg_ref[:, i * d:(i + 1) * d]
        merged = merged + jax.nn.sigmoid(logits) * br
    mix = jnp.dot(merged.astype(BF16), wo_ref[...], preferred_element_type=F32)
    x1 = _layer_norm(alpha * x + mix, g_ref[...], b_ref[...])
    out_ref[...] = x1
    packed_ref[...] = _pack_halves(x1)


def _merge(alpha, x2, o_dil, lse_dil, y_moba, y_mem, wg, bg, wbd, wbm, wbx, wo, g, b, seq, tm=512):
    n, d = x2.shape
    tiles = seq // tm
    row = lambda w: pl.BlockSpec((tm, w), lambda i: (i, 0))
    full = lambda a: pl.BlockSpec(a.shape, lambda i: (0, 0))
    residue_major = lambda a: pl.BlockSpec((None, a.shape[1], tm // a.shape[1], LANES),
                                           lambda i: (i // tiles, 0, i % tiles, 0))
    weights = (wg, bg, wbd, wbm, wbx, wo, g, b)
    return pl.pallas_call(
        functools.partial(_merge_kernel, alpha),
        grid=(n // tm,),
        in_specs=([row(d)] + [residue_major(a) for a in (*o_dil, *lse_dil)] + [row(W_MOBA), row(W_MEM)]
                  + [full(a) for a in weights]),
        out_specs=[row(d), row(PACKED)],
        out_shape=[jax.ShapeDtypeStruct((n, d), F32), jax.ShapeDtypeStruct((n, PACKED), jnp.uint32)],
        scratch_shapes=[pltpu.VMEM((2 * len(DIL_PAIRS), tm, LANES), F32)],
        compiler_params=_cparams("parallel"),
        name="merge_outproj_ln1",
    )(x2, *o_dil, *lse_dil, y_moba, y_mem, *weights)


def _token_mixer(x2, mem, positions, w_in, w_mem_kv, w_gate, b_gate, w_br_dil, w_br_moba, w_br_mem,
                 w_out, ln1_g, ln1_b, batch, seq, alpha):
    cos_t, sin_t = _rope_tables(positions)
    (q_dil, k_dil, v_dil), (qm, km, vm, qx) = _in_projection(x2, w_in.astype(BF16), cos_t, sin_t, batch, seq)
    o_dil, lse_dil = [], []
    for g in range(len(DIL_PAIRS)):
        o, lse = _dilated_attention(q_dil[g], k_dil[g], v_dil[g])
        o_dil.append(o)
        lse_dil.append(lse)
    qm3, km3, vm3 = (t.reshape(batch, seq, W_MOBA) for t in (qm, km, vm))
    kmh, kml = _moba_kmean(km3)
    y_moba = _moba_attention(qm3, km3, vm3, kmh, kml).reshape(batch * seq, W_MOBA)
    k_mem, v_mem = _mem_kv(mem, w_mem_kv.astype(BF16))
    y_mem = _mem_attention(qx.reshape(batch, seq, W_MEM), k_mem, v_mem).reshape(batch * seq, W_MEM)
    return _merge(alpha, x2, o_dil, lse_dil, y_moba, y_mem,
                  w_gate.astype(BF16), b_gate.reshape(1, -1), w_br_dil.astype(BF16),
                  w_br_moba.astype(BF16), w_br_mem.astype(BF16), w_out.astype(BF16),
                  ln1_g.reshape(1, -1), ln1_b.reshape(1, -1), seq)


EXPERT_ROWS = 256
EXPERT_RING = 4
TOKEN_TILE = 256
PACKED = 512
SC_WINDOW = 128


def _first_index_of_max(v, iota_f, size):
    m = jnp.max(v, axis=0, keepdims=True)
    idx = jnp.min(jnp.where(v == m, iota_f, float(size)), axis=0, keepdims=True)
    return m, idx


def _route_kernel(x_ref, wh_ref, wl_ref, bias_ref, eidx_ref, gw_ref, rank_ref, cnt_ref, carry):
    step = pl.program_id(0)

    @pl.when(step == 0)
    def _():
        carry[...] = jnp.zeros_like(carry)

    x = x_ref[...]
    tm = x.shape[0]
    xh = x.astype(BF16)
    xl = (x - xh.astype(F32)).astype(BF16)
    wh = wh_ref[...]
    logits = _dot_nt(wh, xh) + _dot_nt(wh, xl) + _dot_nt(wl_ref[...], xh)
    scores = jax.nn.sigmoid(logits)
    biased = scores + bias_ref[...][:, :1]

    giota = lax.broadcasted_iota(jnp.int32, (GROUP_SIZE, tm), 0).astype(F32)
    group_scores = []
    for g in range(N_GROUPS):
        slab = biased[g * GROUP_SIZE:(g + 1) * GROUP_SIZE, :]
        m1, i1 = _first_index_of_max(slab, giota, GROUP_SIZE)
        m2 = jnp.max(jnp.where(giota == i1, -jnp.inf, slab), axis=0, keepdims=True)
        group_scores.append(m1 + m2)
    gs = jnp.concatenate(group_scores, axis=0)
    gidx = lax.broadcasted_iota(jnp.int32, (N_GROUPS, tm), 0)
    beaten = jnp.zeros((N_GROUPS, tm), jnp.int32)
    for gp in range(N_GROUPS):
        row = gs[gp:gp + 1, :]
        beaten = beaten + jnp.where((row > gs) | ((row == gs) & (gidx > gp)), 1, 0)
    keep = beaten < TOPK_GROUPS
    masked = jnp.concatenate(
        [jnp.where(keep[g:g + 1, :], biased[g * GROUP_SIZE:(g + 1) * GROUP_SIZE, :], -jnp.inf)
         for g in range(N_GROUPS)], axis=0)

    eiota = lax.broadcasted_iota(jnp.int32, (N_EXPERTS, tm), 0).astype(F32)
    idx_rows, gw_rows = [], []
    for _ in range(TOP_K):
        _, idx = _first_index_of_max(masked, eiota, N_EXPERTS)
        hit = eiota == idx
        gw_rows.append(jnp.sum(jnp.where(hit, scores, 0.0), axis=0, keepdims=True))
        masked = jnp.where(hit, -jnp.inf, masked)
        idx_rows.append(idx)
    idx8 = jnp.concatenate(idx_rows, axis=0)
    gw8 = jnp.concatenate(gw_rows, axis=0)
    gw_ref[...] = gw8 / jnp.sum(gw8, axis=0, keepdims=True) * ROUTED_SCALE
    eidx_ref[...] = idx8.astype(jnp.int32)

    onehot = jnp.zeros((N_EXPERTS, tm), F32)
    for k in range(TOP_K):
        onehot = onehot + jnp.where(eiota == idx8[k:k + 1, :], 1.0, 0.0)
    t_row = lax.broadcasted_iota(jnp.int32, (tm, tm), 0)
    t_col = lax.broadcasted_iota(jnp.int32, (tm, tm), 1)
    earlier = jnp.where(t_row < t_col, 1.0, 0.0).astype(BF16)
    prefix = jnp.dot(onehot.astype(BF16), earlier, preferred_element_type=F32)
    base = carry[...]
    prefix = prefix + jnp.concatenate([base] * (tm // LANES), axis=1)
    rank_rows = [jnp.sum(jnp.where(eiota == idx8[k:k + 1, :], prefix, 0.0), axis=0, keepdims=True)
                 for k in range(TOP_K)]
    rank_ref[...] = jnp.concatenate(rank_rows, axis=0).astype(jnp.int32)
    total = base + jnp.sum(onehot, axis=1, keepdims=True)
    carry[...] = total
    cnt_ref[...] = total.astype(jnp.int32)


def _route(x1, wr_hi_t, wr_lo_t, bias_b, tm=512):
    n, d = x1.shape
    tok = lambda dt: jax.ShapeDtypeStruct((TOP_K, n), dt)
    return pl.pallas_call(
        _route_kernel,
        grid=(n // tm,),
        in_specs=[pl.BlockSpec((tm, d), lambda i: (i, 0)),
                  pl.BlockSpec((N_EXPERTS, d), lambda i: (0, 0)),
                  pl.BlockSpec((N_EXPERTS, d), lambda i: (0, 0)),
                  pl.BlockSpec((N_EXPERTS, LANES), lambda i: (0, 0))],
        out_specs=[pl.BlockSpec((TOP_K, tm), lambda i: (0, i)),
                   pl.BlockSpec((TOP_K, tm), lambda i: (0, i)),
                   pl.BlockSpec((TOP_K, tm), lambda i: (0, i)),
                   pl.BlockSpec((N_EXPERTS, LANES), lambda i: (0, 0))],
        out_shape=[tok(jnp.int32), tok(F32), tok(jnp.int32),
                   jax.ShapeDtypeStruct((N_EXPERTS, LANES), jnp.int32)],
        scratch_shapes=[pltpu.VMEM((N_EXPERTS, LANES), F32)],
        compiler_params=_cparams("arbitrary"),
        name="moe_route",
    )(x1, wr_hi_t, wr_lo_t, bias_b)


def _dest_kernel(eidx_ref, rank_ref, start_ref, dest_ref):
    eidx = eidx_ref[...]
    tm = eidx.shape[1]
    eiota = lax.broadcasted_iota(jnp.int32, (N_EXPERTS, tm), 0)
    start = start_ref[...][:, :1]
    rows = [jnp.sum(jnp.where(eiota == eidx[k:k + 1, :], start, 0.0), axis=0, keepdims=True)
            for k in range(TOP_K)]
    dest_ref[...] = jnp.concatenate(rows, axis=0).astype(jnp.int32) + rank_ref[...]


def _dest_rows(eidx, rank, start_b):
    n = eidx.shape[1]
    nt = n // TOKEN_TILE
    return pl.pallas_call(
        _dest_kernel,
        grid=(nt,),
        in_specs=[pl.BlockSpec((TOP_K, TOKEN_TILE), lambda i: (0, i)),
                  pl.BlockSpec((TOP_K, TOKEN_TILE), lambda i: (0, i)),
                  pl.BlockSpec((N_EXPERTS, LANES), lambda i: (0, 0))],
        out_specs=pl.BlockSpec((TOP_K, TOKEN_TILE), lambda i: (0, i)),
        out_shape=jax.ShapeDtypeStruct((TOP_K, n), jnp.int32),
        compiler_params=_cparams("parallel"),
        name="moe_dest_rows",
    )(eidx, rank, start_b)


def _pack_halves(v):
    return pltpu.pack_elementwise([v[:, :PACKED], v[:, PACKED:]], packed_dtype=BF16)


def _unpack_half(p, index):
    return pltpu.unpack_elementwise(p, index=index, packed_dtype=BF16, unpacked_dtype=F32)


def _sc_mesh():
    return plsc.VectorSubcoreMesh(core_axis_name="core", subcore_axis_name="subcore")


def _sc_move_rows(src, src_idx, dst_idx, out_rows, name):
    width = src.shape[1]
    count = src_idx.shape[1]

    @functools.partial(pl.kernel, out_type=jax.ShapeDtypeStruct((out_rows, width), src.dtype),
                       mesh=_sc_mesh(), scratch_types=[pltpu.VMEM((SC_WINDOW, width), src.dtype)], name=name)
    def move(src_hbm, sidx_hbm, didx_hbm, out_hbm, rows_vmem):
        def body(sidx_vmem, didx_vmem):
            pltpu.sync_copy(src_hbm.at[sidx_vmem.at[0]], rows_vmem)
            pltpu.sync_copy(rows_vmem, out_hbm.at[didx_vmem.at[0]])

        idx_spec = pl.BlockSpec((1, SC_WINDOW), index_map=lambda i: (0, i))
        pltpu.emit_pipeline(
            body,
            grid=(count // SC_WINDOW,),
            in_specs=[idx_spec, idx_spec],
            out_specs=[],
            core_axis_name=("core", "subcore"),
            dimension_semantics=(pltpu.PARALLEL,),
        )(sidx_hbm, didx_hbm)

    return move(src, src_idx, dst_idx)


def _sc_dispatch_rows(src, token_idx, dest, out_rows):
    n, width = src.shape
    slots = dest.shape[0]

    @functools.partial(pl.kernel, out_type=jax.ShapeDtypeStruct((out_rows, width), src.dtype),
                       mesh=_sc_mesh(), scratch_types=[pltpu.VMEM((SC_WINDOW, width), src.dtype)],
                       name="moe_dispatch_sc")
    def dispatch(src_hbm, tidx_hbm, dest_hbm, out_hbm, rows_vmem):
        def body(tidx_vmem, dest_vmem):
            pltpu.sync_copy(src_hbm.at[tidx_vmem.at[0]], rows_vmem)
            for k in range(slots):
                pltpu.sync_copy(rows_vmem, out_hbm.at[dest_vmem.at[k]])

        pltpu.emit_pipeline(
            body,
            grid=(n // SC_WINDOW,),
            in_specs=[pl.BlockSpec((1, SC_WINDOW), index_map=lambda i: (0, i)),
                      pl.BlockSpec((slots, SC_WINDOW), index_map=lambda i: (0, i))],
            out_specs=[],
            core_axis_name=("core", "subcore"),
            dimension_semantics=(pltpu.PARALLEL,),
        )(tidx_hbm, dest_hbm)

    return dispatch(src, token_idx, dest)


def _expert_kernel(first_blk_ref, count_ref, total_ref, xs_hbm, wg_ref, wu_ref, wd_ref, ys_hbm,
                   wg_b, wu_b, wd_b, x_ring, y_ring, in_sem, out_sem):
    e = pl.program_id(0)
    total = total_ref[0]
    first = first_blk_ref[e]
    count = count_ref[e]
    nblk = (count + EXPERT_ROWS - 1) // EXPERT_ROWS

    def in_copy(g):
        slot = g % EXPERT_RING
        return pltpu.make_async_copy(xs_hbm.at[pl.ds(pl.multiple_of(g * EXPERT_ROWS, EXPERT_ROWS), EXPERT_ROWS)],
                                     x_ring.at[slot], in_sem.at[slot])

    def out_copy(g):
        slot = g % EXPERT_RING
        return pltpu.make_async_copy(y_ring.at[slot],
                                     ys_hbm.at[pl.ds(pl.multiple_of(g * EXPERT_ROWS, EXPERT_ROWS), EXPERT_ROWS)],
                                     out_sem.at[slot])

    @pl.when(e == 0)
    def _():
        for g in range(EXPERT_RING - 1):
            @pl.when(g < total)
            def _(g=g):
                in_copy(g).start()

    @pl.when(nblk > 0)
    def _():
        wg_b[...] = wg_ref[...].astype(BF16)
        wu_b[...] = wu_ref[...].astype(BF16)
        wd_b[...] = wd_ref[...].astype(BF16)

    def block(j, carry):
        g = first + j
        slot = g % EXPERT_RING
        in_copy(g).wait()

        @pl.when(g + EXPERT_RING - 1 < total)
        def _():
            in_copy(g + EXPERT_RING - 1).start()

        p = x_ring[slot]
        row = lax.broadcasted_iota(jnp.int32, p.shape, 0) + j * EXPERT_ROWS
        p = jnp.where(row < count, p, jnp.zeros_like(p))
        lo = _unpack_half(p, 0).astype(BF16)
        hi = _unpack_half(p, 1).astype(BF16)

        def up(w):
            return (jnp.dot(lo, w[:PACKED, :], preferred_element_type=F32)
                    + jnp.dot(hi, w[PACKED:, :], preferred_element_type=F32))

        hid = (jax.nn.silu(up(wg_b)) * up(wu_b)).astype(BF16)
        y = jnp.dot(hid, wd_b[...], preferred_element_type=F32)

        @pl.when(g >= EXPERT_RING)
        def _():
            out_copy(g - EXPERT_RING).wait()
        y_ring[slot] = _pack_halves(y)
        out_copy(g).start()
        return carry

    lax.fori_loop(0, nblk, block, 0)

    @pl.when(e == pl.num_programs(0) - 1)
    def _():
        def drain(g, carry):
            out_copy(g).wait()
            return carry

        lax.fori_loop(jnp.maximum(total - EXPERT_RING, 0), total, drain, 0)


def _expert_ffn(xs, first_blk, counts, total, w_e_gate, w_e_up, w_e_down):
    rows = xs.shape[0]
    n_experts, d = w_e_gate.shape[0], w_e_gate.shape[1]
    w_map = lambda e, fb, ct, tt: (e, 0, 0)
    ring = pltpu.VMEM((EXPERT_RING, EXPERT_ROWS, PACKED), jnp.uint32)
    return pl.pallas_call(
        _expert_kernel,
        grid_spec=pltpu.PrefetchScalarGridSpec(
            num_scalar_prefetch=3,
            grid=(n_experts,),
            in_specs=[pl.BlockSpec(memory_space=pl.ANY),
                      pl.BlockSpec((None, d, D_EXPERT), w_map),
                      pl.BlockSpec((None, d, D_EXPERT), w_map),
                      pl.BlockSpec((None, D_EXPERT, d), w_map)],
            out_specs=pl.BlockSpec(memory_space=pl.ANY),
            scratch_shapes=[pltpu.VMEM((d, D_EXPERT), BF16), pltpu.VMEM((d, D_EXPERT), BF16),
                            pltpu.VMEM((D_EXPERT, d), BF16), ring, ring,
                            pltpu.SemaphoreType.DMA((EXPERT_RING,)), pltpu.SemaphoreType.DMA((EXPERT_RING,))],
        ),
        out_shape=jax.ShapeDtypeStruct((rows, PACKED), jnp.uint32),
        compiler_params=_cparams("arbitrary"),
        name="moe_expert_ffn",
    )(first_blk, counts, total, xs, w_e_gate, w_e_up, w_e_down)


def _combine_kernel(alpha, x_ref, gw_ref, ys_ref, wsg_ref, wsu_ref, wsd_ref, g_ref, b_ref, out_ref):
    x = x_ref[...]
    xb = x.astype(BF16)
    hid = (jax.nn.silu(jnp.dot(xb, wsg_ref[...], preferred_element_type=F32))
           * jnp.dot(xb, wsu_ref[...], preferred_element_type=F32)).astype(BF16)
    shared = jnp.dot(hid, wsd_ref[...], preferred_element_type=F32)
    gw = gw_ref[...]
    lo = jnp.zeros((TOKEN_TILE, PACKED), F32)
    hi = jnp.zeros((TOKEN_TILE, PACKED), F32)
    for k in range(TOP_K):
        p = ys_ref[k]
        w = gw[:, k:k + 1]
        lo = lo + w * _unpack_half(p, 0)
        hi = hi + w * _unpack_half(p, 1)
    routed = jnp.concatenate([lo, hi], axis=1)
    out_ref[...] = _layer_norm(alpha * x + (routed + shared), g_ref[...], b_ref[...])


def _combine(alpha, x1, gw_t, ys_tok, wsg, wsu, wsd, g, b):
    n, d = x1.shape
    full = lambda a: pl.BlockSpec(a.shape, lambda i: (0, 0))
    weights = (wsg, wsu, wsd, g, b)
    return pl.pallas_call(
        functools.partial(_combine_kernel, alpha),
        grid=(n // TOKEN_TILE,),
        in_specs=[pl.BlockSpec((TOKEN_TILE, d), lambda i: (i, 0)),
                  pl.BlockSpec((TOKEN_TILE, TOP_K), lambda i: (i, 0)),
                  pl.BlockSpec((TOP_K, TOKEN_TILE, PACKED), lambda i: (0, i, 0))] + [full(a) for a in weights],
        out_specs=pl.BlockSpec((TOKEN_TILE, d), lambda i: (i, 0)),
        out_shape=jax.ShapeDtypeStruct((n, d), F32),
        compiler_params=_cparams("parallel"),
        name="moe_combine_shared_ln2",
    )(x1, gw_t, ys_tok, *weights)


def _moe_layer(x1, x1_packed, w_router, router_bias, w_e_gate, w_e_up, w_e_down, w_s_gate, w_s_up, w_s_down,
               ln2_g, ln2_b, alpha):
    n, d = x1.shape
    wr_t = w_router.T
    wr_hi = wr_t.astype(BF16)
    wr_lo = (wr_t - wr_hi.astype(F32)).astype(BF16)
    bias_b = jnp.broadcast_to(router_bias.astype(F32)[:, None], (N_EXPERTS, LANES))
    eidx, gw, rank, cnt = _route(x1, wr_hi, wr_lo, bias_b)

    counts = cnt[:, 0]
    padded = (counts + EXPERT_ROWS - 1) // EXPERT_ROWS * EXPERT_ROWS
    seg_end = jnp.cumsum(padded)
    seg_start = seg_end - padded
    rows = n * TOP_K + N_EXPERTS * EXPERT_ROWS
    first_blk = (seg_start // EXPERT_ROWS).astype(jnp.int32)
    total_blk = (seg_end[-1:] // EXPERT_ROWS).astype(jnp.int32)
    start_b = jnp.broadcast_to(seg_start.astype(F32)[:, None], (N_EXPERTS, LANES))

    dest = _dest_rows(eidx, rank, start_b)
    dest_flat = dest.reshape(1, TOP_K * n)
    assign = jnp.arange(TOP_K * n, dtype=jnp.int32).reshape(1, TOP_K * n)
    xs = _sc_dispatch_rows(x1_packed, assign[:, :n], dest, rows)
    ys = _expert_ffn(xs, first_blk, counts, total_blk, w_e_gate, w_e_up, w_e_down)
    ys_tok = _sc_move_rows(ys, dest_flat, assign, TOP_K * n, "moe_gather_sc").reshape(TOP_K, n, PACKED)
    return _combine(alpha, x1, gw.T, ys_tok, w_s_gate.astype(BF16), w_s_up.astype(BF16),
                    w_s_down.astype(BF16), ln2_g.reshape(1, -1), ln2_b.reshape(1, -1))


def kernel(x, mem, positions, w_in, w_mem_kv, w_gate, b_gate, w_br_dil, w_br_moba, w_br_mem, w_out, ln1_g, ln1_b, w_router, router_bias, w_e_gate, w_e_up, w_e_down, w_s_gate, w_s_up, w_s_down, ln2_g, ln2_b):
    batch, seq, d = x.shape
    depth = w_in.shape[0]
    alpha = (2.0 * depth) ** 0.25
    h = x.reshape(batch * seq, d)
    for l in range(depth):
        h, h_packed = _token_mixer(h, mem, positions, w_in[l], w_mem_kv[l], w_gate[l], b_gate[l], w_br_dil[l],
                                   w_br_moba[l], w_br_mem[l], w_out[l], ln1_g[l], ln1_b[l], batch, seq, alpha)
        h = _moe_layer(h, h_packed, w_router[l], router_bias[l], w_e_gate[l], w_e_up[l], w_e_down[l],
                       w_s_gate[l], w_s_up[l], w_s_down[l], ln2_g[l], ln2_b[l], alpha)
    return h.reshape(batch, seq, d)
```

```python
import functools

import jax
import jax.numpy as jnp
from jax import lax
from jax.experimental import pallas as pl
from jax.experimental.pallas import tpu as pltpu
from jax.experimental.pallas import tpu_sc as plsc

F32 = jnp.float32
BF16 = jnp.bfloat16

LANES = 128
VMEM_LIMIT_BYTES = 48 * 1024 * 1024

HEAD_DIM = 64
ROPE_DIM = HEAD_DIM // 4
ROPE_HALF = ROPE_DIM // 2
ROPE_THETA = 500000.0
DIL_PAIRS = ((128, 1), (512, 4), (2048, 16))
BAND = 128
DIL_TILE = 512
W_DIL = 384
W_MOBA = 384
W_MEM = 256
MOBA_BLOCK = 256
MOBA_TOPK = 3
N_EXPERTS = 256
N_GROUPS = 8
GROUP_SIZE = N_EXPERTS // N_GROUPS
TOPK_GROUPS = 4
TOP_K = 8
D_EXPERT = 256
ROUTED_SCALE = 2.5
LN_EPS = 1e-5
NEG = -1e30
QK_SCALE = HEAD_DIM ** -0.5


def _cparams(*sem):
    return pltpu.CompilerParams(dimension_semantics=sem, vmem_limit_bytes=VMEM_LIMIT_BYTES)


def _dot_nt(a, b):
    return lax.dot_general(a, b, (((1,), (1,)), ((), ())), preferred_element_type=F32)


def _layer_norm(v, g, b):
    mu = jnp.mean(v, axis=-1, keepdims=True)
    c = v - mu
    var = jnp.mean(c * c, axis=-1, keepdims=True)
    return c * lax.rsqrt(var + LN_EPS) * g + b


_IN_SECTIONS = (
    (0, W_DIL, True, True),
    (W_DIL, W_DIL, True, False),
    (2 * W_DIL, W_DIL, False, False),
    (3 * W_DIL, W_MOBA, True, True),
    (3 * W_DIL + W_MOBA, W_MOBA, True, False),
    (3 * W_DIL + 2 * W_MOBA, W_MOBA, False, False),
    (3 * W_DIL + 3 * W_MOBA, W_MEM, False, True),
)


N_DIL_SECTIONS = 3


def _inproj_kernel(x_ref, w_ref, cos_ref, sin_ref, *refs):
    n_dil = N_DIL_SECTIONS * len(DIL_PAIRS)
    dil_refs, flat_refs, stage = refs[:n_dil], refs[n_dil:-1], refs[-1]
    xb = x_ref[...].astype(BF16)
    cos = cos_ref[...]
    sin = sin_ref[...]
    tm = xb.shape[0]
    lane = lax.broadcasted_iota(jnp.int32, (tm, LANES), 1)
    first_half = (lane % ROPE_DIM) < ROPE_HALF

    def rope(t):
        partner = jnp.where(first_half, pltpu.roll(t, LANES - ROPE_HALF, 1), pltpu.roll(t, ROPE_HALF, 1))
        return t * cos + partner * sin

    for sec, (off, width, roped, scaled) in enumerate(_IN_SECTIONS):
        acc = jnp.dot(xb, w_ref[:, off:off + width], preferred_element_type=F32)
        for c in range(width // LANES):
            t = acc[:, c * LANES:(c + 1) * LANES]
            if roped:
                t = rope(t)
            if scaled:
                t = t * QK_SCALE
            if sec >= N_DIL_SECTIONS:
                flat_refs[sec - N_DIL_SECTIONS][:, c * LANES:(c + 1) * LANES] = t.astype(BF16)
                continue
            o_ref = dil_refs[sec * len(DIL_PAIRS) + c]
            dilation = DIL_PAIRS[c][1]
            if dilation == 1:
                o_ref[0] = t.astype(BF16)
                continue
            slot = sec * len(DIL_PAIRS) + c
            stage[slot] = t
            for r in range(dilation):
                o_ref[r] = stage[slot, pl.ds(r, tm // dilation, stride=dilation), :].astype(BF16)


def _in_projection(x2, w_in_b, cos_t, sin_t, batch, seq, tm=512):
    n, d = x2.shape
    w_total = w_in_b.shape[1]
    tiles = seq // tm
    dil_specs, dil_shapes = [], []
    for _ in range(N_DIL_SECTIONS):
        for _, dilation in DIL_PAIRS:
            dil_specs.append(pl.BlockSpec((None, dilation, tm // dilation, LANES),
                                          lambda i: (i // tiles, 0, i % tiles, 0)))
            dil_shapes.append(jax.ShapeDtypeStruct((batch, dilation, seq // dilation, LANES), BF16))
    widths = [s[1] for s in _IN_SECTIONS[N_DIL_SECTIONS:]]
    outs = pl.pallas_call(
        _inproj_kernel,
        grid=(n // tm,),
        in_specs=[
            pl.BlockSpec((tm, d), lambda i: (i, 0)),
            pl.BlockSpec((d, w_total), lambda i: (0, 0)),
            pl.BlockSpec((tm, LANES), lambda i: (i, 0)),
            pl.BlockSpec((tm, LANES), lambda i: (i, 0)),
        ],
        out_specs=dil_specs + [pl.BlockSpec((tm, w), lambda i: (i, 0)) for w in widths],
        out_shape=dil_shapes + [jax.ShapeDtypeStruct((n, w), BF16) for w in widths],
        scratch_shapes=[pltpu.VMEM((N_DIL_SECTIONS * len(DIL_PAIRS), tm, LANES), F32)],
        compiler_params=_cparams("parallel"),
        name="in_proj_rope",
    )(x2, w_in_b, cos_t, sin_t)
    n_dil = len(dil_specs)
    ng = len(DIL_PAIRS)
    qkv_dil = [outs[s * ng:(s + 1) * ng] for s in range(N_DIL_SECTIONS)]
    return qkv_dil, outs[n_dil:]


def _rope_tables(positions):
    lane = jnp.arange(LANES)
    rotary = (lane % HEAD_DIM) < ROPE_DIM
    inv_freq = ROPE_THETA ** (-(lane % ROPE_HALF).astype(F32) / ROPE_HALF)
    ang = positions.reshape(-1).astype(F32)[:, None] * jnp.where(rotary, inv_freq, 0.0)
    sign = jnp.where((lane % ROPE_DIM) < ROPE_HALF, -1.0, 1.0)
    return jnp.cos(ang), jnp.sin(ang) * sign


def _head_mask(shape, h):
    lane = lax.broadcasted_iota(jnp.int32, shape, 1)
    return (lane // HEAD_DIM) == h


def _dil_kernel(q_ref, kp_ref, k_ref, vp_ref, v_ref, o_ref, lse_ref):
    i = pl.program_id(2)
    tq = q_ref.shape[0]
    qi = lax.broadcasted_iota(jnp.int32, (BAND, 2 * BAND), 0)
    kj = lax.broadcasted_iota(jnp.int32, (BAND, 2 * BAND), 1)
    dist = qi + BAND - kj
    band = (dist >= 0) & (dist <= BAND)
    first = _head_mask((BAND, LANES), 0)
    for j in range(tq // BAND):
        rows = slice(j * BAND, (j + 1) * BAND)
        q = q_ref[rows, :]
        if j == 0:
            k_prev, v_prev = kp_ref[...], vp_ref[...]
            allowed = band & ((kj >= BAND) | (i > 0))
        else:
            prev_rows = slice((j - 1) * BAND, j * BAND)
            k_prev, v_prev = k_ref[prev_rows, :], v_ref[prev_rows, :]
            allowed = band
        k = jnp.concatenate([k_prev, k_ref[rows, :]], axis=0)
        v = jnp.concatenate([v_prev, v_ref[rows, :]], axis=0)
        outs, lses = [], []
        for h in range(2):
            qh = jnp.where(_head_mask(q.shape, h), q, jnp.zeros_like(q))
            s = jnp.where(allowed, _dot_nt(qh, k), NEG)
            m = jnp.max(s, axis=1, keepdims=True)
            p = jnp.exp(s - m)
            l = jnp.sum(p, axis=1, keepdims=True)
            outs.append(jnp.dot(p.astype(BF16), v, preferred_element_type=F32) / l)
            lses.append(jnp.broadcast_to(m + jnp.log(l), (BAND, LANES)))
        o_ref[rows, :] = jnp.where(first, outs[0], outs[1]).astype(o_ref.dtype)
        lse_ref[rows, :] = jnp.where(first, lses[0], lses[1])


def _dilated_attention(q4, k4, v4):
    batch, dilation, steps, _ = q4.shape
    tq = min(steps, DIL_TILE)
    per_tile = tq // BAND
    cur = pl.BlockSpec((None, None, tq, LANES), lambda b, r, i: (b, r, i, 0))
    prev = pl.BlockSpec((None, None, BAND, LANES), lambda b, r, i: (b, r, jnp.maximum(i * per_tile - 1, 0), 0))
    return pl.pallas_call(
        _dil_kernel,
        grid=(batch, dilation, steps // tq),
        in_specs=[cur, prev, cur, prev, cur],
        out_specs=[cur, cur],
        out_shape=[jax.ShapeDtypeStruct(q4.shape, BF16), jax.ShapeDtypeStruct(q4.shape, F32)],
        compiler_params=_cparams("parallel", "parallel", "arbitrary"),
        name=f"dilated_attn_d{dilation}",
    )(q4, k4, k4, v4, v4)


def _kmean_kernel(k_ref, hi_ref, lo_ref):
    k = k_ref[...].astype(F32)
    s, w = k.shape
    mean = jnp.sum(k.reshape(s // MOBA_BLOCK, MOBA_BLOCK, w), axis=1) / MOBA_BLOCK
    hi = mean.astype(BF16)
    hi_ref[...] = hi
    lo_ref[...] = (mean - hi.astype(F32)).astype(BF16)


def _moba_kmean(km3):
    batch, seq, w = km3.shape
    nb = seq // MOBA_BLOCK
    return pl.pallas_call(
        _kmean_kernel,
        grid=(batch,),
        in_specs=[pl.BlockSpec((None, seq, w), lambda b: (b, 0, 0))],
        out_specs=[pl.BlockSpec((None, nb, w), lambda b: (b, 0, 0))] * 2,
        out_shape=[jax.ShapeDtypeStruct((batch, nb, w), BF16)] * 2,
        compiler_params=_cparams("parallel"),
        name="moba_kmean",
    )(km3)


def _moba_kernel(q_ref, k_ref, v_ref, kmh_ref, kml_ref, o_ref, qaug, kaug, vaug, m_scr, acc_scr):
    s = pl.program_id(2)
    nsteps = pl.num_programs(2)
    nb = kmh_ref.shape[0]
    tq = MOBA_BLOCK
    lane = lax.broadcasted_iota(jnp.int32, (tq, LANES), 1)
    blk = lax.broadcasted_iota(jnp.int32, (nb, tq), 0)
    slot = s % 2

    def select(step, dst_slot):
        for t in range(2):
            qi = step if t == 0 else nb - 1 - step
            q = q_ref[pl.ds(pl.multiple_of(qi * tq, tq), tq), :]
            for h in range(2):
                hm = _head_mask(q.shape, h)
                qh = jnp.where(hm, q, jnp.zeros_like(q))
                gate = _dot_nt(kmh_ref[...], qh) + _dot_nt(kml_ref[...], qh)
                cnt = jnp.zeros((nb, tq), jnp.int32)
                for jp in range(nb):
                    g_jp = gate[jp:jp + 1, :]
                    beats = (g_jp > gate) | ((g_jp == gate) & (blk > jp))
                    cnt = cnt + jnp.where(beats & (qi > jp), 1, 0)
                sel = ((blk < qi) & (cnt < MOBA_TOPK)) | (blk == qi)
                bias_t = jnp.where(sel, 0.0, NEG).astype(F32)
                spare = HEAD_DIM * (1 - h)
                pieces = [jnp.zeros((spare, tq), F32)] if spare else []
                pieces += [bias_t, jnp.zeros((LANES - spare - nb, tq), F32)]
                qaug[dst_slot, t, h] = jnp.where(hm, q, jnp.concatenate(pieces, axis=0).T.astype(BF16))

    @pl.when(s == 0)
    def _():
        for h in range(2):
            spare = HEAD_DIM * (1 - h)
            for j in range(nb):
                onehot = jnp.where(lane == spare + j, 1.0, 0.0).astype(BF16)
                kaug[h, j * tq:(j + 1) * tq, :] = jnp.where(
                    _head_mask((tq, LANES), h), k_ref[j * tq:(j + 1) * tq, :], onehot)
                vaug[h, j * tq:(j + 1) * tq, :] = jnp.where(
                    _head_mask((tq, LANES), h), v_ref[j * tq:(j + 1) * tq, :], jnp.ones((tq, LANES), BF16))
        select(0, 0)

    for t in range(2):
        for h in range(2):
            m_scr[t, h] = jnp.full((tq, LANES), NEG, F32)
            acc_scr[t, h] = jnp.zeros((tq, LANES), F32)

    row = lax.broadcasted_iota(jnp.int32, (tq, tq), 0)
    col = lax.broadcasted_iota(jnp.int32, (tq, tq), 1)
    causal_bias = jnp.where(col <= row, 0.0, NEG).astype(F32)
    starts = [pl.multiple_of(s * tq, tq), pl.multiple_of((nb - 1 - s) * tq, tq)]

    for it in range(nb + 1):
        if it < 2:
            t, kstart = it, starts[it]
        else:
            first = it - 2 < s
            t = jnp.where(first, 0, 1)
            kstart = pl.multiple_of(jnp.where(first, it - 2, it - 2 - s) * tq, tq)
        for h in range(2):
            sc = _dot_nt(qaug[slot, t, h], kaug[h, pl.ds(kstart, tq), :])
            if it < 2:
                sc = sc + causal_bias
            m_old = m_scr[t, h]
            m_new = jnp.maximum(m_old, jnp.max(sc, axis=1, keepdims=True))
            p = jnp.exp(sc - jnp.concatenate([m_new, m_new], axis=1))
            acc_scr[t, h] = (jnp.exp(m_old - m_new) * acc_scr[t, h]
                             + jnp.dot(p.astype(BF16), vaug[h, pl.ds(kstart, tq), :], preferred_element_type=F32))
            m_scr[t, h] = m_new

    first_head = _head_mask((tq, LANES), 0)
    for t in range(2):
        acc = jnp.where(first_head, acc_scr[t, 0], acc_scr[t, 1])
        den = jnp.where(first_head, pltpu.roll(acc_scr[t, 0], HEAD_DIM, 1), pltpu.roll(acc_scr[t, 1], HEAD_DIM, 1))
        o_ref[pl.ds(starts[t], tq), :] = (acc / den).astype(o_ref.dtype)

    select(jnp.minimum(s + 1, nsteps - 1), 1 - slot)


def _moba_attention(qm3, km3, vm3, kmh, kml):
    batch, seq, w = qm3.shape
    nb = seq // MOBA_BLOCK
    assert nb % 2 == 0 and nb <= HEAD_DIM
    npair = w // LANES
    tq = MOBA_BLOCK
    seq_spec = pl.BlockSpec((None, seq, LANES), lambda b, p, i: (b, 0, p))
    km_spec = pl.BlockSpec((None, nb, LANES), lambda b, p, i: (b, 0, p))
    state = pltpu.VMEM((2, 2, tq, LANES), F32)
    return pl.pallas_call(
        _moba_kernel,
        grid=(batch, npair, nb // 2),
        in_specs=[seq_spec, seq_spec, seq_spec, km_spec, km_spec],
        out_specs=seq_spec,
        out_shape=jax.ShapeDtypeStruct((batch, seq, w), BF16),
        scratch_shapes=[pltpu.VMEM((2, 2, 2, tq, LANES), BF16), pltpu.VMEM((2, seq, LANES), BF16),
                        pltpu.VMEM((2, seq, LANES), BF16), state, state],
        compiler_params=_cparams("parallel", "parallel", "arbitrary"),
        name="moba_attn",
    )(qm3, km3, vm3, kmh, kml)


def _memkv_kernel(mem_ref, w_ref, k_ref, v_ref):
    kv = jnp.dot(mem_ref[...].astype(BF16), w_ref[...], preferred_element_type=F32)
    k_ref[...] = kv[:, :W_MEM].astype(BF16)
    v_ref[...] = kv[:, W_MEM:].astype(BF16)


def _mem_kv(mem, w_kv_b):
    batch, m, d = mem.shape
    return pl.pallas_call(
        _memkv_kernel,
        grid=(batch,),
        in_specs=[pl.BlockSpec((None, m, d), lambda b: (b, 0, 0)),
                  pl.BlockSpec((d, 2 * W_MEM), lambda b: (0, 0))],
        out_specs=[pl.BlockSpec((None, m, W_MEM), lambda b: (b, 0, 0))] * 2,
        out_shape=[jax.ShapeDtypeStruct((batch, m, W_MEM), BF16)] * 2,
        compiler_params=_cparams("parallel"),
        name="mem_kv_proj",
    )(mem, w_kv_b)


def _memattn_kernel(q_ref, k_ref, v_ref, o_ref):
    q = q_ref[...]
    k = k_ref[...]
    v = v_ref[...]
    outs = []
    for h in range(2):
        qh = jnp.where(_head_mask(q.shape, h), q, jnp.zeros_like(q))
        s = _dot_nt(qh, k)
        m = jnp.max(s, axis=1, keepdims=True)
        p = jnp.exp(s - m)
        l = jnp.sum(p, axis=1, keepdims=True)
        outs.append(jnp.dot(p.astype(BF16), v, preferred_element_type=F32) / l)
    o_ref[...] = jnp.where(_head_mask(outs[0].shape, 0), outs[0], outs[1]).astype(o_ref.dtype)


def _mem_attention(qx3, k_mem, v_mem, tq=512):
    batch, seq, w = qx3.shape
    m = k_mem.shape[1]
    return pl.pallas_call(
        _memattn_kernel,
        grid=(batch, w // LANES, seq // tq),
        in_specs=[pl.BlockSpec((None, tq, LANES), lambda b, p, i: (b, i, p)),
                  pl.BlockSpec((None, m, LANES), lambda b, p, i: (b, 0, p)),
                  pl.BlockSpec((None, m, LANES), lambda b, p, i: (b, 0, p))],
        out_specs=pl.BlockSpec((None, tq, LANES), lambda b, p, i: (b, i, p)),
        out_shape=jax.ShapeDtypeStruct((batch, seq, w), BF16),
        compiler_params=_cparams("parallel", "parallel", "parallel"),
        name="mem_attn",
    )(qx3, k_mem, v_mem)


def _merge_kernel(alpha, x_ref, o1_ref, o2_ref, o3_ref, l1_ref, l2_ref, l3_ref, ym_ref, yx_ref,
                  wg_ref, bg_ref, wbd_ref, wbm_ref, wbx_ref, wo_ref, g_ref, b_ref, out_ref, packed_ref, stage):
    x = x_ref[...]
    xb = x.astype(BF16)
    tm, d = x.shape

    def token_major(ref, slot):
        dilation = ref.shape[0]
        if dilation == 1:
            return ref[0].astype(F32)
        for r in range(dilation):
            stage[slot, pl.ds(r, tm // dilation, stride=dilation), :] = ref[r].astype(F32)
        return stage[slot]

    o1, o2, o3 = (token_major(r, s) for s, r in enumerate((o1_ref, o2_ref, o3_ref)))
    l1, l2, l3 = (token_major(r, s + 3) for s, r in enumerate((l1_ref, l2_ref, l3_ref)))
    mx = jnp.maximum(jnp.maximum(l1, l2), l3)
    e1, e2, e3 = jnp.exp(l1 - mx), jnp.exp(l2 - mx), jnp.exp(l3 - mx)
    y_dil = (e1 * o1 + e2 * o2 + e3 * o3) / (e1 + e2 + e3)
    branches = (
        jnp.dot(y_dil.astype(BF16), wbd_ref[...], preferred_element_type=F32),
        jnp.dot(ym_ref[...], wbm_ref[...], preferred_element_type=F32),
        jnp.dot(yx_ref[...], wbx_ref[...], preferred_element_type=F32),
    )
    merged = jnp.zeros_like(x)
    for i, br in enumerate(branches):
        logits = jnp.dot(xb, wg_ref[:, i * d:(i + 1) * d], preferred_element_type=F32) + bg_ref[:, i * d:(i + 1) * d]
        merged = merged + jax.nn.sigmoid(logits) * br
    mix = jnp.dot(merged.astype(BF16), wo_ref[...], preferred_element_type=F32)
    x1 = _layer_norm(alpha * x + mix, g_ref[...], b_ref[...])
    out_ref[...] = x1
    packed_ref[...] = _pack_halves(x1)


def _merge(alpha, x2, o_dil, lse_dil, y_moba, y_mem, wg, bg, wbd, wbm, wbx, wo, g, b, seq, tm=512):
    n, d = x2.shape
    tiles = seq // tm
    row = lambda w: pl.BlockSpec((tm, w), lambda i: (i, 0))
    full = lambda a: pl.BlockSpec(a.shape, lambda i: (0, 0))
    residue_major = lambda a: pl.BlockSpec((None, a.shape[1], tm // a.shape[1], LANES),
                                           lambda i: (i // tiles, 0, i % tiles, 0))
    weights = (wg, bg, wbd, wbm, wbx, wo, g, b)
    return pl.pallas_call(
        functools.partial(_merge_kernel, alpha),
        grid=(n // tm,),
        in_specs=([row(d)] + [residue_major(a) for a in (*o_dil, *lse_dil)] + [row(W_MOBA), row(W_MEM)]
                  + [full(a) for a in weights]),
        out_specs=[row(d), row(PACKED)],
        out_shape=[jax.ShapeDtypeStruct((n, d), F32), jax.ShapeDtypeStruct((n, PACKED), jnp.uint32)],
        scratch_shapes=[pltpu.VMEM((2 * len(DIL_PAIRS), tm, LANES), F32)],
        compiler_params=_cparams("parallel"),
        name="merge_outproj_ln1",
    )(x2, *o_dil, *lse_dil, y_moba, y_mem, *weights)


def _token_mixer(x2, mem, positions, w_in, w_mem_kv, w_gate, b_gate, w_br_dil, w_br_moba, w_br_mem,
                 w_out, ln1_g, ln1_b, batch, seq, alpha):
    cos_t, sin_t = _rope_tables(positions)
    (q_dil, k_dil, v_dil), (qm, km, vm, qx) = _in_projection(x2, w_in.astype(BF16), cos_t, sin_t, batch, seq)
    o_dil, lse_dil = [], []
    for g in range(len(DIL_PAIRS)):
        o, lse = _dilated_attention(q_dil[g], k_dil[g], v_dil[g])
        o_dil.append(o)
        lse_dil.append(lse)
    qm3, km3, vm3 = (t.reshape(batch, seq, W_MOBA) for t in (qm, km, vm))
    kmh, kml = _moba_kmean(km3)
    y_moba = _moba_attention(qm3, km3, vm3, kmh, kml).reshape(batch * seq, W_MOBA)
    k_mem, v_mem = _mem_kv(mem, w_mem_kv.astype(BF16))
    y_mem = _mem_attention(qx.reshape(batch, seq, W_MEM), k_mem, v_mem).reshape(batch * seq, W_MEM)
    return _merge(alpha, x2, o_dil, lse_dil, y_moba, y_mem,
                  w_gate.astype(BF16), b_gate.reshape(1, -1), w_br_dil.astype(BF16),
                  w_br_moba.astype(BF16), w_br_mem.astype(BF16), w_out.astype(BF16),
                  ln1_g.reshape(1, -1), ln1_b.reshape(1, -1), seq)


EXPERT_ROWS = 256
EXPERT_RING = 6
TOKEN_TILE = 256
PACKED = 512
SC_WINDOW = 128


def _first_index_of_max(v, iota_f, size):
    m = jnp.max(v, axis=0, keepdims=True)
    idx = jnp.min(jnp.where(v == m, iota_f, float(size)), axis=0, keepdims=True)
    return m, idx


def _route_kernel(x_ref, wh_ref, wl_ref, bias_ref, eidx_ref, gw_ref, rank_ref, cnt_ref, carry):
    step = pl.program_id(0)

    @pl.when(step == 0)
    def _():
        carry[...] = jnp.zeros_like(carry)

    x = x_ref[...]
    tm = x.shape[0]
    xh = x.astype(BF16)
    xl = (x - xh.astype(F32)).astype(BF16)
    wh = wh_ref[...]
    logits = _dot_nt(wh, xh) + _dot_nt(wh, xl) + _dot_nt(wl_ref[...], xh)
    scores = jax.nn.sigmoid(logits)
    biased = scores + bias_ref[...][:, :1]

    giota = lax.broadcasted_iota(jnp.int32, (GROUP_SIZE, tm), 0).astype(F32)
    group_scores = []
    for g in range(N_GROUPS):
        slab = biased[g * GROUP_SIZE:(g + 1) * GROUP_SIZE, :]
        m1, i1 = _first_index_of_max(slab, giota, GROUP_SIZE)
        m2 = jnp.max(jnp.where(giota == i1, -jnp.inf, slab), axis=0, keepdims=True)
        group_scores.append(m1 + m2)
    gs = jnp.concatenate(group_scores, axis=0)
    gidx = lax.broadcasted_iota(jnp.int32, (N_GROUPS, tm), 0)
    beaten = jnp.zeros((N_GROUPS, tm), jnp.int32)
    for gp in range(N_GROUPS):
        row = gs[gp:gp + 1, :]
        beaten = beaten + jnp.where((row > gs) | ((row == gs) & (gidx > gp)), 1, 0)
    keep = beaten < TOPK_GROUPS
    masked = jnp.concatenate(
        [jnp.where(keep[g:g + 1, :], biased[g * GROUP_SIZE:(g + 1) * GROUP_SIZE, :], -jnp.inf)
         for g in range(N_GROUPS)], axis=0)

    eiota = lax.broadcasted_iota(jnp.int32, (N_EXPERTS, tm), 0).astype(F32)
    idx_rows, gw_rows = [], []
    for _ in range(TOP_K):
        _, idx = _first_index_of_max(masked, eiota, N_EXPERTS)
        hit = eiota == idx
        gw_rows.append(jnp.sum(jnp.where(hit, scores, 0.0), axis=0, keepdims=True))
        masked = jnp.where(hit, -jnp.inf, masked)
        idx_rows.append(idx)
    idx8 = jnp.concatenate(idx_rows, axis=0)
    gw8 = jnp.concatenate(gw_rows, axis=0)
    gw_ref[...] = gw8 / jnp.sum(gw8, axis=0, keepdims=True) * ROUTED_SCALE
    eidx_ref[...] = idx8.astype(jnp.int32)

    onehot = jnp.zeros((N_EXPERTS, tm), F32)
    for k in range(TOP_K):
        onehot = onehot + jnp.where(eiota == idx8[k:k + 1, :], 1.0, 0.0)
    t_row = lax.broadcasted_iota(jnp.int32, (tm, tm), 0)
    t_col = lax.broadcasted_iota(jnp.int32, (tm, tm), 1)
    earlier = jnp.where(t_row < t_col, 1.0, 0.0).astype(BF16)
    prefix = jnp.dot(onehot.astype(BF16), earlier, preferred_element_type=F32)
    base = carry[...]
    prefix = prefix + jnp.concatenate([base] * (tm // LANES), axis=1)
    rank_rows = [jnp.sum(jnp.where(eiota == idx8[k:k + 1, :], prefix, 0.0), axis=0, keepdims=True)
                 for k in range(TOP_K)]
    rank_ref[...] = jnp.concatenate(rank_rows, axis=0).astype(jnp.int32)
    total = base + jnp.sum(onehot, axis=1, keepdims=True)
    carry[...] = total
    cnt_ref[...] = total.astype(jnp.int32)


def _route(x1, wr_hi_t, wr_lo_t, bias_b, tm=512):
    n, d = x1.shape
    tok = lambda dt: jax.ShapeDtypeStruct((TOP_K, n), dt)
    return pl.pallas_call(
        _route_kernel,
        grid=(n // tm,),
        in_specs=[pl.BlockSpec((tm, d), lambda i: (i, 0)),
                  pl.BlockSpec((N_EXPERTS, d), lambda i: (0, 0)),
                  pl.BlockSpec((N_EXPERTS, d), lambda i: (0, 0)),
                  pl.BlockSpec((N_EXPERTS, LANES), lambda i: (0, 0))],
        out_specs=[pl.BlockSpec((TOP_K, tm), lambda i: (0, i)),
                   pl.BlockSpec((TOP_K, tm), lambda i: (0, i)),
                   pl.BlockSpec((TOP_K, tm), lambda i: (0, i)),
                   pl.BlockSpec((N_EXPERTS, LANES), lambda i: (0, 0))],
        out_shape=[tok(jnp.int32), tok(F32), tok(jnp.int32),
                   jax.ShapeDtypeStruct((N_EXPERTS, LANES), jnp.int32)],
        scratch_shapes=[pltpu.VMEM((N_EXPERTS, LANES), F32)],
        compiler_params=_cparams("arbitrary"),
        name="moe_route",
    )(x1, wr_hi_t, wr_lo_t, bias_b)


def _dest_kernel(eidx_ref, rank_ref, start_ref, dest_ref):
    eidx = eidx_ref[...]
    tm = eidx.shape[1]
    eiota = lax.broadcasted_iota(jnp.int32, (N_EXPERTS, tm), 0)
    start = start_ref[...][:, :1]
    rows = [jnp.sum(jnp.where(eiota == eidx[k:k + 1, :], start, 0.0), axis=0, keepdims=True)
            for k in range(TOP_K)]
    dest_ref[...] = jnp.concatenate(rows, axis=0).astype(jnp.int32) + rank_ref[...]


def _dest_rows(eidx, rank, start_b):
    n = eidx.shape[1]
    nt = n // TOKEN_TILE
    return pl.pallas_call(
        _dest_kernel,
        grid=(nt,),
        in_specs=[pl.BlockSpec((TOP_K, TOKEN_TILE), lambda i: (0, i)),
                  pl.BlockSpec((TOP_K, TOKEN_TILE), lambda i: (0, i)),
                  pl.BlockSpec((N_EXPERTS, LANES), lambda i: (0, 0))],
        out_specs=pl.BlockSpec((TOP_K, TOKEN_TILE), lambda i: (0, i)),
        out_shape=jax.ShapeDtypeStruct((TOP_K, n), jnp.int32),
        compiler_params=_cparams("parallel"),
        name="moe_dest_rows",
    )(eidx, rank, start_b)


def _pack_halves(v):
    return pltpu.pack_elementwise([v[:, :PACKED], v[:, PACKED:]], packed_dtype=BF16)


def _unpack_half(p, index):
    return pltpu.unpack_elementwise(p, index=index, packed_dtype=BF16, unpacked_dtype=F32)


def _sc_mesh():
    return plsc.VectorSubcoreMesh(core_axis_name="core", subcore_axis_name="subcore")


def _sc_move_rows(src, src_idx, dst_idx, out_rows, name):
    width = src.shape[1]
    count = src_idx.shape[1]

    @functools.partial(pl.kernel, out_type=jax.ShapeDtypeStruct((out_rows, width), src.dtype),
                       mesh=_sc_mesh(), scratch_types=[pltpu.VMEM((SC_WINDOW, width), src.dtype)], name=name)
    def move(src_hbm, sidx_hbm, didx_hbm, out_hbm, rows_vmem):
        def body(sidx_vmem, didx_vmem):
            pltpu.sync_copy(src_hbm.at[sidx_vmem.at[0]], rows_vmem)
            pltpu.sync_copy(rows_vmem, out_hbm.at[didx_vmem.at[0]])

        idx_spec = pl.BlockSpec((1, SC_WINDOW), index_map=lambda i: (0, i))
        pltpu.emit_pipeline(
            body,
            grid=(count // SC_WINDOW,),
            in_specs=[idx_spec, idx_spec],
            out_specs=[],
            core_axis_name=("core", "subcore"),
            dimension_semantics=(pltpu.PARALLEL,),
        )(sidx_hbm, didx_hbm)

    return move(src, src_idx, dst_idx)


def _sc_dispatch_rows(src, token_idx, dest, out_rows):
    n, width = src.shape
    slots = dest.shape[0]

    @functools.partial(pl.kernel, out_type=jax.ShapeDtypeStruct((out_rows, width), src.dtype),
                       mesh=_sc_mesh(), scratch_types=[pltpu.VMEM((SC_WINDOW, width), src.dtype)],
                       name="moe_dispatch_sc")
    def dispatch(src_hbm, tidx_hbm, dest_hbm, out_hbm, rows_vmem):
        def body(tidx_vmem, dest_vmem):
            pltpu.sync_copy(src_hbm.at[tidx_vmem.at[0]], rows_vmem)
            for k in range(slots):
                pltpu.sync_copy(rows_vmem, out_hbm.at[dest_vmem.at[k]])

        pltpu.emit_pipeline(
            body,
            grid=(n // SC_WINDOW,),
            in_specs=[pl.BlockSpec((1, SC_WINDOW), index_map=lambda i: (0, i)),
                      pl.BlockSpec((slots, SC_WINDOW), index_map=lambda i: (0, i))],
            out_specs=[],
            core_axis_name=("core", "subcore"),
            dimension_semantics=(pltpu.PARALLEL,),
        )(tidx_hbm, dest_hbm)

    return dispatch(src, token_idx, dest)


def _expert_kernel(first_blk_ref, count_ref, total_ref, xs_hbm, wg_ref, wu_ref, wd_ref, ys_hbm,
                   wg_b, wu_b, wd_b, x_ring, y_ring, in_sem, out_sem):
    e = pl.program_id(0)
    total = total_ref[0]
    first = first_blk_ref[e]
    count = count_ref[e]
    nblk = (count + EXPERT_ROWS - 1) // EXPERT_ROWS
    spare_block = ys_hbm.shape[0] // EXPERT_ROWS - EXPERT_RING

    def rows_of(g):
        return pl.ds(pl.multiple_of(g * EXPERT_ROWS, EXPERT_ROWS), EXPERT_ROWS)

    def in_copy(g):
        slot = g % EXPERT_RING
        return pltpu.make_async_copy(xs_hbm.at[rows_of(g)], x_ring.at[slot], in_sem.at[slot])

    def out_copy(g, slot):
        return pltpu.make_async_copy(y_ring.at[slot], ys_hbm.at[rows_of(g)], out_sem.at[slot])

    @pl.when(e == 0)
    def _():
        y_ring[...] = jnp.zeros_like(y_ring)
        for g in range(EXPERT_RING - 1):
            in_copy(g).start()
        for slot in range(EXPERT_RING):
            out_copy(spare_block + slot, slot).start()

    @pl.when(nblk > 0)
    def _():
        wg_b[...] = wg_ref[...].astype(BF16)
        wu_b[...] = wu_ref[...].astype(BF16)
        wd_b[...] = wd_ref[...].astype(BF16)

    def blocks(js):
        gs = [first + j for j in js]
        for g in gs:
            in_copy(g).wait()
        ys = []
        for j, g in zip(js, gs):
            p = x_ring[g % EXPERT_RING]
            row = lax.broadcasted_iota(jnp.int32, p.shape, 0) + j * EXPERT_ROWS
            p = jnp.where(row < count, p, jnp.zeros_like(p))
            lo = _unpack_half(p, 0).astype(BF16)
            hi = _unpack_half(p, 1).astype(BF16)

            def up(w, lo=lo, hi=hi):
                return (jnp.dot(lo, w[:PACKED, :], preferred_element_type=F32)
                        + jnp.dot(hi, w[PACKED:, :], preferred_element_type=F32))

            hid = (jax.nn.silu(up(wg_b)) * up(wu_b)).astype(BF16)
            ys.append(_pack_halves(jnp.dot(hid, wd_b[...], preferred_element_type=F32)))
        for g in gs:
            in_copy(g + EXPERT_RING - 1).start()
        for g in gs:
            out_copy(g, g % EXPERT_RING).wait()
        for g, y in zip(gs, ys):
            y_ring[g % EXPERT_RING] = y
        for g in gs:
            out_copy(g, g % EXPERT_RING).start()

    def pair(i, carry):
        blocks([2 * i, 2 * i + 1])
        return carry

    lax.fori_loop(0, nblk // 2, pair, 0)

    @pl.when(nblk % 2 == 1)
    def _():
        blocks([nblk - 1])

    @pl.when(e == pl.num_programs(0) - 1)
    def _():
        for slot in range(EXPERT_RING):
            out_copy(spare_block + slot, slot).wait()
        for g in range(EXPERT_RING - 1):
            in_copy(total + g).wait()


def _expert_ffn(xs, first_blk, counts, total, w_e_gate, w_e_up, w_e_down):
    rows = xs.shape[0] + EXPERT_ROWS
    n_experts, d = w_e_gate.shape[0], w_e_gate.shape[1]
    w_map = lambda e, fb, ct, tt: (e, 0, 0)
    ring = pltpu.VMEM((EXPERT_RING, EXPERT_ROWS, PACKED), jnp.uint32)
    return pl.pallas_call(
        _expert_kernel,
        grid_spec=pltpu.PrefetchScalarGridSpec(
            num_scalar_prefetch=3,
            grid=(n_experts,),
            in_specs=[pl.BlockSpec(memory_space=pl.ANY),
                      pl.BlockSpec((None, d, D_EXPERT), w_map),
                      pl.BlockSpec((None, d, D_EXPERT), w_map),
                      pl.BlockSpec((None, D_EXPERT, d), w_map)],
            out_specs=pl.BlockSpec(memory_space=pl.ANY),
            scratch_shapes=[pltpu.VMEM((d, D_EXPERT), BF16), pltpu.VMEM((d, D_EXPERT), BF16),
                            pltpu.VMEM((D_EXPERT, d), BF16), ring, ring,
                            pltpu.SemaphoreType.DMA((EXPERT_RING,)), pltpu.SemaphoreType.DMA((EXPERT_RING,))],
        ),
        out_shape=jax.ShapeDtypeStruct((rows, PACKED), jnp.uint32),
        compiler_params=_cparams("arbitrary"),
        name="moe_expert_ffn",
    )(first_blk, counts, total, xs, w_e_gate, w_e_up, w_e_down)


def _combine_kernel(alpha, x_ref, gw_ref, ys_ref, wsg_ref, wsu_ref, wsd_ref, g_ref, b_ref, out_ref):
    x = x_ref[...]
    xb = x.astype(BF16)
    hid = (jax.nn.silu(jnp.dot(xb, wsg_ref[...], preferred_element_type=F32))
           * jnp.dot(xb, wsu_ref[...], preferred_element_type=F32)).astype(BF16)
    shared = jnp.dot(hid, wsd_ref[...], preferred_element_type=F32)
    gw = gw_ref[...]
    lo = jnp.zeros((TOKEN_TILE, PACKED), F32)
    hi = jnp.zeros((TOKEN_TILE, PACKED), F32)
    for k in range(TOP_K):
        p = ys_ref[k]
        w = gw[:, k:k + 1]
        lo = lo + w * _unpack_half(p, 0)
        hi = hi + w * _unpack_half(p, 1)
    routed = jnp.concatenate([lo, hi], axis=1)
    out_ref[...] = _layer_norm(alpha * x + (routed + shared), g_ref[...], b_ref[...])


def _combine(alpha, x1, gw_t, ys_tok, wsg, wsu, wsd, g, b):
    n, d = x1.shape
    full = lambda a: pl.BlockSpec(a.shape, lambda i: (0, 0))
    weights = (wsg, wsu, wsd, g, b)
    return pl.pallas_call(
        functools.partial(_combine_kernel, alpha),
        grid=(n // TOKEN_TILE,),
        in_specs=[pl.BlockSpec((TOKEN_TILE, d), lambda i: (i, 0)),
                  pl.BlockSpec((TOKEN_TILE, TOP_K), lambda i: (i, 0)),
                  pl.BlockSpec((TOP_K, TOKEN_TILE, PACKED), lambda i: (0, i, 0))] + [full(a) for a in weights],
        out_specs=pl.BlockSpec((TOKEN_TILE, d), lambda i: (i, 0)),
        out_shape=jax.ShapeDtypeStruct((n, d), F32),
        compiler_params=_cparams("parallel"),
        name="moe_combine_shared_ln2",
    )(x1, gw_t, ys_tok, *weights)


def _moe_layer(x1, x1_packed, w_router, router_bias, w_e_gate, w_e_up, w_e_down, w_s_gate, w_s_up, w_s_down,
               ln2_g, ln2_b, alpha):
    n, d = x1.shape
    wr_t = w_router.T
    wr_hi = wr_t.astype(BF16)
    wr_lo = (wr_t - wr_hi.astype(F32)).astype(BF16)
    bias_b = jnp.broadcast_to(router_bias.astype(F32)[:, None], (N_EXPERTS, LANES))
    eidx, gw, rank, cnt = _route(x1, wr_hi, wr_lo, bias_b)

    counts = cnt[:, 0]
    padded = (counts + EXPERT_ROWS - 1) // EXPERT_ROWS * EXPERT_ROWS
    seg_end = jnp.cumsum(padded)
    seg_start = seg_end - padded
    rows = n * TOP_K + N_EXPERTS * EXPERT_ROWS
    first_blk = (seg_start // EXPERT_ROWS).astype(jnp.int32)
    total_blk = (seg_end[-1:] // EXPERT_ROWS).astype(jnp.int32)
    start_b = jnp.broadcast_to(seg_start.astype(F32)[:, None], (N_EXPERTS, LANES))

    dest = _dest_rows(eidx, rank, start_b)
    dest_flat = dest.reshape(1, TOP_K * n)
    assign = jnp.arange(TOP_K * n, dtype=jnp.int32).reshape(1, TOP_K * n)
    xs = _sc_dispatch_rows(x1_packed, assign[:, :n], dest, rows + (EXPERT_RING - 1) * EXPERT_ROWS)
    ys = _expert_ffn(xs, first_blk, counts, total_blk, w_e_gate, w_e_up, w_e_down)
    ys_tok = _sc_move_rows(ys, dest_flat, assign, TOP_K * n, "moe_gather_sc").reshape(TOP_K, n, PACKED)
    return _combine(alpha, x1, gw.T, ys_tok, w_s_gate.astype(BF16), w_s_up.astype(BF16),
                    w_s_down.astype(BF16), ln2_g.reshape(1, -1), ln2_b.reshape(1, -1))


def kernel(x, mem, positions, w_in, w_mem_kv, w_gate, b_gate, w_br_dil, w_br_moba, w_br_mem, w_out, ln1_g, ln1_b, w_router, router_bias, w_e_gate, w_e_up, w_e_down, w_s_gate, w_s_up, w_s_down, ln2_g, ln2_b):
    batch, seq, d = x.shape
    depth = w_in.shape[0]
    alpha = (2.0 * depth) ** 0.25
    h = x.reshape(batch * seq, d)
    for l in range(depth):
        h, h_packed = _token_mixer(h, mem, positions, w_in[l], w_mem_kv[l], w_gate[l], b_gate[l], w_br_dil[l],
                                   w_br_moba[l], w_br_mem[l], w_out[l], ln1_g[l], ln1_b[l], batch, seq, alpha)
        h = _moe_layer(h, h_packed, w_router[l], router_bias[l], w_e_gate[l], w_e_up[l], w_e_down[l],
                       w_s_gate[l], w_s_up[l], w_s_down[l], ln2_g[l], ln2_b[l], alpha)
    return h.reshape(batch, seq, d)
```

```python
import functools

import jax
import jax.numpy as jnp
from jax import lax
from jax.experimental import pallas as pl
from jax.experimental.pallas import tpu as pltpu
from jax.experimental.pallas import tpu_sc as plsc

F32 = jnp.float32
BF16 = jnp.bfloat16

LANES = 128
VMEM_LIMIT_BYTES = 48 * 1024 * 1024

HEAD_DIM = 64
ROPE_DIM = HEAD_DIM // 4
ROPE_HALF = ROPE_DIM // 2
ROPE_THETA = 500000.0
DIL_PAIRS = ((128, 1), (512, 4), (2048, 16))
BAND = 128
DIL_TILE = 512
W_DIL = 384
W_MOBA = 384
W_MEM = 256
MOBA_BLOCK = 256
MOBA_TOPK = 3
N_EXPERTS = 256
N_GROUPS = 8
GROUP_SIZE = N_EXPERTS // N_GROUPS
TOPK_GROUPS = 4
TOP_K = 8
D_EXPERT = 256
ROUTED_SCALE = 2.5
LN_EPS = 1e-5
NEG = -1e30
QK_SCALE = HEAD_DIM ** -0.5


def _cparams(*sem):
    return pltpu.CompilerParams(dimension_semantics=sem, vmem_limit_bytes=VMEM_LIMIT_BYTES)


def _dot_nt(a, b):
    return lax.dot_general(a, b, (((1,), (1,)), ((), ())), preferred_element_type=F32)


def _layer_norm(v, g, b):
    mu = jnp.mean(v, axis=-1, keepdims=True)
    c = v - mu
    var = jnp.mean(c * c, axis=-1, keepdims=True)
    return c * lax.rsqrt(var + LN_EPS) * g + b


_IN_SECTIONS = (
    (0, W_DIL, True, True),
    (W_DIL, W_DIL, True, False),
    (2 * W_DIL, W_DIL, False, False),
    (3 * W_DIL, W_MOBA, True, True),
    (3 * W_DIL + W_MOBA, W_MOBA, True, False),
    (3 * W_DIL + 2 * W_MOBA, W_MOBA, False, False),
    (3 * W_DIL + 3 * W_MOBA, W_MEM, False, True),
)


N_DIL_SECTIONS = 3


def _inproj_kernel(x_ref, w_ref, cos_ref, sin_ref, *refs):
    n_dil = N_DIL_SECTIONS * len(DIL_PAIRS)
    dil_refs, flat_refs, stage = refs[:n_dil], refs[n_dil:-1], refs[-1]
    xb = x_ref[...].astype(BF16)
    cos = cos_ref[...]
    sin = sin_ref[...]
    tm = xb.shape[0]
    lane = lax.broadcasted_iota(jnp.int32, (tm, LANES), 1)
    first_half = (lane % ROPE_DIM) < ROPE_HALF

    def rope(t):
        partner = jnp.where(first_half, pltpu.roll(t, LANES - ROPE_HALF, 1), pltpu.roll(t, ROPE_HALF, 1))
        return t * cos + partner * sin

    for sec, (off, width, roped, scaled) in enumerate(_IN_SECTIONS):
        acc = jnp.dot(xb, w_ref[:, off:off + width], preferred_element_type=F32)
        for c in range(width // LANES):
            t = acc[:, c * LANES:(c + 1) * LANES]
            if roped:
                t = rope(t)
            if scaled:
                t = t * QK_SCALE
            if sec >= N_DIL_SECTIONS:
                flat_refs[sec - N_DIL_SECTIONS][:, c * LANES:(c + 1) * LANES] = t.astype(BF16)
                continue
            o_ref = dil_refs[sec * len(DIL_PAIRS) + c]
            dilation = DIL_PAIRS[c][1]
            if dilation == 1:
                o_ref[0] = t.astype(BF16)
                continue
            slot = sec * len(DIL_PAIRS) + c
            stage[slot] = t
            for r in range(dilation):
                o_ref[r] = stage[slot, pl.ds(r, tm // dilation, stride=dilation), :].astype(BF16)


def _in_projection(x2, w_in_b, cos_t, sin_t, batch, seq, tm=512):
    n, d = x2.shape
    w_total = w_in_b.shape[1]
    tiles = seq // tm
    dil_specs, dil_shapes = [], []
    for _ in range(N_DIL_SECTIONS):
        for _, dilation in DIL_PAIRS:
            dil_specs.append(pl.BlockSpec((None, dilation, tm // dilation, LANES),
                                          lambda i: (i // tiles, 0, i % tiles, 0)))
            dil_shapes.append(jax.ShapeDtypeStruct((batch, dilation, seq // dilation, LANES), BF16))
    widths = [s[1] for s in _IN_SECTIONS[N_DIL_SECTIONS:]]
    outs = pl.pallas_call(
        _inproj_kernel,
        grid=(n // tm,),
        in_specs=[
            pl.BlockSpec((tm, d), lambda i: (i, 0)),
            pl.BlockSpec((d, w_total), lambda i: (0, 0)),
            pl.BlockSpec((tm, LANES), lambda i: (i, 0)),
            pl.BlockSpec((tm, LANES), lambda i: (i, 0)),
        ],
        out_specs=dil_specs + [pl.BlockSpec((tm, w), lambda i: (i, 0)) for w in widths],
        out_shape=dil_shapes + [jax.ShapeDtypeStruct((n, w), BF16) for w in widths],
        scratch_shapes=[pltpu.VMEM((N_DIL_SECTIONS * len(DIL_PAIRS), tm, LANES), F32)],
        compiler_params=_cparams("parallel"),
        name="in_proj_rope",
    )(x2, w_in_b, cos_t, sin_t)
    n_dil = len(dil_specs)
    ng = len(DIL_PAIRS)
    qkv_dil = [outs[s * ng:(s + 1) * ng] for s in range(N_DIL_SECTIONS)]
    return qkv_dil, outs[n_dil:]


def _rope_tables(positions):
    lane = jnp.arange(LANES)
    rotary = (lane % HEAD_DIM) < ROPE_DIM
    inv_freq = ROPE_THETA ** (-(lane % ROPE_HALF).astype(F32) / ROPE_HALF)
    ang = positions.reshape(-1).astype(F32)[:, None] * jnp.where(rotary, inv_freq, 0.0)
    sign = jnp.where((lane % ROPE_DIM) < ROPE_HALF, -1.0, 1.0)
    return jnp.cos(ang), jnp.sin(ang) * sign


def _head_mask(shape, h):
    lane = lax.broadcasted_iota(jnp.int32, shape, 1)
    return (lane // HEAD_DIM) == h


def _dil_kernel(q_ref, kp_ref, k_ref, vp_ref, v_ref, o_ref, lse_ref):
    i = pl.program_id(2)
    tq = q_ref.shape[0]
    qi = lax.broadcasted_iota(jnp.int32, (BAND, 2 * BAND), 0)
    kj = lax.broadcasted_iota(jnp.int32, (BAND, 2 * BAND), 1)
    dist = qi + BAND - kj
    band = (dist >= 0) & (dist <= BAND)
    first = _head_mask((BAND, LANES), 0)
    for j in range(tq // BAND):
        rows = slice(j * BAND, (j + 1) * BAND)
        q = q_ref[rows, :]
        if j == 0:
            k_prev, v_prev = kp_ref[...], vp_ref[...]
            allowed = band & ((kj >= BAND) | (i > 0))
        else:
            prev_rows = slice((j - 1) * BAND, j * BAND)
            k_prev, v_prev = k_ref[prev_rows, :], v_ref[prev_rows, :]
            allowed = band
        k = jnp.concatenate([k_prev, k_ref[rows, :]], axis=0)
        v = jnp.concatenate([v_prev, v_ref[rows, :]], axis=0)
        outs, lses = [], []
        for h in range(2):
            qh = jnp.where(_head_mask(q.shape, h), q, jnp.zeros_like(q))
            s = jnp.where(allowed, _dot_nt(qh, k), NEG)
            m = jnp.max(s, axis=1, keepdims=True)
            p = jnp.exp(s - m)
            l = jnp.sum(p, axis=1, keepdims=True)
            outs.append(jnp.dot(p.astype(BF16), v, preferred_element_type=F32) / l)
            lses.append(jnp.broadcast_to(m + jnp.log(l), (BAND, LANES)))
        o_ref[rows, :] = jnp.where(first, outs[0], outs[1]).astype(o_ref.dtype)
        lse_ref[rows, :] = jnp.where(first, lses[0], lses[1])


def _dilated_attention(q4, k4, v4):
    batch, dilation, steps, _ = q4.shape
    tq = min(steps, DIL_TILE)
    per_tile = tq // BAND
    cur = pl.BlockSpec((None, None, tq, LANES), lambda b, r, i: (b, r, i, 0))
    prev = pl.BlockSpec((None, None, BAND, LANES), lambda b, r, i: (b, r, jnp.maximum(i * per_tile - 1, 0), 0))
    return pl.pallas_call(
        _dil_kernel,
        grid=(batch, dilation, steps // tq),
        in_specs=[cur, prev, cur, prev, cur],
        out_specs=[cur, cur],
        out_shape=[jax.ShapeDtypeStruct(q4.shape, BF16), jax.ShapeDtypeStruct(q4.shape, F32)],
        compiler_params=_cparams("parallel", "parallel", "arbitrary"),
        name=f"dilated_attn_d{dilation}",
    )(q4, k4, k4, v4, v4)


def _kmean_kernel(k_ref, hi_ref, lo_ref):
    k = k_ref[...].astype(F32)
    s, w = k.shape
    mean = jnp.sum(k.reshape(s // MOBA_BLOCK, MOBA_BLOCK, w), axis=1) / MOBA_BLOCK
    hi = mean.astype(BF16)
    hi_ref[...] = hi
    lo_ref[...] = (mean - hi.astype(F32)).astype(BF16)


def _moba_kmean(km3):
    batch, seq, w = km3.shape
    nb = seq // MOBA_BLOCK
    return pl.pallas_call(
        _kmean_kernel,
        grid=(batch,),
        in_specs=[pl.BlockSpec((None, seq, w), lambda b: (b, 0, 0))],
        out_specs=[pl.BlockSpec((None, nb, w), lambda b: (b, 0, 0))] * 2,
        out_shape=[jax.ShapeDtypeStruct((batch, nb, w), BF16)] * 2,
        compiler_params=_cparams("parallel"),
        name="moba_kmean",
    )(km3)


def _moba_kernel(q_ref, k_ref, v_ref, kmh_ref, kml_ref, o_ref, qaug, kaug, vaug, m_scr, acc_scr):
    s = pl.program_id(2)
    nsteps = pl.num_programs(2)
    nb = kmh_ref.shape[0]
    tq = MOBA_BLOCK
    lane = lax.broadcasted_iota(jnp.int32, (tq, LANES), 1)
    blk = lax.broadcasted_iota(jnp.int32, (nb, tq), 0)
    slot = s % 2

    def select(step, dst_slot):
        for t in range(2):
            qi = step if t == 0 else nb - 1 - step
            q = q_ref[pl.ds(pl.multiple_of(qi * tq, tq), tq), :]
            for h in range(2):
                hm = _head_mask(q.shape, h)
                qh = jnp.where(hm, q, jnp.zeros_like(q))
                gate = _dot_nt(kmh_ref[...], qh) + _dot_nt(kml_ref[...], qh)
                cnt = jnp.zeros((nb, tq), jnp.int32)
                for jp in range(nb):
                    g_jp = gate[jp:jp + 1, :]
                    beats = (g_jp > gate) | ((g_jp == gate) & (blk > jp))
                    cnt = cnt + jnp.where(beats & (qi > jp), 1, 0)
                sel = ((blk < qi) & (cnt < MOBA_TOPK)) | (blk == qi)
                bias_t = jnp.where(sel, 0.0, NEG).astype(F32)
                spare = HEAD_DIM * (1 - h)
                pieces = [jnp.zeros((spare, tq), F32)] if spare else []
                pieces += [bias_t, jnp.zeros((LANES - spare - nb, tq), F32)]
                qaug[dst_slot, t, h] = jnp.where(hm, q, jnp.concatenate(pieces, axis=0).T.astype(BF16))

    @pl.when(s == 0)
    def _():
        for h in range(2):
            spare = HEAD_DIM * (1 - h)
            for j in range(nb):
                onehot = jnp.where(lane == spare + j, 1.0, 0.0).astype(BF16)
                kaug[h, j * tq:(j + 1) * tq, :] = jnp.where(
                    _head_mask((tq, LANES), h), k_ref[j * tq:(j + 1) * tq, :], onehot)
                vaug[h, j * tq:(j + 1) * tq, :] = jnp.where(
                    _head_mask((tq, LANES), h), v_ref[j * tq:(j + 1) * tq, :], jnp.ones((tq, LANES), BF16))
        select(0, 0)

    for t in range(2):
        for h in range(2):
            m_scr[t, h] = jnp.full((tq, LANES), NEG, F32)
            acc_scr[t, h] = jnp.zeros((tq, LANES), F32)

    row = lax.broadcasted_iota(jnp.int32, (tq, tq), 0)
    col = lax.broadcasted_iota(jnp.int32, (tq, tq), 1)
    causal_bias = jnp.where(col <= row, 0.0, NEG).astype(F32)
    starts = [pl.multiple_of(s * tq, tq), pl.multiple_of((nb - 1 - s) * tq, tq)]

    for it in range(nb + 1):
        if it < 2:
            t, kstart = it, starts[it]
        else:
            first = it - 2 < s
            t = jnp.where(first, 0, 1)
            kstart = pl.multiple_of(jnp.where(first, it - 2, it - 2 - s) * tq, tq)
        for h in range(2):
            sc = _dot_nt(qaug[slot, t, h], kaug[h, pl.ds(kstart, tq), :])
            if it < 2:
                sc = sc + causal_bias
            m_old = m_scr[t, h]
            m_new = jnp.maximum(m_old, jnp.max(sc, axis=1, keepdims=True))
            p = jnp.exp(sc - jnp.concatenate([m_new, m_new], axis=1))
            acc_scr[t, h] = (jnp.exp(m_old - m_new) * acc_scr[t, h]
                             + jnp.dot(p.astype(BF16), vaug[h, pl.ds(kstart, tq), :], preferred_element_type=F32))
            m_scr[t, h] = m_new

    first_head = _head_mask((tq, LANES), 0)
    for t in range(2):
        acc = jnp.where(first_head, acc_scr[t, 0], acc_scr[t, 1])
        den = jnp.where(first_head, pltpu.roll(acc_scr[t, 0], HEAD_DIM, 1), pltpu.roll(acc_scr[t, 1], HEAD_DIM, 1))
        o_ref[pl.ds(starts[t], tq), :] = (acc / den).astype(o_ref.dtype)

    select(jnp.minimum(s + 1, nsteps - 1), 1 - slot)


def _moba_attention(qm3, km3, vm3, kmh, kml):
    batch, seq, w = qm3.shape
    nb = seq // MOBA_BLOCK
    assert nb % 2 == 0 and nb <= HEAD_DIM
    npair = w // LANES
    tq = MOBA_BLOCK
    seq_spec = pl.BlockSpec((None, seq, LANES), lambda b, p, i: (b, 0, p))
    km_spec = pl.BlockSpec((None, nb, LANES), lambda b, p, i: (b, 0, p))
    state = pltpu.VMEM((2, 2, tq, LANES), F32)
    return pl.pallas_call(
        _moba_kernel,
        grid=(batch, npair, nb // 2),
        in_specs=[seq_spec, seq_spec, seq_spec, km_spec, km_spec],
        out_specs=seq_spec,
        out_shape=jax.ShapeDtypeStruct((batch, seq, w), BF16),
        scratch_shapes=[pltpu.VMEM((2, 2, 2, tq, LANES), BF16), pltpu.VMEM((2, seq, LANES), BF16),
                        pltpu.VMEM((2, seq, LANES), BF16), state, state],
        compiler_params=_cparams("parallel", "parallel", "arbitrary"),
        name="moba_attn",
    )(qm3, km3, vm3, kmh, kml)


def _memkv_kernel(mem_ref, w_ref, k_ref, v_ref):
    kv = jnp.dot(mem_ref[...].astype(BF16), w_ref[...], preferred_element_type=F32)
    k_ref[...] = kv[:, :W_MEM].astype(BF16)
    v_ref[...] = kv[:, W_MEM:].astype(BF16)


def _mem_kv(mem, w_kv_b):
    batch, m, d = mem.shape
    return pl.pallas_call(
        _memkv_kernel,
        grid=(batch,),
        in_specs=[pl.BlockSpec((None, m, d), lambda b: (b, 0, 0)),
                  pl.BlockSpec((d, 2 * W_MEM), lambda b: (0, 0))],
        out_specs=[pl.BlockSpec((None, m, W_MEM), lambda b: (b, 0, 0))] * 2,
        out_shape=[jax.ShapeDtypeStruct((batch, m, W_MEM), BF16)] * 2,
        compiler_params=_cparams("parallel"),
        name="mem_kv_proj",
    )(mem, w_kv_b)


def _memattn_kernel(q_ref, k_ref, v_ref, o_ref):
    q = q_ref[...]
    k = k_ref[...]
    v = v_ref[...]
    outs = []
    for h in range(2):
        qh = jnp.where(_head_mask(q.shape, h), q, jnp.zeros_like(q))
        s = _dot_nt(qh, k)
        m = jnp.max(s, axis=1, keepdims=True)
        p = jnp.exp(s - m)
        l = jnp.sum(p, axis=1, keepdims=True)
        outs.append(jnp.dot(p.astype(BF16), v, preferred_element_type=F32) / l)
    o_ref[...] = jnp.where(_head_mask(outs[0].shape, 0), outs[0], outs[1]).astype(o_ref.dtype)


def _mem_attention(qx3, k_mem, v_mem, tq=512):
    batch, seq, w = qx3.shape
    m = k_mem.shape[1]
    return pl.pallas_call(
        _memattn_kernel,
        grid=(batch, w // LANES, seq // tq),
        in_specs=[pl.BlockSpec((None, tq, LANES), lambda b, p, i: (b, i, p)),
                  pl.BlockSpec((None, m, LANES), lambda b, p, i: (b, 0, p)),
                  pl.BlockSpec((None, m, LANES), lambda b, p, i: (b, 0, p))],
        out_specs=pl.BlockSpec((None, tq, LANES), lambda b, p, i: (b, i, p)),
        out_shape=jax.ShapeDtypeStruct((batch, seq, w), BF16),
        compiler_params=_cparams("parallel", "parallel", "parallel"),
        name="mem_attn",
    )(qx3, k_mem, v_mem)


def _merge_kernel(alpha, x_ref, o1_ref, o2_ref, o3_ref, l1_ref, l2_ref, l3_ref, ym_ref, yx_ref,
                  wg_ref, bg_ref, wbd_ref, wbm_ref, wbx_ref, wo_ref, g_ref, b_ref, wrh_ref, wrl_ref, rb_ref,
                  out_ref, packed_ref, eidx_ref, gw_ref, rank_ref, cnt_ref, stage, carry, x1_prev):
    step = pl.program_id(0)
    route_refs = (wrh_ref, wrl_ref, rb_ref, eidx_ref, gw_ref, rank_ref, cnt_ref, carry)

    @pl.when(step == 0)
    def _():
        carry[...] = jnp.zeros_like(carry)
        x1_prev[...] = jnp.zeros_like(x1_prev)

    route = _route_tile(x1_prev[...], step > 0, *route_refs)
    next(route)
    x = x_ref[...]
    xb = x.astype(BF16)
    tm, d = x.shape

    def token_major(ref, slot):
        dilation = ref.shape[0]
        if dilation == 1:
            return ref[0].astype(F32)
        for r in range(dilation):
            stage[slot, pl.ds(r, tm // dilation, stride=dilation), :] = ref[r].astype(F32)
        return stage[slot]

    o1, o2, o3 = (token_major(r, s) for s, r in enumerate((o1_ref, o2_ref, o3_ref)))
    l1, l2, l3 = (token_major(r, s + 3) for s, r in enumerate((l1_ref, l2_ref, l3_ref)))
    mx = jnp.maximum(jnp.maximum(l1, l2), l3)
    e1, e2, e3 = jnp.exp(l1 - mx), jnp.exp(l2 - mx), jnp.exp(l3 - mx)
    y_dil = (e1 * o1 + e2 * o2 + e3 * o3) / (e1 + e2 + e3)
    next(route)
    branches = (
        jnp.dot(y_dil.astype(BF16), wbd_ref[...], preferred_element_type=F32),
        jnp.dot(ym_ref[...], wbm_ref[...], preferred_element_type=F32),
        jnp.dot(yx_ref[...], wbx_ref[...], preferred_element_type=F32),
    )
    merged = jnp.zeros_like(x)
    for i, br in enumerate(branches):
        next(route)
        logits = jnp.dot(xb, wg_ref[:, i * d:(i + 1) * d], preferred_element_type=F32) + bg_ref[:, i * d:(i + 1) * d]
        merged = merged + jax.nn.sigmoid(logits) * br
    next(route)
    mix = jnp.dot(merged.astype(BF16), wo_ref[...], preferred_element_type=F32)
    for _ in route:
        pass
    x1 = _layer_norm(alpha * x + mix, g_ref[...], b_ref[...])
    out_ref[...] = x1
    packed_ref[...] = _pack_halves(x1)
    x1_prev[...] = x1


def _merge(alpha, x2, o_dil, lse_dil, y_moba, y_mem, wg, bg, wbd, wbm, wbx, wo, g, b, wr_hi_t, wr_lo_t, rbias_b,
           seq, tm=512):
    n, d = x2.shape
    tiles = seq // tm
    last = n // tm - 1
    cur = lambda i: jnp.minimum(i, last)
    row = lambda w: pl.BlockSpec((tm, w), lambda i: (cur(i), 0))
    full = lambda a: pl.BlockSpec(a.shape, lambda i: (0, 0))
    residue_major = lambda a: pl.BlockSpec((None, a.shape[1], tm // a.shape[1], LANES),
                                           lambda i: (cur(i) // tiles, 0, cur(i) % tiles, 0))
    slots = pl.BlockSpec((TOP_K, tm), lambda i: (0, jnp.maximum(i - 1, 0)))
    weights = (wg, bg, wbd, wbm, wbx, wo, g, b, wr_hi_t, wr_lo_t, rbias_b)
    tok = lambda dt: jax.ShapeDtypeStruct((TOP_K, n), dt)
    return pl.pallas_call(
        functools.partial(_merge_kernel, alpha),
        grid=(n // tm + 1,),
        in_specs=([row(d)] + [residue_major(a) for a in (*o_dil, *lse_dil)] + [row(W_MOBA), row(W_MEM)]
                  + [full(a) for a in weights]),
        out_specs=[row(d), row(PACKED), slots, slots, slots, pl.BlockSpec((N_EXPERTS, LANES), lambda i: (0, 0))],
        out_shape=[jax.ShapeDtypeStruct((n, d), F32), jax.ShapeDtypeStruct((n, PACKED), jnp.uint32),
                   tok(jnp.int32), tok(F32), tok(jnp.int32), jax.ShapeDtypeStruct((N_EXPERTS, LANES), jnp.int32)],
        scratch_shapes=[pltpu.VMEM((2 * len(DIL_PAIRS), tm, LANES), F32), pltpu.VMEM((N_EXPERTS, LANES), F32),
                        pltpu.VMEM((tm, d), F32)],
        compiler_params=_cparams("arbitrary"),
        name="merge_outproj_ln1_route",
    )(x2, *o_dil, *lse_dil, y_moba, y_mem, *weights)


def _token_mixer(x2, mem, positions, w_in, w_mem_kv, w_gate, b_gate, w_br_dil, w_br_moba, w_br_mem,
                 w_out, ln1_g, ln1_b, w_router, router_bias, batch, seq, alpha):
    wr_t = w_router.T
    wr_hi = wr_t.astype(BF16)
    wr_lo = (wr_t - wr_hi.astype(F32)).astype(BF16)
    rbias_b = jnp.broadcast_to(router_bias.astype(F32)[:, None], (N_EXPERTS, LANES))
    cos_t, sin_t = _rope_tables(positions)
    (q_dil, k_dil, v_dil), (qm, km, vm, qx) = _in_projection(x2, w_in.astype(BF16), cos_t, sin_t, batch, seq)
    o_dil, lse_dil = [], []
    for g in range(len(DIL_PAIRS)):
        o, lse = _dilated_attention(q_dil[g], k_dil[g], v_dil[g])
        o_dil.append(o)
        lse_dil.append(lse)
    qm3, km3, vm3 = (t.reshape(batch, seq, W_MOBA) for t in (qm, km, vm))
    kmh, kml = _moba_kmean(km3)
    y_moba = _moba_attention(qm3, km3, vm3, kmh, kml).reshape(batch * seq, W_MOBA)
    k_mem, v_mem = _mem_kv(mem, w_mem_kv.astype(BF16))
    y_mem = _mem_attention(qx.reshape(batch, seq, W_MEM), k_mem, v_mem).reshape(batch * seq, W_MEM)
    return _merge(alpha, x2, o_dil, lse_dil, y_moba, y_mem,
                  w_gate.astype(BF16), b_gate.reshape(1, -1), w_br_dil.astype(BF16),
                  w_br_moba.astype(BF16), w_br_mem.astype(BF16), w_out.astype(BF16),
                  ln1_g.reshape(1, -1), ln1_b.reshape(1, -1), wr_hi, wr_lo, rbias_b, seq)


EXPERT_ROWS = 256
EXPERT_RING = 6
TOKEN_TILE = 256
PACKED = 512
SC_WINDOW = 128


def _first_index_of_max(v, iota_f, size):
    m = jnp.max(v, axis=0, keepdims=True)
    idx = jnp.min(jnp.where(v == m, iota_f, float(size)), axis=0, keepdims=True)
    return m, idx


def _route_tile(x, counted, wh_ref, wl_ref, bias_ref, eidx_ref, gw_ref, rank_ref, cnt_ref, carry):
    tm = x.shape[0]
    xh = x.astype(BF16)
    xl = (x - xh.astype(F32)).astype(BF16)
    wh = wh_ref[...]
    logits = _dot_nt(wh, xh) + _dot_nt(wh, xl) + _dot_nt(wl_ref[...], xh)
    yield
    scores = jax.nn.sigmoid(logits)
    biased = scores + bias_ref[...][:, :1]

    giota = lax.broadcasted_iota(jnp.int32, (GROUP_SIZE, tm), 0).astype(F32)
    group_scores = []
    for g in range(N_GROUPS):
        slab = biased[g * GROUP_SIZE:(g + 1) * GROUP_SIZE, :]
        m1, i1 = _first_index_of_max(slab, giota, GROUP_SIZE)
        m2 = jnp.max(jnp.where(giota == i1, -jnp.inf, slab), axis=0, keepdims=True)
        group_scores.append(m1 + m2)
    gs = jnp.concatenate(group_scores, axis=0)
    gidx = lax.broadcasted_iota(jnp.int32, (N_GROUPS, tm), 0)
    beaten = jnp.zeros((N_GROUPS, tm), jnp.int32)
    for gp in range(N_GROUPS):
        row = gs[gp:gp + 1, :]
        beaten = beaten + jnp.where((row > gs) | ((row == gs) & (gidx > gp)), 1, 0)
    keep = beaten < TOPK_GROUPS
    masked = jnp.concatenate(
        [jnp.where(keep[g:g + 1, :], biased[g * GROUP_SIZE:(g + 1) * GROUP_SIZE, :], -jnp.inf)
         for g in range(N_GROUPS)], axis=0)
    yield

    eiota = lax.broadcasted_iota(jnp.int32, (N_EXPERTS, tm), 0).astype(F32)
    idx_rows, gw_rows = [], []
    for k in range(TOP_K):
        _, idx = _first_index_of_max(masked, eiota, N_EXPERTS)
        hit = eiota == idx
        gw_rows.append(jnp.sum(jnp.where(hit, scores, 0.0), axis=0, keepdims=True))
        masked = jnp.where(hit, -jnp.inf, masked)
        idx_rows.append(idx)
        if k % 2 == 1:
            yield
    idx8 = jnp.concatenate(idx_rows, axis=0)
    gw8 = jnp.concatenate(gw_rows, axis=0)
    gw_ref[...] = gw8 / jnp.sum(gw8, axis=0, keepdims=True) * ROUTED_SCALE
    eidx_ref[...] = idx8.astype(jnp.int32)

    onehot = jnp.zeros((N_EXPERTS, tm), F32)
    for k in range(TOP_K):
        onehot = onehot + jnp.where(eiota == idx8[k:k + 1, :], 1.0, 0.0)
    t_row = lax.broadcasted_iota(jnp.int32, (tm, tm), 0)
    t_col = lax.broadcasted_iota(jnp.int32, (tm, tm), 1)
    earlier = jnp.where(t_row < t_col, 1.0, 0.0).astype(BF16)
    prefix = jnp.dot(onehot.astype(BF16), earlier, preferred_element_type=F32)
    base = carry[...]
    prefix = prefix + jnp.concatenate([base] * (tm // LANES), axis=1)
    rank_rows = [jnp.sum(jnp.where(eiota == idx8[k:k + 1, :], prefix, 0.0), axis=0, keepdims=True)
                 for k in range(TOP_K)]
    rank_ref[...] = jnp.concatenate(rank_rows, axis=0).astype(jnp.int32)
    total = base + jnp.where(counted, jnp.sum(onehot, axis=1, keepdims=True), 0.0)
    carry[...] = total
    cnt_ref[...] = total.astype(jnp.int32)


def _dest_kernel(eidx_ref, rank_ref, start_ref, dest_ref):
    eidx = eidx_ref[...]
    tm = eidx.shape[1]
    eiota = lax.broadcasted_iota(jnp.int32, (N_EXPERTS, tm), 0)
    start = start_ref[...][:, :1]
    rows = [jnp.sum(jnp.where(eiota == eidx[k:k + 1, :], start, 0.0), axis=0, keepdims=True)
            for k in range(TOP_K)]
    dest_ref[...] = jnp.concatenate(rows, axis=0).astype(jnp.int32) + rank_ref[...]


def _dest_rows(eidx, rank, start_b):
    n = eidx.shape[1]
    nt = n // TOKEN_TILE
    return pl.pallas_call(
        _dest_kernel,
        grid=(nt,),
        in_specs=[pl.BlockSpec((TOP_K, TOKEN_TILE), lambda i: (0, i)),
                  pl.BlockSpec((TOP_K, TOKEN_TILE), lambda i: (0, i)),
                  pl.BlockSpec((N_EXPERTS, LANES), lambda i: (0, 0))],
        out_specs=pl.BlockSpec((TOP_K, TOKEN_TILE), lambda i: (0, i)),
        out_shape=jax.ShapeDtypeStruct((TOP_K, n), jnp.int32),
        compiler_params=_cparams("parallel"),
        name="moe_dest_rows",
    )(eidx, rank, start_b)


def _pack_halves(v):
    return pltpu.pack_elementwise([v[:, :PACKED], v[:, PACKED:]], packed_dtype=BF16)


def _unpack_half(p, index):
    return pltpu.unpack_elementwise(p, index=index, packed_dtype=BF16, unpacked_dtype=F32)


def _sc_mesh():
    return plsc.VectorSubcoreMesh(core_axis_name="core", subcore_axis_name="subcore")


def _sc_move_rows(src, src_idx, dst_idx, out_rows, name):
    width = src.shape[1]
    count = src_idx.shape[1]

    @functools.partial(pl.kernel, out_type=jax.ShapeDtypeStruct((out_rows, width), src.dtype),
                       mesh=_sc_mesh(), scratch_types=[pltpu.VMEM((SC_WINDOW, width), src.dtype)], name=name)
    def move(src_hbm, sidx_hbm, didx_hbm, out_hbm, rows_vmem):
        def body(sidx_vmem, didx_vmem):
            pltpu.sync_copy(src_hbm.at[sidx_vmem.at[0]], rows_vmem)
            pltpu.sync_copy(rows_vmem, out_hbm.at[didx_vmem.at[0]])

        idx_spec = pl.BlockSpec((1, SC_WINDOW), index_map=lambda i: (0, i))
        pltpu.emit_pipeline(
            body,
            grid=(count // SC_WINDOW,),
            in_specs=[idx_spec, idx_spec],
            out_specs=[],
            core_axis_name=("core", "subcore"),
            dimension_semantics=(pltpu.PARALLEL,),
        )(sidx_hbm, didx_hbm)

    return move(src, src_idx, dst_idx)


def _sc_dispatch_rows(src, token_idx, dest, out_rows):
    n, width = src.shape
    slots = dest.shape[0]

    @functools.partial(pl.kernel, out_type=jax.ShapeDtypeStruct((out_rows, width), src.dtype),
                       mesh=_sc_mesh(), scratch_types=[pltpu.VMEM((SC_WINDOW, width), src.dtype)],
                       name="moe_dispatch_sc")
    def dispatch(src_hbm, tidx_hbm, dest_hbm, out_hbm, rows_vmem):
        def body(tidx_vmem, dest_vmem):
            pltpu.sync_copy(src_hbm.at[tidx_vmem.at[0]], rows_vmem)
            for k in range(slots):
                pltpu.sync_copy(rows_vmem, out_hbm.at[dest_vmem.at[k]])

        pltpu.emit_pipeline(
            body,
            grid=(n // SC_WINDOW,),
            in_specs=[pl.BlockSpec((1, SC_WINDOW), index_map=lambda i: (0, i)),
                      pl.BlockSpec((slots, SC_WINDOW), index_map=lambda i: (0, i))],
            out_specs=[],
            core_axis_name=("core", "subcore"),
            dimension_semantics=(pltpu.PARALLEL,),
        )(tidx_hbm, dest_hbm)

    return dispatch(src, token_idx, dest)


def _expert_kernel(first_blk_ref, count_ref, total_ref, xs_hbm, wg_ref, wu_ref, wd_ref, ys_hbm,
                   wg_b, wu_b, wd_b, x_ring, y_ring, in_sem, out_sem):
    e = pl.program_id(0)
    total = total_ref[0]
    first = first_blk_ref[e]
    count = count_ref[e]
    nblk = (count + EXPERT_ROWS - 1) // EXPERT_ROWS
    spare_block = ys_hbm.shape[0] // EXPERT_ROWS - EXPERT_RING

    def rows_of(g):
        return pl.ds(pl.multiple_of(g * EXPERT_ROWS, EXPERT_ROWS), EXPERT_ROWS)

    def in_copy(g):
        slot = g % EXPERT_RING
        return pltpu.make_async_copy(xs_hbm.at[rows_of(g)], x_ring.at[slot], in_sem.at[slot])

    def out_copy(g, slot):
        return pltpu.make_async_copy(y_ring.at[slot], ys_hbm.at[rows_of(g)], out_sem.at[slot])

    @pl.when(e == 0)
    def _():
        y_ring[...] = jnp.zeros_like(y_ring)
        for g in range(EXPERT_RING - 1):
            in_copy(g).start()
        for slot in range(EXPERT_RING):
            out_copy(spare_block + slot, slot).start()

    @pl.when(nblk > 0)
    def _():
        wg_b[...] = wg_ref[...].astype(BF16)
        wu_b[...] = wu_ref[...].astype(BF16)
        wd_b[...] = wd_ref[...].astype(BF16)

    def blocks(js):
        gs = [first + j for j in js]
        for g in gs:
            in_copy(g).wait()
        ys = []
        for j, g in zip(js, gs):
            p = x_ring[g % EXPERT_RING]
            row = lax.broadcasted_iota(jnp.int32, p.shape, 0) + j * EXPERT_ROWS
            p = jnp.where(row < count, p, jnp.zeros_like(p))
            lo = _unpack_half(p, 0).astype(BF16)
            hi = _unpack_half(p, 1).astype(BF16)

            def up(w, lo=lo, hi=hi):
                return (jnp.dot(lo, w[:PACKED, :], preferred_element_type=F32)
                        + jnp.dot(hi, w[PACKED:, :], preferred_element_type=F32))

            hid = (jax.nn.silu(up(wg_b)) * up(wu_b)).astype(BF16)
            ys.append(_pack_halves(jnp.dot(hid, wd_b[...], preferred_element_type=F32)))
        for g in gs:
            in_copy(g + EXPERT_RING - 1).start()
        for g in gs:
            out_copy(g, g % EXPERT_RING).wait()
        for g, y in zip(gs, ys):
            y_ring[g % EXPERT_RING] = y
        for g in gs:
            out_copy(g, g % EXPERT_RING).start()

    def pair(i, carry):
        blocks([2 * i, 2 * i + 1])
        return carry

    lax.fori_loop(0, nblk // 2, pair, 0)

    @pl.when(nblk % 2 == 1)
    def _():
        blocks([nblk - 1])

    @pl.when(e == pl.num_programs(0) - 1)
    def _():
        for slot in range(EXPERT_RING):
            out_copy(spare_block + slot, slot).wait()
        for g in range(EXPERT_RING - 1):
            in_copy(total + g).wait()


def _expert_ffn(xs, first_blk, counts, total, w_e_gate, w_e_up, w_e_down):
    rows = xs.shape[0] + EXPERT_ROWS
    n_experts, d = w_e_gate.shape[0], w_e_gate.shape[1]
    w_map = lambda e, fb, ct, tt: (e, 0, 0)
    ring = pltpu.VMEM((EXPERT_RING, EXPERT_ROWS, PACKED), jnp.uint32)
    return pl.pallas_call(
        _expert_kernel,
        grid_spec=pltpu.PrefetchScalarGridSpec(
            num_scalar_prefetch=3,
            grid=(n_experts,),
            in_specs=[pl.BlockSpec(memory_space=pl.ANY),
                      pl.BlockSpec((None, d, D_EXPERT), w_map),
                      pl.BlockSpec((None, d, D_EXPERT), w_map),
                      pl.BlockSpec((None, D_EXPERT, d), w_map)],
            out_specs=pl.BlockSpec(memory_space=pl.ANY),
            scratch_shapes=[pltpu.VMEM((d, D_EXPERT), BF16), pltpu.VMEM((d, D_EXPERT), BF16),
                            pltpu.VMEM((D_EXPERT, d), BF16), ring, ring,
                            pltpu.SemaphoreType.DMA((EXPERT_RING,)), pltpu.SemaphoreType.DMA((EXPERT_RING,))],
        ),
        out_shape=jax.ShapeDtypeStruct((rows, PACKED), jnp.uint32),
        compiler_params=_cparams("arbitrary"),
        name="moe_expert_ffn",
    )(first_blk, counts, total, xs, w_e_gate, w_e_up, w_e_down)


def _combine_kernel(alpha, x_ref, gw_ref, ys_ref, wsg_ref, wsu_ref, wsd_ref, g_ref, b_ref, out_ref):
    x = x_ref[...]
    xb = x.astype(BF16)
    hid = (jax.nn.silu(jnp.dot(xb, wsg_ref[...], preferred_element_type=F32))
           * jnp.dot(xb, wsu_ref[...], preferred_element_type=F32)).astype(BF16)
    shared = jnp.dot(hid, wsd_ref[...], preferred_element_type=F32)
    gw = gw_ref[...]
    lo = jnp.zeros((TOKEN_TILE, PACKED), F32)
    hi = jnp.zeros((TOKEN_TILE, PACKED), F32)
    for k in range(TOP_K):
        p = ys_ref[k]
        w = gw[:, k:k + 1]
        lo = lo + w * _unpack_half(p, 0)
        hi = hi + w * _unpack_half(p, 1)
    routed = jnp.concatenate([lo, hi], axis=1)
    out_ref[...] = _layer_norm(alpha * x + (routed + shared), g_ref[...], b_ref[...])


def _combine(alpha, x1, gw_t, ys_tok, wsg, wsu, wsd, g, b):
    n, d = x1.shape
    full = lambda a: pl.BlockSpec(a.shape, lambda i: (0, 0))
    weights = (wsg, wsu, wsd, g, b)
    return pl.pallas_call(
        functools.partial(_combine_kernel, alpha),
        grid=(n // TOKEN_TILE,),
        in_specs=[pl.BlockSpec((TOKEN_TILE, d), lambda i: (i, 0)),
                  pl.BlockSpec((TOKEN_TILE, TOP_K), lambda i: (i, 0)),
                  pl.BlockSpec((TOP_K, TOKEN_TILE, PACKED), lambda i: (0, i, 0))] + [full(a) for a in weights],
        out_specs=pl.BlockSpec((TOKEN_TILE, d), lambda i: (i, 0)),
        out_shape=jax.ShapeDtypeStruct((n, d), F32),
        compiler_params=_cparams("parallel"),
        name="moe_combine_shared_ln2",
    )(x1, gw_t, ys_tok, *weights)


def _moe_layer(x1, x1_packed, eidx, gw, rank, cnt, w_e_gate, w_e_up, w_e_down, w_s_gate, w_s_up, w_s_down,
               ln2_g, ln2_b, alpha):
    n, d = x1.shape

    counts = cnt[:, 0]
    padded = (counts + EXPERT_ROWS - 1) // EXPERT_ROWS * EXPERT_ROWS
    seg_end = jnp.cumsum(padded)
    seg_start = seg_end - padded
    rows = n * TOP_K + N_EXPERTS * EXPERT_ROWS
    first_blk = (seg_start // EXPERT_ROWS).astype(jnp.int32)
    total_blk = (seg_end[-1:] // EXPERT_ROWS).astype(jnp.int32)
    start_b = jnp.broadcast_to(seg_start.astype(F32)[:, None], (N_EXPERTS, LANES))

    dest = _dest_rows(eidx, rank, start_b)
    dest_flat = dest.reshape(1, TOP_K * n)
    assign = jnp.arange(TOP_K * n, dtype=jnp.int32).reshape(1, TOP_K * n)
    xs = _sc_dispatch_rows(x1_packed, assign[:, :n], dest, rows + (EXPERT_RING - 1) * EXPERT_ROWS)
    ys = _expert_ffn(xs, first_blk, counts, total_blk, w_e_gate, w_e_up, w_e_down)
    ys_tok = _sc_move_rows(ys, dest_flat, assign, TOP_K * n, "moe_gather_sc").reshape(TOP_K, n, PACKED)
    return _combine(alpha, x1, gw.T, ys_tok, w_s_gate.astype(BF16), w_s_up.astype(BF16),
                    w_s_down.astype(BF16), ln2_g.reshape(1, -1), ln2_b.reshape(1, -1))


def kernel(x, mem, positions, w_in, w_mem_kv, w_gate, b_gate, w_br_dil, w_br_moba, w_br_mem, w_out, ln1_g, ln1_b, w_router, router_bias, w_e_gate, w_e_up, w_e_down, w_s_gate, w_s_up, w_s_down, ln2_g, ln2_b):
    batch, seq, d = x.shape
    depth = w_in.shape[0]
    alpha = (2.0 * depth) ** 0.25
    h = x.reshape(batch * seq, d)
    for l in range(depth):
        h, h_packed, eidx, gw, rank, cnt = _token_mixer(
            h, mem, positions, w_in[l], w_mem_kv[l], w_gate[l], b_gate[l], w_br_dil[l], w_br_moba[l], w_br_mem[l],
            w_out[l], ln1_g[l], ln1_b[l], w_router[l], router_bias[l], batch, seq, alpha)
        h = _moe_layer(h, h_packed, eidx, gw, rank, cnt, w_e_gate[l], w_e_up[l], w_e_down[l],
                       w_s_gate[l], w_s_up[l], w_s_down[l], ln2_g[l], ln2_b[l], alpha)
    return h.reshape(batch, seq, d)
```

```python
import functools

import jax
import jax.numpy as jnp
from jax import lax
from jax.experimental import pallas as pl
from jax.experimental.pallas import tpu as pltpu
from jax.experimental.pallas import tpu_sc as plsc

F32 = jnp.float32
BF16 = jnp.bfloat16

LANES = 128
VMEM_LIMIT_BYTES = 48 * 1024 * 1024

HEAD_DIM = 64
ROPE_DIM = HEAD_DIM // 4
ROPE_HALF = ROPE_DIM // 2
ROPE_THETA = 500000.0
DIL_PAIRS = ((128, 1), (512, 4), (2048, 16))
BAND = 128
DIL_TILE = 1024
W_DIL = 384
W_MOBA = 384
W_MEM = 256
MOBA_BLOCK = 256
MOBA_TOPK = 3
N_EXPERTS = 256
N_GROUPS = 8
GROUP_SIZE = N_EXPERTS // N_GROUPS
TOPK_GROUPS = 4
TOP_K = 8
D_EXPERT = 256
ROUTED_SCALE = 2.5
LN_EPS = 1e-5
NEG = -1e30
QK_SCALE = HEAD_DIM ** -0.5


def _cparams(*sem):
    return pltpu.CompilerParams(dimension_semantics=sem, vmem_limit_bytes=VMEM_LIMIT_BYTES)


def _dot_nt(a, b):
    return lax.dot_general(a, b, (((1,), (1,)), ((), ())), preferred_element_type=F32)


def _layer_norm(v, g, b):
    mu = jnp.mean(v, axis=-1, keepdims=True)
    c = v - mu
    var = jnp.mean(c * c, axis=-1, keepdims=True)
    return c * lax.rsqrt(var + LN_EPS) * g + b


_IN_SECTIONS = (
    (0, W_DIL, True, True),
    (W_DIL, W_DIL, True, False),
    (2 * W_DIL, W_DIL, False, False),
    (3 * W_DIL, W_MOBA, True, True),
    (3 * W_DIL + W_MOBA, W_MOBA, True, False),
    (3 * W_DIL + 2 * W_MOBA, W_MOBA, False, False),
    (3 * W_DIL + 3 * W_MOBA, W_MEM, False, True),
)


N_DIL_SECTIONS = 3


def _inproj_kernel(x_ref, w_ref, cos_ref, sin_ref, *refs):
    n_dil = N_DIL_SECTIONS * len(DIL_PAIRS)
    dil_refs, flat_refs, stage = refs[:n_dil], refs[n_dil:-1], refs[-1]
    xb = x_ref[...].astype(BF16)
    cos = cos_ref[...]
    sin = sin_ref[...]
    tm = xb.shape[0]
    lane = lax.broadcasted_iota(jnp.int32, (tm, LANES), 1)
    first_half = (lane % ROPE_DIM) < ROPE_HALF

    def rope(t):
        partner = jnp.where(first_half, pltpu.roll(t, LANES - ROPE_HALF, 1), pltpu.roll(t, ROPE_HALF, 1))
        return t * cos + partner * sin

    for sec, (off, width, roped, scaled) in enumerate(_IN_SECTIONS):
        acc = jnp.dot(xb, w_ref[:, off:off + width], preferred_element_type=F32)
        for c in range(width // LANES):
            t = acc[:, c * LANES:(c + 1) * LANES]
            if roped:
                t = rope(t)
            if scaled:
                t = t * QK_SCALE
            if sec >= N_DIL_SECTIONS:
                flat_refs[sec - N_DIL_SECTIONS][:, c * LANES:(c + 1) * LANES] = t.astype(BF16)
                continue
            o_ref = dil_refs[sec * len(DIL_PAIRS) + c]
            dilation = DIL_PAIRS[c][1]
            if dilation == 1:
                o_ref[0] = t.astype(BF16)
                continue
            slot = sec * len(DIL_PAIRS) + c
            stage[slot] = t
            for r in range(dilation):
                o_ref[r] = stage[slot, pl.ds(r, tm // dilation, stride=dilation), :].astype(BF16)


def _in_projection(x2, w_in_b, cos_t, sin_t, batch, seq, tm=512):
    n, d = x2.shape
    w_total = w_in_b.shape[1]
    tiles = seq // tm
    dil_specs, dil_shapes = [], []
    for _ in range(N_DIL_SECTIONS):
        for _, dilation in DIL_PAIRS:
            dil_specs.append(pl.BlockSpec((None, dilation, tm // dilation, LANES),
                                          lambda i: (i // tiles, 0, i % tiles, 0)))
            dil_shapes.append(jax.ShapeDtypeStruct((batch, dilation, seq // dilation, LANES), BF16))
    widths = [s[1] for s in _IN_SECTIONS[N_DIL_SECTIONS:]]
    outs = pl.pallas_call(
        _inproj_kernel,
        grid=(n // tm,),
        in_specs=[
            pl.BlockSpec((tm, d), lambda i: (i, 0)),
            pl.BlockSpec((d, w_total), lambda i: (0, 0)),
            pl.BlockSpec((tm, LANES), lambda i: (i, 0)),
            pl.BlockSpec((tm, LANES), lambda i: (i, 0)),
        ],
        out_specs=dil_specs + [pl.BlockSpec((tm, w), lambda i: (i, 0)) for w in widths],
        out_shape=dil_shapes + [jax.ShapeDtypeStruct((n, w), BF16) for w in widths],
        scratch_shapes=[pltpu.VMEM((N_DIL_SECTIONS * len(DIL_PAIRS), tm, LANES), F32)],
        compiler_params=_cparams("parallel"),
        name="in_proj_rope",
    )(x2, w_in_b, cos_t, sin_t)
    n_dil = len(dil_specs)
    ng = len(DIL_PAIRS)
    qkv_dil = [outs[s * ng:(s + 1) * ng] for s in range(N_DIL_SECTIONS)]
    return qkv_dil, outs[n_dil:]


def _rope_tables(positions):
    lane = jnp.arange(LANES)
    rotary = (lane % HEAD_DIM) < ROPE_DIM
    inv_freq = ROPE_THETA ** (-(lane % ROPE_HALF).astype(F32) / ROPE_HALF)
    ang = positions.reshape(-1).astype(F32)[:, None] * jnp.where(rotary, inv_freq, 0.0)
    sign = jnp.where((lane % ROPE_DIM) < ROPE_HALF, -1.0, 1.0)
    return jnp.cos(ang), jnp.sin(ang) * sign


def _head_mask(shape, h):
    lane = lax.broadcasted_iota(jnp.int32, shape, 1)
    return (lane // HEAD_DIM) == h


def _dil_kernel(q_ref, kp_ref, k_ref, vp_ref, v_ref, o_ref, lse_ref):
    i = pl.program_id(2)
    n_res, tq = q_ref.shape[0], q_ref.shape[1]
    qi = lax.broadcasted_iota(jnp.int32, (BAND, 2 * BAND), 0)
    kj = lax.broadcasted_iota(jnp.int32, (BAND, 2 * BAND), 1)
    dist = qi + BAND - kj
    band = (dist >= 0) & (dist <= BAND)
    first = _head_mask((BAND, LANES), 0)
    for r in range(n_res):
        for j in range(tq // BAND):
            rows = slice(j * BAND, (j + 1) * BAND)
            q = q_ref[r, rows, :]
            if j == 0:
                k_prev, v_prev = kp_ref[r], vp_ref[r]
                allowed = band & ((kj >= BAND) | (i > 0))
            else:
                prev_rows = slice((j - 1) * BAND, j * BAND)
                k_prev, v_prev = k_ref[r, prev_rows, :], v_ref[r, prev_rows, :]
                allowed = band
            k = jnp.concatenate([k_prev, k_ref[r, rows, :]], axis=0)
            v = jnp.concatenate([v_prev, v_ref[r, rows, :]], axis=0)
            outs, lses = [], []
            for h in range(2):
                qh = jnp.where(_head_mask(q.shape, h), q, jnp.zeros_like(q))
                s = jnp.where(allowed, _dot_nt(qh, k), NEG)
                m = jnp.max(s, axis=1, keepdims=True)
                p = jnp.exp(s - m)
                l = jnp.sum(p, axis=1, keepdims=True)
                outs.append(jnp.dot(p.astype(BF16), v, preferred_element_type=F32) / l)
                lses.append(jnp.broadcast_to(m + jnp.log(l), (BAND, LANES)))
            o_ref[r, rows, :] = jnp.where(first, outs[0], outs[1]).astype(o_ref.dtype)
            lse_ref[r, rows, :] = jnp.where(first, lses[0], lses[1])


def _dilated_attention(q4, k4, v4):
    batch, dilation, steps, _ = q4.shape
    tq = min(steps, DIL_TILE)
    n_res = min(dilation, DIL_TILE // tq)
    per_tile = tq // BAND
    cur = pl.BlockSpec((None, n_res, tq, LANES), lambda b, r, i: (b, r, i, 0))
    prev = pl.BlockSpec((None, n_res, BAND, LANES), lambda b, r, i: (b, r, jnp.maximum(i * per_tile - 1, 0), 0))
    return pl.pallas_call(
        _dil_kernel,
        grid=(batch, dilation // n_res, steps // tq),
        in_specs=[cur, prev, cur, prev, cur],
        out_specs=[cur, cur],
        out_shape=[jax.ShapeDtypeStruct(q4.shape, BF16), jax.ShapeDtypeStruct(q4.shape, F32)],
        compiler_params=_cparams("parallel", "parallel", "arbitrary"),
        name=f"dilated_attn_d{dilation}",
    )(q4, k4, k4, v4, v4)


def _kmean_kernel(k_ref, hi_ref, lo_ref):
    k = k_ref[...].astype(F32)
    s, w = k.shape
    mean = jnp.sum(k.reshape(s // MOBA_BLOCK, MOBA_BLOCK, w), axis=1) / MOBA_BLOCK
    hi = mean.astype(BF16)
    hi_ref[...] = hi
    lo_ref[...] = (mean - hi.astype(F32)).astype(BF16)


def _moba_kmean(km3):
    batch, seq, w = km3.shape
    nb = seq // MOBA_BLOCK
    return pl.pallas_call(
        _kmean_kernel,
        grid=(batch,),
        in_specs=[pl.BlockSpec((None, seq, w), lambda b: (b, 0, 0))],
        out_specs=[pl.BlockSpec((None, nb, w), lambda b: (b, 0, 0))] * 2,
        out_shape=[jax.ShapeDtypeStruct((batch, nb, w), BF16)] * 2,
        compiler_params=_cparams("parallel"),
        name="moba_kmean",
    )(km3)


def _moba_kernel(q_ref, k_ref, v_ref, kmh_ref, kml_ref, o_ref, qaug, kaug, vaug, m_scr, acc_scr):
    s = pl.program_id(2)
    nsteps = pl.num_programs(2)
    nb = kmh_ref.shape[0]
    tq = MOBA_BLOCK
    lane = lax.broadcasted_iota(jnp.int32, (tq, LANES), 1)
    blk = lax.broadcasted_iota(jnp.int32, (nb, tq), 0)
    slot = s % 2

    def select(step, dst_slot):
        for t in range(2):
            qi = step if t == 0 else nb - 1 - step
            q = q_ref[pl.ds(pl.multiple_of(qi * tq, tq), tq), :]
            for h in range(2):
                hm = _head_mask(q.shape, h)
                qh = jnp.where(hm, q, jnp.zeros_like(q))
                gate = _dot_nt(kmh_ref[...], qh) + _dot_nt(kml_ref[...], qh)
                cnt = jnp.zeros((nb, tq), jnp.int32)
                for jp in range(nb):
                    g_jp = gate[jp:jp + 1, :]
                    beats = (g_jp > gate) | ((g_jp == gate) & (blk > jp))
                    cnt = cnt + jnp.where(beats & (qi > jp), 1, 0)
                sel = ((blk < qi) & (cnt < MOBA_TOPK)) | (blk == qi)
                bias_t = jnp.where(sel, 0.0, NEG).astype(F32)
                spare = HEAD_DIM * (1 - h)
                pieces = [jnp.zeros((spare, tq), F32)] if spare else []
                pieces += [bias_t, jnp.zeros((LANES - spare - nb, tq), F32)]
                qaug[dst_slot, t, h] = jnp.where(hm, q, jnp.concatenate(pieces, axis=0).T.astype(BF16))

    @pl.when(s == 0)
    def _():
        for h in range(2):
            spare = HEAD_DIM * (1 - h)
            for j in range(nb):
                onehot = jnp.where(lane == spare + j, 1.0, 0.0).astype(BF16)
                kaug[h, j * tq:(j + 1) * tq, :] = jnp.where(
                    _head_mask((tq, LANES), h), k_ref[j * tq:(j + 1) * tq, :], onehot)
                vaug[h, j * tq:(j + 1) * tq, :] = jnp.where(
                    _head_mask((tq, LANES), h), v_ref[j * tq:(j + 1) * tq, :], jnp.ones((tq, LANES), BF16))
        select(0, 0)

    for t in range(2):
        for h in range(2):
            m_scr[t, h] = jnp.full((tq, LANES), NEG, F32)
            acc_scr[t, h] = jnp.zeros((tq, LANES), F32)

    row = lax.broadcasted_iota(jnp.int32, (tq, tq), 0)
    col = lax.broadcasted_iota(jnp.int32, (tq, tq), 1)
    causal_bias = jnp.where(col <= row, 0.0, NEG).astype(F32)
    starts = [pl.multiple_of(s * tq, tq), pl.multiple_of((nb - 1 - s) * tq, tq)]

    for it in range(nb + 1):
        if it < 2:
            t, kstart = it, starts[it]
        else:
            first = it - 2 < s
            t = jnp.where(first, 0, 1)
            kstart = pl.multiple_of(jnp.where(first, it - 2, it - 2 - s) * tq, tq)
        for h in range(2):
            sc = _dot_nt(qaug[slot, t, h], kaug[h, pl.ds(kstart, tq), :])
            if it < 2:
                sc = sc + causal_bias
            m_old = m_scr[t, h]
            m_new = jnp.maximum(m_old, jnp.max(sc, axis=1, keepdims=True))
            p = jnp.exp(sc - jnp.concatenate([m_new, m_new], axis=1))
            acc_scr[t, h] = (jnp.exp(m_old - m_new) * acc_scr[t, h]
                             + jnp.dot(p.astype(BF16), vaug[h, pl.ds(kstart, tq), :], preferred_element_type=F32))
            m_scr[t, h] = m_new

    first_head = _head_mask((tq, LANES), 0)
    for t in range(2):
        acc = jnp.where(first_head, acc_scr[t, 0], acc_scr[t, 1])
        den = jnp.where(first_head, pltpu.roll(acc_scr[t, 0], HEAD_DIM, 1), pltpu.roll(acc_scr[t, 1], HEAD_DIM, 1))
        o_ref[pl.ds(starts[t], tq), :] = (acc / den).astype(o_ref.dtype)

    select(jnp.minimum(s + 1, nsteps - 1), 1 - slot)


def _moba_attention(qm3, km3, vm3, kmh, kml):
    batch, seq, w = qm3.shape
    nb = seq // MOBA_BLOCK
    assert nb % 2 == 0 and nb <= HEAD_DIM
    npair = w // LANES
    tq = MOBA_BLOCK
    seq_spec = pl.BlockSpec((None, seq, LANES), lambda b, p, i: (b, 0, p))
    km_spec = pl.BlockSpec((None, nb, LANES), lambda b, p, i: (b, 0, p))
    state = pltpu.VMEM((2, 2, tq, LANES), F32)
    return pl.pallas_call(
        _moba_kernel,
        grid=(batch, npair, nb // 2),
        in_specs=[seq_spec, seq_spec, seq_spec, km_spec, km_spec],
        out_specs=seq_spec,
        out_shape=jax.ShapeDtypeStruct((batch, seq, w), BF16),
        scratch_shapes=[pltpu.VMEM((2, 2, 2, tq, LANES), BF16), pltpu.VMEM((2, seq, LANES), BF16),
                        pltpu.VMEM((2, seq, LANES), BF16), state, state],
        compiler_params=_cparams("parallel", "parallel", "arbitrary"),
        name="moba_attn",
    )(qm3, km3, vm3, kmh, kml)


def _memkv_kernel(mem_ref, w_ref, k_ref, v_ref):
    kv = jnp.dot(mem_ref[...].astype(BF16), w_ref[...], preferred_element_type=F32)
    k_ref[...] = kv[:, :W_MEM].astype(BF16)
    v_ref[...] = kv[:, W_MEM:].astype(BF16)


def _mem_kv(mem, w_kv_b):
    batch, m, d = mem.shape
    return pl.pallas_call(
        _memkv_kernel,
        grid=(batch,),
        in_specs=[pl.BlockSpec((None, m, d), lambda b: (b, 0, 0)),
                  pl.BlockSpec((d, 2 * W_MEM), lambda b: (0, 0))],
        out_specs=[pl.BlockSpec((None, m, W_MEM), lambda b: (b, 0, 0))] * 2,
        out_shape=[jax.ShapeDtypeStruct((batch, m, W_MEM), BF16)] * 2,
        compiler_params=_cparams("parallel"),
        name="mem_kv_proj",
    )(mem, w_kv_b)


def _memattn_kernel(q_ref, k_ref, v_ref, o_ref):
    q = q_ref[...]
    k = k_ref[...]
    v = v_ref[...]
    outs = []
    for h in range(2):
        qh = jnp.where(_head_mask(q.shape, h), q, jnp.zeros_like(q))
        s = _dot_nt(qh, k)
        m = jnp.max(s, axis=1, keepdims=True)
        p = jnp.exp(s - m)
        l = jnp.sum(p, axis=1, keepdims=True)
        outs.append(jnp.dot(p.astype(BF16), v, preferred_element_type=F32) / l)
    o_ref[...] = jnp.where(_head_mask(outs[0].shape, 0), outs[0], outs[1]).astype(o_ref.dtype)


def _mem_attention(qx3, k_mem, v_mem, tq=512):
    batch, seq, w = qx3.shape
    m = k_mem.shape[1]
    return pl.pallas_call(
        _memattn_kernel,
        grid=(batch, w // LANES, seq // tq),
        in_specs=[pl.BlockSpec((None, tq, LANES), lambda b, p, i: (b, i, p)),
                  pl.BlockSpec((None, m, LANES), lambda b, p, i: (b, 0, p)),
                  pl.BlockSpec((None, m, LANES), lambda b, p, i: (b, 0, p))],
        out_specs=pl.BlockSpec((None, tq, LANES), lambda b, p, i: (b, i, p)),
        out_shape=jax.ShapeDtypeStruct((batch, seq, w), BF16),
        compiler_params=_cparams("parallel", "parallel", "parallel"),
        name="mem_attn",
    )(qx3, k_mem, v_mem)


def _merge_kernel(alpha, x_ref, o1_ref, o2_ref, o3_ref, l1_ref, l2_ref, l3_ref, ym_ref, yx_ref,
                  wg_ref, bg_ref, wbd_ref, wbm_ref, wbx_ref, wo_ref, g_ref, b_ref, wrh_ref, wrl_ref, rb_ref,
                  out_ref, packed_ref, eidx_ref, gw_ref, rank_ref, cnt_ref, stage, carry, x1_prev):
    step = pl.program_id(0)
    route_refs = (wrh_ref, wrl_ref, rb_ref, eidx_ref, gw_ref, rank_ref, cnt_ref, carry)

    @pl.when(step == 0)
    def _():
        carry[...] = jnp.zeros_like(carry)
        x1_prev[...] = jnp.zeros_like(x1_prev)

    route = _route_tile(x1_prev[...], step > 0, *route_refs)
    next(route)
    x = x_ref[...]
    xb = x.astype(BF16)
    tm, d = x.shape

    def token_major(ref, slot):
        dilation = ref.shape[0]
        if dilation == 1:
            return ref[0].astype(F32)
        for r in range(dilation):
            stage[slot, pl.ds(r, tm // dilation, stride=dilation), :] = ref[r].astype(F32)
        return stage[slot]

    o1, o2, o3 = (token_major(r, s) for s, r in enumerate((o1_ref, o2_ref, o3_ref)))
    l1, l2, l3 = (token_major(r, s + 3) for s, r in enumerate((l1_ref, l2_ref, l3_ref)))
    mx = jnp.maximum(jnp.maximum(l1, l2), l3)
    e1, e2, e3 = jnp.exp(l1 - mx), jnp.exp(l2 - mx), jnp.exp(l3 - mx)
    y_dil = (e1 * o1 + e2 * o2 + e3 * o3) / (e1 + e2 + e3)
    next(route)
    branches = (
        jnp.dot(y_dil.astype(BF16), wbd_ref[...], preferred_element_type=F32),
        jnp.dot(ym_ref[...], wbm_ref[...], preferred_element_type=F32),
        jnp.dot(yx_ref[...], wbx_ref[...], preferred_element_type=F32),
    )
    merged = jnp.zeros_like(x)
    for i, br in enumerate(branches):
        next(route)
        logits = jnp.dot(xb, wg_ref[:, i * d:(i + 1) * d], preferred_element_type=F32) + bg_ref[:, i * d:(i + 1) * d]
        merged = merged + jax.nn.sigmoid(logits) * br
    next(route)
    mix = jnp.dot(merged.astype(BF16), wo_ref[...], preferred_element_type=F32)
    for _ in route:
        pass
    x1 = _layer_norm(alpha * x + mix, g_ref[...], b_ref[...])
    out_ref[...] = x1
    packed_ref[...] = _pack_halves(x1)
    x1_prev[...] = x1


def _merge(alpha, x2, o_dil, lse_dil, y_moba, y_mem, wg, bg, wbd, wbm, wbx, wo, g, b, wr_hi_t, wr_lo_t, rbias_b,
           seq, tm=512):
    n, d = x2.shape
    tiles = seq // tm
    last = n // tm - 1
    cur = lambda i: jnp.minimum(i, last)
    row = lambda w: pl.BlockSpec((tm, w), lambda i: (cur(i), 0))
    full = lambda a: pl.BlockSpec(a.shape, lambda i: (0, 0))
    residue_major = lambda a: pl.BlockSpec((None, a.shape[1], tm // a.shape[1], LANES),
                                           lambda i: (cur(i) // tiles, 0, cur(i) % tiles, 0))
    slots = pl.BlockSpec((TOP_K, tm), lambda i: (0, jnp.maximum(i - 1, 0)))
    weights = (wg, bg, wbd, wbm, wbx, wo, g, b, wr_hi_t, wr_lo_t, rbias_b)
    tok = lambda dt: jax.ShapeDtypeStruct((TOP_K, n), dt)
    return pl.pallas_call(
        functools.partial(_merge_kernel, alpha),
        grid=(n // tm + 1,),
        in_specs=([row(d)] + [residue_major(a) for a in (*o_dil, *lse_dil)] + [row(W_MOBA), row(W_MEM)]
                  + [full(a) for a in weights]),
        out_specs=[row(d), row(PACKED), slots, slots, slots, pl.BlockSpec((N_EXPERTS, LANES), lambda i: (0, 0))],
        out_shape=[jax.ShapeDtypeStruct((n, d), F32), jax.ShapeDtypeStruct((n, PACKED), jnp.uint32),
                   tok(jnp.int32), tok(F32), tok(jnp.int32), jax.ShapeDtypeStruct((N_EXPERTS, LANES), jnp.int32)],
        scratch_shapes=[pltpu.VMEM((2 * len(DIL_PAIRS), tm, LANES), F32), pltpu.VMEM((N_EXPERTS, LANES), F32),
                        pltpu.VMEM((tm, d), F32)],
        compiler_params=_cparams("arbitrary"),
        name="merge_outproj_ln1_route",
    )(x2, *o_dil, *lse_dil, y_moba, y_mem, *weights)


def _token_mixer(x2, mem, positions, w_in, w_mem_kv, w_gate, b_gate, w_br_dil, w_br_moba, w_br_mem,
                 w_out, ln1_g, ln1_b, w_router, router_bias, batch, seq, alpha):
    wr_t = w_router.T
    wr_hi = wr_t.astype(BF16)
    wr_lo = (wr_t - wr_hi.astype(F32)).astype(BF16)
    rbias_b = jnp.broadcast_to(router_bias.astype(F32)[:, None], (N_EXPERTS, LANES))
    cos_t, sin_t = _rope_tables(positions)
    (q_dil, k_dil, v_dil), (qm, km, vm, qx) = _in_projection(x2, w_in.astype(BF16), cos_t, sin_t, batch, seq)
    o_dil, lse_dil = [], []
    for g in range(len(DIL_PAIRS)):
        o, lse = _dilated_attention(q_dil[g], k_dil[g], v_dil[g])
        o_dil.append(o)
        lse_dil.append(lse)
    qm3, km3, vm3 = (t.reshape(batch, seq, W_MOBA) for t in (qm, km, vm))
    kmh, kml = _moba_kmean(km3)
    y_moba = _moba_attention(qm3, km3, vm3, kmh, kml).reshape(batch * seq, W_MOBA)
    k_mem, v_mem = _mem_kv(mem, w_mem_kv.astype(BF16))
    y_mem = _mem_attention(qx.reshape(batch, seq, W_MEM), k_mem, v_mem).reshape(batch * seq, W_MEM)
    return _merge(alpha, x2, o_dil, lse_dil, y_moba, y_mem,
                  w_gate.astype(BF16), b_gate.reshape(1, -1), w_br_dil.astype(BF16),
                  w_br_moba.astype(BF16), w_br_mem.astype(BF16), w_out.astype(BF16),
                  ln1_g.reshape(1, -1), ln1_b.reshape(1, -1), wr_hi, wr_lo, rbias_b, seq)


EXPERT_ROWS = 256
EXPERT_RING = 6
TOKEN_TILE = 256
PACKED = 512
SC_WINDOW = 128


def _first_index_of_max(v, iota_f, size):
    m = jnp.max(v, axis=0, keepdims=True)
    idx = jnp.min(jnp.where(v == m, iota_f, float(size)), axis=0, keepdims=True)
    return m, idx


def _route_tile(x, counted, wh_ref, wl_ref, bias_ref, eidx_ref, gw_ref, rank_ref, cnt_ref, carry):
    tm = x.shape[0]
    xh = x.astype(BF16)
    xl = (x - xh.astype(F32)).astype(BF16)
    wh = wh_ref[...]
    logits = _dot_nt(wh, xh) + _dot_nt(wh, xl) + _dot_nt(wl_ref[...], xh)
    yield
    scores = jax.nn.sigmoid(logits)
    biased = scores + bias_ref[...][:, :1]

    giota = lax.broadcasted_iota(jnp.int32, (GROUP_SIZE, tm), 0).astype(F32)
    group_scores = []
    for g in range(N_GROUPS):
        slab = biased[g * GROUP_SIZE:(g + 1) * GROUP_SIZE, :]
        m1, i1 = _first_index_of_max(slab, giota, GROUP_SIZE)
        m2 = jnp.max(jnp.where(giota == i1, -jnp.inf, slab), axis=0, keepdims=True)
        group_scores.append(m1 + m2)
    gs = jnp.concatenate(group_scores, axis=0)
    gidx = lax.broadcasted_iota(jnp.int32, (N_GROUPS, tm), 0)
    beaten = jnp.zeros((N_GROUPS, tm), jnp.int32)
    for gp in range(N_GROUPS):
        row = gs[gp:gp + 1, :]
        beaten = beaten + jnp.where((row > gs) | ((row == gs) & (gidx > gp)), 1, 0)
    keep = beaten < TOPK_GROUPS
    masked = jnp.concatenate(
        [jnp.where(keep[g:g + 1, :], biased[g * GROUP_SIZE:(g + 1) * GROUP_SIZE, :], -jnp.inf)
         for g in range(N_GROUPS)], axis=0)
    yield

    eiota = lax.broadcasted_iota(jnp.int32, (N_EXPERTS, tm), 0).astype(F32)
    idx_rows, gw_rows = [], []
    for k in range(TOP_K):
        _, idx = _first_index_of_max(masked, eiota, N_EXPERTS)
        hit = eiota == idx
        gw_rows.append(jnp.sum(jnp.where(hit, scores, 0.0), axis=0, keepdims=True))
        masked = jnp.where(hit, -jnp.inf, masked)
        idx_rows.append(idx)
        if k % 2 == 1:
            yield
    idx8 = jnp.concatenate(idx_rows, axis=0)
    gw8 = jnp.concatenate(gw_rows, axis=0)
    gw_ref[...] = gw8 / jnp.sum(gw8, axis=0, keepdims=True) * ROUTED_SCALE
    eidx_ref[...] = idx8.astype(jnp.int32)

    onehot = jnp.zeros((N_EXPERTS, tm), F32)
    for k in range(TOP_K):
        onehot = onehot + jnp.where(eiota == idx8[k:k + 1, :], 1.0, 0.0)
    t_row = lax.broadcasted_iota(jnp.int32, (tm, tm), 0)
    t_col = lax.broadcasted_iota(jnp.int32, (tm, tm), 1)
    earlier = jnp.where(t_row < t_col, 1.0, 0.0).astype(BF16)
    prefix = jnp.dot(onehot.astype(BF16), earlier, preferred_element_type=F32)
    base = carry[...]
    prefix = prefix + jnp.concatenate([base] * (tm // LANES), axis=1)
    rank_rows = [jnp.sum(jnp.where(eiota == idx8[k:k + 1, :], prefix, 0.0), axis=0, keepdims=True)
                 for k in range(TOP_K)]
    rank_ref[...] = jnp.concatenate(rank_rows, axis=0).astype(jnp.int32)
    total = base + jnp.where(counted, jnp.sum(onehot, axis=1, keepdims=True), 0.0)
    carry[...] = total
    cnt_ref[...] = total.astype(jnp.int32)


def _dest_kernel(eidx_ref, rank_ref, start_ref, dest_ref):
    eidx = eidx_ref[...]
    tm = eidx.shape[1]
    eiota = lax.broadcasted_iota(jnp.int32, (N_EXPERTS, tm), 0)
    start = start_ref[...][:, :1]
    rows = [jnp.sum(jnp.where(eiota == eidx[k:k + 1, :], start, 0.0), axis=0, keepdims=True)
            for k in range(TOP_K)]
    dest_ref[...] = jnp.concatenate(rows, axis=0).astype(jnp.int32) + rank_ref[...]


def _dest_rows(eidx, rank, start_b):
    n = eidx.shape[1]
    nt = n // TOKEN_TILE
    return pl.pallas_call(
        _dest_kernel,
        grid=(nt,),
        in_specs=[pl.BlockSpec((TOP_K, TOKEN_TILE), lambda i: (0, i)),
                  pl.BlockSpec((TOP_K, TOKEN_TILE), lambda i: (0, i)),
                  pl.BlockSpec((N_EXPERTS, LANES), lambda i: (0, 0))],
        out_specs=pl.BlockSpec((TOP_K, TOKEN_TILE), lambda i: (0, i)),
        out_shape=jax.ShapeDtypeStruct((TOP_K, n), jnp.int32),
        compiler_params=_cparams("parallel"),
        name="moe_dest_rows",
    )(eidx, rank, start_b)


def _pack_halves(v):
    return pltpu.pack_elementwise([v[:, :PACKED], v[:, PACKED:]], packed_dtype=BF16)


def _unpack_half(p, index):
    return pltpu.unpack_elementwise(p, index=index, packed_dtype=BF16, unpacked_dtype=F32)


def _sc_mesh():
    return plsc.VectorSubcoreMesh(core_axis_name="core", subcore_axis_name="subcore")


def _sc_move_rows(src, src_idx, dst_idx, out_rows, name):
    width = src.shape[1]
    count = src_idx.shape[1]

    @functools.partial(pl.kernel, out_type=jax.ShapeDtypeStruct((out_rows, width), src.dtype),
                       mesh=_sc_mesh(), scratch_types=[pltpu.VMEM((SC_WINDOW, width), src.dtype)], name=name)
    def move(src_hbm, sidx_hbm, didx_hbm, out_hbm, rows_vmem):
        def body(sidx_vmem, didx_vmem):
            pltpu.sync_copy(src_hbm.at[sidx_vmem.at[0]], rows_vmem)
            pltpu.sync_copy(rows_vmem, out_hbm.at[didx_vmem.at[0]])

        idx_spec = pl.BlockSpec((1, SC_WINDOW), index_map=lambda i: (0, i))
        pltpu.emit_pipeline(
            body,
            grid=(count // SC_WINDOW,),
            in_specs=[idx_spec, idx_spec],
            out_specs=[],
            core_axis_name=("core", "subcore"),
            dimension_semantics=(pltpu.PARALLEL,),
        )(sidx_hbm, didx_hbm)

    return move(src, src_idx, dst_idx)


def _sc_dispatch_rows(src, token_idx, dest, out_rows):
    n, width = src.shape
    slots = dest.shape[0]

    @functools.partial(pl.kernel, out_type=jax.ShapeDtypeStruct((out_rows, width), src.dtype),
                       mesh=_sc_mesh(), scratch_types=[pltpu.VMEM((SC_WINDOW, width), src.dtype)],
                       name="moe_dispatch_sc")
    def dispatch(src_hbm, tidx_hbm, dest_hbm, out_hbm, rows_vmem):
        def body(tidx_vmem, dest_vmem):
            pltpu.sync_copy(src_hbm.at[tidx_vmem.at[0]], rows_vmem)
            for k in range(slots):
                pltpu.sync_copy(rows_vmem, out_hbm.at[dest_vmem.at[k]])

        pltpu.emit_pipeline(
            body,
            grid=(n // SC_WINDOW,),
            in_specs=[pl.BlockSpec((1, SC_WINDOW), index_map=lambda i: (0, i)),
                      pl.BlockSpec((slots, SC_WINDOW), index_map=lambda i: (0, i))],
            out_specs=[],
            core_axis_name=("core", "subcore"),
            dimension_semantics=(pltpu.PARALLEL,),
        )(tidx_hbm, dest_hbm)

    return dispatch(src, token_idx, dest)


def _expert_kernel(first_blk_ref, count_ref, total_ref, xs_hbm, wg_ref, wu_ref, wd_ref, ys_hbm,
                   wg_b, wu_b, wd_b, x_ring, y_ring, in_sem, out_sem):
    e = pl.program_id(0)
    total = total_ref[0]
    first = first_blk_ref[e]
    count = count_ref[e]
    nblk = (count + EXPERT_ROWS - 1) // EXPERT_ROWS
    spare_block = ys_hbm.shape[0] // EXPERT_ROWS - EXPERT_RING

    def rows_of(g):
        return pl.ds(pl.multiple_of(g * EXPERT_ROWS, EXPERT_ROWS), EXPERT_ROWS)

    def in_copy(g):
        slot = g % EXPERT_RING
        return pltpu.make_async_copy(xs_hbm.at[rows_of(g)], x_ring.at[slot], in_sem.at[slot])

    def out_copy(g, slot):
        return pltpu.make_async_copy(y_ring.at[slot], ys_hbm.at[rows_of(g)], out_sem.at[slot])

    @pl.when(e == 0)
    def _():
        y_ring[...] = jnp.zeros_like(y_ring)
        for g in range(EXPERT_RING - 1):
            in_copy(g).start()
        for slot in range(EXPERT_RING):
            out_copy(spare_block + slot, slot).start()

    @pl.when(nblk > 0)
    def _():
        wg_b[...] = wg_ref[...].astype(BF16)
        wu_b[...] = wu_ref[...].astype(BF16)
        wd_b[...] = wd_ref[...].astype(BF16)

    def blocks(js):
        gs = [first + j for j in js]
        for g in gs:
            in_copy(g).wait()
        ys = []
        for j, g in zip(js, gs):
            p = x_ring[g % EXPERT_RING]
            live = lax.broadcasted_iota(jnp.int32, p.shape, 0) + j * EXPERT_ROWS < count
            lo = jnp.where(live, _unpack_half(p, 0), 0.0).astype(BF16)
            hi = jnp.where(live, _unpack_half(p, 1), 0.0).astype(BF16)

            def up(w, lo=lo, hi=hi):
                return (jnp.dot(lo, w[:PACKED, :], preferred_element_type=F32)
                        + jnp.dot(hi, w[PACKED:, :], preferred_element_type=F32))

            hid = (jax.nn.silu(up(wg_b)) * up(wu_b)).astype(BF16)
            ys.append(_pack_halves(jnp.dot(hid, wd_b[...], preferred_element_type=F32)))
        for g in gs:
            in_copy(g + EXPERT_RING - 1).start()
        for g in gs:
            out_copy(g, g % EXPERT_RING).wait()
        for g, y in zip(gs, ys):
            y_ring[g % EXPERT_RING] = y
        for g in gs:
            out_copy(g, g % EXPERT_RING).start()

    def pair(i, carry):
        blocks([2 * i, 2 * i + 1])
        return carry

    lax.fori_loop(0, nblk // 2, pair, 0)

    @pl.when(nblk % 2 == 1)
    def _():
        blocks([nblk - 1])

    @pl.when(e == pl.num_programs(0) - 1)
    def _():
        for slot in range(EXPERT_RING):
            out_copy(spare_block + slot, slot).wait()
        for g in range(EXPERT_RING - 1):
            in_copy(total + g).wait()


def _expert_ffn(xs, first_blk, counts, total, w_e_gate, w_e_up, w_e_down):
    rows = xs.shape[0] + EXPERT_ROWS
    n_experts, d = w_e_gate.shape[0], w_e_gate.shape[1]
    w_map = lambda e, fb, ct, tt: (e, 0, 0)
    ring = pltpu.VMEM((EXPERT_RING, EXPERT_ROWS, PACKED), jnp.uint32)
    return pl.pallas_call(
        _expert_kernel,
        grid_spec=pltpu.PrefetchScalarGridSpec(
            num_scalar_prefetch=3,
            grid=(n_experts,),
            in_specs=[pl.BlockSpec(memory_space=pl.ANY),
                      pl.BlockSpec((None, d, D_EXPERT), w_map),
                      pl.BlockSpec((None, d, D_EXPERT), w_map),
                      pl.BlockSpec((None, D_EXPERT, d), w_map)],
            out_specs=pl.BlockSpec(memory_space=pl.ANY),
            scratch_shapes=[pltpu.VMEM((d, D_EXPERT), BF16), pltpu.VMEM((d, D_EXPERT), BF16),
                            pltpu.VMEM((D_EXPERT, d), BF16), ring, ring,
                            pltpu.SemaphoreType.DMA((EXPERT_RING,)), pltpu.SemaphoreType.DMA((EXPERT_RING,))],
        ),
        out_shape=jax.ShapeDtypeStruct((rows, PACKED), jnp.uint32),
        compiler_params=_cparams("arbitrary"),
        name="moe_expert_ffn",
    )(first_blk, counts, total, xs, w_e_gate, w_e_up, w_e_down)


def _combine_kernel(alpha, x_ref, gw_ref, ys_ref, wsg_ref, wsu_ref, wsd_ref, g_ref, b_ref, out_ref):
    x = x_ref[...]
    xb = x.astype(BF16)
    hid = (jax.nn.silu(jnp.dot(xb, wsg_ref[...], preferred_element_type=F32))
           * jnp.dot(xb, wsu_ref[...], preferred_element_type=F32)).astype(BF16)
    shared = jnp.dot(hid, wsd_ref[...], preferred_element_type=F32)
    gw = gw_ref[...]
    lo = jnp.zeros((TOKEN_TILE, PACKED), F32)
    hi = jnp.zeros((TOKEN_TILE, PACKED), F32)
    for k in range(TOP_K):
        p = ys_ref[k]
        w = gw[:, k:k + 1]
        lo = lo + w * _unpack_half(p, 0)
        hi = hi + w * _unpack_half(p, 1)
    routed = jnp.concatenate([lo, hi], axis=1)
    out_ref[...] = _layer_norm(alpha * x + (routed + shared), g_ref[...], b_ref[...])


def _combine(alpha, x1, gw_t, ys_tok, wsg, wsu, wsd, g, b):
    n, d = x1.shape
    full = lambda a: pl.BlockSpec(a.shape, lambda i: (0, 0))
    weights = (wsg, wsu, wsd, g, b)
    return pl.pallas_call(
        functools.partial(_combine_kernel, alpha),
        grid=(n // TOKEN_TILE,),
        in_specs=[pl.BlockSpec((TOKEN_TILE, d), lambda i: (i, 0)),
                  pl.BlockSpec((TOKEN_TILE, TOP_K), lambda i: (i, 0)),
                  pl.BlockSpec((TOP_K, TOKEN_TILE, PACKED), lambda i: (0, i, 0))] + [full(a) for a in weights],
        out_specs=pl.BlockSpec((TOKEN_TILE, d), lambda i: (i, 0)),
        out_shape=jax.ShapeDtypeStruct((n, d), F32),
        compiler_params=_cparams("parallel"),
        name="moe_combine_shared_ln2",
    )(x1, gw_t, ys_tok, *weights)


def _moe_layer(x1, x1_packed, eidx, gw, rank, cnt, w_e_gate, w_e_up, w_e_down, w_s_gate, w_s_up, w_s_down,
               ln2_g, ln2_b, alpha):
    n, d = x1.shape

    counts = cnt[:, 0]
    padded = (counts + EXPERT_ROWS - 1) // EXPERT_ROWS * EXPERT_ROWS
    seg_end = jnp.cumsum(padded)
    seg_start = seg_end - padded
    rows = n * TOP_K + N_EXPERTS * EXPERT_ROWS
    first_blk = (seg_start // EXPERT_ROWS).astype(jnp.int32)
    total_blk = (seg_end[-1:] // EXPERT_ROWS).astype(jnp.int32)
    start_b = jnp.broadcast_to(seg_start.astype(F32)[:, None], (N_EXPERTS, LANES))

    dest = _dest_rows(eidx, rank, start_b)
    dest_flat = dest.reshape(1, TOP_K * n)
    assign = jnp.arange(TOP_K * n, dtype=jnp.int32).reshape(1, TOP_K * n)
    xs = _sc_dispatch_rows(x1_packed, assign[:, :n], dest, rows + (EXPERT_RING - 1) * EXPERT_ROWS)
    ys = _expert_ffn(xs, first_blk, counts, total_blk, w_e_gate, w_e_up, w_e_down)
    ys_tok = _sc_move_rows(ys, dest_flat, assign, TOP_K * n, "moe_gather_sc").reshape(TOP_K, n, PACKED)
    return _combine(alpha, x1, gw.T, ys_tok, w_s_gate.astype(BF16), w_s_up.astype(BF16),
                    w_s_down.astype(BF16), ln2_g.reshape(1, -1), ln2_b.reshape(1, -1))


def kernel(x, mem, positions, w_in, w_mem_kv, w_gate, b_gate, w_br_dil, w_br_moba, w_br_mem, w_out, ln1_g, ln1_b, w_router, router_bias, w_e_gate, w_e_up, w_e_down, w_s_gate, w_s_up, w_s_down, ln2_g, ln2_b):
    batch, seq, d = x.shape
    depth = w_in.shape[0]
    alpha = (2.0 * depth) ** 0.25
    h = x.reshape(batch * seq, d)
    for l in range(depth):
        h, h_packed, eidx, gw, rank, cnt = _token_mixer(
            h, mem, positions, w_in[l], w_mem_kv[l], w_gate[l], b_gate[l], w_br_dil[l], w_br_moba[l], w_br_mem[l],
            w_out[l], ln1_g[l], ln1_b[l], w_router[l], router_bias[l], batch, seq, alpha)
        h = _moe_layer(h, h_packed, eidx, gw, rank, cnt, w_e_gate[l], w_e_up[l], w_e_down[l],
                       w_s_gate[l], w_s_up[l], w_s_down[l], ln2_g[l], ln2_b[l], alpha)
    return h.reshape(batch, seq, d)
```

```python
import functools

import jax
import jax.numpy as jnp
from jax import lax
from jax.experimental import pallas as pl
from jax.experimental.pallas import tpu as pltpu
from jax.experimental.pallas import tpu_sc as plsc

F32 = jnp.float32
BF16 = jnp.bfloat16

LANES = 128
VMEM_LIMIT_BYTES = 48 * 1024 * 1024

HEAD_DIM = 64
ROPE_DIM = HEAD_DIM // 4
ROPE_HALF = ROPE_DIM // 2
ROPE_THETA = 500000.0
DIL_PAIRS = ((128, 1), (512, 4), (2048, 16))
BAND = 128
DIL_TILE = 1024
MEM_CHUNK = 512
W_DIL = 384
W_MOBA = 384
W_MEM = 256
MOBA_BLOCK = 256
MOBA_TOPK = 3
N_EXPERTS = 256
N_GROUPS = 8
GROUP_SIZE = N_EXPERTS // N_GROUPS
TOPK_GROUPS = 4
TOP_K = 8
D_EXPERT = 256
ROUTED_SCALE = 2.5
LN_EPS = 1e-5
NEG = -1e30
QK_SCALE = HEAD_DIM ** -0.5


def _cparams(*sem):
    return pltpu.CompilerParams(dimension_semantics=sem, vmem_limit_bytes=VMEM_LIMIT_BYTES)


def _dot_nt(a, b):
    return lax.dot_general(a, b, (((1,), (1,)), ((), ())), preferred_element_type=F32)


def _layer_norm(v, g, b):
    mu = jnp.mean(v, axis=-1, keepdims=True)
    c = v - mu
    var = jnp.mean(c * c, axis=-1, keepdims=True)
    return c * lax.rsqrt(var + LN_EPS) * g + b


_IN_SECTIONS = (
    (0, W_DIL, True, True),
    (W_DIL, W_DIL, True, False),
    (2 * W_DIL, W_DIL, False, False),
    (3 * W_DIL, W_MOBA, True, True),
    (3 * W_DIL + W_MOBA, W_MOBA, True, False),
    (3 * W_DIL + 2 * W_MOBA, W_MOBA, False, False),
    (3 * W_DIL + 3 * W_MOBA, W_MEM, False, True),
)


N_DIL_SECTIONS = 3


def _inproj_kernel(x_ref, w_ref, cos_ref, sin_ref, *refs):
    n_dil = N_DIL_SECTIONS * len(DIL_PAIRS)
    dil_refs, flat_refs, stage = refs[:n_dil], refs[n_dil:-1], refs[-1]
    xb = x_ref[...].astype(BF16)
    cos = cos_ref[...]
    sin = sin_ref[...]
    tm = xb.shape[0]
    lane = lax.broadcasted_iota(jnp.int32, (tm, LANES), 1)
    first_half = (lane % ROPE_DIM) < ROPE_HALF

    def rope(t):
        partner = jnp.where(first_half, pltpu.roll(t, LANES - ROPE_HALF, 1), pltpu.roll(t, ROPE_HALF, 1))
        return t * cos + partner * sin

    for sec, (off, width, roped, scaled) in enumerate(_IN_SECTIONS):
        acc = jnp.dot(xb, w_ref[:, off:off + width], preferred_element_type=F32)
        for c in range(width // LANES):
            t = acc[:, c * LANES:(c + 1) * LANES]
            if roped:
                t = rope(t)
            if scaled:
                t = t * QK_SCALE
            if sec >= N_DIL_SECTIONS:
                flat_refs[sec - N_DIL_SECTIONS][:, c * LANES:(c + 1) * LANES] = t.astype(BF16)
                continue
            o_ref = dil_refs[sec * len(DIL_PAIRS) + c]
            dilation = DIL_PAIRS[c][1]
            if dilation == 1:
                o_ref[0] = t.astype(BF16)
                continue
            slot = sec * len(DIL_PAIRS) + c
            stage[slot] = t
            for r in range(dilation):
                o_ref[r] = stage[slot, pl.ds(r, tm // dilation, stride=dilation), :].astype(BF16)


def _in_projection(x2, w_in_b, cos_t, sin_t, batch, seq, tm=1024):
    n, d = x2.shape
    w_total = w_in_b.shape[1]
    tiles = seq // tm
    dil_specs, dil_shapes = [], []
    for _ in range(N_DIL_SECTIONS):
        for _, dilation in DIL_PAIRS:
            dil_specs.append(pl.BlockSpec((None, dilation, tm // dilation, LANES),
                                          lambda i: (i // tiles, 0, i % tiles, 0)))
            dil_shapes.append(jax.ShapeDtypeStruct((batch, dilation, seq // dilation, LANES), BF16))
    widths = [s[1] for s in _IN_SECTIONS[N_DIL_SECTIONS:]]
    outs = pl.pallas_call(
        _inproj_kernel,
        grid=(n // tm,),
        in_specs=[
            pl.BlockSpec((tm, d), lambda i: (i, 0)),
            pl.BlockSpec((d, w_total), lambda i: (0, 0)),
            pl.BlockSpec((tm, LANES), lambda i: (i, 0)),
            pl.BlockSpec((tm, LANES), lambda i: (i, 0)),
        ],
        out_specs=dil_specs + [pl.BlockSpec((tm, w), lambda i: (i, 0)) for w in widths],
        out_shape=dil_shapes + [jax.ShapeDtypeStruct((n, w), BF16) for w in widths],
        scratch_shapes=[pltpu.VMEM((N_DIL_SECTIONS * len(DIL_PAIRS), tm, LANES), F32)],
        compiler_params=_cparams("parallel"),
        name="in_proj_rope",
    )(x2, w_in_b, cos_t, sin_t)
    n_dil = len(dil_specs)
    ng = len(DIL_PAIRS)
    qkv_dil = [outs[s * ng:(s + 1) * ng] for s in range(N_DIL_SECTIONS)]
    return qkv_dil, outs[n_dil:]


def _rope_tables(positions):
    lane = jnp.arange(LANES)
    rotary = (lane % HEAD_DIM) < ROPE_DIM
    inv_freq = ROPE_THETA ** (-(lane % ROPE_HALF).astype(F32) / ROPE_HALF)
    ang = positions.reshape(-1).astype(F32)[:, None] * jnp.where(rotary, inv_freq, 0.0)
    sign = jnp.where((lane % ROPE_DIM) < ROPE_HALF, -1.0, 1.0)
    return jnp.cos(ang), jnp.sin(ang) * sign


def _head_mask(shape, h):
    lane = lax.broadcasted_iota(jnp.int32, shape, 1)
    return (lane // HEAD_DIM) == h


def _dil_kernel(q_ref, kp_ref, k_ref, vp_ref, v_ref, o_ref, lse_ref):
    i = pl.program_id(2)
    n_res, tq = q_ref.shape[0], q_ref.shape[1]
    qi = lax.broadcasted_iota(jnp.int32, (BAND, 2 * BAND), 0)
    kj = lax.broadcasted_iota(jnp.int32, (BAND, 2 * BAND), 1)
    dist = qi + BAND - kj
    band = (dist >= 0) & (dist <= BAND)
    first = _head_mask((BAND, LANES), 0)
    for r in range(n_res):
        for j in range(tq // BAND):
            rows = slice(j * BAND, (j + 1) * BAND)
            q = q_ref[r, rows, :]
            if j == 0:
                k_prev, v_prev = kp_ref[r], vp_ref[r]
                allowed = band & ((kj >= BAND) | (i > 0))
            else:
                prev_rows = slice((j - 1) * BAND, j * BAND)
                k_prev, v_prev = k_ref[r, prev_rows, :], v_ref[r, prev_rows, :]
                allowed = band
            k = jnp.concatenate([k_prev, k_ref[r, rows, :]], axis=0)
            v = jnp.concatenate([v_prev, v_ref[r, rows, :]], axis=0)
            outs, lses = [], []
            for h in range(2):
                qh = jnp.where(_head_mask(q.shape, h), q, jnp.zeros_like(q))
                s = jnp.where(allowed, _dot_nt(qh, k), NEG)
                m = jnp.max(s, axis=1, keepdims=True)
                p = jnp.exp(s - m)
                l = jnp.sum(p, axis=1, keepdims=True)
                outs.append(jnp.dot(p.astype(BF16), v, preferred_element_type=F32) / l)
                lses.append(jnp.broadcast_to(m + jnp.log(l), (BAND, LANES)))
            o_ref[r, rows, :] = jnp.where(first, outs[0], outs[1]).astype(o_ref.dtype)
            lse_ref[r, rows, :] = jnp.where(first, lses[0], lses[1])


def _dilated_attention(q4, k4, v4):
    batch, dilation, steps, _ = q4.shape
    tq = min(steps, DIL_TILE)
    n_res = min(dilation, DIL_TILE // tq)
    per_tile = tq // BAND
    cur = pl.BlockSpec((None, n_res, tq, LANES), lambda b, r, i: (b, r, i, 0))
    prev = pl.BlockSpec((None, n_res, BAND, LANES), lambda b, r, i: (b, r, jnp.maximum(i * per_tile - 1, 0), 0))
    return pl.pallas_call(
        _dil_kernel,
        grid=(batch, dilation // n_res, steps // tq),
        in_specs=[cur, prev, cur, prev, cur],
        out_specs=[cur, cur],
        out_shape=[jax.ShapeDtypeStruct(q4.shape, BF16), jax.ShapeDtypeStruct(q4.shape, F32)],
        compiler_params=_cparams("parallel", "parallel", "arbitrary"),
        name=f"dilated_attn_d{dilation}",
    )(q4, k4, k4, v4, v4)


def _kmean_kernel(k_ref, hi_ref, lo_ref):
    k = k_ref[...].astype(F32)
    s, w = k.shape
    mean = jnp.sum(k.reshape(s // MOBA_BLOCK, MOBA_BLOCK, w), axis=1) / MOBA_BLOCK
    hi = mean.astype(BF16)
    hi_ref[...] = hi
    lo_ref[...] = (mean - hi.astype(F32)).astype(BF16)


def _moba_kmean(km3):
    batch, seq, w = km3.shape
    nb = seq // MOBA_BLOCK
    return pl.pallas_call(
        _kmean_kernel,
        grid=(batch,),
        in_specs=[pl.BlockSpec((None, seq, w), lambda b: (b, 0, 0))],
        out_specs=[pl.BlockSpec((None, nb, w), lambda b: (b, 0, 0))] * 2,
        out_shape=[jax.ShapeDtypeStruct((batch, nb, w), BF16)] * 2,
        compiler_params=_cparams("parallel"),
        name="moba_kmean",
    )(km3)


def _moba_kernel(q_ref, k_ref, v_ref, kmh_ref, kml_ref, o_ref, qaug, kaug, vaug, m_scr, acc_scr):
    s = pl.program_id(2)
    nsteps = pl.num_programs(2)
    nb = kmh_ref.shape[0]
    tq = MOBA_BLOCK
    lane = lax.broadcasted_iota(jnp.int32, (tq, LANES), 1)
    blk = lax.broadcasted_iota(jnp.int32, (nb, tq), 0)
    slot = s % 2

    def select(step, dst_slot):
        for t in range(2):
            qi = step if t == 0 else nb - 1 - step
            q = q_ref[pl.ds(pl.multiple_of(qi * tq, tq), tq), :]
            for h in range(2):
                hm = _head_mask(q.shape, h)
                qh = jnp.where(hm, q, jnp.zeros_like(q))
                gate = _dot_nt(kmh_ref[...], qh) + _dot_nt(kml_ref[...], qh)
                cnt = jnp.zeros((nb, tq), jnp.int32)
                for jp in range(nb):
                    g_jp = gate[jp:jp + 1, :]
                    beats = (g_jp > gate) | ((g_jp == gate) & (blk > jp))
                    cnt = cnt + jnp.where(beats & (qi > jp), 1, 0)
                sel = ((blk < qi) & (cnt < MOBA_TOPK)) | (blk == qi)
                bias_t = jnp.where(sel, 0.0, NEG).astype(F32)
                spare = HEAD_DIM * (1 - h)
                pieces = [jnp.zeros((spare, tq), F32)] if spare else []
                pieces += [bias_t, jnp.zeros((LANES - spare - nb, tq), F32)]
                qaug[dst_slot, t, h] = jnp.where(hm, q, jnp.concatenate(pieces, axis=0).T.astype(BF16))

    @pl.when(s == 0)
    def _():
        for h in range(2):
            spare = HEAD_DIM * (1 - h)
            for j in range(nb):
                onehot = jnp.where(lane == spare + j, 1.0, 0.0).astype(BF16)
                kaug[h, j * tq:(j + 1) * tq, :] = jnp.where(
                    _head_mask((tq, LANES), h), k_ref[j * tq:(j + 1) * tq, :], onehot)
                vaug[h, j * tq:(j + 1) * tq, :] = jnp.where(
                    _head_mask((tq, LANES), h), v_ref[j * tq:(j + 1) * tq, :], jnp.ones((tq, LANES), BF16))
        select(0, 0)

    for t in range(2):
        for h in range(2):
            m_scr[t, h] = jnp.full((tq, LANES), NEG, F32)
            acc_scr[t, h] = jnp.zeros((tq, LANES), F32)

    row = lax.broadcasted_iota(jnp.int32, (tq, tq), 0)
    col = lax.broadcasted_iota(jnp.int32, (tq, tq), 1)
    causal_bias = jnp.where(col <= row, 0.0, NEG).astype(F32)
    starts = [pl.multiple_of(s * tq, tq), pl.multiple_of((nb - 1 - s) * tq, tq)]

    for it in range(nb + 1):
        if it < 2:
            t, kstart = it, starts[it]
        else:
            first = it - 2 < s
            t = jnp.where(first, 0, 1)
            kstart = pl.multiple_of(jnp.where(first, it - 2, it - 2 - s) * tq, tq)
        for h in range(2):
            sc = _dot_nt(qaug[slot, t, h], kaug[h, pl.ds(kstart, tq), :])
            if it < 2:
                sc = sc + causal_bias
            m_old = m_scr[t, h]
            m_new = jnp.maximum(m_old, jnp.max(sc, axis=1, keepdims=True))
            p = jnp.exp(sc - jnp.concatenate([m_new, m_new], axis=1))
            acc_scr[t, h] = (jnp.exp(m_old - m_new) * acc_scr[t, h]
                             + jnp.dot(p.astype(BF16), vaug[h, pl.ds(kstart, tq), :], preferred_element_type=F32))
            m_scr[t, h] = m_new

    first_head = _head_mask((tq, LANES), 0)
    for t in range(2):
        acc = jnp.where(first_head, acc_scr[t, 0], acc_scr[t, 1])
        den = jnp.where(first_head, pltpu.roll(acc_scr[t, 0], HEAD_DIM, 1), pltpu.roll(acc_scr[t, 1], HEAD_DIM, 1))
        o_ref[pl.ds(starts[t], tq), :] = (acc / den).astype(o_ref.dtype)

    select(jnp.minimum(s + 1, nsteps - 1), 1 - slot)


def _moba_attention(qm3, km3, vm3, kmh, kml):
    batch, seq, w = qm3.shape
    nb = seq // MOBA_BLOCK
    assert nb % 2 == 0 and nb <= HEAD_DIM
    npair = w // LANES
    tq = MOBA_BLOCK
    seq_spec = pl.BlockSpec((None, seq, LANES), lambda b, p, i: (b, 0, p))
    km_spec = pl.BlockSpec((None, nb, LANES), lambda b, p, i: (b, 0, p))
    state = pltpu.VMEM((2, 2, tq, LANES), F32)
    return pl.pallas_call(
        _moba_kernel,
        grid=(batch, npair, nb // 2),
        in_specs=[seq_spec, seq_spec, seq_spec, km_spec, km_spec],
        out_specs=seq_spec,
        out_shape=jax.ShapeDtypeStruct((batch, seq, w), BF16),
        scratch_shapes=[pltpu.VMEM((2, 2, 2, tq, LANES), BF16), pltpu.VMEM((2, seq, LANES), BF16),
                        pltpu.VMEM((2, seq, LANES), BF16), state, state],
        compiler_params=_cparams("parallel", "parallel", "arbitrary"),
        name="moba_attn",
    )(qm3, km3, vm3, kmh, kml)


def _memkv_kernel(mem_ref, w_ref, k_ref, v_ref):
    kv = jnp.dot(mem_ref[...].astype(BF16), w_ref[...], preferred_element_type=F32)
    k_ref[...] = kv[:, :W_MEM].astype(BF16)
    v_ref[...] = kv[:, W_MEM:].astype(BF16)


def _mem_kv(mem, w_kv_b):
    batch, m, d = mem.shape
    return pl.pallas_call(
        _memkv_kernel,
        grid=(batch,),
        in_specs=[pl.BlockSpec((None, m, d), lambda b: (b, 0, 0)),
                  pl.BlockSpec((d, 2 * W_MEM), lambda b: (0, 0))],
        out_specs=[pl.BlockSpec((None, m, W_MEM), lambda b: (b, 0, 0))] * 2,
        out_shape=[jax.ShapeDtypeStruct((batch, m, W_MEM), BF16)] * 2,
        compiler_params=_cparams("parallel"),
        name="mem_kv_proj",
    )(mem, w_kv_b)


def _memattn_kernel(q_ref, k_ref, v_ref, o_ref):
    tq, w = q_ref.shape
    for pair in range(w // LANES):
        lanes = slice(pair * LANES, (pair + 1) * LANES)
        k = k_ref[:, lanes]
        v = v_ref[:, lanes]
        for c in range(tq // MEM_CHUNK):
            rows = slice(c * MEM_CHUNK, (c + 1) * MEM_CHUNK)
            q = q_ref[rows, lanes]
            outs = []
            for h in range(2):
                qh = jnp.where(_head_mask(q.shape, h), q, jnp.zeros_like(q))
                s = _dot_nt(qh, k)
                m = jnp.max(s, axis=1, keepdims=True)
                p = jnp.exp(s - m)
                l = jnp.sum(p, axis=1, keepdims=True)
                outs.append(jnp.dot(p.astype(BF16), v, preferred_element_type=F32) / l)
            o_ref[rows, lanes] = jnp.where(_head_mask(outs[0].shape, 0), outs[0], outs[1]).astype(o_ref.dtype)


def _mem_attention(qx3, k_mem, v_mem, tq=2048):
    batch, seq, w = qx3.shape
    m = k_mem.shape[1]
    tq = min(tq, seq)
    return pl.pallas_call(
        _memattn_kernel,
        grid=(batch, seq // tq),
        in_specs=[pl.BlockSpec((None, tq, w), lambda b, i: (b, i, 0)),
                  pl.BlockSpec((None, m, w), lambda b, i: (b, 0, 0)),
                  pl.BlockSpec((None, m, w), lambda b, i: (b, 0, 0))],
        out_specs=pl.BlockSpec((None, tq, w), lambda b, i: (b, i, 0)),
        out_shape=jax.ShapeDtypeStruct((batch, seq, w), BF16),
        compiler_params=_cparams("parallel", "parallel"),
        name="mem_attn",
    )(qx3, k_mem, v_mem)


def _merge_kernel(alpha, x_ref, o1_ref, o2_ref, o3_ref, l1_ref, l2_ref, l3_ref, ym_ref, yx_ref,
                  wg_ref, bg_ref, wbd_ref, wbm_ref, wbx_ref, wo_ref, g_ref, b_ref, wrh_ref, wrl_ref, rb_ref,
                  out_ref, packed_ref, eidx_ref, gw_ref, rank_ref, cnt_ref, stage, carry, x1_prev):
    step = pl.program_id(0)
    route_refs = (wrh_ref, wrl_ref, rb_ref, eidx_ref, gw_ref, rank_ref, cnt_ref, carry)

    @pl.when(step == 0)
    def _():
        carry[...] = jnp.zeros_like(carry)
        x1_prev[...] = jnp.zeros_like(x1_prev)

    route = _route_tile(x1_prev[...], step > 0, *route_refs)
    next(route)
    x = x_ref[...]
    xb = x.astype(BF16)
    tm, d = x.shape

    def token_major(ref, slot):
        dilation = ref.shape[0]
        if dilation == 1:
            return ref[0].astype(F32)
        for r in range(dilation):
            stage[slot, pl.ds(r, tm // dilation, stride=dilation), :] = ref[r].astype(F32)
        return stage[slot]

    o1, o2, o3 = (token_major(r, s) for s, r in enumerate((o1_ref, o2_ref, o3_ref)))
    l1, l2, l3 = (token_major(r, s + 3) for s, r in enumerate((l1_ref, l2_ref, l3_ref)))
    mx = jnp.maximum(jnp.maximum(l1, l2), l3)
    e1, e2, e3 = jnp.exp(l1 - mx), jnp.exp(l2 - mx), jnp.exp(l3 - mx)
    y_dil = (e1 * o1 + e2 * o2 + e3 * o3) / (e1 + e2 + e3)
    next(route)
    branches = (
        jnp.dot(y_dil.astype(BF16), wbd_ref[...], preferred_element_type=F32),
        jnp.dot(ym_ref[...], wbm_ref[...], preferred_element_type=F32),
        jnp.dot(yx_ref[...], wbx_ref[...], preferred_element_type=F32),
    )
    merged = jnp.zeros_like(x)
    for i, br in enumerate(branches):
        next(route)
        logits = jnp.dot(xb, wg_ref[:, i * d:(i + 1) * d], preferred_element_type=F32) + bg_ref[:, i * d:(i + 1) * d]
        merged = merged + jax.nn.sigmoid(logits) * br
    next(route)
    mix = jnp.dot(merged.astype(BF16), wo_ref[...], preferred_element_type=F32)
    for _ in route:
        pass
    x1 = _layer_norm(alpha * x + mix, g_ref[...], b_ref[...])
    out_ref[...] = x1
    packed_ref[...] = _pack_halves(x1)
    x1_prev[...] = x1


def _merge(alpha, x2, o_dil, lse_dil, y_moba, y_mem, wg, bg, wbd, wbm, wbx, wo, g, b, wr_hi_t, wr_lo_t, rbias_b,
           seq, tm=512):
    n, d = x2.shape
    tiles = seq // tm
    last = n // tm - 1
    cur = lambda i: jnp.minimum(i, last)
    row = lambda w: pl.BlockSpec((tm, w), lambda i: (cur(i), 0))
    full = lambda a: pl.BlockSpec(a.shape, lambda i: (0, 0))
    residue_major = lambda a: pl.BlockSpec((None, a.shape[1], tm // a.shape[1], LANES),
                                           lambda i: (cur(i) // tiles, 0, cur(i) % tiles, 0))
    slots = pl.BlockSpec((TOP_K, tm), lambda i: (0, jnp.maximum(i - 1, 0)))
    weights = (wg, bg, wbd, wbm, wbx, wo, g, b, wr_hi_t, wr_lo_t, rbias_b)
    tok = lambda dt: jax.ShapeDtypeStruct((TOP_K, n), dt)
    return pl.pallas_call(
        functools.partial(_merge_kernel, alpha),
        grid=(n // tm + 1,),
        in_specs=([row(d)] + [residue_major(a) for a in (*o_dil, *lse_dil)] + [row(W_MOBA), row(W_MEM)]
                  + [full(a) for a in weights]),
        out_specs=[row(d), row(PACKED), slots, slots, slots, pl.BlockSpec((N_EXPERTS, LANES), lambda i: (0, 0))],
        out_shape=[jax.ShapeDtypeStruct((n, d), F32), jax.ShapeDtypeStruct((n, PACKED), jnp.uint32),
                   tok(jnp.int32), tok(F32), tok(jnp.int32), jax.ShapeDtypeStruct((N_EXPERTS, LANES), jnp.int32)],
        scratch_shapes=[pltpu.VMEM((2 * len(DIL_PAIRS), tm, LANES), F32), pltpu.VMEM((N_EXPERTS, LANES), F32),
                        pltpu.VMEM((tm, d), F32)],
        compiler_params=_cparams("arbitrary"),
        name="merge_outproj_ln1_route",
    )(x2, *o_dil, *lse_dil, y_moba, y_mem, *weights)


def _token_mixer(x2, mem, positions, w_in, w_mem_kv, w_gate, b_gate, w_br_dil, w_br_moba, w_br_mem,
                 w_out, ln1_g, ln1_b, w_router, router_bias, batch, seq, alpha):
    wr_t = w_router.T
    wr_hi = wr_t.astype(BF16)
    wr_lo = (wr_t - wr_hi.astype(F32)).astype(BF16)
    rbias_b = jnp.broadcast_to(router_bias.astype(F32)[:, None], (N_EXPERTS, LANES))
    cos_t, sin_t = _rope_tables(positions)
    (q_dil, k_dil, v_dil), (qm, km, vm, qx) = _in_projection(x2, w_in.astype(BF16), cos_t, sin_t, batch, seq)
    o_dil, lse_dil = [], []
    for g in range(len(DIL_PAIRS)):
        o, lse = _dilated_attention(q_dil[g], k_dil[g], v_dil[g])
        o_dil.append(o)
        lse_dil.append(lse)
    qm3, km3, vm3 = (t.reshape(batch, seq, W_MOBA) for t in (qm, km, vm))
    kmh, kml = _moba_kmean(km3)
    y_moba = _moba_attention(qm3, km3, vm3, kmh, kml).reshape(batch * seq, W_MOBA)
    k_mem, v_mem = _mem_kv(mem, w_mem_kv.astype(BF16))
    y_mem = _mem_attention(qx.reshape(batch, seq, W_MEM), k_mem, v_mem).reshape(batch * seq, W_MEM)
    return _merge(alpha, x2, o_dil, lse_dil, y_moba, y_mem,
                  w_gate.astype(BF16), b_gate.reshape(1, -1), w_br_dil.astype(BF16),
                  w_br_moba.astype(BF16), w_br_mem.astype(BF16), w_out.astype(BF16),
                  ln1_g.reshape(1, -1), ln1_b.reshape(1, -1), wr_hi, wr_lo, rbias_b, seq)


EXPERT_ROWS = 256
EXPERT_RING = 6
TOKEN_TILE = 512
PACKED = 512
SC_WINDOW = 128


def _first_index_of_max(v, iota_f, size):
    m = jnp.max(v, axis=0, keepdims=True)
    idx = jnp.min(jnp.where(v == m, iota_f, float(size)), axis=0, keepdims=True)
    return m, idx


def _route_tile(x, counted, wh_ref, wl_ref, bias_ref, eidx_ref, gw_ref, rank_ref, cnt_ref, carry):
    tm = x.shape[0]
    xh = x.astype(BF16)
    xl = (x - xh.astype(F32)).astype(BF16)
    wh = wh_ref[...]
    logits = _dot_nt(wh, xh) + _dot_nt(wh, xl) + _dot_nt(wl_ref[...], xh)
    yield
    scores = jax.nn.sigmoid(logits)
    biased = scores + bias_ref[...][:, :1]

    giota = lax.broadcasted_iota(jnp.int32, (GROUP_SIZE, tm), 0).astype(F32)
    group_scores = []
    for g in range(N_GROUPS):
        slab = biased[g * GROUP_SIZE:(g + 1) * GROUP_SIZE, :]
        m1, i1 = _first_index_of_max(slab, giota, GROUP_SIZE)
        m2 = jnp.max(jnp.where(giota == i1, -jnp.inf, slab), axis=0, keepdims=True)
        group_scores.append(m1 + m2)
    gs = jnp.concatenate(group_scores, axis=0)
    gidx = lax.broadcasted_iota(jnp.int32, (N_GROUPS, tm), 0)
    beaten = jnp.zeros((N_GROUPS, tm), jnp.int32)
    for gp in range(N_GROUPS):
        row = gs[gp:gp + 1, :]
        beaten = beaten + jnp.where((row > gs) | ((row == gs) & (gidx > gp)), 1, 0)
    keep = beaten < TOPK_GROUPS
    masked = jnp.concatenate(
        [jnp.where(keep[g:g + 1, :], biased[g * GROUP_SIZE:(g + 1) * GROUP_SIZE, :], -jnp.inf)
         for g in range(N_GROUPS)], axis=0)
    yield

    eiota = lax.broadcasted_iota(jnp.int32, (N_EXPERTS, tm), 0).astype(F32)
    idx_rows, gw_rows = [], []
    for k in range(TOP_K):
        _, idx = _first_index_of_max(masked, eiota, N_EXPERTS)
        hit = eiota == idx
        gw_rows.append(jnp.sum(jnp.where(hit, scores, 0.0), axis=0, keepdims=True))
        masked = jnp.where(hit, -jnp.inf, masked)
        idx_rows.append(idx)
        if k % 2 == 1:
            yield
    idx8 = jnp.concatenate(idx_rows, axis=0)
    gw8 = jnp.concatenate(gw_rows, axis=0)
    gw_ref[...] = gw8 / jnp.sum(gw8, axis=0, keepdims=True) * ROUTED_SCALE
    eidx_ref[...] = idx8.astype(jnp.int32)

    onehot = jnp.zeros((N_EXPERTS, tm), F32)
    for k in range(TOP_K):
        onehot = onehot + jnp.where(eiota == idx8[k:k + 1, :], 1.0, 0.0)
    t_row = lax.broadcasted_iota(jnp.int32, (tm, tm), 0)
    t_col = lax.broadcasted_iota(jnp.int32, (tm, tm), 1)
    earlier = jnp.where(t_row < t_col, 1.0, 0.0).astype(BF16)
    prefix = jnp.dot(onehot.astype(BF16), earlier, preferred_element_type=F32)
    base = carry[...]
    prefix = prefix + jnp.concatenate([base] * (tm // LANES), axis=1)
    rank_rows = [jnp.sum(jnp.where(eiota == idx8[k:k + 1, :], prefix, 0.0), axis=0, keepdims=True)
                 for k in range(TOP_K)]
    rank_ref[...] = jnp.concatenate(rank_rows, axis=0).astype(jnp.int32)
    total = base + jnp.where(counted, jnp.sum(onehot, axis=1, keepdims=True), 0.0)
    carry[...] = total
    cnt_ref[...] = total.astype(jnp.int32)


def _dest_kernel(eidx_ref, rank_ref, start_ref, dest_ref):
    eidx = eidx_ref[...]
    tm = eidx.shape[1]
    eiota = lax.broadcasted_iota(jnp.int32, (N_EXPERTS, tm), 0)
    start = start_ref[...][:, :1]
    rows = [jnp.sum(jnp.where(eiota == eidx[k:k + 1, :], start, 0.0), axis=0, keepdims=True)
            for k in range(TOP_K)]
    dest_ref[...] = jnp.concatenate(rows, axis=0).astype(jnp.int32) + rank_ref[...]


def _dest_rows(eidx, rank, start_b):
    n = eidx.shape[1]
    nt = n // TOKEN_TILE
    return pl.pallas_call(
        _dest_kernel,
        grid=(nt,),
        in_specs=[pl.BlockSpec((TOP_K, TOKEN_TILE), lambda i: (0, i)),
                  pl.BlockSpec((TOP_K, TOKEN_TILE), lambda i: (0, i)),
                  pl.BlockSpec((N_EXPERTS, LANES), lambda i: (0, 0))],
        out_specs=pl.BlockSpec((TOP_K, TOKEN_TILE), lambda i: (0, i)),
        out_shape=jax.ShapeDtypeStruct((TOP_K, n), jnp.int32),
        compiler_params=_cparams("parallel"),
        name="moe_dest_rows",
    )(eidx, rank, start_b)


def _pack_halves(v):
    return pltpu.pack_elementwise([v[:, :PACKED], v[:, PACKED:]], packed_dtype=BF16)


def _unpack_half(p, index):
    return pltpu.unpack_elementwise(p, index=index, packed_dtype=BF16, unpacked_dtype=F32)


def _sc_mesh():
    return plsc.VectorSubcoreMesh(core_axis_name="core", subcore_axis_name="subcore")


def _sc_move_rows(src, src_idx, dst_idx, out_rows, name):
    width = src.shape[1]
    count = src_idx.shape[1]

    @functools.partial(pl.kernel, out_type=jax.ShapeDtypeStruct((out_rows, width), src.dtype),
                       mesh=_sc_mesh(), scratch_types=[pltpu.VMEM((SC_WINDOW, width), src.dtype)], name=name)
    def move(src_hbm, sidx_hbm, didx_hbm, out_hbm, rows_vmem):
        def body(sidx_vmem, didx_vmem):
            pltpu.sync_copy(src_hbm.at[sidx_vmem.at[0]], rows_vmem)
            pltpu.sync_copy(rows_vmem, out_hbm.at[didx_vmem.at[0]])

        idx_spec = pl.BlockSpec((1, SC_WINDOW), index_map=lambda i: (0, i))
        pltpu.emit_pipeline(
            body,
            grid=(count // SC_WINDOW,),
            in_specs=[idx_spec, idx_spec],
            out_specs=[],
            core_axis_name=("core", "subcore"),
            dimension_semantics=(pltpu.PARALLEL,),
        )(sidx_hbm, didx_hbm)

    return move(src, src_idx, dst_idx)


def _sc_dispatch_rows(src, token_idx, dest, out_rows):
    n, width = src.shape
    slots = dest.shape[0]

    @functools.partial(pl.kernel, out_type=jax.ShapeDtypeStruct((out_rows, width), src.dtype),
                       mesh=_sc_mesh(), scratch_types=[pltpu.VMEM((SC_WINDOW, width), src.dtype)],
                       name="moe_dispatch_sc")
    def dispatch(src_hbm, tidx_hbm, dest_hbm, out_hbm, rows_vmem):
        def body(tidx_vmem, dest_vmem):
            pltpu.sync_copy(src_hbm.at[tidx_vmem.at[0]], rows_vmem)
            for k in range(slots):
                pltpu.sync_copy(rows_vmem, out_hbm.at[dest_vmem.at[k]])

        pltpu.emit_pipeline(
            body,
            grid=(n // SC_WINDOW,),
            in_specs=[pl.BlockSpec((1, SC_WINDOW), index_map=lambda i: (0, i)),
                      pl.BlockSpec((slots, SC_WINDOW), index_map=lambda i: (0, i))],
            out_specs=[],
            core_axis_name=("core", "subcore"),
            dimension_semantics=(pltpu.PARALLEL,),
        )(tidx_hbm, dest_hbm)

    return dispatch(src, token_idx, dest)


def _expert_kernel(first_blk_ref, count_ref, total_ref, xs_hbm, wg_ref, wu_ref, wd_ref, ys_hbm,
                   wg_b, wu_b, wd_b, x_ring, y_ring, in_sem, out_sem):
    e = pl.program_id(0)
    total = total_ref[0]
    first = first_blk_ref[e]
    count = count_ref[e]
    nblk = (count + EXPERT_ROWS - 1) // EXPERT_ROWS
    spare_block = ys_hbm.shape[0] // EXPERT_ROWS - EXPERT_RING

    def rows_of(g):
        return pl.ds(pl.multiple_of(g * EXPERT_ROWS, EXPERT_ROWS), EXPERT_ROWS)

    def in_copy(g):
        slot = g % EXPERT_RING
        return pltpu.make_async_copy(xs_hbm.at[rows_of(g)], x_ring.at[slot], in_sem.at[slot])

    def out_copy(g, slot):
        return pltpu.make_async_copy(y_ring.at[slot], ys_hbm.at[rows_of(g)], out_sem.at[slot])

    @pl.when(e == 0)
    def _():
        y_ring[...] = jnp.zeros_like(y_ring)
        for g in range(EXPERT_RING - 1):
            in_copy(g).start()
        for slot in range(EXPERT_RING):
            out_copy(spare_block + slot, slot).start()

    @pl.when(nblk > 0)
    def _():
        wg_b[...] = wg_ref[...].astype(BF16)
        wu_b[...] = wu_ref[...].astype(BF16)
        wd_b[...] = wd_ref[...].astype(BF16)

    def blocks(js):
        gs = [first + j for j in js]
        for g in gs:
            in_copy(g).wait()
        ys = []
        for j, g in zip(js, gs):
            p = x_ring[g % EXPERT_RING]
            live = lax.broadcasted_iota(jnp.int32, p.shape, 0) + j * EXPERT_ROWS < count
            lo = jnp.where(live, _unpack_half(p, 0), 0.0).astype(BF16)
            hi = jnp.where(live, _unpack_half(p, 1), 0.0).astype(BF16)

            def up(w, lo=lo, hi=hi):
                return (jnp.dot(lo, w[:PACKED, :], preferred_element_type=F32)
                        + jnp.dot(hi, w[PACKED:, :], preferred_element_type=F32))

            hid = (jax.nn.silu(up(wg_b)) * up(wu_b)).astype(BF16)
            ys.append(_pack_halves(jnp.dot(hid, wd_b[...], preferred_element_type=F32)))
        for g in gs:
            in_copy(g + EXPERT_RING - 1).start()
        for g in gs:
            out_copy(g, g % EXPERT_RING).wait()
        for g, y in zip(gs, ys):
            y_ring[g % EXPERT_RING] = y
        for g in gs:
            out_copy(g, g % EXPERT_RING).start()

    def pair(i, carry):
        blocks([2 * i, 2 * i + 1])
        return carry

    lax.fori_loop(0, nblk // 2, pair, 0)

    @pl.when(nblk % 2 == 1)
    def _():
        blocks([nblk - 1])

    @pl.when(e == pl.num_programs(0) - 1)
    def _():
        for slot in range(EXPERT_RING):
            out_copy(spare_block + slot, slot).wait()
        for g in range(EXPERT_RING - 1):
            in_copy(total + g).wait()


def _expert_ffn(xs, first_blk, counts, total, w_e_gate, w_e_up, w_e_down):
    rows = xs.shape[0] + EXPERT_ROWS
    n_experts, d = w_e_gate.shape[0], w_e_gate.shape[1]
    w_map = lambda e, fb, ct, tt: (e, 0, 0)
    ring = pltpu.VMEM((EXPERT_RING, EXPERT_ROWS, PACKED), jnp.uint32)
    return pl.pallas_call(
        _expert_kernel,
        grid_spec=pltpu.PrefetchScalarGridSpec(
            num_scalar_prefetch=3,
            grid=(n_experts,),
            in_specs=[pl.BlockSpec(memory_space=pl.ANY),
                      pl.BlockSpec((None, d, D_EXPERT), w_map),
                      pl.BlockSpec((None, d, D_EXPERT), w_map),
                      pl.BlockSpec((None, D_EXPERT, d), w_map)],
            out_specs=pl.BlockSpec(memory_space=pl.ANY),
            scratch_shapes=[pltpu.VMEM((d, D_EXPERT), BF16), pltpu.VMEM((d, D_EXPERT), BF16),
                            pltpu.VMEM((D_EXPERT, d), BF16), ring, ring,
                            pltpu.SemaphoreType.DMA((EXPERT_RING,)), pltpu.SemaphoreType.DMA((EXPERT_RING,))],
        ),
        out_shape=jax.ShapeDtypeStruct((rows, PACKED), jnp.uint32),
        compiler_params=_cparams("arbitrary"),
        name="moe_expert_ffn",
    )(first_blk, counts, total, xs, w_e_gate, w_e_up, w_e_down)


def _combine_kernel(alpha, x_ref, gw_ref, ys_ref, wsg_ref, wsu_ref, wsd_ref, g_ref, b_ref, out_ref):
    x = x_ref[...]
    xb = x.astype(BF16)
    hid = (jax.nn.silu(jnp.dot(xb, wsg_ref[...], preferred_element_type=F32))
           * jnp.dot(xb, wsu_ref[...], preferred_element_type=F32)).astype(BF16)
    shared = jnp.dot(hid, wsd_ref[...], preferred_element_type=F32)
    gw = gw_ref[...]
    lo = jnp.zeros((TOKEN_TILE, PACKED), F32)
    hi = jnp.zeros((TOKEN_TILE, PACKED), F32)
    for k in range(TOP_K):
        p = ys_ref[k]
        w = gw[:, k:k + 1]
        lo = lo + w * _unpack_half(p, 0)
        hi = hi + w * _unpack_half(p, 1)
    routed = jnp.concatenate([lo, hi], axis=1)
    out_ref[...] = _layer_norm(alpha * x + (routed + shared), g_ref[...], b_ref[...])


def _combine(alpha, x1, gw_t, ys_tok, wsg, wsu, wsd, g, b):
    n, d = x1.shape
    full = lambda a: pl.BlockSpec(a.shape, lambda i: (0, 0))
    weights = (wsg, wsu, wsd, g, b)
    return pl.pallas_call(
        functools.partial(_combine_kernel, alpha),
        grid=(n // TOKEN_TILE,),
        in_specs=[pl.BlockSpec((TOKEN_TILE, d), lambda i: (i, 0)),
                  pl.BlockSpec((TOKEN_TILE, TOP_K), lambda i: (i, 0)),
                  pl.BlockSpec((TOP_K, TOKEN_TILE, PACKED), lambda i: (0, i, 0))] + [full(a) for a in weights],
        out_specs=pl.BlockSpec((TOKEN_TILE, d), lambda i: (i, 0)),
        out_shape=jax.ShapeDtypeStruct((n, d), F32),
        compiler_params=_cparams("parallel"),
        name="moe_combine_shared_ln2",
    )(x1, gw_t, ys_tok, *weights)


def _moe_layer(x1, x1_packed, eidx, gw, rank, cnt, w_e_gate, w_e_up, w_e_down, w_s_gate, w_s_up, w_s_down,
               ln2_g, ln2_b, alpha):
    n, d = x1.shape

    counts = cnt[:, 0]
    padded = (counts + EXPERT_ROWS - 1) // EXPERT_ROWS * EXPERT_ROWS
    seg_end = jnp.cumsum(padded)
    seg_start = seg_end - padded
    rows = n * TOP_K + N_EXPERTS * EXPERT_ROWS
    first_blk = (seg_start // EXPERT_ROWS).astype(jnp.int32)
    total_blk = (seg_end[-1:] // EXPERT_ROWS).astype(jnp.int32)
    start_b = jnp.broadcast_to(seg_start.astype(F32)[:, None], (N_EXPERTS, LANES))

    dest = _dest_rows(eidx, rank, start_b)
    dest_flat = dest.reshape(1, TOP_K * n)
    assign = jnp.arange(TOP_K * n, dtype=jnp.int32).reshape(1, TOP_K * n)
    xs = _sc_dispatch_rows(x1_packed, assign[:, :n], dest, rows + (EXPERT_RING - 1) * EXPERT_ROWS)
    ys = _expert_ffn(xs, first_blk, counts, total_blk, w_e_gate, w_e_up, w_e_down)
    ys_tok = _sc_move_rows(ys, dest_flat, assign, TOP_K * n, "moe_gather_sc").reshape(TOP_K, n, PACKED)
    return _combine(alpha, x1, gw.T, ys_tok, w_s_gate.astype(BF16), w_s_up.astype(BF16),
                    w_s_down.astype(BF16), ln2_g.reshape(1, -1), ln2_b.reshape(1, -1))


def kernel(x, mem, positions, w_in, w_mem_kv, w_gate, b_gate, w_br_dil, w_br_moba, w_br_mem, w_out, ln1_g, ln1_b, w_router, router_bias, w_e_gate, w_e_up, w_e_down, w_s_gate, w_s_up, w_s_down, ln2_g, ln2_b):
    batch, seq, d = x.shape
    depth = w_in.shape[0]
    alpha = (2.0 * depth) ** 0.25
    h = x.reshape(batch * seq, d)
    for l in range(depth):
        h, h_packed, eidx, gw, rank, cnt = _token_mixer(
            h, mem, positions, w_in[l], w_mem_kv[l], w_gate[l], b_gate[l], w_br_dil[l], w_br_moba[l], w_br_mem[l],
            w_out[l], ln1_g[l], ln1_b[l], w_router[l], router_bias[l], batch, seq, alpha)
        h = _moe_layer(h, h_packed, eidx, gw, rank, cnt, w_e_gate[l], w_e_up[l], w_e_down[l],
                       w_s_gate[l], w_s_up[l], w_s_down[l], ln2_g[l], ln2_b[l], alpha)
    return h.reshape(batch, seq, d)
```

```python
import functools

import jax
import jax.numpy as jnp
from jax import lax
from jax.experimental import pallas as pl
from jax.experimental.pallas import tpu as pltpu
from jax.experimental.pallas import tpu_sc as plsc

F32 = jnp.float32
BF16 = jnp.bfloat16

LANES = 128
VMEM_LIMIT_BYTES = 48 * 1024 * 1024

HEAD_DIM = 64
ROPE_DIM = HEAD_DIM // 4
ROPE_HALF = ROPE_DIM // 2
ROPE_THETA = 500000.0
DIL_PAIRS = ((128, 1), (512, 4), (2048, 16))
BAND = 128
DIL_TILE = 1024
MEM_CHUNK = 512
W_DIL = 384
W_MOBA = 384
W_MEM = 256
MOBA_BLOCK = 256
MOBA_TOPK = 3
N_EXPERTS = 256
N_GROUPS = 8
GROUP_SIZE = N_EXPERTS // N_GROUPS
TOPK_GROUPS = 4
TOP_K = 8
D_EXPERT = 256
ROUTED_SCALE = 2.5
LN_EPS = 1e-5
NEG = -1e30
QK_SCALE = HEAD_DIM ** -0.5


def _cparams(*sem):
    return pltpu.CompilerParams(dimension_semantics=sem, vmem_limit_bytes=VMEM_LIMIT_BYTES)


def _dot_nt(a, b):
    return lax.dot_general(a, b, (((1,), (1,)), ((), ())), preferred_element_type=F32)


def _layer_norm(v, g, b):
    mu = jnp.mean(v, axis=-1, keepdims=True)
    c = v - mu
    var = jnp.mean(c * c, axis=-1, keepdims=True)
    return c * lax.rsqrt(var + LN_EPS) * g + b


_IN_SECTIONS = (
    (0, W_DIL, True, True),
    (W_DIL, W_DIL, True, False),
    (2 * W_DIL, W_DIL, False, False),
    (3 * W_DIL, W_MOBA, True, True),
    (3 * W_DIL + W_MOBA, W_MOBA, True, False),
    (3 * W_DIL + 2 * W_MOBA, W_MOBA, False, False),
    (3 * W_DIL + 3 * W_MOBA, W_MEM, False, True),
)


N_DIL_SECTIONS = 3


def _inproj_kernel(x_ref, w_ref, cos_ref, sin_ref, *refs):
    n_dil = N_DIL_SECTIONS * len(DIL_PAIRS)
    dil_refs, flat_refs, stage = refs[:n_dil], refs[n_dil:-1], refs[-1]
    xb = x_ref[...].astype(BF16)
    cos = cos_ref[...]
    sin = sin_ref[...]
    tm = xb.shape[0]
    lane = lax.broadcasted_iota(jnp.int32, (tm, LANES), 1)
    first_half = (lane % ROPE_DIM) < ROPE_HALF

    def rope(t):
        partner = jnp.where(first_half, pltpu.roll(t, LANES - ROPE_HALF, 1), pltpu.roll(t, ROPE_HALF, 1))
        return t * cos + partner * sin

    for sec, (off, width, roped, scaled) in enumerate(_IN_SECTIONS):
        acc = jnp.dot(xb, w_ref[:, off:off + width], preferred_element_type=F32)
        for c in range(width // LANES):
            t = acc[:, c * LANES:(c + 1) * LANES]
            if roped:
                t = rope(t)
            if scaled:
                t = t * QK_SCALE
            if sec >= N_DIL_SECTIONS:
                flat_refs[sec - N_DIL_SECTIONS][:, c * LANES:(c + 1) * LANES] = t.astype(BF16)
                continue
            o_ref = dil_refs[sec * len(DIL_PAIRS) + c]
            dilation = DIL_PAIRS[c][1]
            if dilation == 1:
                o_ref[0] = t.astype(BF16)
                continue
            slot = sec * len(DIL_PAIRS) + c
            stage[slot] = t
            for r in range(dilation):
                o_ref[r] = stage[slot, pl.ds(r, tm // dilation, stride=dilation), :].astype(BF16)


def _in_projection(x2, w_in_b, cos_t, sin_t, batch, seq, tm=1024):
    n, d = x2.shape
    w_total = w_in_b.shape[1]
    tiles = seq // tm
    dil_specs, dil_shapes = [], []
    for _ in range(N_DIL_SECTIONS):
        for _, dilation in DIL_PAIRS:
            dil_specs.append(pl.BlockSpec((None, dilation, tm // dilation, LANES),
                                          lambda i: (i // tiles, 0, i % tiles, 0)))
            dil_shapes.append(jax.ShapeDtypeStruct((batch, dilation, seq // dilation, LANES), BF16))
    widths = [s[1] for s in _IN_SECTIONS[N_DIL_SECTIONS:]]
    outs = pl.pallas_call(
        _inproj_kernel,
        grid=(n // tm,),
        in_specs=[
            pl.BlockSpec((tm, d), lambda i: (i, 0)),
            pl.BlockSpec((d, w_total), lambda i: (0, 0)),
            pl.BlockSpec((tm, LANES), lambda i: (i, 0)),
            pl.BlockSpec((tm, LANES), lambda i: (i, 0)),
        ],
        out_specs=dil_specs + [pl.BlockSpec((tm, w), lambda i: (i, 0)) for w in widths],
        out_shape=dil_shapes + [jax.ShapeDtypeStruct((n, w), BF16) for w in widths],
        scratch_shapes=[pltpu.VMEM((N_DIL_SECTIONS * len(DIL_PAIRS), tm, LANES), F32)],
        compiler_params=_cparams("parallel"),
        name="in_proj_rope",
    )(x2, w_in_b, cos_t, sin_t)
    n_dil = len(dil_specs)
    ng = len(DIL_PAIRS)
    qkv_dil = [outs[s * ng:(s + 1) * ng] for s in range(N_DIL_SECTIONS)]
    return qkv_dil, outs[n_dil:]


def _rope_tables(positions):
    lane = jnp.arange(LANES)
    rotary = (lane % HEAD_DIM) < ROPE_DIM
    inv_freq = ROPE_THETA ** (-(lane % ROPE_HALF).astype(F32) / ROPE_HALF)
    ang = positions.reshape(-1).astype(F32)[:, None] * jnp.where(rotary, inv_freq, 0.0)
    sign = jnp.where((lane % ROPE_DIM) < ROPE_HALF, -1.0, 1.0)
    return jnp.cos(ang), jnp.sin(ang) * sign


def _head_mask(shape, h):
    lane = lax.broadcasted_iota(jnp.int32, shape, 1)
    return (lane // HEAD_DIM) == h


def _dil_kernel(q_ref, kp_ref, k_ref, vp_ref, v_ref, o_ref, lse_ref):
    i = pl.program_id(2)
    n_res, tq = q_ref.shape[0], q_ref.shape[1]
    qi = lax.broadcasted_iota(jnp.int32, (BAND, 2 * BAND), 0)
    kj = lax.broadcasted_iota(jnp.int32, (BAND, 2 * BAND), 1)
    dist = qi + BAND - kj
    band = (dist >= 0) & (dist <= BAND)
    first = _head_mask((BAND, LANES), 0)
    for r in range(n_res):
        for j in range(tq // BAND):
            rows = slice(j * BAND, (j + 1) * BAND)
            q = q_ref[r, rows, :]
            if j == 0:
                k_prev, v_prev = kp_ref[r], vp_ref[r]
                allowed = band & ((kj >= BAND) | (i > 0))
            else:
                prev_rows = slice((j - 1) * BAND, j * BAND)
                k_prev, v_prev = k_ref[r, prev_rows, :], v_ref[r, prev_rows, :]
                allowed = band
            k = jnp.concatenate([k_prev, k_ref[r, rows, :]], axis=0)
            v = jnp.concatenate([v_prev, v_ref[r, rows, :]], axis=0)
            outs, lses = [], []
            for h in range(2):
                qh = jnp.where(_head_mask(q.shape, h), q, jnp.zeros_like(q))
                s = jnp.where(allowed, _dot_nt(qh, k), NEG)
                m = jnp.max(s, axis=1, keepdims=True)
                p = jnp.exp(s - m)
                l = jnp.sum(p, axis=1, keepdims=True)
                outs.append(jnp.dot(p.astype(BF16), v, preferred_element_type=F32) / l)
                lses.append(jnp.broadcast_to(m + jnp.log(l), (BAND, LANES)))
            o_ref[r, rows, :] = jnp.where(first, outs[0], outs[1]).astype(o_ref.dtype)
            lse_ref[r, rows, :] = jnp.where(first, lses[0], lses[1])


def _dilated_attention(q4, k4, v4):
    batch, dilation, steps, _ = q4.shape
    tq = min(steps, DIL_TILE)
    n_res = min(dilation, DIL_TILE // tq)
    per_tile = tq // BAND
    cur = pl.BlockSpec((None, n_res, tq, LANES), lambda b, r, i: (b, r, i, 0))
    prev = pl.BlockSpec((None, n_res, BAND, LANES), lambda b, r, i: (b, r, jnp.maximum(i * per_tile - 1, 0), 0))
    return pl.pallas_call(
        _dil_kernel,
        grid=(batch, dilation // n_res, steps // tq),
        in_specs=[cur, prev, cur, prev, cur],
        out_specs=[cur, cur],
        out_shape=[jax.ShapeDtypeStruct(q4.shape, BF16), jax.ShapeDtypeStruct(q4.shape, F32)],
        compiler_params=_cparams("parallel", "parallel", "arbitrary"),
        name=f"dilated_attn_d{dilation}",
    )(q4, k4, k4, v4, v4)


def _kmean_kernel(k_ref, hi_ref, lo_ref):
    k = k_ref[...].astype(F32)
    s, w = k.shape
    mean = jnp.sum(k.reshape(s // MOBA_BLOCK, MOBA_BLOCK, w), axis=1) / MOBA_BLOCK
    hi = mean.astype(BF16)
    hi_ref[...] = hi
    lo_ref[...] = (mean - hi.astype(F32)).astype(BF16)


def _moba_kmean(km3):
    batch, seq, w = km3.shape
    nb = seq // MOBA_BLOCK
    return pl.pallas_call(
        _kmean_kernel,
        grid=(batch,),
        in_specs=[pl.BlockSpec((None, seq, w), lambda b: (b, 0, 0))],
        out_specs=[pl.BlockSpec((None, nb, w), lambda b: (b, 0, 0))] * 2,
        out_shape=[jax.ShapeDtypeStruct((batch, nb, w), BF16)] * 2,
        compiler_params=_cparams("parallel"),
        name="moba_kmean",
    )(km3)


def _moba_kernel(q_ref, k_ref, v_ref, kmh_ref, kml_ref, o_ref, qaug, kaug, vaug, m_scr, acc_scr):
    s = pl.program_id(2)
    nsteps = pl.num_programs(2)
    nb = kmh_ref.shape[0]
    tq = MOBA_BLOCK
    lane = lax.broadcasted_iota(jnp.int32, (tq, LANES), 1)
    blk = lax.broadcasted_iota(jnp.int32, (nb, tq), 0)
    slot = s % 2

    def select(step, dst_slot):
        for t in range(2):
            qi = step if t == 0 else nb - 1 - step
            q = q_ref[pl.ds(pl.multiple_of(qi * tq, tq), tq), :]
            for h in range(2):
                hm = _head_mask(q.shape, h)
                qh = jnp.where(hm, q, jnp.zeros_like(q))
                gate = _dot_nt(kmh_ref[...], qh) + _dot_nt(kml_ref[...], qh)
                cnt = jnp.zeros((nb, tq), jnp.int32)
                for jp in range(nb):
                    g_jp = gate[jp:jp + 1, :]
                    beats = (g_jp > gate) | ((g_jp == gate) & (blk > jp))
                    cnt = cnt + jnp.where(beats & (qi > jp), 1, 0)
                sel = ((blk < qi) & (cnt < MOBA_TOPK)) | (blk == qi)
                bias_t = jnp.where(sel, 0.0, NEG).astype(F32)
                spare = HEAD_DIM * (1 - h)
                pieces = [jnp.zeros((spare, tq), F32)] if spare else []
                pieces += [bias_t, jnp.zeros((LANES - spare - nb, tq), F32)]
                qaug[dst_slot, t, h] = jnp.where(hm, q, jnp.concatenate(pieces, axis=0).T.astype(BF16))

    @pl.when(s == 0)
    def _():
        for h in range(2):
            spare = HEAD_DIM * (1 - h)
            for j in range(nb):
                onehot = jnp.where(lane == spare + j, 1.0, 0.0).astype(BF16)
                kaug[h, j * tq:(j + 1) * tq, :] = jnp.where(
                    _head_mask((tq, LANES), h), k_ref[j * tq:(j + 1) * tq, :], onehot)
                vaug[h, j * tq:(j + 1) * tq, :] = jnp.where(
                    _head_mask((tq, LANES), h), v_ref[j * tq:(j + 1) * tq, :], jnp.ones((tq, LANES), BF16))
        select(0, 0)

    for t in range(2):
        for h in range(2):
            m_scr[t, h] = jnp.full((tq, LANES), NEG, F32)
            acc_scr[t, h] = jnp.zeros((tq, LANES), F32)

    row = lax.broadcasted_iota(jnp.int32, (tq, tq), 0)
    col = lax.broadcasted_iota(jnp.int32, (tq, tq), 1)
    causal_bias = jnp.where(col <= row, 0.0, NEG).astype(F32)
    starts = [pl.multiple_of(s * tq, tq), pl.multiple_of((nb - 1 - s) * tq, tq)]

    for it in range(nb + 1):
        if it < 2:
            t, kstart = it, starts[it]
        else:
            first = it - 2 < s
            t = jnp.where(first, 0, 1)
            kstart = pl.multiple_of(jnp.where(first, it - 2, it - 2 - s) * tq, tq)
        for h in range(2):
            sc = _dot_nt(qaug[slot, t, h], kaug[h, pl.ds(kstart, tq), :])
            if it < 2:
                sc = sc + causal_bias
            m_old = m_scr[t, h]
            m_new = jnp.maximum(m_old, jnp.max(sc, axis=1, keepdims=True))
            p = jnp.exp(sc - jnp.concatenate([m_new, m_new], axis=1))
            acc_scr[t, h] = (jnp.exp(m_old - m_new) * acc_scr[t, h]
                             + jnp.dot(p.astype(BF16), vaug[h, pl.ds(kstart, tq), :], preferred_element_type=F32))
            m_scr[t, h] = m_new

    first_head = _head_mask((tq, LANES), 0)
    for t in range(2):
        acc = jnp.where(first_head, acc_scr[t, 0], acc_scr[t, 1])
        den = jnp.where(first_head, pltpu.roll(acc_scr[t, 0], HEAD_DIM, 1), pltpu.roll(acc_scr[t, 1], HEAD_DIM, 1))
        o_ref[pl.ds(starts[t], tq), :] = (acc / den).astype(o_ref.dtype)

    select(jnp.minimum(s + 1, nsteps - 1), 1 - slot)


def _moba_attention(qm3, km3, vm3, kmh, kml):
    batch, seq, w = qm3.shape
    nb = seq // MOBA_BLOCK
    assert nb % 2 == 0 and nb <= HEAD_DIM
    npair = w // LANES
    tq = MOBA_BLOCK
    seq_spec = pl.BlockSpec((None, seq, LANES), lambda b, p, i: (b, 0, p))
    km_spec = pl.BlockSpec((None, nb, LANES), lambda b, p, i: (b, 0, p))
    state = pltpu.VMEM((2, 2, tq, LANES), F32)
    return pl.pallas_call(
        _moba_kernel,
        grid=(batch, npair, nb // 2),
        in_specs=[seq_spec, seq_spec, seq_spec, km_spec, km_spec],
        out_specs=seq_spec,
        out_shape=jax.ShapeDtypeStruct((batch, seq, w), BF16),
        scratch_shapes=[pltpu.VMEM((2, 2, 2, tq, LANES), BF16), pltpu.VMEM((2, seq, LANES), BF16),
                        pltpu.VMEM((2, seq, LANES), BF16), state, state],
        compiler_params=_cparams("parallel", "parallel", "arbitrary"),
        name="moba_attn",
    )(qm3, km3, vm3, kmh, kml)


def _memkv_kernel(mem_ref, w_ref, k_ref, v_ref):
    kv = jnp.dot(mem_ref[...].astype(BF16), w_ref[...], preferred_element_type=F32)
    k_ref[...] = kv[:, :W_MEM].astype(BF16)
    v_ref[...] = kv[:, W_MEM:].astype(BF16)


def _mem_kv(mem, w_kv_b):
    batch, m, d = mem.shape
    return pl.pallas_call(
        _memkv_kernel,
        grid=(batch,),
        in_specs=[pl.BlockSpec((None, m, d), lambda b: (b, 0, 0)),
                  pl.BlockSpec((d, 2 * W_MEM), lambda b: (0, 0))],
        out_specs=[pl.BlockSpec((None, m, W_MEM), lambda b: (b, 0, 0))] * 2,
        out_shape=[jax.ShapeDtypeStruct((batch, m, W_MEM), BF16)] * 2,
        compiler_params=_cparams("parallel"),
        name="mem_kv_proj",
    )(mem, w_kv_b)


def _memattn_kernel(q_ref, k_ref, v_ref, o_ref):
    tq, w = q_ref.shape
    for pair in range(w // LANES):
        lanes = slice(pair * LANES, (pair + 1) * LANES)
        k = k_ref[:, lanes]
        v = v_ref[:, lanes]
        for c in range(tq // MEM_CHUNK):
            rows = slice(c * MEM_CHUNK, (c + 1) * MEM_CHUNK)
            q = q_ref[rows, lanes]
            outs = []
            for h in range(2):
                qh = jnp.where(_head_mask(q.shape, h), q, jnp.zeros_like(q))
                s = _dot_nt(qh, k)
                m = jnp.max(s, axis=1, keepdims=True)
                p = jnp.exp(s - m)
                l = jnp.sum(p, axis=1, keepdims=True)
                outs.append(jnp.dot(p.astype(BF16), v, preferred_element_type=F32) / l)
            o_ref[rows, lanes] = jnp.where(_head_mask(outs[0].shape, 0), outs[0], outs[1]).astype(o_ref.dtype)


def _mem_attention(qx3, k_mem, v_mem, tq=2048):
    batch, seq, w = qx3.shape
    m = k_mem.shape[1]
    tq = min(tq, seq)
    return pl.pallas_call(
        _memattn_kernel,
        grid=(batch, seq // tq),
        in_specs=[pl.BlockSpec((None, tq, w), lambda b, i: (b, i, 0)),
                  pl.BlockSpec((None, m, w), lambda b, i: (b, 0, 0)),
                  pl.BlockSpec((None, m, w), lambda b, i: (b, 0, 0))],
        out_specs=pl.BlockSpec((None, tq, w), lambda b, i: (b, i, 0)),
        out_shape=jax.ShapeDtypeStruct((batch, seq, w), BF16),
        compiler_params=_cparams("parallel", "parallel"),
        name="mem_attn",
    )(qx3, k_mem, v_mem)


def _merge_kernel(alpha, x_ref, o1_ref, o2_ref, o3_ref, l1_ref, l2_ref, l3_ref, ym_ref, yx_ref,
                  wg_ref, bg_ref, wbd_ref, wbm_ref, wbx_ref, wo_ref, g_ref, b_ref, wrh_ref, wrl_ref, rb_ref,
                  out_ref, packed_ref, eidx_ref, gw_ref, rank_ref, cnt_ref, stage, carry, x1_prev):
    step = pl.program_id(0)
    route_refs = (wrh_ref, wrl_ref, rb_ref, eidx_ref, gw_ref, rank_ref, cnt_ref, carry)

    @pl.when(step == 0)
    def _():
        carry[...] = jnp.zeros_like(carry)
        x1_prev[...] = jnp.zeros_like(x1_prev)

    route = _route_tile(x1_prev[...], step > 0, *route_refs)
    next(route)
    x = x_ref[...]
    xb = x.astype(BF16)
    tm, d = x.shape

    def token_major(ref, slot):
        dilation = ref.shape[0]
        if dilation == 1:
            return ref[0].astype(F32)
        for r in range(dilation):
            stage[slot, pl.ds(r, tm // dilation, stride=dilation), :] = ref[r].astype(F32)
        return stage[slot]

    o1, o2, o3 = (token_major(r, s) for s, r in enumerate((o1_ref, o2_ref, o3_ref)))
    l1, l2, l3 = (token_major(r, s + 3) for s, r in enumerate((l1_ref, l2_ref, l3_ref)))
    mx = jnp.maximum(jnp.maximum(l1, l2), l3)
    e1, e2, e3 = jnp.exp(l1 - mx), jnp.exp(l2 - mx), jnp.exp(l3 - mx)
    y_dil = (e1 * o1 + e2 * o2 + e3 * o3) / (e1 + e2 + e3)
    next(route)
    branches = (
        jnp.dot(y_dil.astype(BF16), wbd_ref[...], preferred_element_type=F32),
        jnp.dot(ym_ref[...], wbm_ref[...], preferred_element_type=F32),
        jnp.dot(yx_ref[...], wbx_ref[...], preferred_element_type=F32),
    )
    merged = jnp.zeros_like(x)
    for i, br in enumerate(branches):
        next(route)
        logits = jnp.dot(xb, wg_ref[:, i * d:(i + 1) * d], preferred_element_type=F32) + bg_ref[:, i * d:(i + 1) * d]
        merged = merged + jax.nn.sigmoid(logits) * br
    next(route)
    mix = jnp.dot(merged.astype(BF16), wo_ref[...], preferred_element_type=F32)
    for _ in route:
        pass
    x1 = _layer_norm(alpha * x + mix, g_ref[...], b_ref[...])
    out_ref[...] = x1
    packed_ref[...] = _pack_halves(x1)
    x1_prev[...] = x1


def _merge(alpha, x2, o_dil, lse_dil, y_moba, y_mem, wg, bg, wbd, wbm, wbx, wo, g, b, wr_hi_t, wr_lo_t, rbias_b,
           seq, tm=512):
    n, d = x2.shape
    tiles = seq // tm
    last = n // tm - 1
    cur = lambda i: jnp.minimum(i, last)
    row = lambda w: pl.BlockSpec((tm, w), lambda i: (cur(i), 0))
    full = lambda a: pl.BlockSpec(a.shape, lambda i: (0, 0))
    residue_major = lambda a: pl.BlockSpec((None, a.shape[1], tm // a.shape[1], LANES),
                                           lambda i: (cur(i) // tiles, 0, cur(i) % tiles, 0))
    slots = pl.BlockSpec((TOP_K, tm), lambda i: (0, jnp.maximum(i - 1, 0)))
    weights = (wg, bg, wbd, wbm, wbx, wo, g, b, wr_hi_t, wr_lo_t, rbias_b)
    tok = lambda dt: jax.ShapeDtypeStruct((TOP_K, n), dt)
    return pl.pallas_call(
        functools.partial(_merge_kernel, alpha),
        grid=(n // tm + 1,),
        in_specs=([row(d)] + [residue_major(a) for a in (*o_dil, *lse_dil)] + [row(W_MOBA), row(W_MEM)]
                  + [full(a) for a in weights]),
        out_specs=[row(d), row(PACKED), slots, slots, slots, pl.BlockSpec((N_EXPERTS, LANES), lambda i: (0, 0))],
        out_shape=[jax.ShapeDtypeStruct((n, d), F32), jax.ShapeDtypeStruct((n, PACKED), jnp.uint32),
                   tok(jnp.int32), tok(F32), tok(jnp.int32), jax.ShapeDtypeStruct((N_EXPERTS, LANES), jnp.int32)],
        scratch_shapes=[pltpu.VMEM((2 * len(DIL_PAIRS), tm, LANES), F32), pltpu.VMEM((N_EXPERTS, LANES), F32),
                        pltpu.VMEM((tm, d), F32)],
        compiler_params=_cparams("arbitrary"),
        name="merge_outproj_ln1_route",
    )(x2, *o_dil, *lse_dil, y_moba, y_mem, *weights)


def _token_mixer(x2, mem, positions, w_in, w_mem_kv, w_gate, b_gate, w_br_dil, w_br_moba, w_br_mem,
                 w_out, ln1_g, ln1_b, w_router, router_bias, batch, seq, alpha):
    wr_t = w_router.T
    wr_hi = wr_t.astype(BF16)
    wr_lo = (wr_t - wr_hi.astype(F32)).astype(BF16)
    rbias_b = jnp.broadcast_to(router_bias.astype(F32)[:, None], (N_EXPERTS, LANES))
    cos_t, sin_t = _rope_tables(positions)
    (q_dil, k_dil, v_dil), (qm, km, vm, qx) = _in_projection(x2, w_in.astype(BF16), cos_t, sin_t, batch, seq)
    o_dil, lse_dil = [], []
    for g in range(len(DIL_PAIRS)):
        o, lse = _dilated_attention(q_dil[g], k_dil[g], v_dil[g])
        o_dil.append(o)
        lse_dil.append(lse)
    qm3, km3, vm3 = (t.reshape(batch, seq, W_MOBA) for t in (qm, km, vm))
    kmh, kml = _moba_kmean(km3)
    y_moba = _moba_attention(qm3, km3, vm3, kmh, kml).reshape(batch * seq, W_MOBA)
    k_mem, v_mem = _mem_kv(mem, w_mem_kv.astype(BF16))
    y_mem = _mem_attention(qx.reshape(batch, seq, W_MEM), k_mem, v_mem).reshape(batch * seq, W_MEM)
    return _merge(alpha, x2, o_dil, lse_dil, y_moba, y_mem,
                  w_gate.astype(BF16), b_gate.reshape(1, -1), w_br_dil.astype(BF16),
                  w_br_moba.astype(BF16), w_br_mem.astype(BF16), w_out.astype(BF16),
                  ln1_g.reshape(1, -1), ln1_b.reshape(1, -1), wr_hi, wr_lo, rbias_b, seq)


EXPERT_ROWS = 256
EXPERT_RING = 6
TOKEN_TILE = 512
PACKED = 512
SC_WINDOW = 128


def _first_index_of_max(v, iota_f, size):
    m = jnp.max(v, axis=0, keepdims=True)
    idx = jnp.min(jnp.where(v == m, iota_f, float(size)), axis=0, keepdims=True)
    return m, idx


def _route_tile(x, counted, wh_ref, wl_ref, bias_ref, eidx_ref, gw_ref, rank_ref, cnt_ref, carry):
    tm = x.shape[0]
    xh = x.astype(BF16)
    xl = (x - xh.astype(F32)).astype(BF16)
    wh = wh_ref[...]
    logits = _dot_nt(wh, xh) + _dot_nt(wh, xl) + _dot_nt(wl_ref[...], xh)
    yield
    scores = jax.nn.sigmoid(logits)
    biased = scores + bias_ref[...][:, :1]

    giota = lax.broadcasted_iota(jnp.int32, (GROUP_SIZE, tm), 0).astype(F32)
    group_scores = []
    for g in range(N_GROUPS):
        slab = biased[g * GROUP_SIZE:(g + 1) * GROUP_SIZE, :]
        m1, i1 = _first_index_of_max(slab, giota, GROUP_SIZE)
        m2 = jnp.max(jnp.where(giota == i1, -jnp.inf, slab), axis=0, keepdims=True)
        group_scores.append(m1 + m2)
    gs = jnp.concatenate(group_scores, axis=0)
    gidx = lax.broadcasted_iota(jnp.int32, (N_GROUPS, tm), 0)
    beaten = jnp.zeros((N_GROUPS, tm), jnp.int32)
    for gp in range(N_GROUPS):
        row = gs[gp:gp + 1, :]
        beaten = beaten + jnp.where((row > gs) | ((row == gs) & (gidx > gp)), 1, 0)
    keep = beaten < TOPK_GROUPS

    ahead = jnp.zeros((1, tm), jnp.int32)
    slab_b = [jnp.full((GROUP_SIZE, tm), -jnp.inf, F32) for _ in range(TOPK_GROUPS)]
    slab_s = [jnp.zeros((GROUP_SIZE, tm), F32) for _ in range(TOPK_GROUPS)]
    slab_g = [jnp.zeros((1, tm), F32) for _ in range(TOPK_GROUPS)]
    for g in range(N_GROUPS):
        kept = keep[g:g + 1, :]
        rows = slice(g * GROUP_SIZE, (g + 1) * GROUP_SIZE)
        for j in range(min(g, TOPK_GROUPS - 1) + 1):
            here = kept & (ahead == j)
            slab_b[j] = jnp.where(here, biased[rows, :], slab_b[j])
            slab_s[j] = jnp.where(here, scores[rows, :], slab_s[j])
            slab_g[j] = jnp.where(here, float(g), slab_g[j])
        ahead = ahead + jnp.where(kept, 1, 0)
    masked = jnp.concatenate(slab_b, axis=0)
    kept_scores = jnp.concatenate(slab_s, axis=0)
    yield

    n_kept = TOPK_GROUPS * GROUP_SIZE
    ciota = lax.broadcasted_iota(jnp.int32, (n_kept, tm), 0).astype(F32)
    eiota = lax.broadcasted_iota(jnp.int32, (N_EXPERTS, tm), 0).astype(F32)
    idx_rows, gw_rows = [], []
    for k in range(TOP_K):
        _, cidx = _first_index_of_max(masked, ciota, n_kept)
        hit = ciota == cidx
        gw_rows.append(jnp.sum(jnp.where(hit, kept_scores, 0.0), axis=0, keepdims=True))
        masked = jnp.where(hit, -jnp.inf, masked)
        slab = sum(jnp.where(cidx >= float(j * GROUP_SIZE), 1.0, 0.0) for j in range(1, TOPK_GROUPS))
        group = sum(jnp.where(slab == float(j), slab_g[j], 0.0) for j in range(TOPK_GROUPS))
        idx_rows.append((group - slab) * GROUP_SIZE + cidx)
        if k % 2 == 1:
            yield
    idx8 = jnp.concatenate(idx_rows, axis=0)
    gw8 = jnp.concatenate(gw_rows, axis=0)
    gw_ref[...] = gw8 / jnp.sum(gw8, axis=0, keepdims=True) * ROUTED_SCALE
    eidx_ref[...] = idx8.astype(jnp.int32)

    onehot = jnp.zeros((N_EXPERTS, tm), F32)
    for k in range(TOP_K):
        onehot = onehot + jnp.where(eiota == idx8[k:k + 1, :], 1.0, 0.0)
    t_row = lax.broadcasted_iota(jnp.int32, (tm, tm), 0)
    t_col = lax.broadcasted_iota(jnp.int32, (tm, tm), 1)
    earlier = jnp.where(t_row < t_col, 1.0, 0.0).astype(BF16)
    prefix = jnp.dot(onehot.astype(BF16), earlier, preferred_element_type=F32)
    base = carry[...]
    prefix = prefix + jnp.concatenate([base] * (tm // LANES), axis=1)
    rank_rows = [jnp.sum(jnp.where(eiota == idx8[k:k + 1, :], prefix, 0.0), axis=0, keepdims=True)
                 for k in range(TOP_K)]
    rank_ref[...] = jnp.concatenate(rank_rows, axis=0).astype(jnp.int32)
    total = base + jnp.where(counted, jnp.sum(onehot, axis=1, keepdims=True), 0.0)
    carry[...] = total
    cnt_ref[...] = total.astype(jnp.int32)


def _dest_kernel(eidx_ref, rank_ref, start_ref, dest_ref):
    eidx = eidx_ref[...]
    tm = eidx.shape[1]
    eiota = lax.broadcasted_iota(jnp.int32, (N_EXPERTS, tm), 0)
    start = start_ref[...][:, :1]
    rows = [jnp.sum(jnp.where(eiota == eidx[k:k + 1, :], start, 0.0), axis=0, keepdims=True)
            for k in range(TOP_K)]
    dest_ref[...] = jnp.concatenate(rows, axis=0).astype(jnp.int32) + rank_ref[...]


def _dest_rows(eidx, rank, start_b):
    n = eidx.shape[1]
    nt = n // TOKEN_TILE
    return pl.pallas_call(
        _dest_kernel,
        grid=(nt,),
        in_specs=[pl.BlockSpec((TOP_K, TOKEN_TILE), lambda i: (0, i)),
                  pl.BlockSpec((TOP_K, TOKEN_TILE), lambda i: (0, i)),
                  pl.BlockSpec((N_EXPERTS, LANES), lambda i: (0, 0))],
        out_specs=pl.BlockSpec((TOP_K, TOKEN_TILE), lambda i: (0, i)),
        out_shape=jax.ShapeDtypeStruct((TOP_K, n), jnp.int32),
        compiler_params=_cparams("parallel"),
        name="moe_dest_rows",
    )(eidx, rank, start_b)


def _pack_halves(v):
    return pltpu.pack_elementwise([v[:, :PACKED], v[:, PACKED:]], packed_dtype=BF16)


def _unpack_half(p, index):
    return pltpu.unpack_elementwise(p, index=index, packed_dtype=BF16, unpacked_dtype=F32)


def _sc_mesh():
    return plsc.VectorSubcoreMesh(core_axis_name="core", subcore_axis_name="subcore")


def _sc_move_rows(src, src_idx, dst_idx, out_rows, name):
    width = src.shape[1]
    count = src_idx.shape[1]

    @functools.partial(pl.kernel, out_type=jax.ShapeDtypeStruct((out_rows, width), src.dtype),
                       mesh=_sc_mesh(), scratch_types=[pltpu.VMEM((SC_WINDOW, width), src.dtype)], name=name)
    def move(src_hbm, sidx_hbm, didx_hbm, out_hbm, rows_vmem):
        def body(sidx_vmem, didx_vmem):
            pltpu.sync_copy(src_hbm.at[sidx_vmem.at[0]], rows_vmem)
            pltpu.sync_copy(rows_vmem, out_hbm.at[didx_vmem.at[0]])

        idx_spec = pl.BlockSpec((1, SC_WINDOW), index_map=lambda i: (0, i))
        pltpu.emit_pipeline(
            body,
            grid=(count // SC_WINDOW,),
            in_specs=[idx_spec, idx_spec],
            out_specs=[],
            core_axis_name=("core", "subcore"),
            dimension_semantics=(pltpu.PARALLEL,),
        )(sidx_hbm, didx_hbm)

    return move(src, src_idx, dst_idx)


def _sc_dispatch_rows(src, token_idx, dest, out_rows):
    n, width = src.shape
    slots = dest.shape[0]

    @functools.partial(pl.kernel, out_type=jax.ShapeDtypeStruct((out_rows, width), src.dtype),
                       mesh=_sc_mesh(), scratch_types=[pltpu.VMEM((SC_WINDOW, width), src.dtype)],
                       name="moe_dispatch_sc")
    def dispatch(src_hbm, tidx_hbm, dest_hbm, out_hbm, rows_vmem):
        def body(tidx_vmem, dest_vmem):
            pltpu.sync_copy(src_hbm.at[tidx_vmem.at[0]], rows_vmem)
            for k in range(slots):
                pltpu.sync_copy(rows_vmem, out_hbm.at[dest_vmem.at[k]])

        pltpu.emit_pipeline(
            body,
            grid=(n // SC_WINDOW,),
            in_specs=[pl.BlockSpec((1, SC_WINDOW), index_map=lambda i: (0, i)),
                      pl.BlockSpec((slots, SC_WINDOW), index_map=lambda i: (0, i))],
            out_specs=[],
            core_axis_name=("core", "subcore"),
            dimension_semantics=(pltpu.PARALLEL,),
        )(tidx_hbm, dest_hbm)

    return dispatch(src, token_idx, dest)


def _expert_kernel(first_blk_ref, count_ref, total_ref, xs_hbm, wg_ref, wu_ref, wd_ref, ys_hbm,
                   wg_b, wu_b, wd_b, x_ring, y_ring, in_sem, out_sem):
    e = pl.program_id(0)
    total = total_ref[0]
    first = first_blk_ref[e]
    count = count_ref[e]
    nblk = (count + EXPERT_ROWS - 1) // EXPERT_ROWS
    spare_block = ys_hbm.shape[0] // EXPERT_ROWS - EXPERT_RING

    def rows_of(g):
        return pl.ds(pl.multiple_of(g * EXPERT_ROWS, EXPERT_ROWS), EXPERT_ROWS)

    def in_copy(g):
        slot = g % EXPERT_RING
        return pltpu.make_async_copy(xs_hbm.at[rows_of(g)], x_ring.at[slot], in_sem.at[slot])

    def out_copy(g, slot):
        return pltpu.make_async_copy(y_ring.at[slot], ys_hbm.at[rows_of(g)], out_sem.at[slot])

    @pl.when(e == 0)
    def _():
        y_ring[...] = jnp.zeros_like(y_ring)
        for g in range(EXPERT_RING - 1):
            in_copy(g).start()
        for slot in range(EXPERT_RING):
            out_copy(spare_block + slot, slot).start()

    @pl.when(nblk > 0)
    def _():
        wg_b[...] = wg_ref[...].astype(BF16)
        wu_b[...] = wu_ref[...].astype(BF16)
        wd_b[...] = wd_ref[...].astype(BF16)

    def blocks(js):
        gs = [first + j for j in js]
        for g in gs:
            in_copy(g).wait()
        ys = []
        for j, g in zip(js, gs):
            p = x_ring[g % EXPERT_RING]
            live = lax.broadcasted_iota(jnp.int32, p.shape, 0) + j * EXPERT_ROWS < count
            lo = jnp.where(live, _unpack_half(p, 0), 0.0).astype(BF16)
            hi = jnp.where(live, _unpack_half(p, 1), 0.0).astype(BF16)

            def up(w, lo=lo, hi=hi):
                return (jnp.dot(lo, w[:PACKED, :], preferred_element_type=F32)
                        + jnp.dot(hi, w[PACKED:, :], preferred_element_type=F32))

            hid = (jax.nn.silu(up(wg_b)) * up(wu_b)).astype(BF16)
            ys.append(_pack_halves(jnp.dot(hid, wd_b[...], preferred_element_type=F32)))
        for g in gs:
            in_copy(g + EXPERT_RING - 1).start()
        for g in gs:
            out_copy(g, g % EXPERT_RING).wait()
        for g, y in zip(gs, ys):
            y_ring[g % EXPERT_RING] = y
        for g in gs:
            out_copy(g, g % EXPERT_RING).start()

    def pair(i, carry):
        blocks([2 * i, 2 * i + 1])
        return carry

    lax.fori_loop(0, nblk // 2, pair, 0)

    @pl.when(nblk % 2 == 1)
    def _():
        blocks([nblk - 1])

    @pl.when(e == pl.num_programs(0) - 1)
    def _():
        for slot in range(EXPERT_RING):
            out_copy(spare_block + slot, slot).wait()
        for g in range(EXPERT_RING - 1):
            in_copy(total + g).wait()


def _expert_ffn(xs, first_blk, counts, total, w_e_gate, w_e_up, w_e_down):
    rows = xs.shape[0] + EXPERT_ROWS
    n_experts, d = w_e_gate.shape[0], w_e_gate.shape[1]
    w_map = lambda e, fb, ct, tt: (e, 0, 0)
    ring = pltpu.VMEM((EXPERT_RING, EXPERT_ROWS, PACKED), jnp.uint32)
    return pl.pallas_call(
        _expert_kernel,
        grid_spec=pltpu.PrefetchScalarGridSpec(
            num_scalar_prefetch=3,
            grid=(n_experts,),
            in_specs=[pl.BlockSpec(memory_space=pl.ANY),
                      pl.BlockSpec((None, d, D_EXPERT), w_map),
                      pl.BlockSpec((None, d, D_EXPERT), w_map),
                      pl.BlockSpec((None, D_EXPERT, d), w_map)],
            out_specs=pl.BlockSpec(memory_space=pl.ANY),
            scratch_shapes=[pltpu.VMEM((d, D_EXPERT), BF16), pltpu.VMEM((d, D_EXPERT), BF16),
                            pltpu.VMEM((D_EXPERT, d), BF16), ring, ring,
                            pltpu.SemaphoreType.DMA((EXPERT_RING,)), pltpu.SemaphoreType.DMA((EXPERT_RING,))],
        ),
        out_shape=jax.ShapeDtypeStruct((rows, PACKED), jnp.uint32),
        compiler_params=_cparams("arbitrary"),
        name="moe_expert_ffn",
    )(first_blk, counts, total, xs, w_e_gate, w_e_up, w_e_down)


def _combine_kernel(alpha, x_ref, gw_ref, ys_ref, wsg_ref, wsu_ref, wsd_ref, g_ref, b_ref, out_ref):
    x = x_ref[...]
    xb = x.astype(BF16)
    hid = (jax.nn.silu(jnp.dot(xb, wsg_ref[...], preferred_element_type=F32))
           * jnp.dot(xb, wsu_ref[...], preferred_element_type=F32)).astype(BF16)
    shared = jnp.dot(hid, wsd_ref[...], preferred_element_type=F32)
    gw = gw_ref[...]
    lo = jnp.zeros((TOKEN_TILE, PACKED), F32)
    hi = jnp.zeros((TOKEN_TILE, PACKED), F32)
    for k in range(TOP_K):
        p = ys_ref[k]
        w = gw[:, k:k + 1]
        lo = lo + w * _unpack_half(p, 0)
        hi = hi + w * _unpack_half(p, 1)
    routed = jnp.concatenate([lo, hi], axis=1)
    out_ref[...] = _layer_norm(alpha * x + (routed + shared), g_ref[...], b_ref[...])


def _combine(alpha, x1, gw_t, ys_tok, wsg, wsu, wsd, g, b):
    n, d = x1.shape
    full = lambda a: pl.BlockSpec(a.shape, lambda i: (0, 0))
    weights = (wsg, wsu, wsd, g, b)
    return pl.pallas_call(
        functools.partial(_combine_kernel, alpha),
        grid=(n // TOKEN_TILE,),
        in_specs=[pl.BlockSpec((TOKEN_TILE, d), lambda i: (i, 0)),
                  pl.BlockSpec((TOKEN_TILE, TOP_K), lambda i: (i, 0)),
                  pl.BlockSpec((TOP_K, TOKEN_TILE, PACKED), lambda i: (0, i, 0))] + [full(a) for a in weights],
        out_specs=pl.BlockSpec((TOKEN_TILE, d), lambda i: (i, 0)),
        out_shape=jax.ShapeDtypeStruct((n, d), F32),
        compiler_params=_cparams("parallel"),
        name="moe_combine_shared_ln2",
    )(x1, gw_t, ys_tok, *weights)


def _moe_layer(x1, x1_packed, eidx, gw, rank, cnt, w_e_gate, w_e_up, w_e_down, w_s_gate, w_s_up, w_s_down,
               ln2_g, ln2_b, alpha):
    n, d = x1.shape

    counts = cnt[:, 0]
    padded = (counts + EXPERT_ROWS - 1) // EXPERT_ROWS * EXPERT_ROWS
    seg_end = jnp.cumsum(padded)
    seg_start = seg_end - padded
    rows = n * TOP_K + N_EXPERTS * EXPERT_ROWS
    first_blk = (seg_start // EXPERT_ROWS).astype(jnp.int32)
    total_blk = (seg_end[-1:] // EXPERT_ROWS).astype(jnp.int32)
    start_b = jnp.broadcast_to(seg_start.astype(F32)[:, None], (N_EXPERTS, LANES))

    dest = _dest_rows(eidx, rank, start_b)
    dest_flat = dest.reshape(1, TOP_K * n)
    assign = jnp.arange(TOP_K * n, dtype=jnp.int32).reshape(1, TOP_K * n)
    xs = _sc_dispatch_rows(x1_packed, assign[:, :n], dest, rows + (EXPERT_RING - 1) * EXPERT_ROWS)
    ys = _expert_ffn(xs, first_blk, counts, total_blk, w_e_gate, w_e_up, w_e_down)
    ys_tok = _sc_move_rows(ys, dest_flat, assign, TOP_K * n, "moe_gather_sc").reshape(TOP_K, n, PACKED)
    return _combine(alpha, x1, gw.T, ys_tok, w_s_gate.astype(BF16), w_s_up.astype(BF16),
                    w_s_down.astype(BF16), ln2_g.reshape(1, -1), ln2_b.reshape(1, -1))


def kernel(x, mem, positions, w_in, w_mem_kv, w_gate, b_gate, w_br_dil, w_br_moba, w_br_mem, w_out, ln1_g, ln1_b, w_router, router_bias, w_e_gate, w_e_up, w_e_down, w_s_gate, w_s_up, w_s_down, ln2_g, ln2_b):
    batch, seq, d = x.shape
    depth = w_in.shape[0]
    alpha = (2.0 * depth) ** 0.25
    h = x.reshape(batch * seq, d)
    for l in range(depth):
        h, h_packed, eidx, gw, rank, cnt = _token_mixer(
            h, mem, positions, w_in[l], w_mem_kv[l], w_gate[l], b_gate[l], w_br_dil[l], w_br_moba[l], w_br_mem[l],
            w_out[l], ln1_g[l], ln1_b[l], w_router[l], router_bias[l], batch, seq, alpha)
        h = _moe_layer(h, h_packed, eidx, gw, rank, cnt, w_e_gate[l], w_e_up[l], w_e_down[l],
                       w_s_gate[l], w_s_up[l], w_s_down[l], ln2_g[l], ln2_b[l], alpha)
    return h.reshape(batch, seq, d)
```

```python
import functools

import jax
import jax.numpy as jnp
from jax import lax
from jax.experimental import pallas as pl
from jax.experimental.pallas import tpu as pltpu
from jax.experimental.pallas import tpu_sc as plsc

F32 = jnp.float32
BF16 = jnp.bfloat16

LANES = 128
VMEM_LIMIT_BYTES = 48 * 1024 * 1024

HEAD_DIM = 64
ROPE_DIM = HEAD_DIM // 4
ROPE_HALF = ROPE_DIM // 2
ROPE_THETA = 500000.0
DIL_PAIRS = ((128, 1), (512, 4), (2048, 16))
BAND = 128
DIL_TILE = 1024
MEM_CHUNK = 512
W_DIL = 384
W_MOBA = 384
W_MEM = 256
MOBA_BLOCK = 256
MOBA_TOPK = 3
N_EXPERTS = 256
N_GROUPS = 8
GROUP_SIZE = N_EXPERTS // N_GROUPS
TOPK_GROUPS = 4
TOP_K = 8
D_EXPERT = 256
ROUTED_SCALE = 2.5
LN_EPS = 1e-5
NEG = -1e30
QK_SCALE = HEAD_DIM ** -0.5


def _cparams(*sem):
    return pltpu.CompilerParams(dimension_semantics=sem, vmem_limit_bytes=VMEM_LIMIT_BYTES)


def _dot_nt(a, b):
    return lax.dot_general(a, b, (((1,), (1,)), ((), ())), preferred_element_type=F32)


def _layer_norm(v, g, b):
    mu = jnp.mean(v, axis=-1, keepdims=True)
    c = v - mu
    var = jnp.mean(c * c, axis=-1, keepdims=True)
    return c * lax.rsqrt(var + LN_EPS) * g + b


_IN_SECTIONS = (
    (0, W_DIL, True, True),
    (W_DIL, W_DIL, True, False),
    (2 * W_DIL, W_DIL, False, False),
    (3 * W_DIL, W_MOBA, True, True),
    (3 * W_DIL + W_MOBA, W_MOBA, True, False),
    (3 * W_DIL + 2 * W_MOBA, W_MOBA, False, False),
    (3 * W_DIL + 3 * W_MOBA, W_MEM, False, True),
)


N_DIL_SECTIONS = 3


def _inproj_kernel(x_ref, w_ref, cos_ref, sin_ref, *refs):
    n_dil = N_DIL_SECTIONS * len(DIL_PAIRS)
    dil_refs, flat_refs, stage = refs[:n_dil], refs[n_dil:-1], refs[-1]
    xb = x_ref[...].astype(BF16)
    cos = cos_ref[...]
    sin = sin_ref[...]
    tm = xb.shape[0]
    lane = lax.broadcasted_iota(jnp.int32, (tm, LANES), 1)
    first_half = (lane % ROPE_DIM) < ROPE_HALF

    def rope(t):
        partner = jnp.where(first_half, pltpu.roll(t, LANES - ROPE_HALF, 1), pltpu.roll(t, ROPE_HALF, 1))
        return t * cos + partner * sin

    for sec, (off, width, roped, scaled) in enumerate(_IN_SECTIONS):
        acc = jnp.dot(xb, w_ref[:, off:off + width], preferred_element_type=F32)
        for c in range(width // LANES):
            t = acc[:, c * LANES:(c + 1) * LANES]
            if roped:
                t = rope(t)
            if scaled:
                t = t * QK_SCALE
            if sec >= N_DIL_SECTIONS:
                flat_refs[sec - N_DIL_SECTIONS][:, c * LANES:(c + 1) * LANES] = t.astype(BF16)
                continue
            o_ref = dil_refs[sec * len(DIL_PAIRS) + c]
            dilation = DIL_PAIRS[c][1]
            if dilation == 1:
                o_ref[0] = t.astype(BF16)
                continue
            slot = sec * len(DIL_PAIRS) + c
            stage[slot] = t
            for r in range(dilation):
                o_ref[r] = stage[slot, pl.ds(r, tm // dilation, stride=dilation), :].astype(BF16)


def _in_projection(x2, w_in_b, cos_t, sin_t, batch, seq, tm=1024):
    n, d = x2.shape
    w_total = w_in_b.shape[1]
    tiles = seq // tm
    dil_specs, dil_shapes = [], []
    for _ in range(N_DIL_SECTIONS):
        for _, dilation in DIL_PAIRS:
            dil_specs.append(pl.BlockSpec((None, dilation, tm // dilation, LANES),
                                          lambda i: (i // tiles, 0, i % tiles, 0)))
            dil_shapes.append(jax.ShapeDtypeStruct((batch, dilation, seq // dilation, LANES), BF16))
    widths = [s[1] for s in _IN_SECTIONS[N_DIL_SECTIONS:]]
    outs = pl.pallas_call(
        _inproj_kernel,
        grid=(n // tm,),
        in_specs=[
            pl.BlockSpec((tm, d), lambda i: (i, 0)),
            pl.BlockSpec((d, w_total), lambda i: (0, 0)),
            pl.BlockSpec((tm, LANES), lambda i: (i, 0)),
            pl.BlockSpec((tm, LANES), lambda i: (i, 0)),
        ],
        out_specs=dil_specs + [pl.BlockSpec((tm, w), lambda i: (i, 0)) for w in widths],
        out_shape=dil_shapes + [jax.ShapeDtypeStruct((n, w), BF16) for w in widths],
        scratch_shapes=[pltpu.VMEM((N_DIL_SECTIONS * len(DIL_PAIRS), tm, LANES), F32)],
        compiler_params=_cparams("parallel"),
        name="in_proj_rope",
    )(x2, w_in_b, cos_t, sin_t)
    n_dil = len(dil_specs)
    ng = len(DIL_PAIRS)
    qkv_dil = [outs[s * ng:(s + 1) * ng] for s in range(N_DIL_SECTIONS)]
    return qkv_dil, outs[n_dil:]


def _rope_tables(positions):
    lane = jnp.arange(LANES)
    rotary = (lane % HEAD_DIM) < ROPE_DIM
    inv_freq = ROPE_THETA ** (-(lane % ROPE_HALF).astype(F32) / ROPE_HALF)
    ang = positions.reshape(-1).astype(F32)[:, None] * jnp.where(rotary, inv_freq, 0.0)
    sign = jnp.where((lane % ROPE_DIM) < ROPE_HALF, -1.0, 1.0)
    return jnp.cos(ang), jnp.sin(ang) * sign


def _head_mask(shape, h):
    lane = lax.broadcasted_iota(jnp.int32, shape, 1)
    return (lane // HEAD_DIM) == h


def _dil_kernel(q_ref, kp_ref, k_ref, vp_ref, v_ref, o_ref, lse_ref):
    i = pl.program_id(2)
    n_res, tq = q_ref.shape[0], q_ref.shape[1]
    qi = lax.broadcasted_iota(jnp.int32, (BAND, 2 * BAND), 0)
    kj = lax.broadcasted_iota(jnp.int32, (BAND, 2 * BAND), 1)
    dist = qi + BAND - kj
    band = (dist >= 0) & (dist <= BAND)
    first = _head_mask((BAND, LANES), 0)
    for r in range(n_res):
        for j in range(tq // BAND):
            rows = slice(j * BAND, (j + 1) * BAND)
            q = q_ref[r, rows, :]
            if j == 0:
                k_prev, v_prev = kp_ref[r], vp_ref[r]
                allowed = band & ((kj >= BAND) | (i > 0))
            else:
                prev_rows = slice((j - 1) * BAND, j * BAND)
                k_prev, v_prev = k_ref[r, prev_rows, :], v_ref[r, prev_rows, :]
                allowed = band
            k = jnp.concatenate([k_prev, k_ref[r, rows, :]], axis=0)
            v = jnp.concatenate([v_prev, v_ref[r, rows, :]], axis=0)
            outs, lses = [], []
            for h in range(2):
                qh = jnp.where(_head_mask(q.shape, h), q, jnp.zeros_like(q))
                s = jnp.where(allowed, _dot_nt(qh, k), NEG)
                m = jnp.max(s, axis=1, keepdims=True)
                p = jnp.exp(s - m)
                l = jnp.sum(p, axis=1, keepdims=True)
                outs.append(jnp.dot(p.astype(BF16), v, preferred_element_type=F32) / l)
                lses.append(jnp.broadcast_to(m + jnp.log(l), (BAND, LANES)))
            o_ref[r, rows, :] = jnp.where(first, outs[0], outs[1]).astype(o_ref.dtype)
            lse_ref[r, rows, :] = jnp.where(first, lses[0], lses[1])


def _dilated_attention(q4, k4, v4):
    batch, dilation, steps, _ = q4.shape
    tq = min(steps, DIL_TILE)
    n_res = min(dilation, DIL_TILE // tq)
    per_tile = tq // BAND
    cur = pl.BlockSpec((None, n_res, tq, LANES), lambda b, r, i: (b, r, i, 0))
    prev = pl.BlockSpec((None, n_res, BAND, LANES), lambda b, r, i: (b, r, jnp.maximum(i * per_tile - 1, 0), 0))
    return pl.pallas_call(
        _dil_kernel,
        grid=(batch, dilation // n_res, steps // tq),
        in_specs=[cur, prev, cur, prev, cur],
        out_specs=[cur, cur],
        out_shape=[jax.ShapeDtypeStruct(q4.shape, BF16), jax.ShapeDtypeStruct(q4.shape, F32)],
        compiler_params=_cparams("parallel", "parallel", "arbitrary"),
        name=f"dilated_attn_d{dilation}",
    )(q4, k4, k4, v4, v4)


def _kmean_kernel(k_ref, hi_ref, lo_ref):
    k = k_ref[...].astype(F32)
    s, w = k.shape
    mean = jnp.sum(k.reshape(s // MOBA_BLOCK, MOBA_BLOCK, w), axis=1) / MOBA_BLOCK
    hi = mean.astype(BF16)
    hi_ref[...] = hi
    lo_ref[...] = (mean - hi.astype(F32)).astype(BF16)


def _moba_kmean(km3):
    batch, seq, w = km3.shape
    nb = seq // MOBA_BLOCK
    return pl.pallas_call(
        _kmean_kernel,
        grid=(batch,),
        in_specs=[pl.BlockSpec((None, seq, w), lambda b: (b, 0, 0))],
        out_specs=[pl.BlockSpec((None, nb, w), lambda b: (b, 0, 0))] * 2,
        out_shape=[jax.ShapeDtypeStruct((batch, nb, w), BF16)] * 2,
        compiler_params=_cparams("parallel"),
        name="moba_kmean",
    )(km3)


def _moba_kernel(q_ref, k_ref, v_ref, kmh_ref, kml_ref, o_ref, qaug, kaug, vaug, m_scr, acc_scr):
    s = pl.program_id(2)
    nsteps = pl.num_programs(2)
    nb = kmh_ref.shape[0]
    tq = MOBA_BLOCK
    lane = lax.broadcasted_iota(jnp.int32, (tq, LANES), 1)
    blk = lax.broadcasted_iota(jnp.int32, (nb, tq), 0)
    slot = s % 2

    def select(step, dst_slot):
        for t in range(2):
            qi = step if t == 0 else nb - 1 - step
            q = q_ref[pl.ds(pl.multiple_of(qi * tq, tq), tq), :]
            for h in range(2):
                hm = _head_mask(q.shape, h)
                qh = jnp.where(hm, q, jnp.zeros_like(q))
                gate = _dot_nt(kmh_ref[...], qh) + _dot_nt(kml_ref[...], qh)
                cnt = jnp.zeros((nb, tq), jnp.int32)
                for jp in range(nb):
                    g_jp = gate[jp:jp + 1, :]
                    beats = (g_jp > gate) | ((g_jp == gate) & (blk > jp))
                    cnt = cnt + jnp.where(beats & (qi > jp), 1, 0)
                sel = ((blk < qi) & (cnt < MOBA_TOPK)) | (blk == qi)
                bias_t = jnp.where(sel, 0.0, NEG).astype(F32)
                spare = HEAD_DIM * (1 - h)
                pieces = [jnp.zeros((spare, tq), F32)] if spare else []
                pieces += [bias_t, jnp.zeros((LANES - spare - nb, tq), F32)]
                qaug[dst_slot, t, h] = jnp.where(hm, q, jnp.concatenate(pieces, axis=0).T.astype(BF16))

    @pl.when(s == 0)
    def _():
        for h in range(2):
            spare = HEAD_DIM * (1 - h)
            for j in range(nb):
                onehot = jnp.where(lane == spare + j, 1.0, 0.0).astype(BF16)
                kaug[h, j * tq:(j + 1) * tq, :] = jnp.where(
                    _head_mask((tq, LANES), h), k_ref[j * tq:(j + 1) * tq, :], onehot)
                vaug[h, j * tq:(j + 1) * tq, :] = jnp.where(
                    _head_mask((tq, LANES), h), v_ref[j * tq:(j + 1) * tq, :], jnp.ones((tq, LANES), BF16))
        select(0, 0)

    for t in range(2):
        for h in range(2):
            m_scr[t, h] = jnp.full((tq, LANES), NEG, F32)
            acc_scr[t, h] = jnp.zeros((tq, LANES), F32)

    row = lax.broadcasted_iota(jnp.int32, (tq, tq), 0)
    col = lax.broadcasted_iota(jnp.int32, (tq, tq), 1)
    causal_bias = jnp.where(col <= row, 0.0, NEG).astype(F32)
    starts = [pl.multiple_of(s * tq, tq), pl.multiple_of((nb - 1 - s) * tq, tq)]

    for it in range(nb + 1):
        if it < 2:
            t, kstart = it, starts[it]
        else:
            first = it - 2 < s
            t = jnp.where(first, 0, 1)
            kstart = pl.multiple_of(jnp.where(first, it - 2, it - 2 - s) * tq, tq)
        for h in range(2):
            sc = _dot_nt(qaug[slot, t, h], kaug[h, pl.ds(kstart, tq), :])
            if it < 2:
                sc = sc + causal_bias
            m_old = m_scr[t, h]
            m_new = jnp.maximum(m_old, jnp.max(sc, axis=1, keepdims=True))
            p = jnp.exp(sc - jnp.concatenate([m_new, m_new], axis=1))
            acc_scr[t, h] = (jnp.exp(m_old - m_new) * acc_scr[t, h]
                             + jnp.dot(p.astype(BF16), vaug[h, pl.ds(kstart, tq), :], preferred_element_type=F32))
            m_scr[t, h] = m_new

    first_head = _head_mask((tq, LANES), 0)
    for t in range(2):
        acc = jnp.where(first_head, acc_scr[t, 0], acc_scr[t, 1])
        den = jnp.where(first_head, pltpu.roll(acc_scr[t, 0], HEAD_DIM, 1), pltpu.roll(acc_scr[t, 1], HEAD_DIM, 1))
        o_ref[pl.ds(starts[t], tq), :] = (acc / den).astype(o_ref.dtype)

    select(jnp.minimum(s + 1, nsteps - 1), 1 - slot)


def _moba_attention(qm3, km3, vm3, kmh, kml):
    batch, seq, w = qm3.shape
    nb = seq // MOBA_BLOCK
    assert nb % 2 == 0 and nb <= HEAD_DIM
    npair = w // LANES
    tq = MOBA_BLOCK
    seq_spec = pl.BlockSpec((None, seq, LANES), lambda b, p, i: (b, 0, p))
    km_spec = pl.BlockSpec((None, nb, LANES), lambda b, p, i: (b, 0, p))
    state = pltpu.VMEM((2, 2, tq, LANES), F32)
    return pl.pallas_call(
        _moba_kernel,
        grid=(batch, npair, nb // 2),
        in_specs=[seq_spec, seq_spec, seq_spec, km_spec, km_spec],
        out_specs=seq_spec,
        out_shape=jax.ShapeDtypeStruct((batch, seq, w), BF16),
        scratch_shapes=[pltpu.VMEM((2, 2, 2, tq, LANES), BF16), pltpu.VMEM((2, seq, LANES), BF16),
                        pltpu.VMEM((2, seq, LANES), BF16), state, state],
        compiler_params=_cparams("parallel", "parallel", "arbitrary"),
        name="moba_attn",
    )(qm3, km3, vm3, kmh, kml)


def _memkv_kernel(mem_ref, w_ref, k_ref, v_ref):
    kv = jnp.dot(mem_ref[...].astype(BF16), w_ref[...], preferred_element_type=F32)
    k_ref[...] = kv[:, :W_MEM].astype(BF16)
    v_ref[...] = kv[:, W_MEM:].astype(BF16)


def _mem_kv(mem, w_kv_b):
    batch, m, d = mem.shape
    return pl.pallas_call(
        _memkv_kernel,
        grid=(batch,),
        in_specs=[pl.BlockSpec((None, m, d), lambda b: (b, 0, 0)),
                  pl.BlockSpec((d, 2 * W_MEM), lambda b: (0, 0))],
        out_specs=[pl.BlockSpec((None, m, W_MEM), lambda b: (b, 0, 0))] * 2,
        out_shape=[jax.ShapeDtypeStruct((batch, m, W_MEM), BF16)] * 2,
        compiler_params=_cparams("parallel"),
        name="mem_kv_proj",
    )(mem, w_kv_b)


def _memattn_kernel(q_ref, k_ref, v_ref, o_ref):
    tq, w = q_ref.shape
    for pair in range(w // LANES):
        lanes = slice(pair * LANES, (pair + 1) * LANES)
        k = k_ref[:, lanes]
        v = v_ref[:, lanes]
        for c in range(tq // MEM_CHUNK):
            rows = slice(c * MEM_CHUNK, (c + 1) * MEM_CHUNK)
            q = q_ref[rows, lanes]
            outs = []
            for h in range(2):
                qh = jnp.where(_head_mask(q.shape, h), q, jnp.zeros_like(q))
                s = _dot_nt(qh, k)
                m = jnp.max(s, axis=1, keepdims=True)
                p = jnp.exp(s - m)
                l = jnp.sum(p, axis=1, keepdims=True)
                outs.append(jnp.dot(p.astype(BF16), v, preferred_element_type=F32) / l)
            o_ref[rows, lanes] = jnp.where(_head_mask(outs[0].shape, 0), outs[0], outs[1]).astype(o_ref.dtype)


def _mem_attention(qx3, k_mem, v_mem, tq=2048):
    batch, seq, w = qx3.shape
    m = k_mem.shape[1]
    tq = min(tq, seq)
    return pl.pallas_call(
        _memattn_kernel,
        grid=(batch, seq // tq),
        in_specs=[pl.BlockSpec((None, tq, w), lambda b, i: (b, i, 0)),
                  pl.BlockSpec((None, m, w), lambda b, i: (b, 0, 0)),
                  pl.BlockSpec((None, m, w), lambda b, i: (b, 0, 0))],
        out_specs=pl.BlockSpec((None, tq, w), lambda b, i: (b, i, 0)),
        out_shape=jax.ShapeDtypeStruct((batch, seq, w), BF16),
        compiler_params=_cparams("parallel", "parallel"),
        name="mem_attn",
    )(qx3, k_mem, v_mem)


def _merge_kernel(alpha, x_ref, o1_ref, o2_ref, o3_ref, l1_ref, l2_ref, l3_ref, ym_ref, yx_ref,
                  wg_ref, bg_ref, wbd_ref, wbm_ref, wbx_ref, wo_ref, g_ref, b_ref, wrh_ref, wrl_ref, rb_ref,
                  out_ref, packed_ref, eidx_ref, gw_ref, rank_ref, cnt_ref, stage, carry, x1_prev):
    step = pl.program_id(0)
    route_refs = (wrh_ref, wrl_ref, rb_ref, eidx_ref, gw_ref, rank_ref, cnt_ref, carry)

    @pl.when(step == 0)
    def _():
        carry[...] = jnp.zeros_like(carry)
        x1_prev[...] = jnp.zeros_like(x1_prev)

    route = _route_tile(x1_prev[...], step > 0, *route_refs)
    next(route)
    x = x_ref[...]
    xb = x.astype(BF16)
    tm, d = x.shape

    def token_major(ref, slot):
        dilation = ref.shape[0]
        if dilation == 1:
            return ref[0].astype(F32)
        for r in range(dilation):
            stage[slot, pl.ds(r, tm // dilation, stride=dilation), :] = ref[r].astype(F32)
        return stage[slot]

    o1, o2, o3 = (token_major(r, s) for s, r in enumerate((o1_ref, o2_ref, o3_ref)))
    l1, l2, l3 = (token_major(r, s + 3) for s, r in enumerate((l1_ref, l2_ref, l3_ref)))
    mx = jnp.maximum(jnp.maximum(l1, l2), l3)
    e1, e2, e3 = jnp.exp(l1 - mx), jnp.exp(l2 - mx), jnp.exp(l3 - mx)
    y_dil = (e1 * o1 + e2 * o2 + e3 * o3) / (e1 + e2 + e3)
    next(route)
    branches = (
        jnp.dot(y_dil.astype(BF16), wbd_ref[...], preferred_element_type=F32),
        jnp.dot(ym_ref[...], wbm_ref[...], preferred_element_type=F32),
        jnp.dot(yx_ref[...], wbx_ref[...], preferred_element_type=F32),
    )
    merged = jnp.zeros_like(x)
    for i, br in enumerate(branches):
        next(route)
        logits = jnp.dot(xb, wg_ref[:, i * d:(i + 1) * d], preferred_element_type=F32) + bg_ref[:, i * d:(i + 1) * d]
        merged = merged + jax.nn.sigmoid(logits) * br
    next(route)
    mix = jnp.dot(merged.astype(BF16), wo_ref[...], preferred_element_type=F32)
    for _ in route:
        pass
    x1 = _layer_norm(alpha * x + mix, g_ref[...], b_ref[...])
    out_ref[...] = x1
    packed_ref[...] = _pack_halves(x1)
    x1_prev[...] = x1


def _merge(alpha, x2, o_dil, lse_dil, y_moba, y_mem, wg, bg, wbd, wbm, wbx, wo, g, b, wr_hi_t, wr_lo_t, rbias_b,
           seq, tm=512):
    n, d = x2.shape
    tiles = seq // tm
    last = n // tm - 1
    cur = lambda i: jnp.minimum(i, last)
    row = lambda w: pl.BlockSpec((tm, w), lambda i: (cur(i), 0))
    full = lambda a: pl.BlockSpec(a.shape, lambda i: (0, 0))
    residue_major = lambda a: pl.BlockSpec((None, a.shape[1], tm // a.shape[1], LANES),
                                           lambda i: (cur(i) // tiles, 0, cur(i) % tiles, 0))
    slots = pl.BlockSpec((TOP_K, tm), lambda i: (0, jnp.maximum(i - 1, 0)))
    weights = (wg, bg, wbd, wbm, wbx, wo, g, b, wr_hi_t, wr_lo_t, rbias_b)
    tok = lambda dt: jax.ShapeDtypeStruct((TOP_K, n), dt)
    return pl.pallas_call(
        functools.partial(_merge_kernel, alpha),
        grid=(n // tm + 1,),
        in_specs=([row(d)] + [residue_major(a) for a in (*o_dil, *lse_dil)] + [row(W_MOBA), row(W_MEM)]
                  + [full(a) for a in weights]),
        out_specs=[row(d), row(PACKED), slots, slots, slots, pl.BlockSpec((N_EXPERTS, LANES), lambda i: (0, 0))],
        out_shape=[jax.ShapeDtypeStruct((n, d), F32), jax.ShapeDtypeStruct((n, PACKED), jnp.uint32),
                   tok(jnp.int32), tok(F32), tok(jnp.int32), jax.ShapeDtypeStruct((N_EXPERTS, LANES), jnp.int32)],
        scratch_shapes=[pltpu.VMEM((2 * len(DIL_PAIRS), tm, LANES), F32), pltpu.VMEM((N_EXPERTS, LANES), F32),
                        pltpu.VMEM((tm, d), F32)],
        compiler_params=_cparams("arbitrary"),
        name="merge_outproj_ln1_route",
    )(x2, *o_dil, *lse_dil, y_moba, y_mem, *weights)


def _token_mixer(x2, mem, positions, w_in, w_mem_kv, w_gate, b_gate, w_br_dil, w_br_moba, w_br_mem,
                 w_out, ln1_g, ln1_b, w_router, router_bias, batch, seq, alpha):
    wr_t = w_router.T
    wr_hi = wr_t.astype(BF16)
    wr_lo = (wr_t - wr_hi.astype(F32)).astype(BF16)
    rbias_b = jnp.broadcast_to(router_bias.astype(F32)[:, None], (N_EXPERTS, LANES))
    cos_t, sin_t = _rope_tables(positions)
    (q_dil, k_dil, v_dil), (qm, km, vm, qx) = _in_projection(x2, w_in.astype(BF16), cos_t, sin_t, batch, seq)
    o_dil, lse_dil = [], []
    for g in range(len(DIL_PAIRS)):
        o, lse = _dilated_attention(q_dil[g], k_dil[g], v_dil[g])
        o_dil.append(o)
        lse_dil.append(lse)
    qm3, km3, vm3 = (t.reshape(batch, seq, W_MOBA) for t in (qm, km, vm))
    kmh, kml = _moba_kmean(km3)
    y_moba = _moba_attention(qm3, km3, vm3, kmh, kml).reshape(batch * seq, W_MOBA)
    k_mem, v_mem = _mem_kv(mem, w_mem_kv.astype(BF16))
    y_mem = _mem_attention(qx.reshape(batch, seq, W_MEM), k_mem, v_mem).reshape(batch * seq, W_MEM)
    return _merge(alpha, x2, o_dil, lse_dil, y_moba, y_mem,
                  w_gate.astype(BF16), b_gate.reshape(1, -1), w_br_dil.astype(BF16),
                  w_br_moba.astype(BF16), w_br_mem.astype(BF16), w_out.astype(BF16),
                  ln1_g.reshape(1, -1), ln1_b.reshape(1, -1), wr_hi, wr_lo, rbias_b, seq)


EXPERT_ROWS = 256
EXPERT_GROUP = 4
EXPERT_RING = 2 * EXPERT_GROUP + 1
TOKEN_TILE = 512
PACKED = 512
SC_WINDOW = 128


def _first_index_of_max(v, iota_f, size):
    m = jnp.max(v, axis=0, keepdims=True)
    idx = jnp.min(jnp.where(v == m, iota_f, float(size)), axis=0, keepdims=True)
    return m, idx


def _route_tile(x, counted, wh_ref, wl_ref, bias_ref, eidx_ref, gw_ref, rank_ref, cnt_ref, carry):
    tm = x.shape[0]
    xh = x.astype(BF16)
    xl = (x - xh.astype(F32)).astype(BF16)
    wh = wh_ref[...]
    logits = _dot_nt(wh, xh) + _dot_nt(wh, xl) + _dot_nt(wl_ref[...], xh)
    yield
    scores = jax.nn.sigmoid(logits)
    biased = scores + bias_ref[...][:, :1]

    giota = lax.broadcasted_iota(jnp.int32, (GROUP_SIZE, tm), 0).astype(F32)
    group_scores = []
    for g in range(N_GROUPS):
        slab = biased[g * GROUP_SIZE:(g + 1) * GROUP_SIZE, :]
        m1, i1 = _first_index_of_max(slab, giota, GROUP_SIZE)
        m2 = jnp.max(jnp.where(giota == i1, -jnp.inf, slab), axis=0, keepdims=True)
        group_scores.append(m1 + m2)
    gs = jnp.concatenate(group_scores, axis=0)
    gidx = lax.broadcasted_iota(jnp.int32, (N_GROUPS, tm), 0)
    beaten = jnp.zeros((N_GROUPS, tm), jnp.int32)
    for gp in range(N_GROUPS):
        row = gs[gp:gp + 1, :]
        beaten = beaten + jnp.where((row > gs) | ((row == gs) & (gidx > gp)), 1, 0)
    keep = beaten < TOPK_GROUPS

    ahead = jnp.zeros((1, tm), jnp.int32)
    slab_b = [jnp.full((GROUP_SIZE, tm), -jnp.inf, F32) for _ in range(TOPK_GROUPS)]
    slab_s = [jnp.zeros((GROUP_SIZE, tm), F32) for _ in range(TOPK_GROUPS)]
    slab_g = [jnp.zeros((1, tm), F32) for _ in range(TOPK_GROUPS)]
    place = {}
    for g in range(N_GROUPS):
        kept = keep[g:g + 1, :]
        rows = slice(g * GROUP_SIZE, (g + 1) * GROUP_SIZE)
        for j in range(min(g, TOPK_GROUPS - 1) + 1):
            here = kept & (ahead == j)
            place[j, g] = here
            slab_b[j] = jnp.where(here, biased[rows, :], slab_b[j])
            slab_s[j] = jnp.where(here, scores[rows, :], slab_s[j])
            slab_g[j] = jnp.where(here, float(g), slab_g[j])
        ahead = ahead + jnp.where(kept, 1, 0)
    masked = jnp.concatenate(slab_b, axis=0)
    kept_scores = jnp.concatenate(slab_s, axis=0)
    yield

    n_kept = TOPK_GROUPS * GROUP_SIZE
    ciota = lax.broadcasted_iota(jnp.int32, (n_kept, tm), 0).astype(F32)
    chosen = jnp.zeros((n_kept, tm), F32)
    cidx_rows, idx_rows, gw_rows = [], [], []
    for k in range(TOP_K):
        _, cidx = _first_index_of_max(masked, ciota, n_kept)
        hit = ciota == cidx
        gw_rows.append(jnp.sum(jnp.where(hit, kept_scores, 0.0), axis=0, keepdims=True))
        masked = jnp.where(hit, -jnp.inf, masked)
        chosen = jnp.where(hit, 1.0, chosen)
        slab = sum(jnp.where(cidx >= float(j * GROUP_SIZE), 1.0, 0.0) for j in range(1, TOPK_GROUPS))
        group = sum(jnp.where(slab == float(j), slab_g[j], 0.0) for j in range(TOPK_GROUPS))
        cidx_rows.append(cidx)
        idx_rows.append((group - slab) * GROUP_SIZE + cidx)
        if k % 2 == 1:
            yield
    idx8 = jnp.concatenate(idx_rows, axis=0)
    gw8 = jnp.concatenate(gw_rows, axis=0)
    gw_ref[...] = gw8 / jnp.sum(gw8, axis=0, keepdims=True) * ROUTED_SCALE
    eidx_ref[...] = idx8.astype(jnp.int32)

    def slabs_to_groups(compact):
        out = []
        for g in range(N_GROUPS):
            acc = jnp.zeros((GROUP_SIZE, tm), F32)
            for j in range(min(g, TOPK_GROUPS - 1) + 1):
                acc = jnp.where(place[j, g], compact[j * GROUP_SIZE:(j + 1) * GROUP_SIZE, :], acc)
            out.append(acc)
        return jnp.concatenate(out, axis=0)

    def groups_to_slabs(full):
        out = []
        for j in range(TOPK_GROUPS):
            acc = jnp.zeros((GROUP_SIZE, tm), F32)
            for g in range(j, N_GROUPS):
                acc = jnp.where(place[j, g], full[g * GROUP_SIZE:(g + 1) * GROUP_SIZE, :], acc)
            out.append(acc)
        return jnp.concatenate(out, axis=0)

    onehot = slabs_to_groups(chosen)
    t_row = lax.broadcasted_iota(jnp.int32, (tm, tm), 0)
    t_col = lax.broadcasted_iota(jnp.int32, (tm, tm), 1)
    earlier = jnp.where(t_row < t_col, 1.0, 0.0).astype(BF16)
    prefix = jnp.dot(onehot.astype(BF16), earlier, preferred_element_type=F32)
    base = carry[...]
    prefix = groups_to_slabs(prefix + jnp.concatenate([base] * (tm // LANES), axis=1))
    rank_rows = [jnp.sum(jnp.where(ciota == cidx, prefix, 0.0), axis=0, keepdims=True) for cidx in cidx_rows]
    rank_ref[...] = jnp.concatenate(rank_rows, axis=0).astype(jnp.int32)
    total = base + jnp.where(counted, jnp.sum(onehot, axis=1, keepdims=True), 0.0)
    carry[...] = total
    cnt_ref[...] = total.astype(jnp.int32)


def _dest_kernel(eidx_ref, rank_ref, start_ref, dest_ref):
    eidx = eidx_ref[...]
    tm = eidx.shape[1]
    eiota = lax.broadcasted_iota(jnp.int32, (N_EXPERTS, tm), 0)
    start = start_ref[...][:, :1]
    rows = [jnp.sum(jnp.where(eiota == eidx[k:k + 1, :], start, 0.0), axis=0, keepdims=True)
            for k in range(TOP_K)]
    dest_ref[...] = jnp.concatenate(rows, axis=0).astype(jnp.int32) + rank_ref[...]


def _dest_rows(eidx, rank, start_b):
    n = eidx.shape[1]
    nt = n // TOKEN_TILE
    return pl.pallas_call(
        _dest_kernel,
        grid=(nt,),
        in_specs=[pl.BlockSpec((TOP_K, TOKEN_TILE), lambda i: (0, i)),
                  pl.BlockSpec((TOP_K, TOKEN_TILE), lambda i: (0, i)),
                  pl.BlockSpec((N_EXPERTS, LANES), lambda i: (0, 0))],
        out_specs=pl.BlockSpec((TOP_K, TOKEN_TILE), lambda i: (0, i)),
        out_shape=jax.ShapeDtypeStruct((TOP_K, n), jnp.int32),
        compiler_params=_cparams("parallel"),
        name="moe_dest_rows",
    )(eidx, rank, start_b)


def _pack_halves(v):
    return pltpu.pack_elementwise([v[:, :PACKED], v[:, PACKED:]], packed_dtype=BF16)


def _unpack_half(p, index):
    return pltpu.unpack_elementwise(p, index=index, packed_dtype=BF16, unpacked_dtype=F32)


def _sc_mesh():
    return plsc.VectorSubcoreMesh(core_axis_name="core", subcore_axis_name="subcore")


def _sc_move_rows(src, src_idx, dst_idx, out_rows, name):
    width = src.shape[1]
    count = src_idx.shape[1]

    @functools.partial(pl.kernel, out_type=jax.ShapeDtypeStruct((out_rows, width), src.dtype),
                       mesh=_sc_mesh(), scratch_types=[pltpu.VMEM((SC_WINDOW, width), src.dtype)], name=name)
    def move(src_hbm, sidx_hbm, didx_hbm, out_hbm, rows_vmem):
        def body(sidx_vmem, didx_vmem):
            pltpu.sync_copy(src_hbm.at[sidx_vmem.at[0]], rows_vmem)
            pltpu.sync_copy(rows_vmem, out_hbm.at[didx_vmem.at[0]])

        idx_spec = pl.BlockSpec((1, SC_WINDOW), index_map=lambda i: (0, i))
        pltpu.emit_pipeline(
            body,
            grid=(count // SC_WINDOW,),
            in_specs=[idx_spec, idx_spec],
            out_specs=[],
            core_axis_name=("core", "subcore"),
            dimension_semantics=(pltpu.PARALLEL,),
        )(sidx_hbm, didx_hbm)

    return move(src, src_idx, dst_idx)


def _sc_dispatch_rows(src, token_idx, dest, out_rows):
    n, width = src.shape
    slots = dest.shape[0]

    @functools.partial(pl.kernel, out_type=jax.ShapeDtypeStruct((out_rows, width), src.dtype),
                       mesh=_sc_mesh(), scratch_types=[pltpu.VMEM((SC_WINDOW, width), src.dtype)],
                       name="moe_dispatch_sc")
    def dispatch(src_hbm, tidx_hbm, dest_hbm, out_hbm, rows_vmem):
        def body(tidx_vmem, dest_vmem):
            pltpu.sync_copy(src_hbm.at[tidx_vmem.at[0]], rows_vmem)
            for k in range(slots):
                pltpu.sync_copy(rows_vmem, out_hbm.at[dest_vmem.at[k]])

        pltpu.emit_pipeline(
            body,
            grid=(n // SC_WINDOW,),
            in_specs=[pl.BlockSpec((1, SC_WINDOW), index_map=lambda i: (0, i)),
                      pl.BlockSpec((slots, SC_WINDOW), index_map=lambda i: (0, i))],
            out_specs=[],
            core_axis_name=("core", "subcore"),
            dimension_semantics=(pltpu.PARALLEL,),
        )(tidx_hbm, dest_hbm)

    return dispatch(src, token_idx, dest)


def _expert_kernel(first_blk_ref, count_ref, total_ref, xs_hbm, wg_ref, wu_ref, wd_ref, ys_hbm,
                   wg_b, wu_b, wd_b, x_ring, y_ring, in_sem, out_sem):
    e = pl.program_id(0)
    total = total_ref[0]
    first = first_blk_ref[e]
    count = count_ref[e]
    nblk = (count + EXPERT_ROWS - 1) // EXPERT_ROWS
    spare_block = ys_hbm.shape[0] // EXPERT_ROWS - EXPERT_RING

    def rows_of(g):
        return pl.ds(pl.multiple_of(g * EXPERT_ROWS, EXPERT_ROWS), EXPERT_ROWS)

    def in_copy(g):
        slot = g % EXPERT_RING
        return pltpu.make_async_copy(xs_hbm.at[rows_of(g)], x_ring.at[slot], in_sem.at[slot])

    def out_copy(g, slot):
        return pltpu.make_async_copy(y_ring.at[slot], ys_hbm.at[rows_of(g)], out_sem.at[slot])

    @pl.when(e == 0)
    def _():
        y_ring[...] = jnp.zeros_like(y_ring)
        for g in range(EXPERT_RING - 1):
            in_copy(g).start()
        for slot in range(EXPERT_RING):
            out_copy(spare_block + slot, slot).start()

    @pl.when(nblk > 0)
    def _():
        wg_b[...] = wg_ref[...].astype(BF16)
        wu_b[...] = wu_ref[...].astype(BF16)
        wd_b[...] = wd_ref[...].astype(BF16)

    def blocks(js):
        gs = [first + j for j in js]
        for g in gs:
            in_copy(g).wait()
        ys = []
        for j, g in zip(js, gs):
            p = x_ring[g % EXPERT_RING]
            live = lax.broadcasted_iota(jnp.int32, p.shape, 0) + j * EXPERT_ROWS < count
            lo = jnp.where(live, _unpack_half(p, 0), 0.0).astype(BF16)
            hi = jnp.where(live, _unpack_half(p, 1), 0.0).astype(BF16)

            def up(w, lo=lo, hi=hi):
                return (jnp.dot(lo, w[:PACKED, :], preferred_element_type=F32)
                        + jnp.dot(hi, w[PACKED:, :], preferred_element_type=F32))

            hid = (jax.nn.silu(up(wg_b)) * up(wu_b)).astype(BF16)
            ys.append(_pack_halves(jnp.dot(hid, wd_b[...], preferred_element_type=F32)))
        for g in gs:
            in_copy(g + EXPERT_RING - 1).start()
        for g in gs:
            out_copy(g, g % EXPERT_RING).wait()
        for g, y in zip(gs, ys):
            y_ring[g % EXPERT_RING] = y
        for g in gs:
            out_copy(g, g % EXPERT_RING).start()

    def group(i, carry):
        blocks([EXPERT_GROUP * i + b for b in range(EXPERT_GROUP)])
        return carry

    lax.fori_loop(0, nblk // EXPERT_GROUP, group, 0)
    left = nblk % EXPERT_GROUP

    @pl.when(left >= 2)
    def _():
        blocks([nblk - left, nblk - left + 1])

    @pl.when(left % 2 == 1)
    def _():
        blocks([nblk - 1])

    @pl.when(e == pl.num_programs(0) - 1)
    def _():
        for slot in range(EXPERT_RING):
            out_copy(spare_block + slot, slot).wait()
        for g in range(EXPERT_RING - 1):
            in_copy(total + g).wait()


def _expert_ffn(xs, first_blk, counts, total, w_e_gate, w_e_up, w_e_down):
    rows = xs.shape[0] + EXPERT_ROWS
    n_experts, d = w_e_gate.shape[0], w_e_gate.shape[1]
    w_map = lambda e, fb, ct, tt: (e, 0, 0)
    ring = pltpu.VMEM((EXPERT_RING, EXPERT_ROWS, PACKED), jnp.uint32)
    return pl.pallas_call(
        _expert_kernel,
        grid_spec=pltpu.PrefetchScalarGridSpec(
            num_scalar_prefetch=3,
            grid=(n_experts,),
            in_specs=[pl.BlockSpec(memory_space=pl.ANY),
                      pl.BlockSpec((None, d, D_EXPERT), w_map),
                      pl.BlockSpec((None, d, D_EXPERT), w_map),
                      pl.BlockSpec((None, D_EXPERT, d), w_map)],
            out_specs=pl.BlockSpec(memory_space=pl.ANY),
            scratch_shapes=[pltpu.VMEM((d, D_EXPERT), BF16), pltpu.VMEM((d, D_EXPERT), BF16),
                            pltpu.VMEM((D_EXPERT, d), BF16), ring, ring,
                            pltpu.SemaphoreType.DMA((EXPERT_RING,)), pltpu.SemaphoreType.DMA((EXPERT_RING,))],
        ),
        out_shape=jax.ShapeDtypeStruct((rows, PACKED), jnp.uint32),
        compiler_params=_cparams("arbitrary"),
        name="moe_expert_ffn",
    )(first_blk, counts, total, xs, w_e_gate, w_e_up, w_e_down)


def _combine_kernel(alpha, x_ref, gw_ref, ys_ref, wsg_ref, wsu_ref, wsd_ref, g_ref, b_ref, out_ref):
    x = x_ref[...]
    xb = x.astype(BF16)
    hid = (jax.nn.silu(jnp.dot(xb, wsg_ref[...], preferred_element_type=F32))
           * jnp.dot(xb, wsu_ref[...], preferred_element_type=F32)).astype(BF16)
    shared = jnp.dot(hid, wsd_ref[...], preferred_element_type=F32)
    gw = gw_ref[...]
    lo = jnp.zeros((TOKEN_TILE, PACKED), F32)
    hi = jnp.zeros((TOKEN_TILE, PACKED), F32)
    for k in range(TOP_K):
        p = ys_ref[k]
        w = gw[:, k:k + 1]
        lo = lo + w * _unpack_half(p, 0)
        hi = hi + w * _unpack_half(p, 1)
    routed = jnp.concatenate([lo, hi], axis=1)
    out_ref[...] = _layer_norm(alpha * x + (routed + shared), g_ref[...], b_ref[...])


def _combine(alpha, x1, gw_t, ys_tok, wsg, wsu, wsd, g, b):
    n, d = x1.shape
    full = lambda a: pl.BlockSpec(a.shape, lambda i: (0, 0))
    weights = (wsg, wsu, wsd, g, b)
    return pl.pallas_call(
        functools.partial(_combine_kernel, alpha),
        grid=(n // TOKEN_TILE,),
        in_specs=[pl.BlockSpec((TOKEN_TILE, d), lambda i: (i, 0)),
                  pl.BlockSpec((TOKEN_TILE, TOP_K), lambda i: (i, 0)),
                  pl.BlockSpec((TOP_K, TOKEN_TILE, PACKED), lambda i: (0, i, 0))] + [full(a) for a in weights],
        out_specs=pl.BlockSpec((TOKEN_TILE, d), lambda i: (i, 0)),
        out_shape=jax.ShapeDtypeStruct((n, d), F32),
        compiler_params=_cparams("parallel"),
        name="moe_combine_shared_ln2",
    )(x1, gw_t, ys_tok, *weights)


def _moe_layer(x1, x1_packed, eidx, gw, rank, cnt, w_e_gate, w_e_up, w_e_down, w_s_gate, w_s_up, w_s_down,
               ln2_g, ln2_b, alpha):
    n, d = x1.shape

    counts = cnt[:, 0]
    padded = (counts + EXPERT_ROWS - 1) // EXPERT_ROWS * EXPERT_ROWS
    seg_end = jnp.cumsum(padded)
    seg_start = seg_end - padded
    rows = n * TOP_K + N_EXPERTS * EXPERT_ROWS
    first_blk = (seg_start // EXPERT_ROWS).astype(jnp.int32)
    total_blk = (seg_end[-1:] // EXPERT_ROWS).astype(jnp.int32)
    start_b = jnp.broadcast_to(seg_start.astype(F32)[:, None], (N_EXPERTS, LANES))

    dest = _dest_rows(eidx, rank, start_b)
    dest_flat = dest.reshape(1, TOP_K * n)
    assign = jnp.arange(TOP_K * n, dtype=jnp.int32).reshape(1, TOP_K * n)
    xs = _sc_dispatch_rows(x1_packed, assign[:, :n], dest, rows + (EXPERT_RING - 1) * EXPERT_ROWS)
    ys = _expert_ffn(xs, first_blk, counts, total_blk, w_e_gate, w_e_up, w_e_down)
    ys_tok = _sc_move_rows(ys, dest_flat, assign, TOP_K * n, "moe_gather_sc").reshape(TOP_K, n, PACKED)
    return _combine(alpha, x1, gw.T, ys_tok, w_s_gate.astype(BF16), w_s_up.astype(BF16),
                    w_s_down.astype(BF16), ln2_g.reshape(1, -1), ln2_b.reshape(1, -1))


def kernel(x, mem, positions, w_in, w_mem_kv, w_gate, b_gate, w_br_dil, w_br_moba, w_br_mem, w_out, ln1_g, ln1_b, w_router, router_bias, w_e_gate, w_e_up, w_e_down, w_s_gate, w_s_up, w_s_down, ln2_g, ln2_b):
    batch, seq, d = x.shape
    depth = w_in.shape[0]
    alpha = (2.0 * depth) ** 0.25
    h = x.reshape(batch * seq, d)
    for l in range(depth):
        h, h_packed, eidx, gw, rank, cnt = _token_mixer(
            h, mem, positions, w_in[l], w_mem_kv[l], w_gate[l], b_gate[l], w_br_dil[l], w_br_moba[l], w_br_mem[l],
            w_out[l], ln1_g[l], ln1_b[l], w_router[l], router_bias[l], batch, seq, alpha)
        h = _moe_layer(h, h_packed, eidx, gw, rank, cnt, w_e_gate[l], w_e_up[l], w_e_down[l],
                       w_s_gate[l], w_s_up[l], w_s_down[l], ln2_g[l], ln2_b[l], alpha)
    return h.reshape(batch, seq, d)
```

```python
import functools

import jax
import jax.numpy as jnp
from jax import lax
from jax.experimental import pallas as pl
from jax.experimental.pallas import tpu as pltpu
from jax.experimental.pallas import tpu_sc as plsc

F32 = jnp.float32
BF16 = jnp.bfloat16

LANES = 128
VMEM_LIMIT_BYTES = 48 * 1024 * 1024

HEAD_DIM = 64
ROPE_DIM = HEAD_DIM // 4
ROPE_HALF = ROPE_DIM // 2
ROPE_THETA = 500000.0
DIL_PAIRS = ((128, 1), (512, 4), (2048, 16))
BAND = 128
DIL_TILE = 1024
MEM_CHUNK = 512
W_DIL = 384
W_MOBA = 384
W_MEM = 256
MOBA_BLOCK = 256
MOBA_TOPK = 3
N_EXPERTS = 256
N_GROUPS = 8
GROUP_SIZE = N_EXPERTS // N_GROUPS
TOPK_GROUPS = 4
TOP_K = 8
D_EXPERT = 256
ROUTED_SCALE = 2.5
LN_EPS = 1e-5
NEG = -1e30
QK_SCALE = HEAD_DIM ** -0.5


def _cparams(*sem):
    return pltpu.CompilerParams(dimension_semantics=sem, vmem_limit_bytes=VMEM_LIMIT_BYTES)


def _dot_nt(a, b):
    return lax.dot_general(a, b, (((1,), (1,)), ((), ())), preferred_element_type=F32)


def _layer_norm(v, g, b):
    mu = jnp.mean(v, axis=-1, keepdims=True)
    c = v - mu
    var = jnp.mean(c * c, axis=-1, keepdims=True)
    return c * lax.rsqrt(var + LN_EPS) * g + b


_IN_SECTIONS = (
    (0, W_DIL, True, True),
    (W_DIL, W_DIL, True, False),
    (2 * W_DIL, W_DIL, False, False),
    (3 * W_DIL, W_MOBA, True, True),
    (3 * W_DIL + W_MOBA, W_MOBA, True, False),
    (3 * W_DIL + 2 * W_MOBA, W_MOBA, False, False),
    (3 * W_DIL + 3 * W_MOBA, W_MEM, False, True),
)


N_DIL_SECTIONS = 3
MATMUL_COLS = 512


def _inproj_kernel(x_ref, w_ref, cos_ref, sin_ref, *refs):
    n_dil = N_DIL_SECTIONS * len(DIL_PAIRS)
    dil_refs, flat_refs, stage = refs[:n_dil], refs[n_dil:-1], refs[-1]
    xb = x_ref[...].astype(BF16)
    cos = cos_ref[...]
    sin = sin_ref[...]
    tm = xb.shape[0]
    lane = lax.broadcasted_iota(jnp.int32, (tm, LANES), 1)
    first_half = (lane % ROPE_DIM) < ROPE_HALF

    def rope(t):
        partner = jnp.where(first_half, pltpu.roll(t, LANES - ROPE_HALF, 1), pltpu.roll(t, ROPE_HALF, 1))
        return t * cos + partner * sin

    chunk_of = {}
    for sec, (off, width, _, _) in enumerate(_IN_SECTIONS):
        for c in range(width // LANES):
            chunk_of[off // LANES + c] = (sec, c)
    per_group = MATMUL_COLS // LANES
    for grp in range(w_ref.shape[1] // MATMUL_COLS):
        acc = jnp.dot(xb, w_ref[:, grp * MATMUL_COLS:(grp + 1) * MATMUL_COLS], preferred_element_type=F32)
        for cc in range(per_group):
            sec, c = chunk_of[grp * per_group + cc]
            _, _, roped, scaled = _IN_SECTIONS[sec]
            t = acc[:, cc * LANES:(cc + 1) * LANES]
            if roped:
                t = rope(t)
            if scaled:
                t = t * QK_SCALE
            if sec >= N_DIL_SECTIONS:
                flat_refs[sec - N_DIL_SECTIONS][:, c * LANES:(c + 1) * LANES] = t.astype(BF16)
                continue
            o_ref = dil_refs[sec * len(DIL_PAIRS) + c]
            dilation = DIL_PAIRS[c][1]
            if dilation == 1:
                o_ref[0] = t.astype(BF16)
                continue
            slot = sec * len(DIL_PAIRS) + c
            stage[slot] = t
            for r in range(dilation):
                o_ref[r] = stage[slot, pl.ds(r, tm // dilation, stride=dilation), :].astype(BF16)


def _in_projection(x2, w_in_b, cos_t, sin_t, batch, seq, tm=1024):
    n, d = x2.shape
    w_total = w_in_b.shape[1]
    tiles = seq // tm
    dil_specs, dil_shapes = [], []
    for _ in range(N_DIL_SECTIONS):
        for _, dilation in DIL_PAIRS:
            dil_specs.append(pl.BlockSpec((None, dilation, tm // dilation, LANES),
                                          lambda i: (i // tiles, 0, i % tiles, 0)))
            dil_shapes.append(jax.ShapeDtypeStruct((batch, dilation, seq // dilation, LANES), BF16))
    widths = [s[1] for s in _IN_SECTIONS[N_DIL_SECTIONS:]]
    outs = pl.pallas_call(
        _inproj_kernel,
        grid=(n // tm,),
        in_specs=[
            pl.BlockSpec((tm, d), lambda i: (i, 0)),
            pl.BlockSpec((d, w_total), lambda i: (0, 0)),
            pl.BlockSpec((tm, LANES), lambda i: (i, 0)),
            pl.BlockSpec((tm, LANES), lambda i: (i, 0)),
        ],
        out_specs=dil_specs + [pl.BlockSpec((tm, w), lambda i: (i, 0)) for w in widths],
        out_shape=dil_shapes + [jax.ShapeDtypeStruct((n, w), BF16) for w in widths],
        scratch_shapes=[pltpu.VMEM((N_DIL_SECTIONS * len(DIL_PAIRS), tm, LANES), F32)],
        compiler_params=_cparams("parallel"),
        name="in_proj_rope",
    )(x2, w_in_b, cos_t, sin_t)
    n_dil = len(dil_specs)
    ng = len(DIL_PAIRS)
    qkv_dil = [outs[s * ng:(s + 1) * ng] for s in range(N_DIL_SECTIONS)]
    return qkv_dil, outs[n_dil:]


def _rope_tables(positions):
    lane = jnp.arange(LANES)
    rotary = (lane % HEAD_DIM) < ROPE_DIM
    inv_freq = ROPE_THETA ** (-(lane % ROPE_HALF).astype(F32) / ROPE_HALF)
    ang = positions.reshape(-1).astype(F32)[:, None] * jnp.where(rotary, inv_freq, 0.0)
    sign = jnp.where((lane % ROPE_DIM) < ROPE_HALF, -1.0, 1.0)
    return jnp.cos(ang), jnp.sin(ang) * sign


def _head_mask(shape, h):
    lane = lax.broadcasted_iota(jnp.int32, shape, 1)
    return (lane // HEAD_DIM) == h


def _dil_kernel(q_ref, kp_ref, k_ref, vp_ref, v_ref, o_ref, lse_ref):
    i = pl.program_id(2)
    n_res, tq = q_ref.shape[0], q_ref.shape[1]
    qi = lax.broadcasted_iota(jnp.int32, (BAND, 2 * BAND), 0)
    kj = lax.broadcasted_iota(jnp.int32, (BAND, 2 * BAND), 1)
    dist = qi + BAND - kj
    band = (dist >= 0) & (dist <= BAND)
    first = _head_mask((BAND, LANES), 0)
    for r in range(n_res):
        for j in range(tq // BAND):
            rows = slice(j * BAND, (j + 1) * BAND)
            q = q_ref[r, rows, :]
            if j == 0:
                k_prev, v_prev = kp_ref[r], vp_ref[r]
                allowed = band & ((kj >= BAND) | (i > 0))
            else:
                prev_rows = slice((j - 1) * BAND, j * BAND)
                k_prev, v_prev = k_ref[r, prev_rows, :], v_ref[r, prev_rows, :]
                allowed = band
            k = jnp.concatenate([k_prev, k_ref[r, rows, :]], axis=0)
            v = jnp.concatenate([v_prev, v_ref[r, rows, :]], axis=0)
            outs, lses = [], []
            for h in range(2):
                qh = jnp.where(_head_mask(q.shape, h), q, jnp.zeros_like(q))
                s = jnp.where(allowed, _dot_nt(qh, k), NEG)
                m = jnp.max(s, axis=1, keepdims=True)
                p = jnp.exp(s - m)
                l = jnp.sum(p, axis=1, keepdims=True)
                outs.append(jnp.dot(p.astype(BF16), v, preferred_element_type=F32) / l)
                lses.append(jnp.broadcast_to(m + jnp.log(l), (BAND, LANES)))
            o_ref[r, rows, :] = jnp.where(first, outs[0], outs[1]).astype(o_ref.dtype)
            lse_ref[r, rows, :] = jnp.where(first, lses[0], lses[1])


def _dilated_attention(q4, k4, v4):
    batch, dilation, steps, _ = q4.shape
    tq = min(steps, DIL_TILE)
    n_res = min(dilation, DIL_TILE // tq)
    per_tile = tq // BAND
    cur = pl.BlockSpec((None, n_res, tq, LANES), lambda b, r, i: (b, r, i, 0))
    prev = pl.BlockSpec((None, n_res, BAND, LANES), lambda b, r, i: (b, r, jnp.maximum(i * per_tile - 1, 0), 0))
    return pl.pallas_call(
        _dil_kernel,
        grid=(batch, dilation // n_res, steps // tq),
        in_specs=[cur, prev, cur, prev, cur],
        out_specs=[cur, cur],
        out_shape=[jax.ShapeDtypeStruct(q4.shape, BF16), jax.ShapeDtypeStruct(q4.shape, F32)],
        compiler_params=_cparams("parallel", "parallel", "arbitrary"),
        name=f"dilated_attn_d{dilation}",
    )(q4, k4, k4, v4, v4)


def _kmean_kernel(k_ref, hi_ref, lo_ref):
    k = k_ref[...].astype(F32)
    s, w = k.shape
    mean = jnp.sum(k.reshape(s // MOBA_BLOCK, MOBA_BLOCK, w), axis=1) / MOBA_BLOCK
    hi = mean.astype(BF16)
    hi_ref[...] = hi
    lo_ref[...] = (mean - hi.astype(F32)).astype(BF16)


def _moba_kmean(km3):
    batch, seq, w = km3.shape
    nb = seq // MOBA_BLOCK
    return pl.pallas_call(
        _kmean_kernel,
        grid=(batch,),
        in_specs=[pl.BlockSpec((None, seq, w), lambda b: (b, 0, 0))],
        out_specs=[pl.BlockSpec((None, nb, w), lambda b: (b, 0, 0))] * 2,
        out_shape=[jax.ShapeDtypeStruct((batch, nb, w), BF16)] * 2,
        compiler_params=_cparams("parallel"),
        name="moba_kmean",
    )(km3)


def _moba_kernel(q_ref, k_ref, v_ref, kmh_ref, kml_ref, o_ref, qaug, kaug, vaug, m_scr, acc_scr):
    s = pl.program_id(2)
    nsteps = pl.num_programs(2)
    nb = kmh_ref.shape[0]
    tq = MOBA_BLOCK
    lane = lax.broadcasted_iota(jnp.int32, (tq, LANES), 1)
    blk = lax.broadcasted_iota(jnp.int32, (nb, tq), 0)
    slot = s % 2

    def select(step, dst_slot):
        for t in range(2):
            qi = step if t == 0 else nb - 1 - step
            q = q_ref[pl.ds(pl.multiple_of(qi * tq, tq), tq), :]
            for h in range(2):
                hm = _head_mask(q.shape, h)
                qh = jnp.where(hm, q, jnp.zeros_like(q))
                gate = _dot_nt(kmh_ref[...], qh) + _dot_nt(kml_ref[...], qh)
                cnt = jnp.zeros((nb, tq), jnp.int32)
                for jp in range(nb):
                    g_jp = gate[jp:jp + 1, :]
                    beats = (g_jp > gate) | ((g_jp == gate) & (blk > jp))
                    cnt = cnt + jnp.where(beats & (qi > jp), 1, 0)
                sel = ((blk < qi) & (cnt < MOBA_TOPK)) | (blk == qi)
                bias_t = jnp.where(sel, 0.0, NEG).astype(F32)
                spare = HEAD_DIM * (1 - h)
                pieces = [jnp.zeros((spare, tq), F32)] if spare else []
                pieces += [bias_t, jnp.zeros((LANES - spare - nb, tq), F32)]
                qaug[dst_slot, t, h] = jnp.where(hm, q, jnp.concatenate(pieces, axis=0).T.astype(BF16))

    @pl.when(s == 0)
    def _():
        for h in range(2):
            spare = HEAD_DIM * (1 - h)
            for j in range(nb):
                onehot = jnp.where(lane == spare + j, 1.0, 0.0).astype(BF16)
                kaug[h, j * tq:(j + 1) * tq, :] = jnp.where(
                    _head_mask((tq, LANES), h), k_ref[j * tq:(j + 1) * tq, :], onehot)
                vaug[h, j * tq:(j + 1) * tq, :] = jnp.where(
                    _head_mask((tq, LANES), h), v_ref[j * tq:(j + 1) * tq, :], jnp.ones((tq, LANES), BF16))
        select(0, 0)

    for t in range(2):
        for h in range(2):
            m_scr[t, h] = jnp.full((tq, LANES), NEG, F32)
            acc_scr[t, h] = jnp.zeros((tq, LANES), F32)

    row = lax.broadcasted_iota(jnp.int32, (tq, tq), 0)
    col = lax.broadcasted_iota(jnp.int32, (tq, tq), 1)
    causal_bias = jnp.where(col <= row, 0.0, NEG).astype(F32)
    starts = [pl.multiple_of(s * tq, tq), pl.multiple_of((nb - 1 - s) * tq, tq)]

    for it in range(nb + 1):
        if it < 2:
            t, kstart = it, starts[it]
        else:
            first = it - 2 < s
            t = jnp.where(first, 0, 1)
            kstart = pl.multiple_of(jnp.where(first, it - 2, it - 2 - s) * tq, tq)
        for h in range(2):
            sc = _dot_nt(qaug[slot, t, h], kaug[h, pl.ds(kstart, tq), :])
            if it < 2:
                sc = sc + causal_bias
            m_old = m_scr[t, h]
            m_new = jnp.maximum(m_old, jnp.max(sc, axis=1, keepdims=True))
            p = jnp.exp(sc - jnp.concatenate([m_new, m_new], axis=1))
            acc_scr[t, h] = (jnp.exp(m_old - m_new) * acc_scr[t, h]
                             + jnp.dot(p.astype(BF16), vaug[h, pl.ds(kstart, tq), :], preferred_element_type=F32))
            m_scr[t, h] = m_new

    first_head = _head_mask((tq, LANES), 0)
    for t in range(2):
        acc = jnp.where(first_head, acc_scr[t, 0], acc_scr[t, 1])
        den = jnp.where(first_head, pltpu.roll(acc_scr[t, 0], HEAD_DIM, 1), pltpu.roll(acc_scr[t, 1], HEAD_DIM, 1))
        o_ref[pl.ds(starts[t], tq), :] = (acc / den).astype(o_ref.dtype)

    select(jnp.minimum(s + 1, nsteps - 1), 1 - slot)


def _moba_attention(qm3, km3, vm3, kmh, kml):
    batch, seq, w = qm3.shape
    nb = seq // MOBA_BLOCK
    assert nb % 2 == 0 and nb <= HEAD_DIM
    npair = w // LANES
    tq = MOBA_BLOCK
    seq_spec = pl.BlockSpec((None, seq, LANES), lambda b, p, i: (b, 0, p))
    km_spec = pl.BlockSpec((None, nb, LANES), lambda b, p, i: (b, 0, p))
    state = pltpu.VMEM((2, 2, tq, LANES), F32)
    return pl.pallas_call(
        _moba_kernel,
        grid=(batch, npair, nb // 2),
        in_specs=[seq_spec, seq_spec, seq_spec, km_spec, km_spec],
        out_specs=seq_spec,
        out_shape=jax.ShapeDtypeStruct((batch, seq, w), BF16),
        scratch_shapes=[pltpu.VMEM((2, 2, 2, tq, LANES), BF16), pltpu.VMEM((2, seq, LANES), BF16),
                        pltpu.VMEM((2, seq, LANES), BF16), state, state],
        compiler_params=_cparams("parallel", "parallel", "arbitrary"),
        name="moba_attn",
    )(qm3, km3, vm3, kmh, kml)


def _memkv_kernel(mem_ref, w_ref, k_ref, v_ref):
    kv = jnp.dot(mem_ref[...].astype(BF16), w_ref[...], preferred_element_type=F32)
    k_ref[...] = kv[:, :W_MEM].astype(BF16)
    v_ref[...] = kv[:, W_MEM:].astype(BF16)


def _mem_kv(mem, w_kv_b):
    batch, m, d = mem.shape
    return pl.pallas_call(
        _memkv_kernel,
        grid=(batch,),
        in_specs=[pl.BlockSpec((None, m, d), lambda b: (b, 0, 0)),
                  pl.BlockSpec((d, 2 * W_MEM), lambda b: (0, 0))],
        out_specs=[pl.BlockSpec((None, m, W_MEM), lambda b: (b, 0, 0))] * 2,
        out_shape=[jax.ShapeDtypeStruct((batch, m, W_MEM), BF16)] * 2,
        compiler_params=_cparams("parallel"),
        name="mem_kv_proj",
    )(mem, w_kv_b)


def _memattn_kernel(q_ref, k_ref, v_ref, o_ref):
    tq, w = q_ref.shape
    for pair in range(w // LANES):
        lanes = slice(pair * LANES, (pair + 1) * LANES)
        k = k_ref[:, lanes]
        v = v_ref[:, lanes]
        for c in range(tq // MEM_CHUNK):
            rows = slice(c * MEM_CHUNK, (c + 1) * MEM_CHUNK)
            q = q_ref[rows, lanes]
            outs = []
            for h in range(2):
                qh = jnp.where(_head_mask(q.shape, h), q, jnp.zeros_like(q))
                s = _dot_nt(qh, k)
                m = jnp.max(s, axis=1, keepdims=True)
                p = jnp.exp(s - m)
                l = jnp.sum(p, axis=1, keepdims=True)
                outs.append(jnp.dot(p.astype(BF16), v, preferred_element_type=F32) / l)
            o_ref[rows, lanes] = jnp.where(_head_mask(outs[0].shape, 0), outs[0], outs[1]).astype(o_ref.dtype)


def _mem_attention(qx3, k_mem, v_mem, tq=2048):
    batch, seq, w = qx3.shape
    m = k_mem.shape[1]
    tq = min(tq, seq)
    return pl.pallas_call(
        _memattn_kernel,
        grid=(batch, seq // tq),
        in_specs=[pl.BlockSpec((None, tq, w), lambda b, i: (b, i, 0)),
                  pl.BlockSpec((None, m, w), lambda b, i: (b, 0, 0)),
                  pl.BlockSpec((None, m, w), lambda b, i: (b, 0, 0))],
        out_specs=pl.BlockSpec((None, tq, w), lambda b, i: (b, i, 0)),
        out_shape=jax.ShapeDtypeStruct((batch, seq, w), BF16),
        compiler_params=_cparams("parallel", "parallel"),
        name="mem_attn",
    )(qx3, k_mem, v_mem)


def _merge_kernel(alpha, x_ref, o1_ref, o2_ref, o3_ref, l1_ref, l2_ref, l3_ref, ym_ref, yx_ref,
                  wg_ref, bg_ref, wbd_ref, wbm_ref, wbx_ref, wo_ref, g_ref, b_ref, wrh_ref, wrl_ref, rb_ref,
                  out_ref, packed_ref, eidx_ref, gw_ref, rank_ref, cnt_ref, stage, carry, x1_prev):
    step = pl.program_id(0)
    route_refs = (wrh_ref, wrl_ref, rb_ref, eidx_ref, gw_ref, rank_ref, cnt_ref, carry)

    @pl.when(step == 0)
    def _():
        carry[...] = jnp.zeros_like(carry)
        x1_prev[...] = jnp.zeros_like(x1_prev)

    route = _route_tile(x1_prev[...], step > 0, *route_refs)
    next(route)
    x = x_ref[...]
    xb = x.astype(BF16)
    tm, d = x.shape

    def token_major(ref, slot):
        dilation = ref.shape[0]
        if dilation == 1:
            return ref[0].astype(F32)
        for r in range(dilation):
            stage[slot, pl.ds(r, tm // dilation, stride=dilation), :] = ref[r].astype(F32)
        return stage[slot]

    o1, o2, o3 = (token_major(r, s) for s, r in enumerate((o1_ref, o2_ref, o3_ref)))
    l1, l2, l3 = (token_major(r, s + 3) for s, r in enumerate((l1_ref, l2_ref, l3_ref)))
    mx = jnp.maximum(jnp.maximum(l1, l2), l3)
    e1, e2, e3 = jnp.exp(l1 - mx), jnp.exp(l2 - mx), jnp.exp(l3 - mx)
    y_dil = (e1 * o1 + e2 * o2 + e3 * o3) / (e1 + e2 + e3)
    next(route)
    branches = (
        jnp.dot(y_dil.astype(BF16), wbd_ref[...], preferred_element_type=F32),
        jnp.dot(ym_ref[...], wbm_ref[...], preferred_element_type=F32),
        jnp.dot(yx_ref[...], wbx_ref[...], preferred_element_type=F32),
    )
    merged = jnp.zeros_like(x)
    for i, br in enumerate(branches):
        next(route)
        logits = jnp.dot(xb, wg_ref[:, i * d:(i + 1) * d], preferred_element_type=F32) + bg_ref[:, i * d:(i + 1) * d]
        merged = merged + jax.nn.sigmoid(logits) * br
    next(route)
    mix = jnp.dot(merged.astype(BF16), wo_ref[...], preferred_element_type=F32)
    for _ in route:
        pass
    x1 = _layer_norm(alpha * x + mix, g_ref[...], b_ref[...])
    out_ref[...] = x1
    packed_ref[...] = _pack_halves(x1)
    x1_prev[...] = x1


def _merge(alpha, x2, o_dil, lse_dil, y_moba, y_mem, wg, bg, wbd, wbm, wbx, wo, g, b, wr_hi_t, wr_lo_t, rbias_b,
           seq, tm=512):
    n, d = x2.shape
    tiles = seq // tm
    last = n // tm - 1
    cur = lambda i: jnp.minimum(i, last)
    row = lambda w: pl.BlockSpec((tm, w), lambda i: (cur(i), 0))
    full = lambda a: pl.BlockSpec(a.shape, lambda i: (0, 0))
    residue_major = lambda a: pl.BlockSpec((None, a.shape[1], tm // a.shape[1], LANES),
                                           lambda i: (cur(i) // tiles, 0, cur(i) % tiles, 0))
    slots = pl.BlockSpec((TOP_K, tm), lambda i: (0, jnp.maximum(i - 1, 0)))
    weights = (wg, bg, wbd, wbm, wbx, wo, g, b, wr_hi_t, wr_lo_t, rbias_b)
    tok = lambda dt: jax.ShapeDtypeStruct((TOP_K, n), dt)
    return pl.pallas_call(
        functools.partial(_merge_kernel, alpha),
        grid=(n // tm + 1,),
        in_specs=([row(d)] + [residue_major(a) for a in (*o_dil, *lse_dil)] + [row(W_MOBA), row(W_MEM)]
                  + [full(a) for a in weights]),
        out_specs=[row(d), row(PACKED), slots, slots, slots, pl.BlockSpec((N_EXPERTS, LANES), lambda i: (0, 0))],
        out_shape=[jax.ShapeDtypeStruct((n, d), F32), jax.ShapeDtypeStruct((n, PACKED), jnp.uint32),
                   tok(jnp.int32), tok(F32), tok(jnp.int32), jax.ShapeDtypeStruct((N_EXPERTS, LANES), jnp.int32)],
        scratch_shapes=[pltpu.VMEM((2 * len(DIL_PAIRS), tm, LANES), F32), pltpu.VMEM((N_EXPERTS, LANES), F32),
                        pltpu.VMEM((tm, d), F32)],
        compiler_params=_cparams("arbitrary"),
        name="merge_outproj_ln1_route",
    )(x2, *o_dil, *lse_dil, y_moba, y_mem, *weights)


def _token_mixer(x2, mem, positions, w_in, w_mem_kv, w_gate, b_gate, w_br_dil, w_br_moba, w_br_mem,
                 w_out, ln1_g, ln1_b, w_router, router_bias, batch, seq, alpha):
    wr_t = w_router.T
    wr_hi = wr_t.astype(BF16)
    wr_lo = (wr_t - wr_hi.astype(F32)).astype(BF16)
    rbias_b = jnp.broadcast_to(router_bias.astype(F32)[:, None], (N_EXPERTS, LANES))
    cos_t, sin_t = _rope_tables(positions)
    (q_dil, k_dil, v_dil), (qm, km, vm, qx) = _in_projection(x2, w_in.astype(BF16), cos_t, sin_t, batch, seq)
    o_dil, lse_dil = [], []
    for g in range(len(DIL_PAIRS)):
        o, lse = _dilated_attention(q_dil[g], k_dil[g], v_dil[g])
        o_dil.append(o)
        lse_dil.append(lse)
    qm3, km3, vm3 = (t.reshape(batch, seq, W_MOBA) for t in (qm, km, vm))
    kmh, kml = _moba_kmean(km3)
    y_moba = _moba_attention(qm3, km3, vm3, kmh, kml).reshape(batch * seq, W_MOBA)
    k_mem, v_mem = _mem_kv(mem, w_mem_kv.astype(BF16))
    y_mem = _mem_attention(qx.reshape(batch, seq, W_MEM), k_mem, v_mem).reshape(batch * seq, W_MEM)
    return _merge(alpha, x2, o_dil, lse_dil, y_moba, y_mem,
                  w_gate.astype(BF16), b_gate.reshape(1, -1), w_br_dil.astype(BF16),
                  w_br_moba.astype(BF16), w_br_mem.astype(BF16), w_out.astype(BF16),
                  ln1_g.reshape(1, -1), ln1_b.reshape(1, -1), wr_hi, wr_lo, rbias_b, seq)


EXPERT_ROWS = 256
EXPERT_GROUP = 4
EXPERT_RING = 2 * EXPERT_GROUP + 1
TOKEN_TILE = 512
PACKED = 512
SC_WINDOW = 128


def _first_index_of_max(v, iota_f, size):
    m = jnp.max(v, axis=0, keepdims=True)
    idx = jnp.min(jnp.where(v == m, iota_f, float(size)), axis=0, keepdims=True)
    return m, idx


def _route_tile(x, counted, wh_ref, wl_ref, bias_ref, eidx_ref, gw_ref, rank_ref, cnt_ref, carry):
    tm = x.shape[0]
    xh = x.astype(BF16)
    xl = (x - xh.astype(F32)).astype(BF16)
    wh = wh_ref[...]
    logits = _dot_nt(wh, xh) + _dot_nt(wh, xl) + _dot_nt(wl_ref[...], xh)
    yield
    scores = jax.nn.sigmoid(logits)
    biased = scores + bias_ref[...][:, :1]

    giota = lax.broadcasted_iota(jnp.int32, (GROUP_SIZE, tm), 0).astype(F32)
    group_scores = []
    for g in range(N_GROUPS):
        slab = biased[g * GROUP_SIZE:(g + 1) * GROUP_SIZE, :]
        m1, i1 = _first_index_of_max(slab, giota, GROUP_SIZE)
        m2 = jnp.max(jnp.where(giota == i1, -jnp.inf, slab), axis=0, keepdims=True)
        group_scores.append(m1 + m2)
    gs = jnp.concatenate(group_scores, axis=0)
    gidx = lax.broadcasted_iota(jnp.int32, (N_GROUPS, tm), 0)
    beaten = jnp.zeros((N_GROUPS, tm), jnp.int32)
    for gp in range(N_GROUPS):
        row = gs[gp:gp + 1, :]
        beaten = beaten + jnp.where((row > gs) | ((row == gs) & (gidx > gp)), 1, 0)
    keep = beaten < TOPK_GROUPS

    ahead = jnp.zeros((1, tm), jnp.int32)
    slab_b = [jnp.full((GROUP_SIZE, tm), -jnp.inf, F32) for _ in range(TOPK_GROUPS)]
    slab_s = [jnp.zeros((GROUP_SIZE, tm), F32) for _ in range(TOPK_GROUPS)]
    slab_g = [jnp.zeros((1, tm), F32) for _ in range(TOPK_GROUPS)]
    place = {}
    for g in range(N_GROUPS):
        kept = keep[g:g + 1, :]
        rows = slice(g * GROUP_SIZE, (g + 1) * GROUP_SIZE)
        for j in range(min(g, TOPK_GROUPS - 1) + 1):
            here = kept & (ahead == j)
            place[j, g] = here
            slab_b[j] = jnp.where(here, biased[rows, :], slab_b[j])
            slab_s[j] = jnp.where(here, scores[rows, :], slab_s[j])
            slab_g[j] = jnp.where(here, float(g), slab_g[j])
        ahead = ahead + jnp.where(kept, 1, 0)
    masked = jnp.concatenate(slab_b, axis=0)
    kept_scores = jnp.concatenate(slab_s, axis=0)
    yield

    n_kept = TOPK_GROUPS * GROUP_SIZE
    ciota = lax.broadcasted_iota(jnp.int32, (n_kept, tm), 0).astype(F32)
    chosen = jnp.zeros((n_kept, tm), F32)
    cidx_rows, idx_rows, gw_rows = [], [], []
    for k in range(TOP_K):
        _, cidx = _first_index_of_max(masked, ciota, n_kept)
        hit = ciota == cidx
        gw_rows.append(jnp.sum(jnp.where(hit, kept_scores, 0.0), axis=0, keepdims=True))
        masked = jnp.where(hit, -jnp.inf, masked)
        chosen = jnp.where(hit, 1.0, chosen)
        slab = sum(jnp.where(cidx >= float(j * GROUP_SIZE), 1.0, 0.0) for j in range(1, TOPK_GROUPS))
        group = sum(jnp.where(slab == float(j), slab_g[j], 0.0) for j in range(TOPK_GROUPS))
        cidx_rows.append(cidx)
        idx_rows.append((group - slab) * GROUP_SIZE + cidx)
        if k % 2 == 1:
            yield
    idx8 = jnp.concatenate(idx_rows, axis=0)
    gw8 = jnp.concatenate(gw_rows, axis=0)
    gw_ref[...] = gw8 / jnp.sum(gw8, axis=0, keepdims=True) * ROUTED_SCALE
    eidx_ref[...] = idx8.astype(jnp.int32)

    def slabs_to_groups(compact):
        out = []
        for g in range(N_GROUPS):
            acc = jnp.zeros((GROUP_SIZE, tm), F32)
            for j in range(min(g, TOPK_GROUPS - 1) + 1):
                acc = jnp.where(place[j, g], compact[j * GROUP_SIZE:(j + 1) * GROUP_SIZE, :], acc)
            out.append(acc)
        return jnp.concatenate(out, axis=0)

    def groups_to_slabs(full):
        out = []
        for j in range(TOPK_GROUPS):
            acc = jnp.zeros((GROUP_SIZE, tm), F32)
            for g in range(j, N_GROUPS):
                acc = jnp.where(place[j, g], full[g * GROUP_SIZE:(g + 1) * GROUP_SIZE, :], acc)
            out.append(acc)
        return jnp.concatenate(out, axis=0)

    onehot = slabs_to_groups(chosen)
    t_row = lax.broadcasted_iota(jnp.int32, (tm, tm), 0)
    t_col = lax.broadcasted_iota(jnp.int32, (tm, tm), 1)
    earlier = jnp.where(t_row < t_col, 1.0, 0.0).astype(BF16)
    prefix = jnp.dot(onehot.astype(BF16), earlier, preferred_element_type=F32)
    base = carry[...]
    prefix = groups_to_slabs(prefix + jnp.concatenate([base] * (tm // LANES), axis=1))
    rank_rows = [jnp.sum(jnp.where(ciota == cidx, prefix, 0.0), axis=0, keepdims=True) for cidx in cidx_rows]
    rank_ref[...] = jnp.concatenate(rank_rows, axis=0).astype(jnp.int32)
    total = base + jnp.where(counted, jnp.sum(onehot, axis=1, keepdims=True), 0.0)
    carry[...] = total
    cnt_ref[...] = total.astype(jnp.int32)


def _dest_kernel(eidx_ref, rank_ref, start_ref, dest_ref):
    eidx = eidx_ref[...]
    tm = eidx.shape[1]
    eiota = lax.broadcasted_iota(jnp.int32, (N_EXPERTS, tm), 0)
    start = start_ref[...][:, :1]
    rows = [jnp.sum(jnp.where(eiota == eidx[k:k + 1, :], start, 0.0), axis=0, keepdims=True)
            for k in range(TOP_K)]
    dest_ref[...] = jnp.concatenate(rows, axis=0).astype(jnp.int32) + rank_ref[...]


def _dest_rows(eidx, rank, start_b):
    n = eidx.shape[1]
    nt = n // TOKEN_TILE
    return pl.pallas_call(
        _dest_kernel,
        grid=(nt,),
        in_specs=[pl.BlockSpec((TOP_K, TOKEN_TILE), lambda i: (0, i)),
                  pl.BlockSpec((TOP_K, TOKEN_TILE), lambda i: (0, i)),
                  pl.BlockSpec((N_EXPERTS, LANES), lambda i: (0, 0))],
        out_specs=pl.BlockSpec((TOP_K, TOKEN_TILE), lambda i: (0, i)),
        out_shape=jax.ShapeDtypeStruct((TOP_K, n), jnp.int32),
        compiler_params=_cparams("parallel"),
        name="moe_dest_rows",
    )(eidx, rank, start_b)


def _pack_halves(v):
    return pltpu.pack_elementwise([v[:, :PACKED], v[:, PACKED:]], packed_dtype=BF16)


def _unpack_half(p, index):
    return pltpu.unpack_elementwise(p, index=index, packed_dtype=BF16, unpacked_dtype=F32)


def _sc_mesh():
    return plsc.VectorSubcoreMesh(core_axis_name="core", subcore_axis_name="subcore")


def _sc_move_rows(src, src_idx, dst_idx, out_rows, name):
    width = src.shape[1]
    count = src_idx.shape[1]

    @functools.partial(pl.kernel, out_type=jax.ShapeDtypeStruct((out_rows, width), src.dtype),
                       mesh=_sc_mesh(), scratch_types=[pltpu.VMEM((SC_WINDOW, width), src.dtype)], name=name)
    def move(src_hbm, sidx_hbm, didx_hbm, out_hbm, rows_vmem):
        def body(sidx_vmem, didx_vmem):
            pltpu.sync_copy(src_hbm.at[sidx_vmem.at[0]], rows_vmem)
            pltpu.sync_copy(rows_vmem, out_hbm.at[didx_vmem.at[0]])

        idx_spec = pl.BlockSpec((1, SC_WINDOW), index_map=lambda i: (0, i))
        pltpu.emit_pipeline(
            body,
            grid=(count // SC_WINDOW,),
            in_specs=[idx_spec, idx_spec],
            out_specs=[],
            core_axis_name=("core", "subcore"),
            dimension_semantics=(pltpu.PARALLEL,),
        )(sidx_hbm, didx_hbm)

    return move(src, src_idx, dst_idx)


def _sc_dispatch_rows(src, token_idx, dest, out_rows):
    n, width = src.shape
    slots = dest.shape[0]

    @functools.partial(pl.kernel, out_type=jax.ShapeDtypeStruct((out_rows, width), src.dtype),
                       mesh=_sc_mesh(), scratch_types=[pltpu.VMEM((SC_WINDOW, width), src.dtype)],
                       name="moe_dispatch_sc")
    def dispatch(src_hbm, tidx_hbm, dest_hbm, out_hbm, rows_vmem):
        def body(tidx_vmem, dest_vmem):
            pltpu.sync_copy(src_hbm.at[tidx_vmem.at[0]], rows_vmem)
            for k in range(slots):
                pltpu.sync_copy(rows_vmem, out_hbm.at[dest_vmem.at[k]])

        pltpu.emit_pipeline(
            body,
            grid=(n // SC_WINDOW,),
            in_specs=[pl.BlockSpec((1, SC_WINDOW), index_map=lambda i: (0, i)),
                      pl.BlockSpec((slots, SC_WINDOW), index_map=lambda i: (0, i))],
            out_specs=[],
            core_axis_name=("core", "subcore"),
            dimension_semantics=(pltpu.PARALLEL,),
        )(tidx_hbm, dest_hbm)

    return dispatch(src, token_idx, dest)


def _expert_kernel(first_blk_ref, count_ref, total_ref, xs_hbm, wg_ref, wu_ref, wd_ref, ys_hbm,
                   wg_b, wu_b, wd_b, x_ring, y_ring, in_sem, out_sem):
    e = pl.program_id(0)
    total = total_ref[0]
    first = first_blk_ref[e]
    count = count_ref[e]
    nblk = (count + EXPERT_ROWS - 1) // EXPERT_ROWS
    spare_block = ys_hbm.shape[0] // EXPERT_ROWS - EXPERT_RING

    def rows_of(g):
        return pl.ds(pl.multiple_of(g * EXPERT_ROWS, EXPERT_ROWS), EXPERT_ROWS)

    def in_copy(g):
        slot = g % EXPERT_RING
        return pltpu.make_async_copy(xs_hbm.at[rows_of(g)], x_ring.at[slot], in_sem.at[slot])

    def out_copy(g, slot):
        return pltpu.make_async_copy(y_ring.at[slot], ys_hbm.at[rows_of(g)], out_sem.at[slot])

    @pl.when(e == 0)
    def _():
        y_ring[...] = jnp.zeros_like(y_ring)
        for g in range(EXPERT_RING - 1):
            in_copy(g).start()
        for slot in range(EXPERT_RING):
            out_copy(spare_block + slot, slot).start()

    @pl.when(nblk > 0)
    def _():
        wg_b[...] = wg_ref[...].astype(BF16)
        wu_b[...] = wu_ref[...].astype(BF16)
        wd_b[...] = wd_ref[...].astype(BF16)

    def blocks(js):
        gs = [first + j for j in js]
        for g in gs:
            in_copy(g).wait()
        ys = []
        for j, g in zip(js, gs):
            p = x_ring[g % EXPERT_RING]
            live = lax.broadcasted_iota(jnp.int32, p.shape, 0) + j * EXPERT_ROWS < count
            lo = jnp.where(live, _unpack_half(p, 0), 0.0).astype(BF16)
            hi = jnp.where(live, _unpack_half(p, 1), 0.0).astype(BF16)

            def up(w, lo=lo, hi=hi):
                return (jnp.dot(lo, w[:PACKED, :], preferred_element_type=F32)
                        + jnp.dot(hi, w[PACKED:, :], preferred_element_type=F32))

            hid = (jax.nn.silu(up(wg_b)) * up(wu_b)).astype(BF16)
            ys.append(_pack_halves(jnp.dot(hid, wd_b[...], preferred_element_type=F32)))
        for g in gs:
            in_copy(g + EXPERT_RING - 1).start()
        for g in gs:
            out_copy(g, g % EXPERT_RING).wait()
        for g, y in zip(gs, ys):
            y_ring[g % EXPERT_RING] = y
        for g in gs:
            out_copy(g, g % EXPERT_RING).start()

    def group(i, carry):
        blocks([EXPERT_GROUP * i + b for b in range(EXPERT_GROUP)])
        return carry

    lax.fori_loop(0, nblk // EXPERT_GROUP, group, 0)
    left = nblk % EXPERT_GROUP

    @pl.when(left >= 2)
    def _():
        blocks([nblk - left, nblk - left + 1])

    @pl.when(left % 2 == 1)
    def _():
        blocks([nblk - 1])

    @pl.when(e == pl.num_programs(0) - 1)
    def _():
        for slot in range(EXPERT_RING):
            out_copy(spare_block + slot, slot).wait()
        for g in range(EXPERT_RING - 1):
            in_copy(total + g).wait()


def _expert_ffn(xs, first_blk, counts, total, w_e_gate, w_e_up, w_e_down):
    rows = xs.shape[0] + EXPERT_ROWS
    n_experts, d = w_e_gate.shape[0], w_e_gate.shape[1]
    w_map = lambda e, fb, ct, tt: (e, 0, 0)
    ring = pltpu.VMEM((EXPERT_RING, EXPERT_ROWS, PACKED), jnp.uint32)
    return pl.pallas_call(
        _expert_kernel,
        grid_spec=pltpu.PrefetchScalarGridSpec(
            num_scalar_prefetch=3,
            grid=(n_experts,),
            in_specs=[pl.BlockSpec(memory_space=pl.ANY),
                      pl.BlockSpec((None, d, D_EXPERT), w_map),
                      pl.BlockSpec((None, d, D_EXPERT), w_map),
                      pl.BlockSpec((None, D_EXPERT, d), w_map)],
            out_specs=pl.BlockSpec(memory_space=pl.ANY),
            scratch_shapes=[pltpu.VMEM((d, D_EXPERT), BF16), pltpu.VMEM((d, D_EXPERT), BF16),
                            pltpu.VMEM((D_EXPERT, d), BF16), ring, ring,
                            pltpu.SemaphoreType.DMA((EXPERT_RING,)), pltpu.SemaphoreType.DMA((EXPERT_RING,))],
        ),
        out_shape=jax.ShapeDtypeStruct((rows, PACKED), jnp.uint32),
        compiler_params=_cparams("arbitrary"),
        name="moe_expert_ffn",
    )(first_blk, counts, total, xs, w_e_gate, w_e_up, w_e_down)


def _combine_kernel(alpha, x_ref, gw_ref, ys_ref, wsg_ref, wsu_ref, wsd_ref, g_ref, b_ref, out_ref):
    x = x_ref[...]
    xb = x.astype(BF16)
    hid = (jax.nn.silu(jnp.dot(xb, wsg_ref[...], preferred_element_type=F32))
           * jnp.dot(xb, wsu_ref[...], preferred_element_type=F32)).astype(BF16)
    shared = jnp.dot(hid, wsd_ref[...], preferred_element_type=F32)
    gw = gw_ref[...]
    lo = jnp.zeros((TOKEN_TILE, PACKED), F32)
    hi = jnp.zeros((TOKEN_TILE, PACKED), F32)
    for k in range(TOP_K):
        p = ys_ref[k]
        w = gw[:, k:k + 1]
        lo = lo + w * _unpack_half(p, 0)
        hi = hi + w * _unpack_half(p, 1)
    routed = jnp.concatenate([lo, hi], axis=1)
    out_ref[...] = _layer_norm(alpha * x + (routed + shared), g_ref[...], b_ref[...])


def _combine(alpha, x1, gw_t, ys_tok, wsg, wsu, wsd, g, b):
    n, d = x1.shape
    full = lambda a: pl.BlockSpec(a.shape, lambda i: (0, 0))
    weights = (wsg, wsu, wsd, g, b)
    return pl.pallas_call(
        functools.partial(_combine_kernel, alpha),
        grid=(n // TOKEN_TILE,),
        in_specs=[pl.BlockSpec((TOKEN_TILE, d), lambda i: (i, 0)),
                  pl.BlockSpec((TOKEN_TILE, TOP_K), lambda i: (i, 0)),
                  pl.BlockSpec((TOP_K, TOKEN_TILE, PACKED), lambda i: (0, i, 0))] + [full(a) for a in weights],
        out_specs=pl.BlockSpec((TOKEN_TILE, d), lambda i: (i, 0)),
        out_shape=jax.ShapeDtypeStruct((n, d), F32),
        compiler_params=_cparams("parallel"),
        name="moe_combine_shared_ln2",
    )(x1, gw_t, ys_tok, *weights)


def _moe_layer(x1, x1_packed, eidx, gw, rank, cnt, w_e_gate, w_e_up, w_e_down, w_s_gate, w_s_up, w_s_down,
               ln2_g, ln2_b, alpha):
    n, d = x1.shape

    counts = cnt[:, 0]
    padded = (counts + EXPERT_ROWS - 1) // EXPERT_ROWS * EXPERT_ROWS
    seg_end = jnp.cumsum(padded)
    seg_start = seg_end - padded
    rows = n * TOP_K + N_EXPERTS * EXPERT_ROWS
    first_blk = (seg_start // EXPERT_ROWS).astype(jnp.int32)
    total_blk = (seg_end[-1:] // EXPERT_ROWS).astype(jnp.int32)
    start_b = jnp.broadcast_to(seg_start.astype(F32)[:, None], (N_EXPERTS, LANES))

    dest = _dest_rows(eidx, rank, start_b)
    dest_flat = dest.reshape(1, TOP_K * n)
    assign = jnp.arange(TOP_K * n, dtype=jnp.int32).reshape(1, TOP_K * n)
    xs = _sc_dispatch_rows(x1_packed, assign[:, :n], dest, rows + (EXPERT_RING - 1) * EXPERT_ROWS)
    ys = _expert_ffn(xs, first_blk, counts, total_blk, w_e_gate, w_e_up, w_e_down)
    ys_tok = _sc_move_rows(ys, dest_flat, assign, TOP_K * n, "moe_gather_sc").reshape(TOP_K, n, PACKED)
    return _combine(alpha, x1, gw.T, ys_tok, w_s_gate.astype(BF16), w_s_up.astype(BF16),
                    w_s_down.astype(BF16), ln2_g.reshape(1, -1), ln2_b.reshape(1, -1))


def kernel(x, mem, positions, w_in, w_mem_kv, w_gate, b_gate, w_br_dil, w_br_moba, w_br_mem, w_out, ln1_g, ln1_b, w_router, router_bias, w_e_gate, w_e_up, w_e_down, w_s_gate, w_s_up, w_s_down, ln2_g, ln2_b):
    batch, seq, d = x.shape
    depth = w_in.shape[0]
    alpha = (2.0 * depth) ** 0.25
    h = x.reshape(batch * seq, d)
    for l in range(depth):
        h, h_packed, eidx, gw, rank, cnt = _token_mixer(
            h, mem, positions, w_in[l], w_mem_kv[l], w_gate[l], b_gate[l], w_br_dil[l], w_br_moba[l], w_br_mem[l],
            w_out[l], ln1_g[l], ln1_b[l], w_router[l], router_bias[l], batch, seq, alpha)
        h = _moe_layer(h, h_packed, eidx, gw, rank, cnt, w_e_gate[l], w_e_up[l], w_e_down[l],
                       w_s_gate[l], w_s_up[l], w_s_down[l], ln2_g[l], ln2_b[l], alpha)
    return h.reshape(batch, seq, d)
```

```python
import functools

import jax
import jax.numpy as jnp
from jax import lax
from jax.experimental import pallas as pl
from jax.experimental.pallas import tpu as pltpu
from jax.experimental.pallas import tpu_sc as plsc

F32 = jnp.float32
BF16 = jnp.bfloat16

LANES = 128
VMEM_LIMIT_BYTES = 48 * 1024 * 1024

HEAD_DIM = 64
ROPE_DIM = HEAD_DIM // 4
ROPE_HALF = ROPE_DIM // 2
ROPE_THETA = 500000.0
DIL_PAIRS = ((128, 1), (512, 4), (2048, 16))
BAND = 128
DIL_TILE = 1024
MEM_CHUNK = 512
W_DIL = 384
W_MOBA = 384
W_MEM = 256
MOBA_BLOCK = 256
MOBA_TOPK = 3
N_EXPERTS = 256
N_GROUPS = 8
GROUP_SIZE = N_EXPERTS // N_GROUPS
TOPK_GROUPS = 4
TOP_K = 8
D_EXPERT = 256
ROUTED_SCALE = 2.5
LN_EPS = 1e-5
NEG = -1e30
QK_SCALE = HEAD_DIM ** -0.5


def _cparams(*sem):
    return pltpu.CompilerParams(dimension_semantics=sem, vmem_limit_bytes=VMEM_LIMIT_BYTES)


def _dot_nt(a, b):
    return lax.dot_general(a, b, (((1,), (1,)), ((), ())), preferred_element_type=F32)


def _layer_norm(v, g, b):
    mu = jnp.mean(v, axis=-1, keepdims=True)
    c = v - mu
    var = jnp.mean(c * c, axis=-1, keepdims=True)
    return c * lax.rsqrt(var + LN_EPS) * g + b


_IN_SECTIONS = (
    (0, W_DIL, True, True),
    (W_DIL, W_DIL, True, False),
    (2 * W_DIL, W_DIL, False, False),
    (3 * W_DIL, W_MOBA, True, True),
    (3 * W_DIL + W_MOBA, W_MOBA, True, False),
    (3 * W_DIL + 2 * W_MOBA, W_MOBA, False, False),
    (3 * W_DIL + 3 * W_MOBA, W_MEM, False, True),
)


N_DIL_SECTIONS = 3
MATMUL_COLS = 512


def _inproj_kernel(x_ref, w_ref, cos_ref, sin_ref, *refs):
    n_dil = N_DIL_SECTIONS * len(DIL_PAIRS)
    dil_refs, flat_refs, stage = refs[:n_dil], refs[n_dil:-1], refs[-1]
    xb = x_ref[...].astype(BF16)
    cos = cos_ref[...]
    sin = sin_ref[...]
    tm = xb.shape[0]
    lane = lax.broadcasted_iota(jnp.int32, (tm, LANES), 1)
    first_half = (lane % ROPE_DIM) < ROPE_HALF

    def rope(t):
        partner = jnp.where(first_half, pltpu.roll(t, LANES - ROPE_HALF, 1), pltpu.roll(t, ROPE_HALF, 1))
        return t * cos + partner * sin

    chunk_of = {}
    for sec, (off, width, _, _) in enumerate(_IN_SECTIONS):
        for c in range(width // LANES):
            chunk_of[off // LANES + c] = (sec, c)
    per_group = MATMUL_COLS // LANES
    for grp in range(w_ref.shape[1] // MATMUL_COLS):
        acc = jnp.dot(xb, w_ref[:, grp * MATMUL_COLS:(grp + 1) * MATMUL_COLS], preferred_element_type=F32)
        for cc in range(per_group):
            sec, c = chunk_of[grp * per_group + cc]
            _, _, roped, scaled = _IN_SECTIONS[sec]
            t = acc[:, cc * LANES:(cc + 1) * LANES]
            if roped:
                t = rope(t)
            if scaled:
                t = t * QK_SCALE
            if sec >= N_DIL_SECTIONS:
                flat_refs[sec - N_DIL_SECTIONS][:, c * LANES:(c + 1) * LANES] = t.astype(BF16)
                continue
            o_ref = dil_refs[sec * len(DIL_PAIRS) + c]
            dilation = DIL_PAIRS[c][1]
            if dilation == 1:
                o_ref[0] = t.astype(BF16)
                continue
            slot = sec * len(DIL_PAIRS) + c
            stage[slot] = t
            for r in range(dilation):
                o_ref[r] = stage[slot, pl.ds(r, tm // dilation, stride=dilation), :].astype(BF16)


def _in_projection(x2, w_in_b, cos_t, sin_t, batch, seq, tm=1024):
    n, d = x2.shape
    w_total = w_in_b.shape[1]
    tiles = seq // tm
    dil_specs, dil_shapes = [], []
    for _ in range(N_DIL_SECTIONS):
        for _, dilation in DIL_PAIRS:
            dil_specs.append(pl.BlockSpec((None, dilation, tm // dilation, LANES),
                                          lambda i: (i // tiles, 0, i % tiles, 0)))
            dil_shapes.append(jax.ShapeDtypeStruct((batch, dilation, seq // dilation, LANES), BF16))
    widths = [s[1] for s in _IN_SECTIONS[N_DIL_SECTIONS:]]
    outs = pl.pallas_call(
        _inproj_kernel,
        grid=(n // tm,),
        in_specs=[
            pl.BlockSpec((tm, d), lambda i: (i, 0)),
            pl.BlockSpec((d, w_total), lambda i: (0, 0)),
            pl.BlockSpec((tm, LANES), lambda i: (i, 0)),
            pl.BlockSpec((tm, LANES), lambda i: (i, 0)),
        ],
        out_specs=dil_specs + [pl.BlockSpec((tm, w), lambda i: (i, 0)) for w in widths],
        out_shape=dil_shapes + [jax.ShapeDtypeStruct((n, w), BF16) for w in widths],
        scratch_shapes=[pltpu.VMEM((N_DIL_SECTIONS * len(DIL_PAIRS), tm, LANES), F32)],
        compiler_params=_cparams("parallel"),
        name="in_proj_rope",
    )(x2, w_in_b, cos_t, sin_t)
    n_dil = len(dil_specs)
    ng = len(DIL_PAIRS)
    qkv_dil = [outs[s * ng:(s + 1) * ng] for s in range(N_DIL_SECTIONS)]
    return qkv_dil, outs[n_dil:]


def _rope_tables(positions):
    lane = jnp.arange(LANES)
    rotary = (lane % HEAD_DIM) < ROPE_DIM
    inv_freq = ROPE_THETA ** (-(lane % ROPE_HALF).astype(F32) / ROPE_HALF)
    ang = positions.reshape(-1).astype(F32)[:, None] * jnp.where(rotary, inv_freq, 0.0)
    sign = jnp.where((lane % ROPE_DIM) < ROPE_HALF, -1.0, 1.0)
    return jnp.cos(ang), jnp.sin(ang) * sign


def _head_mask(shape, h):
    lane = lax.broadcasted_iota(jnp.int32, shape, 1)
    return (lane // HEAD_DIM) == h


def _dil_kernel(q_ref, kp_ref, k_ref, vp_ref, v_ref, o_ref, lse_ref):
    i = pl.program_id(2)
    n_res, tq = q_ref.shape[0], q_ref.shape[1]
    qi = lax.broadcasted_iota(jnp.int32, (BAND, 2 * BAND), 0)
    kj = lax.broadcasted_iota(jnp.int32, (BAND, 2 * BAND), 1)
    dist = qi + BAND - kj
    band = (dist >= 0) & (dist <= BAND)
    first = _head_mask((BAND, LANES), 0)
    for r in range(n_res):
        for j in range(tq // BAND):
            rows = slice(j * BAND, (j + 1) * BAND)
            q = q_ref[r, rows, :]
            if j == 0:
                k_prev, v_prev = kp_ref[r], vp_ref[r]
                allowed = band & ((kj >= BAND) | (i > 0))
            else:
                prev_rows = slice((j - 1) * BAND, j * BAND)
                k_prev, v_prev = k_ref[r, prev_rows, :], v_ref[r, prev_rows, :]
                allowed = band
            k = jnp.concatenate([k_prev, k_ref[r, rows, :]], axis=0)
            v = jnp.concatenate([v_prev, v_ref[r, rows, :]], axis=0)
            outs, lses = [], []
            for h in range(2):
                qh = jnp.where(_head_mask(q.shape, h), q, jnp.zeros_like(q))
                s = jnp.where(allowed, _dot_nt(qh, k), NEG)
                m = jnp.max(s, axis=1, keepdims=True)
                p = jnp.exp(s - m)
                l = jnp.sum(p, axis=1, keepdims=True)
                outs.append(jnp.dot(p.astype(BF16), v, preferred_element_type=F32) / l)
                lses.append(jnp.broadcast_to(m + jnp.log(l), (BAND, LANES)))
            o_ref[r, rows, :] = jnp.where(first, outs[0], outs[1]).astype(o_ref.dtype)
            lse_ref[r, rows, :] = jnp.where(first, lses[0], lses[1])


def _dilated_attention(q4, k4, v4):
    batch, dilation, steps, _ = q4.shape
    tq = min(steps, DIL_TILE)
    n_res = min(dilation, DIL_TILE // tq)
    per_tile = tq // BAND
    cur = pl.BlockSpec((None, n_res, tq, LANES), lambda b, r, i: (b, r, i, 0))
    prev = pl.BlockSpec((None, n_res, BAND, LANES), lambda b, r, i: (b, r, jnp.maximum(i * per_tile - 1, 0), 0))
    return pl.pallas_call(
        _dil_kernel,
        grid=(batch, dilation // n_res, steps // tq),
        in_specs=[cur, prev, cur, prev, cur],
        out_specs=[cur, cur],
        out_shape=[jax.ShapeDtypeStruct(q4.shape, BF16), jax.ShapeDtypeStruct(q4.shape, F32)],
        compiler_params=_cparams("parallel", "parallel", "arbitrary"),
        name=f"dilated_attn_d{dilation}",
    )(q4, k4, k4, v4, v4)


def _kmean_kernel(k_ref, hi_ref, lo_ref):
    k = k_ref[...].astype(F32)
    s, w = k.shape
    mean = jnp.sum(k.reshape(s // MOBA_BLOCK, MOBA_BLOCK, w), axis=1) / MOBA_BLOCK
    hi = mean.astype(BF16)
    hi_ref[...] = hi
    lo_ref[...] = (mean - hi.astype(F32)).astype(BF16)


def _moba_kmean(km3):
    batch, seq, w = km3.shape
    nb = seq // MOBA_BLOCK
    return pl.pallas_call(
        _kmean_kernel,
        grid=(batch,),
        in_specs=[pl.BlockSpec((None, seq, w), lambda b: (b, 0, 0))],
        out_specs=[pl.BlockSpec((None, nb, w), lambda b: (b, 0, 0))] * 2,
        out_shape=[jax.ShapeDtypeStruct((batch, nb, w), BF16)] * 2,
        compiler_params=_cparams("parallel"),
        name="moba_kmean",
    )(km3)


def _moba_kernel(q_ref, k_ref, v_ref, kmh_ref, kml_ref, o_ref, qaug, kaug, vaug, m_scr, acc_scr):
    s = pl.program_id(2)
    nsteps = pl.num_programs(2)
    nb = kmh_ref.shape[0]
    tq = MOBA_BLOCK
    lane = lax.broadcasted_iota(jnp.int32, (tq, LANES), 1)
    blk = lax.broadcasted_iota(jnp.int32, (nb, tq), 0)
    slot = s % 2

    def select(step, dst_slot):
        for t in range(2):
            qi = step if t == 0 else nb - 1 - step
            q = q_ref[pl.ds(pl.multiple_of(qi * tq, tq), tq), :]
            for h in range(2):
                hm = _head_mask(q.shape, h)
                qh = jnp.where(hm, q, jnp.zeros_like(q))
                gate = _dot_nt(kmh_ref[...], qh) + _dot_nt(kml_ref[...], qh)
                cnt = jnp.zeros((nb, tq), jnp.int32)
                for jp in range(nb):
                    g_jp = gate[jp:jp + 1, :]
                    beats = (g_jp > gate) | ((g_jp == gate) & (blk > jp))
                    cnt = cnt + jnp.where(beats & (qi > jp), 1, 0)
                sel = ((blk < qi) & (cnt < MOBA_TOPK)) | (blk == qi)
                bias_t = jnp.where(sel, 0.0, NEG).astype(F32)
                spare = HEAD_DIM * (1 - h)
                pieces = [jnp.zeros((spare, tq), F32)] if spare else []
                pieces += [bias_t, jnp.zeros((LANES - spare - nb, tq), F32)]
                qaug[dst_slot, t, h] = jnp.where(hm, q, jnp.concatenate(pieces, axis=0).T.astype(BF16))

    @pl.when(s == 0)
    def _():
        for h in range(2):
            spare = HEAD_DIM * (1 - h)
            for j in range(nb):
                onehot = jnp.where(lane == spare + j, 1.0, 0.0).astype(BF16)
                kaug[h, j * tq:(j + 1) * tq, :] = jnp.where(
                    _head_mask((tq, LANES), h), k_ref[j * tq:(j + 1) * tq, :], onehot)
                vaug[h, j * tq:(j + 1) * tq, :] = jnp.where(
                    _head_mask((tq, LANES), h), v_ref[j * tq:(j + 1) * tq, :], jnp.ones((tq, LANES), BF16))
        select(0, 0)

    for t in range(2):
        for h in range(2):
            m_scr[t, h] = jnp.full((tq, LANES), NEG, F32)
            acc_scr[t, h] = jnp.zeros((tq, LANES), F32)

    row = lax.broadcasted_iota(jnp.int32, (tq, tq), 0)
    col = lax.broadcasted_iota(jnp.int32, (tq, tq), 1)
    causal_bias = jnp.where(col <= row, 0.0, NEG).astype(F32)
    starts = [pl.multiple_of(s * tq, tq), pl.multiple_of((nb - 1 - s) * tq, tq)]

    half = nb // 2
    for it in range(nb + 1):
        if it < 2:
            t, kstart = it, starts[it]
        elif it < 2 + half:
            t, kstart = 1, (it - 2) * tq
        else:
            i = it - 2 - half
            first = i < s
            t = jnp.where(first, 0, 1)
            kstart = pl.multiple_of(jnp.where(first, i, half + i - s) * tq, tq)
        for h in range(2):
            sc = _dot_nt(qaug[slot, t, h], kaug[h, pl.ds(kstart, tq), :])
            if it < 2:
                sc = sc + causal_bias
            m_old = m_scr[t, h]
            m_new = jnp.maximum(m_old, jnp.max(sc, axis=1, keepdims=True))
            p = jnp.exp(sc - jnp.concatenate([m_new, m_new], axis=1))
            acc_scr[t, h] = (jnp.exp(m_old - m_new) * acc_scr[t, h]
                             + jnp.dot(p.astype(BF16), vaug[h, pl.ds(kstart, tq), :], preferred_element_type=F32))
            m_scr[t, h] = m_new

    first_head = _head_mask((tq, LANES), 0)
    for t in range(2):
        acc = jnp.where(first_head, acc_scr[t, 0], acc_scr[t, 1])
        den = jnp.where(first_head, pltpu.roll(acc_scr[t, 0], HEAD_DIM, 1), pltpu.roll(acc_scr[t, 1], HEAD_DIM, 1))
        o_ref[pl.ds(starts[t], tq), :] = (acc / den).astype(o_ref.dtype)

    select(jnp.minimum(s + 1, nsteps - 1), 1 - slot)


def _moba_attention(qm3, km3, vm3, kmh, kml):
    batch, seq, w = qm3.shape
    nb = seq // MOBA_BLOCK
    assert nb % 2 == 0 and nb <= HEAD_DIM
    npair = w // LANES
    tq = MOBA_BLOCK
    seq_spec = pl.BlockSpec((None, seq, LANES), lambda b, p, i: (b, 0, p))
    km_spec = pl.BlockSpec((None, nb, LANES), lambda b, p, i: (b, 0, p))
    state = pltpu.VMEM((2, 2, tq, LANES), F32)
    return pl.pallas_call(
        _moba_kernel,
        grid=(batch, npair, nb // 2),
        in_specs=[seq_spec, seq_spec, seq_spec, km_spec, km_spec],
        out_specs=seq_spec,
        out_shape=jax.ShapeDtypeStruct((batch, seq, w), BF16),
        scratch_shapes=[pltpu.VMEM((2, 2, 2, tq, LANES), BF16), pltpu.VMEM((2, seq, LANES), BF16),
                        pltpu.VMEM((2, seq, LANES), BF16), state, state],
        compiler_params=_cparams("parallel", "parallel", "arbitrary"),
        name="moba_attn",
    )(qm3, km3, vm3, kmh, kml)


def _memkv_kernel(mem_ref, w_ref, k_ref, v_ref):
    kv = jnp.dot(mem_ref[...].astype(BF16), w_ref[...], preferred_element_type=F32)
    k_ref[...] = kv[:, :W_MEM].astype(BF16)
    v_ref[...] = kv[:, W_MEM:].astype(BF16)


def _mem_kv(mem, w_kv_b):
    batch, m, d = mem.shape
    return pl.pallas_call(
        _memkv_kernel,
        grid=(batch,),
        in_specs=[pl.BlockSpec((None, m, d), lambda b: (b, 0, 0)),
                  pl.BlockSpec((d, 2 * W_MEM), lambda b: (0, 0))],
        out_specs=[pl.BlockSpec((None, m, W_MEM), lambda b: (b, 0, 0))] * 2,
        out_shape=[jax.ShapeDtypeStruct((batch, m, W_MEM), BF16)] * 2,
        compiler_params=_cparams("parallel"),
        name="mem_kv_proj",
    )(mem, w_kv_b)


def _memattn_kernel(q_ref, k_ref, v_ref, o_ref):
    tq, w = q_ref.shape
    for pair in range(w // LANES):
        lanes = slice(pair * LANES, (pair + 1) * LANES)
        k = k_ref[:, lanes]
        v = v_ref[:, lanes]
        for c in range(tq // MEM_CHUNK):
            rows = slice(c * MEM_CHUNK, (c + 1) * MEM_CHUNK)
            q = q_ref[rows, lanes]
            outs = []
            for h in range(2):
                qh = jnp.where(_head_mask(q.shape, h), q, jnp.zeros_like(q))
                s = _dot_nt(qh, k)
                m = jnp.max(s, axis=1, keepdims=True)
                p = jnp.exp(s - m)
                l = jnp.sum(p, axis=1, keepdims=True)
                outs.append(jnp.dot(p.astype(BF16), v, preferred_element_type=F32) / l)
            o_ref[rows, lanes] = jnp.where(_head_mask(outs[0].shape, 0), outs[0], outs[1]).astype(o_ref.dtype)


def _mem_attention(qx3, k_mem, v_mem, tq=2048):
    batch, seq, w = qx3.shape
    m = k_mem.shape[1]
    tq = min(tq, seq)
    return pl.pallas_call(
        _memattn_kernel,
        grid=(batch, seq // tq),
        in_specs=[pl.BlockSpec((None, tq, w), lambda b, i: (b, i, 0)),
                  pl.BlockSpec((None, m, w), lambda b, i: (b, 0, 0)),
                  pl.BlockSpec((None, m, w), lambda b, i: (b, 0, 0))],
        out_specs=pl.BlockSpec((None, tq, w), lambda b, i: (b, i, 0)),
        out_shape=jax.ShapeDtypeStruct((batch, seq, w), BF16),
        compiler_params=_cparams("parallel", "parallel"),
        name="mem_attn",
    )(qx3, k_mem, v_mem)


def _merge_kernel(alpha, x_ref, o1_ref, o2_ref, o3_ref, l1_ref, l2_ref, l3_ref, ym_ref, yx_ref,
                  wg_ref, bg_ref, wbd_ref, wbm_ref, wbx_ref, wo_ref, g_ref, b_ref, wrh_ref, wrl_ref, rb_ref,
                  out_ref, packed_ref, eidx_ref, gw_ref, rank_ref, cnt_ref, stage, carry, x1_prev):
    step = pl.program_id(0)
    route_refs = (wrh_ref, wrl_ref, rb_ref, eidx_ref, gw_ref, rank_ref, cnt_ref, carry)

    @pl.when(step == 0)
    def _():
        carry[...] = jnp.zeros_like(carry)
        x1_prev[...] = jnp.zeros_like(x1_prev)

    route = _route_tile(x1_prev[...], step > 0, *route_refs)
    next(route)
    x = x_ref[...]
    xb = x.astype(BF16)
    tm, d = x.shape

    def token_major(ref, slot):
        dilation = ref.shape[0]
        if dilation == 1:
            return ref[0].astype(F32)
        for r in range(dilation):
            stage[slot, pl.ds(r, tm // dilation, stride=dilation), :] = ref[r].astype(F32)
        return stage[slot]

    o1, o2, o3 = (token_major(r, s) for s, r in enumerate((o1_ref, o2_ref, o3_ref)))
    l1, l2, l3 = (token_major(r, s + 3) for s, r in enumerate((l1_ref, l2_ref, l3_ref)))
    mx = jnp.maximum(jnp.maximum(l1, l2), l3)
    e1, e2, e3 = jnp.exp(l1 - mx), jnp.exp(l2 - mx), jnp.exp(l3 - mx)
    y_dil = (e1 * o1 + e2 * o2 + e3 * o3) / (e1 + e2 + e3)
    next(route)
    branches = (
        jnp.dot(y_dil.astype(BF16), wbd_ref[...], preferred_element_type=F32),
        jnp.dot(ym_ref[...], wbm_ref[...], preferred_element_type=F32),
        jnp.dot(yx_ref[...], wbx_ref[...], preferred_element_type=F32),
    )
    merged = jnp.zeros_like(x)
    for i, br in enumerate(branches):
        next(route)
        logits = jnp.dot(xb, wg_ref[:, i * d:(i + 1) * d], preferred_element_type=F32) + bg_ref[:, i * d:(i + 1) * d]
        merged = merged + jax.nn.sigmoid(logits) * br
    next(route)
    mix = jnp.dot(merged.astype(BF16), wo_ref[...], preferred_element_type=F32)
    for _ in route:
        pass
    x1 = _layer_norm(alpha * x + mix, g_ref[...], b_ref[...])
    out_ref[...] = x1
    packed_ref[...] = _pack_halves(x1)
    x1_prev[...] = x1


def _merge(alpha, x2, o_dil, lse_dil, y_moba, y_mem, wg, bg, wbd, wbm, wbx, wo, g, b, wr_hi_t, wr_lo_t, rbias_b,
           seq, tm=512):
    n, d = x2.shape
    tiles = seq // tm
    last = n // tm - 1
    cur = lambda i: jnp.minimum(i, last)
    row = lambda w: pl.BlockSpec((tm, w), lambda i: (cur(i), 0))
    full = lambda a: pl.BlockSpec(a.shape, lambda i: (0, 0))
    residue_major = lambda a: pl.BlockSpec((None, a.shape[1], tm // a.shape[1], LANES),
                                           lambda i: (cur(i) // tiles, 0, cur(i) % tiles, 0))
    slots = pl.BlockSpec((TOP_K, tm), lambda i: (0, jnp.maximum(i - 1, 0)))
    weights = (wg, bg, wbd, wbm, wbx, wo, g, b, wr_hi_t, wr_lo_t, rbias_b)
    tok = lambda dt: jax.ShapeDtypeStruct((TOP_K, n), dt)
    return pl.pallas_call(
        functools.partial(_merge_kernel, alpha),
        grid=(n // tm + 1,),
        in_specs=([row(d)] + [residue_major(a) for a in (*o_dil, *lse_dil)] + [row(W_MOBA), row(W_MEM)]
                  + [full(a) for a in weights]),
        out_specs=[row(d), row(PACKED), slots, slots, slots, pl.BlockSpec((N_EXPERTS, LANES), lambda i: (0, 0))],
        out_shape=[jax.ShapeDtypeStruct((n, d), F32), jax.ShapeDtypeStruct((n, PACKED), jnp.uint32),
                   tok(jnp.int32), tok(F32), tok(jnp.int32), jax.ShapeDtypeStruct((N_EXPERTS, LANES), jnp.int32)],
        scratch_shapes=[pltpu.VMEM((2 * len(DIL_PAIRS), tm, LANES), F32), pltpu.VMEM((N_EXPERTS, LANES), F32),
                        pltpu.VMEM((tm, d), F32)],
        compiler_params=_cparams("arbitrary"),
        name="merge_outproj_ln1_route",
    )(x2, *o_dil, *lse_dil, y_moba, y_mem, *weights)


def _token_mixer(x2, mem, positions, w_in, w_mem_kv, w_gate, b_gate, w_br_dil, w_br_moba, w_br_mem,
                 w_out, ln1_g, ln1_b, w_router, router_bias, batch, seq, alpha):
    wr_t = w_router.T
    wr_hi = wr_t.astype(BF16)
    wr_lo = (wr_t - wr_hi.astype(F32)).astype(BF16)
    rbias_b = jnp.broadcast_to(router_bias.astype(F32)[:, None], (N_EXPERTS, LANES))
    cos_t, sin_t = _rope_tables(positions)
    (q_dil, k_dil, v_dil), (qm, km, vm, qx) = _in_projection(x2, w_in.astype(BF16), cos_t, sin_t, batch, seq)
    o_dil, lse_dil = [], []
    for g in range(len(DIL_PAIRS)):
        o, lse = _dilated_attention(q_dil[g], k_dil[g], v_dil[g])
        o_dil.append(o)
        lse_dil.append(lse)
    qm3, km3, vm3 = (t.reshape(batch, seq, W_MOBA) for t in (qm, km, vm))
    kmh, kml = _moba_kmean(km3)
    y_moba = _moba_attention(qm3, km3, vm3, kmh, kml).reshape(batch * seq, W_MOBA)
    k_mem, v_mem = _mem_kv(mem, w_mem_kv.astype(BF16))
    y_mem = _mem_attention(qx.reshape(batch, seq, W_MEM), k_mem, v_mem).reshape(batch * seq, W_MEM)
    return _merge(alpha, x2, o_dil, lse_dil, y_moba, y_mem,
                  w_gate.astype(BF16), b_gate.reshape(1, -1), w_br_dil.astype(BF16),
                  w_br_moba.astype(BF16), w_br_mem.astype(BF16), w_out.astype(BF16),
                  ln1_g.reshape(1, -1), ln1_b.reshape(1, -1), wr_hi, wr_lo, rbias_b, seq)


EXPERT_ROWS = 256
EXPERT_GROUP = 4
EXPERT_RING = 2 * EXPERT_GROUP + 1
TOKEN_TILE = 512
PACKED = 512
SC_WINDOW = 128


def _first_index_of_max(v, iota_f, size):
    m = jnp.max(v, axis=0, keepdims=True)
    idx = jnp.min(jnp.where(v == m, iota_f, float(size)), axis=0, keepdims=True)
    return m, idx


def _route_tile(x, counted, wh_ref, wl_ref, bias_ref, eidx_ref, gw_ref, rank_ref, cnt_ref, carry):
    tm = x.shape[0]
    xh = x.astype(BF16)
    xl = (x - xh.astype(F32)).astype(BF16)
    wh = wh_ref[...]
    logits = _dot_nt(wh, xh) + _dot_nt(wh, xl) + _dot_nt(wl_ref[...], xh)
    yield
    scores = jax.nn.sigmoid(logits)
    biased = scores + bias_ref[...][:, :1]

    giota = lax.broadcasted_iota(jnp.int32, (GROUP_SIZE, tm), 0).astype(F32)
    group_scores = []
    for g in range(N_GROUPS):
        slab = biased[g * GROUP_SIZE:(g + 1) * GROUP_SIZE, :]
        m1, i1 = _first_index_of_max(slab, giota, GROUP_SIZE)
        m2 = jnp.max(jnp.where(giota == i1, -jnp.inf, slab), axis=0, keepdims=True)
        group_scores.append(m1 + m2)
    gs = jnp.concatenate(group_scores, axis=0)
    gidx = lax.broadcasted_iota(jnp.int32, (N_GROUPS, tm), 0)
    beaten = jnp.zeros((N_GROUPS, tm), jnp.int32)
    for gp in range(N_GROUPS):
        row = gs[gp:gp + 1, :]
        beaten = beaten + jnp.where((row > gs) | ((row == gs) & (gidx > gp)), 1, 0)
    keep = beaten < TOPK_GROUPS

    ahead = jnp.zeros((1, tm), jnp.int32)
    slab_b = [jnp.full((GROUP_SIZE, tm), -jnp.inf, F32) for _ in range(TOPK_GROUPS)]
    slab_s = [jnp.zeros((GROUP_SIZE, tm), F32) for _ in range(TOPK_GROUPS)]
    slab_g = [jnp.zeros((1, tm), F32) for _ in range(TOPK_GROUPS)]
    place = {}
    for g in range(N_GROUPS):
        kept = keep[g:g + 1, :]
        rows = slice(g * GROUP_SIZE, (g + 1) * GROUP_SIZE)
        for j in range(min(g, TOPK_GROUPS - 1) + 1):
            here = kept & (ahead == j)
            place[j, g] = here
            slab_b[j] = jnp.where(here, biased[rows, :], slab_b[j])
            slab_s[j] = jnp.where(here, scores[rows, :], slab_s[j])
            slab_g[j] = jnp.where(here, float(g), slab_g[j])
        ahead = ahead + jnp.where(kept, 1, 0)
    masked = jnp.concatenate(slab_b, axis=0)
    kept_scores = jnp.concatenate(slab_s, axis=0)
    yield

    n_kept = TOPK_GROUPS * GROUP_SIZE
    ciota = lax.broadcasted_iota(jnp.int32, (n_kept, tm), 0).astype(F32)
    chosen = jnp.zeros((n_kept, tm), F32)
    cidx_rows, idx_rows, gw_rows = [], [], []
    for k in range(TOP_K):
        _, cidx = _first_index_of_max(masked, ciota, n_kept)
        hit = ciota == cidx
        gw_rows.append(jnp.sum(jnp.where(hit, kept_scores, 0.0), axis=0, keepdims=True))
        masked = jnp.where(hit, -jnp.inf, masked)
        chosen = jnp.where(hit, 1.0, chosen)
        slab = sum(jnp.where(cidx >= float(j * GROUP_SIZE), 1.0, 0.0) for j in range(1, TOPK_GROUPS))
        group = sum(jnp.where(slab == float(j), slab_g[j], 0.0) for j in range(TOPK_GROUPS))
        cidx_rows.append(cidx)
        idx_rows.append((group - slab) * GROUP_SIZE + cidx)
        if k % 2 == 1:
            yield
    idx8 = jnp.concatenate(idx_rows, axis=0)
    gw8 = jnp.concatenate(gw_rows, axis=0)
    gw_ref[...] = gw8 / jnp.sum(gw8, axis=0, keepdims=True) * ROUTED_SCALE
    eidx_ref[...] = idx8.astype(jnp.int32)

    def slabs_to_groups(compact):
        out = []
        for g in range(N_GROUPS):
            acc = jnp.zeros((GROUP_SIZE, tm), F32)
            for j in range(min(g, TOPK_GROUPS - 1) + 1):
                acc = jnp.where(place[j, g], compact[j * GROUP_SIZE:(j + 1) * GROUP_SIZE, :], acc)
            out.append(acc)
        return jnp.concatenate(out, axis=0)

    def groups_to_slabs(full):
        out = []
        for j in range(TOPK_GROUPS):
            acc = jnp.zeros((GROUP_SIZE, tm), F32)
            for g in range(j, N_GROUPS):
                acc = jnp.where(place[j, g], full[g * GROUP_SIZE:(g + 1) * GROUP_SIZE, :], acc)
            out.append(acc)
        return jnp.concatenate(out, axis=0)

    onehot = slabs_to_groups(chosen)
    t_row = lax.broadcasted_iota(jnp.int32, (tm, tm), 0)
    t_col = lax.broadcasted_iota(jnp.int32, (tm, tm), 1)
    earlier = jnp.where(t_row < t_col, 1.0, 0.0).astype(BF16)
    prefix = jnp.dot(onehot.astype(BF16), earlier, preferred_element_type=F32)
    base = carry[...]
    prefix = groups_to_slabs(prefix + jnp.concatenate([base] * (tm // LANES), axis=1))
    rank_rows = [jnp.sum(jnp.where(ciota == cidx, prefix, 0.0), axis=0, keepdims=True) for cidx in cidx_rows]
    rank_ref[...] = jnp.concatenate(rank_rows, axis=0).astype(jnp.int32)
    total = base + jnp.where(counted, jnp.sum(onehot, axis=1, keepdims=True), 0.0)
    carry[...] = total
    cnt_ref[...] = total.astype(jnp.int32)


def _dest_kernel(eidx_ref, rank_ref, start_ref, dest_ref):
    eidx = eidx_ref[...]
    tm = eidx.shape[1]
    eiota = lax.broadcasted_iota(jnp.int32, (N_EXPERTS, tm), 0)
    start = start_ref[...][:, :1]
    rows = [jnp.sum(jnp.where(eiota == eidx[k:k + 1, :], start, 0.0), axis=0, keepdims=True)
            for k in range(TOP_K)]
    dest_ref[...] = jnp.concatenate(rows, axis=0).astype(jnp.int32) + rank_ref[...]


def _dest_rows(eidx, rank, start_b):
    n = eidx.shape[1]
    nt = n // TOKEN_TILE
    return pl.pallas_call(
        _dest_kernel,
        grid=(nt,),
        in_specs=[pl.BlockSpec((TOP_K, TOKEN_TILE), lambda i: (0, i)),
                  pl.BlockSpec((TOP_K, TOKEN_TILE), lambda i: (0, i)),
                  pl.BlockSpec((N_EXPERTS, LANES), lambda i: (0, 0))],
        out_specs=pl.BlockSpec((TOP_K, TOKEN_TILE), lambda i: (0, i)),
        out_shape=jax.ShapeDtypeStruct((TOP_K, n), jnp.int32),
        compiler_params=_cparams("parallel"),
        name="moe_dest_rows",
    )(eidx, rank, start_b)


def _pack_halves(v):
    return pltpu.pack_elementwise([v[:, :PACKED], v[:, PACKED:]], packed_dtype=BF16)


def _unpack_half(p, index):
    return pltpu.unpack_elementwise(p, index=index, packed_dtype=BF16, unpacked_dtype=F32)


def _sc_mesh():
    return plsc.VectorSubcoreMesh(core_axis_name="core", subcore_axis_name="subcore")


def _sc_move_rows(src, src_idx, dst_idx, out_rows, name):
    width = src.shape[1]
    count = src_idx.shape[1]

    @functools.partial(pl.kernel, out_type=jax.ShapeDtypeStruct((out_rows, width), src.dtype),
                       mesh=_sc_mesh(), scratch_types=[pltpu.VMEM((SC_WINDOW, width), src.dtype)], name=name)
    def move(src_hbm, sidx_hbm, didx_hbm, out_hbm, rows_vmem):
        def body(sidx_vmem, didx_vmem):
            pltpu.sync_copy(src_hbm.at[sidx_vmem.at[0]], rows_vmem)
            pltpu.sync_copy(rows_vmem, out_hbm.at[didx_vmem.at[0]])

        idx_spec = pl.BlockSpec((1, SC_WINDOW), index_map=lambda i: (0, i))
        pltpu.emit_pipeline(
            body,
            grid=(count // SC_WINDOW,),
            in_specs=[idx_spec, idx_spec],
            out_specs=[],
            core_axis_name=("core", "subcore"),
            dimension_semantics=(pltpu.PARALLEL,),
        )(sidx_hbm, didx_hbm)

    return move(src, src_idx, dst_idx)


def _sc_dispatch_rows(src, token_idx, dest, out_rows):
    n, width = src.shape
    slots = dest.shape[0]

    @functools.partial(pl.kernel, out_type=jax.ShapeDtypeStruct((out_rows, width), src.dtype),
                       mesh=_sc_mesh(), scratch_types=[pltpu.VMEM((SC_WINDOW, width), src.dtype)],
                       name="moe_dispatch_sc")
    def dispatch(src_hbm, tidx_hbm, dest_hbm, out_hbm, rows_vmem):
        def body(tidx_vmem, dest_vmem):
            pltpu.sync_copy(src_hbm.at[tidx_vmem.at[0]], rows_vmem)
            for k in range(slots):
                pltpu.sync_copy(rows_vmem, out_hbm.at[dest_vmem.at[k]])

        pltpu.emit_pipeline(
            body,
            grid=(n // SC_WINDOW,),
            in_specs=[pl.BlockSpec((1, SC_WINDOW), index_map=lambda i: (0, i)),
                      pl.BlockSpec((slots, SC_WINDOW), index_map=lambda i: (0, i))],
            out_specs=[],
            core_axis_name=("core", "subcore"),
            dimension_semantics=(pltpu.PARALLEL,),
        )(tidx_hbm, dest_hbm)

    return dispatch(src, token_idx, dest)


def _expert_kernel(first_blk_ref, count_ref, total_ref, xs_hbm, wg_ref, wu_ref, wd_ref, ys_hbm,
                   wg_b, wu_b, wd_b, x_ring, y_ring, in_sem, out_sem):
    e = pl.program_id(0)
    total = total_ref[0]
    first = first_blk_ref[e]
    count = count_ref[e]
    nblk = (count + EXPERT_ROWS - 1) // EXPERT_ROWS
    spare_block = ys_hbm.shape[0] // EXPERT_ROWS - EXPERT_RING

    def rows_of(g):
        return pl.ds(pl.multiple_of(g * EXPERT_ROWS, EXPERT_ROWS), EXPERT_ROWS)

    def in_copy(g):
        slot = g % EXPERT_RING
        return pltpu.make_async_copy(xs_hbm.at[rows_of(g)], x_ring.at[slot], in_sem.at[slot])

    def out_copy(g, slot):
        return pltpu.make_async_copy(y_ring.at[slot], ys_hbm.at[rows_of(g)], out_sem.at[slot])

    @pl.when(e == 0)
    def _():
        y_ring[...] = jnp.zeros_like(y_ring)
        for g in range(EXPERT_RING - 1):
            in_copy(g).start()
        for slot in range(EXPERT_RING):
            out_copy(spare_block + slot, slot).start()

    @pl.when(nblk > 0)
    def _():
        wg_b[...] = wg_ref[...].astype(BF16)
        wu_b[...] = wu_ref[...].astype(BF16)
        wd_b[...] = wd_ref[...].astype(BF16)

    def blocks(js):
        gs = [first + j for j in js]
        for g in gs:
            in_copy(g).wait()
        ys = []
        for j, g in zip(js, gs):
            p = x_ring[g % EXPERT_RING]
            live = lax.broadcasted_iota(jnp.int32, p.shape, 0) + j * EXPERT_ROWS < count
            lo = jnp.where(live, _unpack_half(p, 0), 0.0).astype(BF16)
            hi = jnp.where(live, _unpack_half(p, 1), 0.0).astype(BF16)

            def up(w, lo=lo, hi=hi):
                return (jnp.dot(lo, w[:PACKED, :], preferred_element_type=F32)
                        + jnp.dot(hi, w[PACKED:, :], preferred_element_type=F32))

            hid = (jax.nn.silu(up(wg_b)) * up(wu_b)).astype(BF16)
            ys.append(_pack_halves(jnp.dot(hid, wd_b[...], preferred_element_type=F32)))
        for g in gs:
            in_copy(g + EXPERT_RING - 1).start()
        for g in gs:
            out_copy(g, g % EXPERT_RING).wait()
        for g, y in zip(gs, ys):
            y_ring[g % EXPERT_RING] = y
        for g in gs:
            out_copy(g, g % EXPERT_RING).start()

    def group(i, carry):
        blocks([EXPERT_GROUP * i + b for b in range(EXPERT_GROUP)])
        return carry

    lax.fori_loop(0, nblk // EXPERT_GROUP, group, 0)
    left = nblk % EXPERT_GROUP

    @pl.when(left >= 2)
    def _():
        blocks([nblk - left, nblk - left + 1])

    @pl.when(left % 2 == 1)
    def _():
        blocks([nblk - 1])

    @pl.when(e == pl.num_programs(0) - 1)
    def _():
        for slot in range(EXPERT_RING):
            out_copy(spare_block + slot, slot).wait()
        for g in range(EXPERT_RING - 1):
            in_copy(total + g).wait()


def _expert_ffn(xs, first_blk, counts, total, w_e_gate, w_e_up, w_e_down):
    rows = xs.shape[0] + EXPERT_ROWS
    n_experts, d = w_e_gate.shape[0], w_e_gate.shape[1]
    w_map = lambda e, fb, ct, tt: (e, 0, 0)
    ring = pltpu.VMEM((EXPERT_RING, EXPERT_ROWS, PACKED), jnp.uint32)
    return pl.pallas_call(
        _expert_kernel,
        grid_spec=pltpu.PrefetchScalarGridSpec(
            num_scalar_prefetch=3,
            grid=(n_experts,),
            in_specs=[pl.BlockSpec(memory_space=pl.ANY),
                      pl.BlockSpec((None, d, D_EXPERT), w_map),
                      pl.BlockSpec((None, d, D_EXPERT), w_map),
                      pl.BlockSpec((None, D_EXPERT, d), w_map)],
            out_specs=pl.BlockSpec(memory_space=pl.ANY),
            scratch_shapes=[pltpu.VMEM((d, D_EXPERT), BF16), pltpu.VMEM((d, D_EXPERT), BF16),
                            pltpu.VMEM((D_EXPERT, d), BF16), ring, ring,
                            pltpu.SemaphoreType.DMA((EXPERT_RING,)), pltpu.SemaphoreType.DMA((EXPERT_RING,))],
        ),
        out_shape=jax.ShapeDtypeStruct((rows, PACKED), jnp.uint32),
        compiler_params=_cparams("arbitrary"),
        name="moe_expert_ffn",
    )(first_blk, counts, total, xs, w_e_gate, w_e_up, w_e_down)


def _combine_kernel(alpha, x_ref, gw_ref, ys_ref, wsg_ref, wsu_ref, wsd_ref, g_ref, b_ref, out_ref):
    x = x_ref[...]
    xb = x.astype(BF16)
    hid = (jax.nn.silu(jnp.dot(xb, wsg_ref[...], preferred_element_type=F32))
           * jnp.dot(xb, wsu_ref[...], preferred_element_type=F32)).astype(BF16)
    shared = jnp.dot(hid, wsd_ref[...], preferred_element_type=F32)
    gw = gw_ref[...]
    lo = jnp.zeros((TOKEN_TILE, PACKED), F32)
    hi = jnp.zeros((TOKEN_TILE, PACKED), F32)
    for k in range(TOP_K):
        p = ys_ref[k]
        w = gw[:, k:k + 1]
        lo = lo + w * _unpack_half(p, 0)
        hi = hi + w * _unpack_half(p, 1)
    routed = jnp.concatenate([lo, hi], axis=1)
    out_ref[...] = _layer_norm(alpha * x + (routed + shared), g_ref[...], b_ref[...])


def _combine(alpha, x1, gw_t, ys_tok, wsg, wsu, wsd, g, b):
    n, d = x1.shape
    full = lambda a: pl.BlockSpec(a.shape, lambda i: (0, 0))
    weights = (wsg, wsu, wsd, g, b)
    return pl.pallas_call(
        functools.partial(_combine_kernel, alpha),
        grid=(n // TOKEN_TILE,),
        in_specs=[pl.BlockSpec((TOKEN_TILE, d), lambda i: (i, 0)),
                  pl.BlockSpec((TOKEN_TILE, TOP_K), lambda i: (i, 0)),
                  pl.BlockSpec((TOP_K, TOKEN_TILE, PACKED), lambda i: (0, i, 0))] + [full(a) for a in weights],
        out_specs=pl.BlockSpec((TOKEN_TILE, d), lambda i: (i, 0)),
        out_shape=jax.ShapeDtypeStruct((n, d), F32),
        compiler_params=_cparams("parallel"),
        name="moe_combine_shared_ln2",
    )(x1, gw_t, ys_tok, *weights)


def _moe_layer(x1, x1_packed, eidx, gw, rank, cnt, w_e_gate, w_e_up, w_e_down, w_s_gate, w_s_up, w_s_down,
               ln2_g, ln2_b, alpha):
    n, d = x1.shape

    counts = cnt[:, 0]
    padded = (counts + EXPERT_ROWS - 1) // EXPERT_ROWS * EXPERT_ROWS
    seg_end = jnp.cumsum(padded)
    seg_start = seg_end - padded
    rows = n * TOP_K + N_EXPERTS * EXPERT_ROWS
    first_blk = (seg_start // EXPERT_ROWS).astype(jnp.int32)
    total_blk = (seg_end[-1:] // EXPERT_ROWS).astype(jnp.int32)
    start_b = jnp.broadcast_to(seg_start.astype(F32)[:, None], (N_EXPERTS, LANES))

    dest = _dest_rows(eidx, rank, start_b)
    dest_flat = dest.reshape(1, TOP_K * n)
    assign = jnp.arange(TOP_K * n, dtype=jnp.int32).reshape(1, TOP_K * n)
    xs = _sc_dispatch_rows(x1_packed, assign[:, :n], dest, rows + (EXPERT_RING - 1) * EXPERT_ROWS)
    ys = _expert_ffn(xs, first_blk, counts, total_blk, w_e_gate, w_e_up, w_e_down)
    ys_tok = _sc_move_rows(ys, dest_flat, assign, TOP_K * n, "moe_gather_sc").reshape(TOP_K, n, PACKED)
    return _combine(alpha, x1, gw.T, ys_tok, w_s_gate.astype(BF16), w_s_up.astype(BF16),
                    w_s_down.astype(BF16), ln2_g.reshape(1, -1), ln2_b.reshape(1, -1))


def kernel(x, mem, positions, w_in, w_mem_kv, w_gate, b_gate, w_br_dil, w_br_moba, w_br_mem, w_out, ln1_g, ln1_b, w_router, router_bias, w_e_gate, w_e_up, w_e_down, w_s_gate, w_s_up, w_s_down, ln2_g, ln2_b):
    batch, seq, d = x.shape
    depth = w_in.shape[0]
    alpha = (2.0 * depth) ** 0.25
    h = x.reshape(batch * seq, d)
    for l in range(depth):
        h, h_packed, eidx, gw, rank, cnt = _token_mixer(
            h, mem, positions, w_in[l], w_mem_kv[l], w_gate[l], b_gate[l], w_br_dil[l], w_br_moba[l], w_br_mem[l],
            w_out[l], ln1_g[l], ln1_b[l], w_router[l], router_bias[l], batch, seq, alpha)
        h = _moe_layer(h, h_packed, eidx, gw, rank, cnt, w_e_gate[l], w_e_up[l], w_e_down[l],
                       w_s_gate[l], w_s_up[l], w_s_down[l], ln2_g[l], ln2_b[l], alpha)
    return h.reshape(batch, seq, d)
```

```python
import functools

import jax
import jax.numpy as jnp
from jax import lax
from jax.experimental import pallas as pl
from jax.experimental.pallas import tpu as pltpu
from jax.experimental.pallas import tpu_sc as plsc

F32 = jnp.float32
BF16 = jnp.bfloat16

LANES = 128
VMEM_LIMIT_BYTES = 48 * 1024 * 1024

HEAD_DIM = 64
ROPE_DIM = HEAD_DIM // 4
ROPE_HALF = ROPE_DIM // 2
ROPE_THETA = 500000.0
DIL_PAIRS = ((128, 1), (512, 4), (2048, 16))
BAND = 128
DIL_TILE = 1024
MEM_CHUNK = 512
W_DIL = 384
W_MOBA = 384
W_MEM = 256
MOBA_BLOCK = 256
MOBA_TOPK = 3
N_EXPERTS = 256
N_GROUPS = 8
GROUP_SIZE = N_EXPERTS // N_GROUPS
TOPK_GROUPS = 4
TOP_K = 8
D_EXPERT = 256
ROUTED_SCALE = 2.5
LN_EPS = 1e-5
NEG = -1e30
QK_SCALE = HEAD_DIM ** -0.5


def _cparams(*sem):
    return pltpu.CompilerParams(dimension_semantics=sem, vmem_limit_bytes=VMEM_LIMIT_BYTES)


def _dot_nt(a, b):
    return lax.dot_general(a, b, (((1,), (1,)), ((), ())), preferred_element_type=F32)


def _layer_norm(v, g, b):
    mu = jnp.mean(v, axis=-1, keepdims=True)
    c = v - mu
    var = jnp.mean(c * c, axis=-1, keepdims=True)
    return c * lax.rsqrt(var + LN_EPS) * g + b


_IN_SECTIONS = (
    (0, W_DIL, True, True),
    (W_DIL, W_DIL, True, False),
    (2 * W_DIL, W_DIL, False, False),
    (3 * W_DIL, W_MOBA, True, True),
    (3 * W_DIL + W_MOBA, W_MOBA, True, False),
    (3 * W_DIL + 2 * W_MOBA, W_MOBA, False, False),
    (3 * W_DIL + 3 * W_MOBA, W_MEM, False, True),
)


N_DIL_SECTIONS = 3
MATMUL_COLS = 512


def _inproj_kernel(x_ref, w_ref, cos_ref, sin_ref, *refs):
    n_dil = N_DIL_SECTIONS * len(DIL_PAIRS)
    dil_refs, flat_refs, stage = refs[:n_dil], refs[n_dil:-1], refs[-1]
    xb = x_ref[...].astype(BF16)
    cos = cos_ref[...]
    sin = sin_ref[...]
    tm = xb.shape[0]
    lane = lax.broadcasted_iota(jnp.int32, (tm, LANES), 1)
    first_half = (lane % ROPE_DIM) < ROPE_HALF

    def rope(t):
        partner = jnp.where(first_half, pltpu.roll(t, LANES - ROPE_HALF, 1), pltpu.roll(t, ROPE_HALF, 1))
        return t * cos + partner * sin

    chunk_of = {}
    for sec, (off, width, _, _) in enumerate(_IN_SECTIONS):
        for c in range(width // LANES):
            chunk_of[off // LANES + c] = (sec, c)
    per_group = MATMUL_COLS // LANES
    for grp in range(w_ref.shape[1] // MATMUL_COLS):
        acc = jnp.dot(xb, w_ref[:, grp * MATMUL_COLS:(grp + 1) * MATMUL_COLS], preferred_element_type=F32)
        for cc in range(per_group):
            sec, c = chunk_of[grp * per_group + cc]
            _, _, roped, scaled = _IN_SECTIONS[sec]
            t = acc[:, cc * LANES:(cc + 1) * LANES]
            if roped:
                t = rope(t)
            if scaled:
                t = t * QK_SCALE
            if sec >= N_DIL_SECTIONS:
                flat_refs[sec - N_DIL_SECTIONS][:, c * LANES:(c + 1) * LANES] = t.astype(BF16)
                continue
            o_ref = dil_refs[sec * len(DIL_PAIRS) + c]
            dilation = DIL_PAIRS[c][1]
            if dilation == 1:
                o_ref[0] = t.astype(BF16)
                continue
            slot = sec * len(DIL_PAIRS) + c
            stage[slot] = t
            for r in range(dilation):
                o_ref[r] = stage[slot, pl.ds(r, tm // dilation, stride=dilation), :].astype(BF16)


def _in_projection(x2, w_in_b, cos_t, sin_t, batch, seq, tm=1024):
    n, d = x2.shape
    w_total = w_in_b.shape[1]
    tiles = seq // tm
    dil_specs, dil_shapes = [], []
    for _ in range(N_DIL_SECTIONS):
        for _, dilation in DIL_PAIRS:
            dil_specs.append(pl.BlockSpec((None, dilation, tm // dilation, LANES),
                                          lambda i: (i // tiles, 0, i % tiles, 0)))
            dil_shapes.append(jax.ShapeDtypeStruct((batch, dilation, seq // dilation, LANES), BF16))
    widths = [s[1] for s in _IN_SECTIONS[N_DIL_SECTIONS:]]
    outs = pl.pallas_call(
        _inproj_kernel,
        grid=(n // tm,),
        in_specs=[
            pl.BlockSpec((tm, d), lambda i: (i, 0)),
            pl.BlockSpec((d, w_total), lambda i: (0, 0)),
            pl.BlockSpec((tm, LANES), lambda i: (i, 0)),
            pl.BlockSpec((tm, LANES), lambda i: (i, 0)),
        ],
        out_specs=dil_specs + [pl.BlockSpec((tm, w), lambda i: (i, 0)) for w in widths],
        out_shape=dil_shapes + [jax.ShapeDtypeStruct((n, w), BF16) for w in widths],
        scratch_shapes=[pltpu.VMEM((N_DIL_SECTIONS * len(DIL_PAIRS), tm, LANES), F32)],
        compiler_params=_cparams("parallel"),
        name="in_proj_rope",
    )(x2, w_in_b, cos_t, sin_t)
    n_dil = len(dil_specs)
    ng = len(DIL_PAIRS)
    qkv_dil = [outs[s * ng:(s + 1) * ng] for s in range(N_DIL_SECTIONS)]
    return qkv_dil, outs[n_dil:]


def _rope_tables(positions):
    lane = jnp.arange(LANES)
    rotary = (lane % HEAD_DIM) < ROPE_DIM
    inv_freq = ROPE_THETA ** (-(lane % ROPE_HALF).astype(F32) / ROPE_HALF)
    ang = positions.reshape(-1).astype(F32)[:, None] * jnp.where(rotary, inv_freq, 0.0)
    sign = jnp.where((lane % ROPE_DIM) < ROPE_HALF, -1.0, 1.0)
    return jnp.cos(ang), jnp.sin(ang) * sign


def _head_mask(shape, h):
    lane = lax.broadcasted_iota(jnp.int32, shape, 1)
    return (lane // HEAD_DIM) == h


def _dil_kernel(q_ref, kp_ref, k_ref, vp_ref, v_ref, o_ref, lse_ref):
    i = pl.program_id(2)
    n_res, tq = q_ref.shape[0], q_ref.shape[1]
    qi = lax.broadcasted_iota(jnp.int32, (BAND, 2 * BAND), 0)
    kj = lax.broadcasted_iota(jnp.int32, (BAND, 2 * BAND), 1)
    dist = qi + BAND - kj
    band = (dist >= 0) & (dist <= BAND)
    first = _head_mask((BAND, LANES), 0)
    for r in range(n_res):
        for j in range(tq // BAND):
            rows = slice(j * BAND, (j + 1) * BAND)
            q = q_ref[r, rows, :]
            if j == 0:
                k_prev, v_prev = kp_ref[r], vp_ref[r]
                allowed = band & ((kj >= BAND) | (i > 0))
            else:
                prev_rows = slice((j - 1) * BAND, j * BAND)
                k_prev, v_prev = k_ref[r, prev_rows, :], v_ref[r, prev_rows, :]
                allowed = band
            k = jnp.concatenate([k_prev, k_ref[r, rows, :]], axis=0)
            v = jnp.concatenate([v_prev, v_ref[r, rows, :]], axis=0)
            outs, lses = [], []
            for h in range(2):
                qh = jnp.where(_head_mask(q.shape, h), q, jnp.zeros_like(q))
                s = jnp.where(allowed, _dot_nt(qh, k), NEG)
                m = jnp.max(s, axis=1, keepdims=True)
                p = jnp.exp(s - m)
                l = jnp.sum(p, axis=1, keepdims=True)
                outs.append(jnp.dot(p.astype(BF16), v, preferred_element_type=F32) / l)
                lses.append(jnp.broadcast_to(m + jnp.log(l), (BAND, LANES)))
            o_ref[r, rows, :] = jnp.where(first, outs[0], outs[1]).astype(o_ref.dtype)
            lse_ref[r, rows, :] = jnp.where(first, lses[0], lses[1])


def _dilated_attention(q4, k4, v4):
    batch, dilation, steps, _ = q4.shape
    tq = min(steps, DIL_TILE)
    n_res = min(dilation, DIL_TILE // tq)
    per_tile = tq // BAND
    cur = pl.BlockSpec((None, n_res, tq, LANES), lambda b, r, i: (b, r, i, 0))
    prev = pl.BlockSpec((None, n_res, BAND, LANES), lambda b, r, i: (b, r, jnp.maximum(i * per_tile - 1, 0), 0))
    return pl.pallas_call(
        _dil_kernel,
        grid=(batch, dilation // n_res, steps // tq),
        in_specs=[cur, prev, cur, prev, cur],
        out_specs=[cur, cur],
        out_shape=[jax.ShapeDtypeStruct(q4.shape, BF16), jax.ShapeDtypeStruct(q4.shape, F32)],
        compiler_params=_cparams("parallel", "parallel", "arbitrary"),
        name=f"dilated_attn_d{dilation}",
    )(q4, k4, k4, v4, v4)


def _kmean_kernel(k_ref, hi_ref, lo_ref):
    k = k_ref[...].astype(F32)
    s, w = k.shape
    mean = jnp.sum(k.reshape(s // MOBA_BLOCK, MOBA_BLOCK, w), axis=1) / MOBA_BLOCK
    hi = mean.astype(BF16)
    hi_ref[...] = hi
    lo_ref[...] = (mean - hi.astype(F32)).astype(BF16)


def _moba_kmean(km3):
    batch, seq, w = km3.shape
    nb = seq // MOBA_BLOCK
    return pl.pallas_call(
        _kmean_kernel,
        grid=(batch,),
        in_specs=[pl.BlockSpec((None, seq, w), lambda b: (b, 0, 0))],
        out_specs=[pl.BlockSpec((None, nb, w), lambda b: (b, 0, 0))] * 2,
        out_shape=[jax.ShapeDtypeStruct((batch, nb, w), BF16)] * 2,
        compiler_params=_cparams("parallel"),
        name="moba_kmean",
    )(km3)


def _moba_kernel(q_ref, k_ref, v_ref, kmh_ref, kml_ref, o_ref, qaug, kaug, vaug, m_scr, acc_scr):
    s = pl.program_id(2)
    nsteps = pl.num_programs(2)
    nb = kmh_ref.shape[0]
    tq = MOBA_BLOCK
    lane = lax.broadcasted_iota(jnp.int32, (tq, LANES), 1)
    blk = lax.broadcasted_iota(jnp.int32, (nb, tq), 0)
    slot = s % 2

    def select(step, dst_slot):
        for t in range(2):
            qi = step if t == 0 else nb - 1 - step
            q = q_ref[pl.ds(pl.multiple_of(qi * tq, tq), tq), :]
            for h in range(2):
                hm = _head_mask(q.shape, h)
                qh = jnp.where(hm, q, jnp.zeros_like(q))
                gate = _dot_nt(kmh_ref[...], qh) + _dot_nt(kml_ref[...], qh)
                cnt = jnp.zeros((nb, tq), jnp.int32)
                for jp in range(nb):
                    g_jp = gate[jp:jp + 1, :]
                    beats = (g_jp > gate) | ((g_jp == gate) & (blk > jp))
                    cnt = cnt + jnp.where(beats & (qi > jp), 1, 0)
                sel = ((blk < qi) & (cnt < MOBA_TOPK)) | (blk == qi)
                bias_t = jnp.where(sel, 0.0, NEG).astype(F32)
                spare = HEAD_DIM * (1 - h)
                pieces = [jnp.zeros((spare, tq), F32)] if spare else []
                pieces += [bias_t, jnp.zeros((LANES - spare - nb, tq), F32)]
                qaug[dst_slot, t, h] = jnp.where(hm, q, jnp.concatenate(pieces, axis=0).T.astype(BF16))

    @pl.when(s == 0)
    def _():
        for h in range(2):
            spare = HEAD_DIM * (1 - h)
            for j in range(nb):
                onehot = jnp.where(lane == spare + j, 1.0, 0.0).astype(BF16)
                kaug[h, j * tq:(j + 1) * tq, :] = jnp.where(
                    _head_mask((tq, LANES), h), k_ref[j * tq:(j + 1) * tq, :], onehot)
                vaug[h, j * tq:(j + 1) * tq, :] = jnp.where(
                    _head_mask((tq, LANES), h), v_ref[j * tq:(j + 1) * tq, :], jnp.ones((tq, LANES), BF16))
        select(0, 0)

    for t in range(2):
        for h in range(2):
            m_scr[t, h] = jnp.full((tq, LANES), NEG, F32)
            acc_scr[t, h] = jnp.zeros((tq, LANES), F32)

    row = lax.broadcasted_iota(jnp.int32, (tq, tq), 0)
    col = lax.broadcasted_iota(jnp.int32, (tq, tq), 1)
    causal_bias = jnp.where(col <= row, 0.0, NEG).astype(F32)
    starts = [pl.multiple_of(s * tq, tq), pl.multiple_of((nb - 1 - s) * tq, tq)]

    half = nb // 2
    for it in range(nb + 1):
        if it < 2:
            t, kstart = it, starts[it]
        elif it < 2 + half:
            t, kstart = 1, (it - 2) * tq
        else:
            i = it - 2 - half
            first = i < s
            t = jnp.where(first, 0, 1)
            kstart = pl.multiple_of(jnp.where(first, i, half + i - s) * tq, tq)
        for h in range(2):
            sc = _dot_nt(qaug[slot, t, h], kaug[h, pl.ds(kstart, tq), :])
            if it < 2:
                sc = sc + causal_bias
            m_old = m_scr[t, h]
            m_new = jnp.maximum(m_old, jnp.max(sc, axis=1, keepdims=True))
            p = jnp.exp(sc - jnp.concatenate([m_new, m_new], axis=1))
            acc_scr[t, h] = (jnp.exp(m_old - m_new) * acc_scr[t, h]
                             + jnp.dot(p.astype(BF16), vaug[h, pl.ds(kstart, tq), :], preferred_element_type=F32))
            m_scr[t, h] = m_new

    first_head = _head_mask((tq, LANES), 0)
    for t in range(2):
        acc = jnp.where(first_head, acc_scr[t, 0], acc_scr[t, 1])
        den = jnp.where(first_head, pltpu.roll(acc_scr[t, 0], HEAD_DIM, 1), pltpu.roll(acc_scr[t, 1], HEAD_DIM, 1))
        o_ref[pl.ds(starts[t], tq), :] = (acc / den).astype(o_ref.dtype)

    select(jnp.minimum(s + 1, nsteps - 1), 1 - slot)


def _moba_attention(qm3, km3, vm3, kmh, kml):
    batch, seq, w = qm3.shape
    nb = seq // MOBA_BLOCK
    assert nb % 2 == 0 and nb <= HEAD_DIM
    npair = w // LANES
    tq = MOBA_BLOCK
    seq_spec = pl.BlockSpec((None, seq, LANES), lambda b, p, i: (b, 0, p))
    km_spec = pl.BlockSpec((None, nb, LANES), lambda b, p, i: (b, 0, p))
    state = pltpu.VMEM((2, 2, tq, LANES), F32)
    return pl.pallas_call(
        _moba_kernel,
        grid=(batch, npair, nb // 2),
        in_specs=[seq_spec, seq_spec, seq_spec, km_spec, km_spec],
        out_specs=seq_spec,
        out_shape=jax.ShapeDtypeStruct((batch, seq, w), BF16),
        scratch_shapes=[pltpu.VMEM((2, 2, 2, tq, LANES), BF16), pltpu.VMEM((2, seq, LANES), BF16),
                        pltpu.VMEM((2, seq, LANES), BF16), state, state],
        compiler_params=_cparams("parallel", "parallel", "arbitrary"),
        name="moba_attn",
    )(qm3, km3, vm3, kmh, kml)


def _memkv_kernel(mem_ref, w_ref, k_ref, v_ref):
    kv = jnp.dot(mem_ref[...].astype(BF16), w_ref[...], preferred_element_type=F32)
    k_ref[...] = kv[:, :W_MEM].astype(BF16)
    v_ref[...] = kv[:, W_MEM:].astype(BF16)


def _mem_kv(mem, w_kv_b):
    batch, m, d = mem.shape
    return pl.pallas_call(
        _memkv_kernel,
        grid=(batch,),
        in_specs=[pl.BlockSpec((None, m, d), lambda b: (b, 0, 0)),
                  pl.BlockSpec((d, 2 * W_MEM), lambda b: (0, 0))],
        out_specs=[pl.BlockSpec((None, m, W_MEM), lambda b: (b, 0, 0))] * 2,
        out_shape=[jax.ShapeDtypeStruct((batch, m, W_MEM), BF16)] * 2,
        compiler_params=_cparams("parallel"),
        name="mem_kv_proj",
    )(mem, w_kv_b)


def _memattn_kernel(q_ref, k_ref, v_ref, o_ref):
    tq, w = q_ref.shape
    for pair in range(w // LANES):
        lanes = slice(pair * LANES, (pair + 1) * LANES)
        k = k_ref[:, lanes]
        v = v_ref[:, lanes]
        for c in range(tq // MEM_CHUNK):
            rows = slice(c * MEM_CHUNK, (c + 1) * MEM_CHUNK)
            q = q_ref[rows, lanes]
            outs = []
            for h in range(2):
                qh = jnp.where(_head_mask(q.shape, h), q, jnp.zeros_like(q))
                s = _dot_nt(qh, k)
                m = jnp.max(s, axis=1, keepdims=True)
                p = jnp.exp(s - m)
                l = jnp.sum(p, axis=1, keepdims=True)
                outs.append(jnp.dot(p.astype(BF16), v, preferred_element_type=F32) / l)
            o_ref[rows, lanes] = jnp.where(_head_mask(outs[0].shape, 0), outs[0], outs[1]).astype(o_ref.dtype)


def _mem_attention(qx3, k_mem, v_mem, tq=2048):
    batch, seq, w = qx3.shape
    m = k_mem.shape[1]
    tq = min(tq, seq)
    return pl.pallas_call(
        _memattn_kernel,
        grid=(batch, seq // tq),
        in_specs=[pl.BlockSpec((None, tq, w), lambda b, i: (b, i, 0)),
                  pl.BlockSpec((None, m, w), lambda b, i: (b, 0, 0)),
                  pl.BlockSpec((None, m, w), lambda b, i: (b, 0, 0))],
        out_specs=pl.BlockSpec((None, tq, w), lambda b, i: (b, i, 0)),
        out_shape=jax.ShapeDtypeStruct((batch, seq, w), BF16),
        compiler_params=_cparams("parallel", "parallel"),
        name="mem_attn",
    )(qx3, k_mem, v_mem)


def _merge_kernel(alpha, x_ref, o1_ref, o2_ref, o3_ref, l1_ref, l2_ref, l3_ref, ym_ref, yx_ref,
                  wg_ref, bg_ref, wbd_ref, wbm_ref, wbx_ref, wo_ref, g_ref, b_ref, wrh_ref, wrl_ref, rb_ref,
                  out_ref, packed_ref, eidx_ref, gw_ref, rank_ref, cnt_ref, stage, carry, x1_prev):
    step = pl.program_id(0)
    route_refs = (wrh_ref, wrl_ref, rb_ref, eidx_ref, gw_ref, rank_ref, cnt_ref, carry)

    @pl.when(step == 0)
    def _():
        carry[...] = jnp.zeros_like(carry)
        x1_prev[...] = jnp.zeros_like(x1_prev)

    route = _route_tile(x1_prev[...], step > 0, *route_refs)

    def route_phases():
        for _ in range(ROUTE_SPLIT):
            next(route)

    route_phases()
    x = x_ref[...]
    xb = x.astype(BF16)
    tm, d = x.shape

    def token_major(ref, slot):
        dilation = ref.shape[0]
        if dilation == 1:
            return ref[0].astype(F32)
        for r in range(dilation):
            stage[slot, pl.ds(r, tm // dilation, stride=dilation), :] = ref[r].astype(F32)
        return stage[slot]

    o1, o2, o3 = (token_major(r, s) for s, r in enumerate((o1_ref, o2_ref, o3_ref)))
    l1, l2, l3 = (token_major(r, s + 3) for s, r in enumerate((l1_ref, l2_ref, l3_ref)))
    mx = jnp.maximum(jnp.maximum(l1, l2), l3)
    e1, e2, e3 = jnp.exp(l1 - mx), jnp.exp(l2 - mx), jnp.exp(l3 - mx)
    y_dil = (e1 * o1 + e2 * o2 + e3 * o3) / (e1 + e2 + e3)
    route_phases()
    branches = (
        jnp.dot(y_dil.astype(BF16), wbd_ref[...], preferred_element_type=F32),
        jnp.dot(ym_ref[...], wbm_ref[...], preferred_element_type=F32),
        jnp.dot(yx_ref[...], wbx_ref[...], preferred_element_type=F32),
    )
    merged = jnp.zeros_like(x)
    for i, br in enumerate(branches):
        route_phases()
        logits = jnp.dot(xb, wg_ref[:, i * d:(i + 1) * d], preferred_element_type=F32) + bg_ref[:, i * d:(i + 1) * d]
        merged = merged + jax.nn.sigmoid(logits) * br
    route_phases()
    mix =jnp.dot(merged.astype(BF16), wo_ref[...], preferred_element_type=F32)
    for _ in route:
        pass
    x1 = _layer_norm(alpha * x + mix, g_ref[...], b_ref[...])
    out_ref[...] = x1
    packed_ref[...] = _pack_halves(x1)
    x1_prev[...] = x1


def _merge(alpha, x2, o_dil, lse_dil, y_moba, y_mem, wg, bg, wbd, wbm, wbx, wo, g, b, wr_hi_t, wr_lo_t, rbias_b,
           seq, tm=512):
    n, d = x2.shape
    tiles = seq // tm
    last = n // tm - 1
    cur = lambda i: jnp.minimum(i, last)
    row = lambda w: pl.BlockSpec((tm, w), lambda i: (cur(i), 0))
    full = lambda a: pl.BlockSpec(a.shape, lambda i: (0, 0))
    residue_major = lambda a: pl.BlockSpec((None, a.shape[1], tm // a.shape[1], LANES),
                                           lambda i: (cur(i) // tiles, 0, cur(i) % tiles, 0))
    slots = pl.BlockSpec((TOP_K, tm), lambda i: (0, jnp.maximum(i - 1, 0)))
    weights = (wg, bg, wbd, wbm, wbx, wo, g, b, wr_hi_t, wr_lo_t, rbias_b)
    tok = lambda dt: jax.ShapeDtypeStruct((TOP_K, n), dt)
    return pl.pallas_call(
        functools.partial(_merge_kernel, alpha),
        grid=(n // tm + 1,),
        in_specs=([row(d)] + [residue_major(a) for a in (*o_dil, *lse_dil)] + [row(W_MOBA), row(W_MEM)]
                  + [full(a) for a in weights]),
        out_specs=[row(d), row(PACKED), slots, slots, slots, pl.BlockSpec((N_EXPERTS, LANES), lambda i: (0, 0))],
        out_shape=[jax.ShapeDtypeStruct((n, d), F32), jax.ShapeDtypeStruct((n, PACKED), jnp.uint32),
                   tok(jnp.int32), tok(F32), tok(jnp.int32), jax.ShapeDtypeStruct((N_EXPERTS, LANES), jnp.int32)],
        scratch_shapes=[pltpu.VMEM((2 * len(DIL_PAIRS), tm, LANES), F32), pltpu.VMEM((N_EXPERTS, LANES), F32),
                        pltpu.VMEM((tm, d), F32)],
        compiler_params=_cparams("arbitrary"),
        name="merge_outproj_ln1_route",
    )(x2, *o_dil, *lse_dil, y_moba, y_mem, *weights)


def _token_mixer(x2, mem, positions, w_in, w_mem_kv, w_gate, b_gate, w_br_dil, w_br_moba, w_br_mem,
                 w_out, ln1_g, ln1_b, w_router, router_bias, batch, seq, alpha):
    wr_t = w_router.T
    wr_hi = wr_t.astype(BF16)
    wr_lo = (wr_t - wr_hi.astype(F32)).astype(BF16)
    rbias_b = jnp.broadcast_to(router_bias.astype(F32)[:, None], (N_EXPERTS, LANES))
    cos_t, sin_t = _rope_tables(positions)
    (q_dil, k_dil, v_dil), (qm, km, vm, qx) = _in_projection(x2, w_in.astype(BF16), cos_t, sin_t, batch, seq)
    o_dil, lse_dil = [], []
    for g in range(len(DIL_PAIRS)):
        o, lse = _dilated_attention(q_dil[g], k_dil[g], v_dil[g])
        o_dil.append(o)
        lse_dil.append(lse)
    qm3, km3, vm3 = (t.reshape(batch, seq, W_MOBA) for t in (qm, km, vm))
    kmh, kml = _moba_kmean(km3)
    y_moba = _moba_attention(qm3, km3, vm3, kmh, kml).reshape(batch * seq, W_MOBA)
    k_mem, v_mem = _mem_kv(mem, w_mem_kv.astype(BF16))
    y_mem = _mem_attention(qx.reshape(batch, seq, W_MEM), k_mem, v_mem).reshape(batch * seq, W_MEM)
    return _merge(alpha, x2, o_dil, lse_dil, y_moba, y_mem,
                  w_gate.astype(BF16), b_gate.reshape(1, -1), w_br_dil.astype(BF16),
                  w_br_moba.astype(BF16), w_br_mem.astype(BF16), w_out.astype(BF16),
                  ln1_g.reshape(1, -1), ln1_b.reshape(1, -1), wr_hi, wr_lo, rbias_b, seq)


EXPERT_ROWS = 256
EXPERT_GROUP = 4
EXPERT_RING = 2 * EXPERT_GROUP + 1
TOKEN_TILE = 512
ROUTE_SPLIT = 2
PACKED = 512
SC_WINDOW = 128


def _first_index_of_max(v, iota_f, size):
    m = jnp.max(v, axis=0, keepdims=True)
    idx = jnp.min(jnp.where(v == m, iota_f, float(size)), axis=0, keepdims=True)
    return m, idx


def _route_tile(x, counted, *refs):
    part = x.shape[0] // ROUTE_SPLIT
    for i in range(ROUTE_SPLIT):
        yield from _route_part(x[i * part:(i + 1) * part, :], counted, slice(i * part, (i + 1) * part), *refs)


def _route_part(x, counted, cols, wh_ref, wl_ref, bias_ref, eidx_ref, gw_ref, rank_ref, cnt_ref, carry):
    tm = x.shape[0]
    xh = x.astype(BF16)
    xl = (x - xh.astype(F32)).astype(BF16)
    wh = wh_ref[...]
    logits = _dot_nt(wh, xh) + _dot_nt(wh, xl) + _dot_nt(wl_ref[...], xh)
    yield
    scores = jax.nn.sigmoid(logits)
    biased = scores + bias_ref[...][:, :1]

    giota = lax.broadcasted_iota(jnp.int32, (GROUP_SIZE, tm), 0).astype(F32)
    group_scores = []
    for g in range(N_GROUPS):
        slab = biased[g * GROUP_SIZE:(g + 1) * GROUP_SIZE, :]
        m1, i1 = _first_index_of_max(slab, giota, GROUP_SIZE)
        m2 = jnp.max(jnp.where(giota == i1, -jnp.inf, slab), axis=0, keepdims=True)
        group_scores.append(m1 + m2)
    gs = jnp.concatenate(group_scores, axis=0)
    gidx = lax.broadcasted_iota(jnp.int32, (N_GROUPS, tm), 0)
    beaten = jnp.zeros((N_GROUPS, tm), jnp.int32)
    for gp in range(N_GROUPS):
        row = gs[gp:gp + 1, :]
        beaten = beaten + jnp.where((row > gs) | ((row == gs) & (gidx > gp)), 1, 0)
    keep = beaten < TOPK_GROUPS

    ahead = jnp.zeros((1, tm), jnp.int32)
    slab_b = [jnp.full((GROUP_SIZE, tm), -jnp.inf, F32) for _ in range(TOPK_GROUPS)]
    slab_s = [jnp.zeros((GROUP_SIZE, tm), F32) for _ in range(TOPK_GROUPS)]
    slab_g = [jnp.zeros((1, tm), F32) for _ in range(TOPK_GROUPS)]
    place = {}
    for g in range(N_GROUPS):
        kept = keep[g:g + 1, :]
        rows = slice(g * GROUP_SIZE, (g + 1) * GROUP_SIZE)
        for j in range(min(g, TOPK_GROUPS - 1) + 1):
            here = kept & (ahead == j)
            place[j, g] = here
            slab_b[j] = jnp.where(here, biased[rows, :], slab_b[j])
            slab_s[j] = jnp.where(here, scores[rows, :], slab_s[j])
            slab_g[j] = jnp.where(here, float(g), slab_g[j])
        ahead = ahead + jnp.where(kept, 1, 0)
    masked = jnp.concatenate(slab_b, axis=0)
    kept_scores = jnp.concatenate(slab_s, axis=0)
    yield

    n_kept = TOPK_GROUPS * GROUP_SIZE
    ciota = lax.broadcasted_iota(jnp.int32, (n_kept, tm), 0).astype(F32)
    chosen = jnp.zeros((n_kept, tm), F32)
    cidx_rows, idx_rows, gw_rows = [], [], []
    for k in range(TOP_K):
        _, cidx = _first_index_of_max(masked, ciota, n_kept)
        hit = ciota == cidx
        gw_rows.append(jnp.sum(jnp.where(hit, kept_scores, 0.0), axis=0, keepdims=True))
        masked = jnp.where(hit, -jnp.inf, masked)
        chosen = jnp.where(hit, 1.0, chosen)
        slab = sum(jnp.where(cidx >= float(j * GROUP_SIZE), 1.0, 0.0) for j in range(1, TOPK_GROUPS))
        group = sum(jnp.where(slab == float(j), slab_g[j], 0.0) for j in range(TOPK_GROUPS))
        cidx_rows.append(cidx)
        idx_rows.append((group - slab) * GROUP_SIZE + cidx)
        if k % 2 == 1:
            yield
    idx8 = jnp.concatenate(idx_rows, axis=0)
    gw8 = jnp.concatenate(gw_rows, axis=0)
    gw_ref[:, cols] = gw8 / jnp.sum(gw8, axis=0, keepdims=True) * ROUTED_SCALE
    eidx_ref[:, cols] = idx8.astype(jnp.int32)

    def slabs_to_groups(compact):
        out = []
        for g in range(N_GROUPS):
            acc = jnp.zeros((GROUP_SIZE, tm), F32)
            for j in range(min(g, TOPK_GROUPS - 1) + 1):
                acc = jnp.where(place[j, g], compact[j * GROUP_SIZE:(j + 1) * GROUP_SIZE, :], acc)
            out.append(acc)
        return jnp.concatenate(out, axis=0)

    def groups_to_slabs(full):
        out = []
        for j in range(TOPK_GROUPS):
            acc = jnp.zeros((GROUP_SIZE, tm), F32)
            for g in range(j, N_GROUPS):
                acc = jnp.where(place[j, g], full[g * GROUP_SIZE:(g + 1) * GROUP_SIZE, :], acc)
            out.append(acc)
        return jnp.concatenate(out, axis=0)

    onehot = slabs_to_groups(chosen)
    t_row = lax.broadcasted_iota(jnp.int32, (tm, tm), 0)
    t_col = lax.broadcasted_iota(jnp.int32, (tm, tm), 1)
    earlier = jnp.where(t_row < t_col, 1.0, 0.0).astype(BF16)
    prefix = jnp.dot(onehot.astype(BF16), earlier, preferred_element_type=F32)
    base = carry[...]
    prefix = groups_to_slabs(prefix + jnp.concatenate([base] * (tm // LANES), axis=1))
    rank_rows = [jnp.sum(jnp.where(ciota == cidx, prefix, 0.0), axis=0, keepdims=True) for cidx in cidx_rows]
    rank_ref[:, cols] = jnp.concatenate(rank_rows, axis=0).astype(jnp.int32)
    total = base + jnp.where(counted, jnp.sum(onehot, axis=1, keepdims=True), 0.0)
    carry[...] = total
    cnt_ref[...] = total.astype(jnp.int32)


def _dest_kernel(eidx_ref, rank_ref, start_ref, dest_ref):
    eidx = eidx_ref[...]
    tm = eidx.shape[1]
    eiota = lax.broadcasted_iota(jnp.int32, (N_EXPERTS, tm), 0)
    start = start_ref[...][:, :1]
    rows = [jnp.sum(jnp.where(eiota == eidx[k:k + 1, :], start, 0.0), axis=0, keepdims=True)
            for k in range(TOP_K)]
    dest_ref[...] = jnp.concatenate(rows, axis=0).astype(jnp.int32) + rank_ref[...]


def _dest_rows(eidx, rank, start_b):
    n = eidx.shape[1]
    nt = n // TOKEN_TILE
    return pl.pallas_call(
        _dest_kernel,
        grid=(nt,),
        in_specs=[pl.BlockSpec((TOP_K, TOKEN_TILE), lambda i: (0, i)),
                  pl.BlockSpec((TOP_K, TOKEN_TILE), lambda i: (0, i)),
                  pl.BlockSpec((N_EXPERTS, LANES), lambda i: (0, 0))],
        out_specs=pl.BlockSpec((TOP_K, TOKEN_TILE), lambda i: (0, i)),
        out_shape=jax.ShapeDtypeStruct((TOP_K, n), jnp.int32),
        compiler_params=_cparams("parallel"),
        name="moe_dest_rows",
    )(eidx, rank, start_b)


def _pack_halves(v):
    return pltpu.pack_elementwise([v[:, :PACKED], v[:, PACKED:]], packed_dtype=BF16)


def _unpack_half(p, index):
    return pltpu.unpack_elementwise(p, index=index, packed_dtype=BF16, unpacked_dtype=F32)


def _sc_mesh():
    return plsc.VectorSubcoreMesh(core_axis_name="core", subcore_axis_name="subcore")


def _sc_move_rows(src, src_idx, dst_idx, out_rows, name):
    width = src.shape[1]
    count = src_idx.shape[1]

    @functools.partial(pl.kernel, out_type=jax.ShapeDtypeStruct((out_rows, width), src.dtype),
                       mesh=_sc_mesh(), scratch_types=[pltpu.VMEM((SC_WINDOW, width), src.dtype)], name=name)
    def move(src_hbm, sidx_hbm, didx_hbm, out_hbm, rows_vmem):
        def body(sidx_vmem, didx_vmem):
            pltpu.sync_copy(src_hbm.at[sidx_vmem.at[0]], rows_vmem)
            pltpu.sync_copy(rows_vmem, out_hbm.at[didx_vmem.at[0]])

        idx_spec = pl.BlockSpec((1, SC_WINDOW), index_map=lambda i: (0, i))
        pltpu.emit_pipeline(
            body,
            grid=(count // SC_WINDOW,),
            in_specs=[idx_spec, idx_spec],
            out_specs=[],
            core_axis_name=("core", "subcore"),
            dimension_semantics=(pltpu.PARALLEL,),
        )(sidx_hbm, didx_hbm)

    return move(src, src_idx, dst_idx)


def _sc_dispatch_rows(src, token_idx, dest, out_rows):
    n, width = src.shape
    slots = dest.shape[0]

    @functools.partial(pl.kernel, out_type=jax.ShapeDtypeStruct((out_rows, width), src.dtype),
                       mesh=_sc_mesh(), scratch_types=[pltpu.VMEM((SC_WINDOW, width), src.dtype)],
                       name="moe_dispatch_sc")
    def dispatch(src_hbm, tidx_hbm, dest_hbm, out_hbm, rows_vmem):
        def body(tidx_vmem, dest_vmem):
            pltpu.sync_copy(src_hbm.at[tidx_vmem.at[0]], rows_vmem)
            for k in range(slots):
                pltpu.sync_copy(rows_vmem, out_hbm.at[dest_vmem.at[k]])

        pltpu.emit_pipeline(
            body,
            grid=(n // SC_WINDOW,),
            in_specs=[pl.BlockSpec((1, SC_WINDOW), index_map=lambda i: (0, i)),
                      pl.BlockSpec((slots, SC_WINDOW), index_map=lambda i: (0, i))],
            out_specs=[],
            core_axis_name=("core", "subcore"),
            dimension_semantics=(pltpu.PARALLEL,),
        )(tidx_hbm, dest_hbm)

    return dispatch(src, token_idx, dest)


def _expert_kernel(first_blk_ref, count_ref, total_ref, xs_hbm, wg_ref, wu_ref, wd_ref, ys_hbm,
                   wg_b, wu_b, wd_b, x_ring, y_ring, in_sem, out_sem):
    e = pl.program_id(0)
    total = total_ref[0]
    first = first_blk_ref[e]
    count = count_ref[e]
    nblk = (count + EXPERT_ROWS - 1) // EXPERT_ROWS
    spare_block = ys_hbm.shape[0] // EXPERT_ROWS - EXPERT_RING

    def rows_of(g):
        return pl.ds(pl.multiple_of(g * EXPERT_ROWS, EXPERT_ROWS), EXPERT_ROWS)

    def in_copy(g):
        slot = g % EXPERT_RING
        return pltpu.make_async_copy(xs_hbm.at[rows_of(g)], x_ring.at[slot], in_sem.at[slot])

    def out_copy(g, slot):
        return pltpu.make_async_copy(y_ring.at[slot], ys_hbm.at[rows_of(g)], out_sem.at[slot])

    @pl.when(e == 0)
    def _():
        y_ring[...] = jnp.zeros_like(y_ring)
        for g in range(EXPERT_RING - 1):
            in_copy(g).start()
        for slot in range(EXPERT_RING):
            out_copy(spare_block + slot, slot).start()

    @pl.when(nblk > 0)
    def _():
        wg_b[...] = wg_ref[...].astype(BF16)
        wu_b[...] = wu_ref[...].astype(BF16)
        wd_b[...] = wd_ref[...].astype(BF16)

    def blocks(js):
        gs = [first + j for j in js]
        for g in gs:
            in_copy(g).wait()
        ys = []
        for j, g in zip(js, gs):
            p = x_ring[g % EXPERT_RING]
            live = lax.broadcasted_iota(jnp.int32, p.shape, 0) + j * EXPERT_ROWS < count
            lo = jnp.where(live, _unpack_half(p, 0), 0.0).astype(BF16)
            hi = jnp.where(live, _unpack_half(p, 1), 0.0).astype(BF16)

            def up(w, lo=lo, hi=hi):
                return (jnp.dot(lo, w[:PACKED, :], preferred_element_type=F32)
                        + jnp.dot(hi, w[PACKED:, :], preferred_element_type=F32))

            hid = (jax.nn.silu(up(wg_b)) * up(wu_b)).astype(BF16)
            ys.append(_pack_halves(jnp.dot(hid, wd_b[...], preferred_element_type=F32)))
        for g in gs:
            in_copy(g + EXPERT_RING - 1).start()
        for g in gs:
            out_copy(g, g % EXPERT_RING).wait()
        for g, y in zip(gs, ys):
            y_ring[g % EXPERT_RING] = y
        for g in gs:
            out_copy(g, g % EXPERT_RING).start()

    def group(i, carry):
        blocks([EXPERT_GROUP * i + b for b in range(EXPERT_GROUP)])
        return carry

    lax.fori_loop(0, nblk // EXPERT_GROUP, group, 0)
    left = nblk % EXPERT_GROUP

    @pl.when(left >= 2)
    def _():
        blocks([nblk - left, nblk - left + 1])

    @pl.when(left % 2 == 1)
    def _():
        blocks([nblk - 1])

    @pl.when(e == pl.num_programs(0) - 1)
    def _():
        for slot in range(EXPERT_RING):
            out_copy(spare_block + slot, slot).wait()
        for g in range(EXPERT_RING - 1):
            in_copy(total + g).wait()


def _expert_ffn(xs, first_blk, counts, total, w_e_gate, w_e_up, w_e_down):
    rows = xs.shape[0] + EXPERT_ROWS
    n_experts, d = w_e_gate.shape[0], w_e_gate.shape[1]
    w_map = lambda e, fb, ct, tt: (e, 0, 0)
    ring = pltpu.VMEM((EXPERT_RING, EXPERT_ROWS, PACKED), jnp.uint32)
    return pl.pallas_call(
        _expert_kernel,
        grid_spec=pltpu.PrefetchScalarGridSpec(
            num_scalar_prefetch=3,
            grid=(n_experts,),
            in_specs=[pl.BlockSpec(memory_space=pl.ANY),
                      pl.BlockSpec((None, d, D_EXPERT), w_map),
                      pl.BlockSpec((None, d, D_EXPERT), w_map),
                      pl.BlockSpec((None, D_EXPERT, d), w_map)],
            out_specs=pl.BlockSpec(memory_space=pl.ANY),
            scratch_shapes=[pltpu.VMEM((d, D_EXPERT), BF16), pltpu.VMEM((d, D_EXPERT), BF16),
                            pltpu.VMEM((D_EXPERT, d), BF16), ring, ring,
                            pltpu.SemaphoreType.DMA((EXPERT_RING,)), pltpu.SemaphoreType.DMA((EXPERT_RING,))],
        ),
        out_shape=jax.ShapeDtypeStruct((rows, PACKED), jnp.uint32),
        compiler_params=_cparams("arbitrary"),
        name="moe_expert_ffn",
    )(first_blk, counts, total, xs, w_e_gate, w_e_up, w_e_down)


def _combine_kernel(alpha, x_ref, gw_ref, ys_ref, wsg_ref, wsu_ref, wsd_ref, g_ref, b_ref, out_ref):
    x = x_ref[...]
    xb = x.astype(BF16)
    hid = (jax.nn.silu(jnp.dot(xb, wsg_ref[...], preferred_element_type=F32))
           * jnp.dot(xb, wsu_ref[...], preferred_element_type=F32)).astype(BF16)
    shared = jnp.dot(hid, wsd_ref[...], preferred_element_type=F32)
    gw = gw_ref[...]
    lo = jnp.zeros((TOKEN_TILE, PACKED), F32)
    hi = jnp.zeros((TOKEN_TILE, PACKED), F32)
    for k in range(TOP_K):
        p = ys_ref[k]
        w = gw[:, k:k + 1]
        lo = lo + w * _unpack_half(p, 0)
        hi = hi + w * _unpack_half(p, 1)
    routed = jnp.concatenate([lo, hi], axis=1)
    out_ref[...] = _layer_norm(alpha * x + (routed + shared), g_ref[...], b_ref[...])


def _combine(alpha, x1, gw_t, ys_tok, wsg, wsu, wsd, g, b):
    n, d = x1.shape
    full = lambda a: pl.BlockSpec(a.shape, lambda i: (0, 0))
    weights = (wsg, wsu, wsd, g, b)
    return pl.pallas_call(
        functools.partial(_combine_kernel, alpha),
        grid=(n // TOKEN_TILE,),
        in_specs=[pl.BlockSpec((TOKEN_TILE, d), lambda i: (i, 0)),
                  pl.BlockSpec((TOKEN_TILE, TOP_K), lambda i: (i, 0)),
                  pl.BlockSpec((TOP_K, TOKEN_TILE, PACKED), lambda i: (0, i, 0))] + [full(a) for a in weights],
        out_specs=pl.BlockSpec((TOKEN_TILE, d), lambda i: (i, 0)),
        out_shape=jax.ShapeDtypeStruct((n, d), F32),
        compiler_params=_cparams("parallel"),
        name="moe_combine_shared_ln2",
    )(x1, gw_t, ys_tok, *weights)


def _moe_layer(x1, x1_packed, eidx, gw, rank, cnt, w_e_gate, w_e_up, w_e_down, w_s_gate, w_s_up, w_s_down,
               ln2_g, ln2_b, alpha):
    n, d = x1.shape

    counts = cnt[:, 0]
    padded = (counts + EXPERT_ROWS - 1) // EXPERT_ROWS * EXPERT_ROWS
    seg_end = jnp.cumsum(padded)
    seg_start = seg_end - padded
    rows = n * TOP_K + N_EXPERTS * EXPERT_ROWS
    first_blk = (seg_start // EXPERT_ROWS).astype(jnp.int32)
    total_blk = (seg_end[-1:] // EXPERT_ROWS).astype(jnp.int32)
    start_b = jnp.broadcast_to(seg_start.astype(F32)[:, None], (N_EXPERTS, LANES))

    dest = _dest_rows(eidx, rank, start_b)
    dest_flat = dest.reshape(1, TOP_K * n)
    assign = jnp.arange(TOP_K * n, dtype=jnp.int32).reshape(1, TOP_K * n)
    xs = _sc_dispatch_rows(x1_packed, assign[:, :n], dest, rows + (EXPERT_RING - 1) * EXPERT_ROWS)
    ys = _expert_ffn(xs, first_blk, counts, total_blk, w_e_gate, w_e_up, w_e_down)
    ys_tok = _sc_move_rows(ys, dest_flat, assign, TOP_K * n, "moe_gather_sc").reshape(TOP_K, n, PACKED)
    return _combine(alpha, x1, gw.T, ys_tok, w_s_gate.astype(BF16), w_s_up.astype(BF16),
                    w_s_down.astype(BF16), ln2_g.reshape(1, -1), ln2_b.reshape(1, -1))


def kernel(x, mem, positions, w_in, w_mem_kv, w_gate, b_gate, w_br_dil, w_br_moba, w_br_mem, w_out, ln1_g, ln1_b, w_router, router_bias, w_e_gate, w_e_up, w_e_down, w_s_gate, w_s_up, w_s_down, ln2_g, ln2_b):
    batch, seq, d = x.shape
    depth = w_in.shape[0]
    alpha = (2.0 * depth) ** 0.25
    h = x.reshape(batch * seq, d)
    for l in range(depth):
        h, h_packed, eidx, gw, rank, cnt = _token_mixer(
            h, mem, positions, w_in[l], w_mem_kv[l], w_gate[l], b_gate[l], w_br_dil[l], w_br_moba[l], w_br_mem[l],
            w_out[l], ln1_g[l], ln1_b[l], w_router[l], router_bias[l], batch, seq, alpha)
        h = _moe_layer(h, h_packed, eidx, gw, rank, cnt, w_e_gate[l], w_e_up[l], w_e_down[l],
                       w_s_gate[l], w_s_up[l], w_s_down[l], ln2_g[l], ln2_b[l], alpha)
    return h.reshape(batch, seq, d)
```

```python
import functools

import jax
import jax.numpy as jnp
from jax import lax
from jax.experimental import pallas as pl
from jax.experimental.pallas import tpu as pltpu
from jax.experimental.pallas import tpu_sc as plsc

F32 = jnp.float32
BF16 = jnp.bfloat16

LANES = 128
VMEM_LIMIT_BYTES = 48 * 1024 * 1024

HEAD_DIM = 64
ROPE_DIM = HEAD_DIM // 4
ROPE_HALF = ROPE_DIM // 2
ROPE_THETA = 500000.0
DIL_PAIRS = ((128, 1), (512, 4), (2048, 16))
BAND = 128
DIL_TILE = 1024
MEM_CHUNK = 512
W_DIL = 384
W_MOBA = 384
W_MEM = 256
MOBA_BLOCK = 256
MOBA_TOPK = 3
N_EXPERTS = 256
N_GROUPS = 8
GROUP_SIZE = N_EXPERTS // N_GROUPS
TOPK_GROUPS = 4
TOP_K = 8
D_EXPERT = 256
ROUTED_SCALE = 2.5
LN_EPS = 1e-5
NEG = -1e30
QK_SCALE = HEAD_DIM ** -0.5


def _cparams(*sem):
    return pltpu.CompilerParams(dimension_semantics=sem, vmem_limit_bytes=VMEM_LIMIT_BYTES)


def _dot_nt(a, b):
    return lax.dot_general(a, b, (((1,), (1,)), ((), ())), preferred_element_type=F32)


def _layer_norm(v, g, b):
    mu = jnp.mean(v, axis=-1, keepdims=True)
    c = v - mu
    var = jnp.mean(c * c, axis=-1, keepdims=True)
    return c * lax.rsqrt(var + LN_EPS) * g + b


_IN_SECTIONS = (
    (0, W_DIL, True, True),
    (W_DIL, W_DIL, True, False),
    (2 * W_DIL, W_DIL, False, False),
    (3 * W_DIL, W_MOBA, True, True),
    (3 * W_DIL + W_MOBA, W_MOBA, True, False),
    (3 * W_DIL + 2 * W_MOBA, W_MOBA, False, False),
    (3 * W_DIL + 3 * W_MOBA, W_MEM, False, True),
)


N_DIL_SECTIONS = 3
MATMUL_COLS = 512


def _inproj_kernel(x_ref, w_ref, cos_ref, sin_ref, *refs):
    n_dil = N_DIL_SECTIONS * len(DIL_PAIRS)
    dil_refs, flat_refs, stage = refs[:n_dil], refs[n_dil:-1], refs[-1]
    xb = x_ref[...].astype(BF16)
    cos = cos_ref[...]
    sin = sin_ref[...]
    tm = xb.shape[0]
    lane = lax.broadcasted_iota(jnp.int32, (tm, LANES), 1)
    first_half = (lane % ROPE_DIM) < ROPE_HALF

    def rope(t):
        partner = jnp.where(first_half, pltpu.roll(t, LANES - ROPE_HALF, 1), pltpu.roll(t, ROPE_HALF, 1))
        return t * cos + partner * sin

    chunk_of = {}
    for sec, (off, width, _, _) in enumerate(_IN_SECTIONS):
        for c in range(width // LANES):
            chunk_of[off // LANES + c] = (sec, c)
    per_group = MATMUL_COLS // LANES
    for grp in range(w_ref.shape[1] // MATMUL_COLS):
        acc = jnp.dot(xb, w_ref[:, grp * MATMUL_COLS:(grp + 1) * MATMUL_COLS], preferred_element_type=F32)
        for cc in range(per_group):
            sec, c = chunk_of[grp * per_group + cc]
            _, _, roped, scaled = _IN_SECTIONS[sec]
            t = acc[:, cc * LANES:(cc + 1) * LANES]
            if roped:
                t = rope(t)
            if scaled:
                t = t * QK_SCALE
            if sec >= N_DIL_SECTIONS:
                flat_refs[sec - N_DIL_SECTIONS][:, c * LANES:(c + 1) * LANES] = t.astype(BF16)
                continue
            o_ref = dil_refs[sec * len(DIL_PAIRS) + c]
            dilation = DIL_PAIRS[c][1]
            if dilation == 1:
                o_ref[0] = t.astype(BF16)
                continue
            slot = sec * len(DIL_PAIRS) + c
            stage[slot] = t
            for r in range(dilation):
                o_ref[r] = stage[slot, pl.ds(r, tm // dilation, stride=dilation), :].astype(BF16)


def _in_projection(x2, w_in_b, cos_t, sin_t, batch, seq, tm=1024):
    n, d = x2.shape
    w_total = w_in_b.shape[1]
    tiles = seq // tm
    dil_specs, dil_shapes = [], []
    for _ in range(N_DIL_SECTIONS):
        for _, dilation in DIL_PAIRS:
            dil_specs.append(pl.BlockSpec((None, dilation, tm // dilation, LANES),
                                          lambda i: (i // tiles, 0, i % tiles, 0)))
            dil_shapes.append(jax.ShapeDtypeStruct((batch, dilation, seq // dilation, LANES), BF16))
    widths = [s[1] for s in _IN_SECTIONS[N_DIL_SECTIONS:]]
    outs = pl.pallas_call(
        _inproj_kernel,
        grid=(n // tm,),
        in_specs=[
            pl.BlockSpec((tm, d), lambda i: (i, 0)),
            pl.BlockSpec((d, w_total), lambda i: (0, 0)),
            pl.BlockSpec((tm, LANES), lambda i: (i, 0)),
            pl.BlockSpec((tm, LANES), lambda i: (i, 0)),
        ],
        out_specs=dil_specs + [pl.BlockSpec((tm, w), lambda i: (i, 0)) for w in widths],
        out_shape=dil_shapes + [jax.ShapeDtypeStruct((n, w), BF16) for w in widths],
        scratch_shapes=[pltpu.VMEM((N_DIL_SECTIONS * len(DIL_PAIRS), tm, LANES), F32)],
        compiler_params=_cparams("parallel"),
        name="in_proj_rope",
    )(x2, w_in_b, cos_t, sin_t)
    n_dil = len(dil_specs)
    ng = len(DIL_PAIRS)
    qkv_dil = [outs[s * ng:(s + 1) * ng] for s in range(N_DIL_SECTIONS)]
    return qkv_dil, outs[n_dil:]


def _rope_tables(positions):
    lane = jnp.arange(LANES)
    rotary = (lane % HEAD_DIM) < ROPE_DIM
    inv_freq = ROPE_THETA ** (-(lane % ROPE_HALF).astype(F32) / ROPE_HALF)
    ang = positions.reshape(-1).astype(F32)[:, None] * jnp.where(rotary, inv_freq, 0.0)
    sign = jnp.where((lane % ROPE_DIM) < ROPE_HALF, -1.0, 1.0)
    return jnp.cos(ang), jnp.sin(ang) * sign


def _head_mask(shape, h):
    lane = lax.broadcasted_iota(jnp.int32, shape, 1)
    return (lane // HEAD_DIM) == h


def _dil_kernel(q_ref, kp_ref, k_ref, vp_ref, v_ref, o_ref, lse_ref):
    i = pl.program_id(2)
    n_res, tq = q_ref.shape[0], q_ref.shape[1]
    qi = lax.broadcasted_iota(jnp.int32, (BAND, 2 * BAND), 0)
    kj = lax.broadcasted_iota(jnp.int32, (BAND, 2 * BAND), 1)
    dist = qi + BAND - kj
    band = (dist >= 0) & (dist <= BAND)
    first = _head_mask((BAND, LANES), 0)
    for r in range(n_res):
        for j in range(tq // BAND):
            rows = slice(j * BAND, (j + 1) * BAND)
            q = q_ref[r, rows, :]
            if j == 0:
                k_prev, v_prev = kp_ref[r], vp_ref[r]
                allowed = band & ((kj >= BAND) | (i > 0))
            else:
                prev_rows = slice((j - 1) * BAND, j * BAND)
                k_prev, v_prev = k_ref[r, prev_rows, :], v_ref[r, prev_rows, :]
                allowed = band
            k = jnp.concatenate([k_prev, k_ref[r, rows, :]], axis=0)
            v = jnp.concatenate([v_prev, v_ref[r, rows, :]], axis=0)
            outs, lses = [], []
            for h in range(2):
                qh = jnp.where(_head_mask(q.shape, h), q, jnp.zeros_like(q))
                s = jnp.where(allowed, _dot_nt(qh, k), NEG)
                m = jnp.max(s, axis=1, keepdims=True)
                p = jnp.exp(s - m)
                l = jnp.sum(p, axis=1, keepdims=True)
                outs.append(jnp.dot(p.astype(BF16), v, preferred_element_type=F32) / l)
                lses.append(jnp.broadcast_to(m + jnp.log(l), (BAND, LANES)))
            o_ref[r, rows, :] = jnp.where(first, outs[0], outs[1]).astype(o_ref.dtype)
            lse_ref[r, rows, :] = jnp.where(first, lses[0], lses[1])


def _dilated_attention(q4, k4, v4):
    batch, dilation, steps, _ = q4.shape
    tq = min(steps, DIL_TILE)
    n_res = min(dilation, DIL_TILE // tq)
    per_tile = tq // BAND
    cur = pl.BlockSpec((None, n_res, tq, LANES), lambda b, r, i: (b, r, i, 0))
    prev = pl.BlockSpec((None, n_res, BAND, LANES), lambda b, r, i: (b, r, jnp.maximum(i * per_tile - 1, 0), 0))
    return pl.pallas_call(
        _dil_kernel,
        grid=(batch, dilation // n_res, steps // tq),
        in_specs=[cur, prev, cur, prev, cur],
        out_specs=[cur, cur],
        out_shape=[jax.ShapeDtypeStruct(q4.shape, BF16), jax.ShapeDtypeStruct(q4.shape, F32)],
        compiler_params=_cparams("parallel", "parallel", "arbitrary"),
        name=f"dilated_attn_d{dilation}",
    )(q4, k4, k4, v4, v4)


def _kmean_kernel(k_ref, hi_ref, lo_ref):
    k = k_ref[...].astype(F32)
    s, w = k.shape
    mean = jnp.sum(k.reshape(s // MOBA_BLOCK, MOBA_BLOCK, w), axis=1) / MOBA_BLOCK
    hi = mean.astype(BF16)
    hi_ref[...] = hi
    lo_ref[...] = (mean - hi.astype(F32)).astype(BF16)


def _moba_kmean(km3):
    batch, seq, w = km3.shape
    nb = seq // MOBA_BLOCK
    return pl.pallas_call(
        _kmean_kernel,
        grid=(batch,),
        in_specs=[pl.BlockSpec((None, seq, w), lambda b: (b, 0, 0))],
        out_specs=[pl.BlockSpec((None, nb, w), lambda b: (b, 0, 0))] * 2,
        out_shape=[jax.ShapeDtypeStruct((batch, nb, w), BF16)] * 2,
        compiler_params=_cparams("parallel"),
        name="moba_kmean",
    )(km3)


def _moba_kernel(q_ref, k_ref, v_ref, kmh_ref, kml_ref, o_ref, qaug, kaug, vaug, m_scr, acc_scr):
    s = pl.program_id(2)
    nsteps = pl.num_programs(2)
    nb = kmh_ref.shape[0]
    tq = MOBA_BLOCK
    lane = lax.broadcasted_iota(jnp.int32, (tq, LANES), 1)
    blk = lax.broadcasted_iota(jnp.int32, (nb, tq), 0)
    slot = s % 2

    def select(step, dst_slot):
        for t in range(2):
            qi = step if t == 0 else nb - 1 - step
            q = q_ref[pl.ds(pl.multiple_of(qi * tq, tq), tq), :]
            for h in range(2):
                hm = _head_mask(q.shape, h)
                qh = jnp.where(hm, q, jnp.zeros_like(q))
                gate = _dot_nt(kmh_ref[...], qh) + _dot_nt(kml_ref[...], qh)
                cnt = jnp.zeros((nb, tq), jnp.int32)
                for jp in range(nb):
                    g_jp = gate[jp:jp + 1, :]
                    beats = (g_jp > gate) | ((g_jp == gate) & (blk > jp))
                    cnt = cnt + jnp.where(beats & (qi > jp), 1, 0)
                sel = ((blk < qi) & (cnt < MOBA_TOPK)) | (blk == qi)
                bias_t = jnp.where(sel, 0.0, NEG).astype(F32)
                spare = HEAD_DIM * (1 - h)
                pieces = [jnp.zeros((spare, tq), F32)] if spare else []
                pieces += [bias_t, jnp.zeros((LANES - spare - nb, tq), F32)]
                qaug[dst_slot, t, h] = jnp.where(hm, q, jnp.concatenate(pieces, axis=0).T.astype(BF16))

    @pl.when(s == 0)
    def _():
        for h in range(2):
            spare = HEAD_DIM * (1 - h)
            for j in range(nb):
                onehot = jnp.where(lane == spare + j, 1.0, 0.0).astype(BF16)
                kaug[h, j * tq:(j + 1) * tq, :] = jnp.where(
                    _head_mask((tq, LANES), h), k_ref[j * tq:(j + 1) * tq, :], onehot)
                vaug[h, j * tq:(j + 1) * tq, :] = jnp.where(
                    _head_mask((tq, LANES), h), v_ref[j * tq:(j + 1) * tq, :], jnp.ones((tq, LANES), BF16))
        select(0, 0)

    for t in range(2):
        for h in range(2):
            m_scr[t, h] = jnp.full((tq, LANES), NEG, F32)
            acc_scr[t, h] = jnp.zeros((tq, LANES), F32)

    row = lax.broadcasted_iota(jnp.int32, (tq, tq), 0)
    col = lax.broadcasted_iota(jnp.int32, (tq, tq), 1)
    causal_bias = jnp.where(col <= row, 0.0, NEG).astype(F32)
    starts = [pl.multiple_of(s * tq, tq), pl.multiple_of((nb - 1 - s) * tq, tq)]

    half = nb // 2
    for it in range(nb + 1):
        if it < 2:
            t, kstart = it, starts[it]
        elif it < 2 + half:
            t, kstart = 1, (it - 2) * tq
        else:
            i = it - 2 - half
            first = i < s
            t = jnp.where(first, 0, 1)
            kstart = pl.multiple_of(jnp.where(first, i, half + i - s) * tq, tq)
        for h in range(2):
            sc = _dot_nt(qaug[slot, t, h], kaug[h, pl.ds(kstart, tq), :])
            if it < 2:
                sc = sc + causal_bias
            m_old = m_scr[t, h]
            m_new = jnp.maximum(m_old, jnp.max(sc, axis=1, keepdims=True))
            p = jnp.exp(sc - jnp.concatenate([m_new, m_new], axis=1))
            acc_scr[t, h] = (jnp.exp(m_old - m_new) * acc_scr[t, h]
                             + jnp.dot(p.astype(BF16), vaug[h, pl.ds(kstart, tq), :], preferred_element_type=F32))
            m_scr[t, h] = m_new

    first_head = _head_mask((tq, LANES), 0)
    for t in range(2):
        acc = jnp.where(first_head, acc_scr[t, 0], acc_scr[t, 1])
        den = jnp.where(first_head, pltpu.roll(acc_scr[t, 0], HEAD_DIM, 1), pltpu.roll(acc_scr[t, 1], HEAD_DIM, 1))
        o_ref[pl.ds(starts[t], tq), :] = (acc / den).astype(o_ref.dtype)

    select(jnp.minimum(s + 1, nsteps - 1), 1 - slot)


def _moba_attention(qm3, km3, vm3, kmh, kml):
    batch, seq, w = qm3.shape
    nb = seq // MOBA_BLOCK
    assert nb % 2 == 0 and nb <= HEAD_DIM
    npair = w // LANES
    tq = MOBA_BLOCK
    seq_spec = pl.BlockSpec((None, seq, LANES), lambda b, p, i: (b, 0, p))
    km_spec = pl.BlockSpec((None, nb, LANES), lambda b, p, i: (b, 0, p))
    state = pltpu.VMEM((2, 2, tq, LANES), F32)
    return pl.pallas_call(
        _moba_kernel,
        grid=(batch, npair, nb // 2),
        in_specs=[seq_spec, seq_spec, seq_spec, km_spec, km_spec],
        out_specs=seq_spec,
        out_shape=jax.ShapeDtypeStruct((batch, seq, w), BF16),
        scratch_shapes=[pltpu.VMEM((2, 2, 2, tq, LANES), BF16), pltpu.VMEM((2, seq, LANES), BF16),
                        pltpu.VMEM((2, seq, LANES), BF16), state, state],
        compiler_params=_cparams("parallel", "parallel", "arbitrary"),
        name="moba_attn",
    )(qm3, km3, vm3, kmh, kml)


def _memkv_kernel(mem_ref, w_ref, k_ref, v_ref):
    kv = jnp.dot(mem_ref[...].astype(BF16), w_ref[...], preferred_element_type=F32)
    k_ref[...] = kv[:, :W_MEM].astype(BF16)
    v_ref[...] = kv[:, W_MEM:].astype(BF16)


def _mem_kv(mem, w_kv_b):
    batch, m, d = mem.shape
    return pl.pallas_call(
        _memkv_kernel,
        grid=(batch,),
        in_specs=[pl.BlockSpec((None, m, d), lambda b: (b, 0, 0)),
                  pl.BlockSpec((d, 2 * W_MEM), lambda b: (0, 0))],
        out_specs=[pl.BlockSpec((None, m, W_MEM), lambda b: (b, 0, 0))] * 2,
        out_shape=[jax.ShapeDtypeStruct((batch, m, W_MEM), BF16)] * 2,
        compiler_params=_cparams("parallel"),
        name="mem_kv_proj",
    )(mem, w_kv_b)


def _memattn_kernel(q_ref, k_ref, v_ref, o_ref):
    tq, w = q_ref.shape
    for pair in range(w // LANES):
        lanes = slice(pair * LANES, (pair + 1) * LANES)
        k = k_ref[:, lanes]
        v = v_ref[:, lanes]
        for c in range(tq // MEM_CHUNK):
            rows = slice(c * MEM_CHUNK, (c + 1) * MEM_CHUNK)
            q = q_ref[rows, lanes]
            outs = []
            for h in range(2):
                qh = jnp.where(_head_mask(q.shape, h), q, jnp.zeros_like(q))
                s = _dot_nt(qh, k)
                m = jnp.max(s, axis=1, keepdims=True)
                p = jnp.exp(s - m)
                l = jnp.sum(p, axis=1, keepdims=True)
                outs.append(jnp.dot(p.astype(BF16), v, preferred_element_type=F32) / l)
            o_ref[rows, lanes] = jnp.where(_head_mask(outs[0].shape, 0), outs[0], outs[1]).astype(o_ref.dtype)


def _mem_attention(qx3, k_mem, v_mem, tq=2048):
    batch, seq, w = qx3.shape
    m = k_mem.shape[1]
    tq = min(tq, seq)
    return pl.pallas_call(
        _memattn_kernel,
        grid=(batch, seq // tq),
        in_specs=[pl.BlockSpec((None, tq, w), lambda b, i: (b, i, 0)),
                  pl.BlockSpec((None, m, w), lambda b, i: (b, 0, 0)),
                  pl.BlockSpec((None, m, w), lambda b, i: (b, 0, 0))],
        out_specs=pl.BlockSpec((None, tq, w), lambda b, i: (b, i, 0)),
        out_shape=jax.ShapeDtypeStruct((batch, seq, w), BF16),
        compiler_params=_cparams("parallel", "parallel"),
        name="mem_attn",
    )(qx3, k_mem, v_mem)


def _merge_kernel(alpha, x_ref, o1_ref, o2_ref, o3_ref, l1_ref, l2_ref, l3_ref, ym_ref, yx_ref,
                  wg_ref, bg_ref, wbd_ref, wbm_ref, wbx_ref, wo_ref, g_ref, b_ref, wrh_ref, wrl_ref, rb_ref,
                  out_ref, packed_ref, eidx_ref, gw_ref, rank_ref, cnt_ref, stage, carry, x1_prev):
    step = pl.program_id(0)
    route_refs = (wrh_ref, wrl_ref, rb_ref, eidx_ref, gw_ref, rank_ref, cnt_ref, carry)

    @pl.when(step == 0)
    def _():
        carry[...] = jnp.zeros_like(carry)
        x1_prev[...] = jnp.zeros_like(x1_prev)

    route = _route_tile(x1_prev[...], step > 0, *route_refs)

    def route_phases():
        for _ in range(ROUTE_SPLIT):
            next(route)

    route_phases()
    x = x_ref[...]
    xb = x.astype(BF16)
    tm, d = x.shape

    def token_major(ref, slot):
        dilation = ref.shape[0]
        if dilation == 1:
            return ref[0].astype(F32)
        for r in range(dilation):
            stage[slot, pl.ds(r, tm // dilation, stride=dilation), :] = ref[r].astype(F32)
        return stage[slot]

    o1, o2, o3 = (token_major(r, s) for s, r in enumerate((o1_ref, o2_ref, o3_ref)))
    l1, l2, l3 = (token_major(r, s + 3) for s, r in enumerate((l1_ref, l2_ref, l3_ref)))
    mx = jnp.maximum(jnp.maximum(l1, l2), l3)
    e1, e2, e3 = jnp.exp(l1 - mx), jnp.exp(l2 - mx), jnp.exp(l3 - mx)
    y_dil = (e1 * o1 + e2 * o2 + e3 * o3) / (e1 + e2 + e3)
    route_phases()
    branches = (
        jnp.dot(y_dil.astype(BF16), wbd_ref[...], preferred_element_type=F32),
        jnp.dot(ym_ref[...], wbm_ref[...], preferred_element_type=F32),
        jnp.dot(yx_ref[...], wbx_ref[...], preferred_element_type=F32),
    )
    merged = jnp.zeros_like(x)
    for i, br in enumerate(branches):
        route_phases()
        logits = jnp.dot(xb, wg_ref[:, i * d:(i + 1) * d], preferred_element_type=F32) + bg_ref[:, i * d:(i + 1) * d]
        merged = merged + jax.nn.sigmoid(logits) * br
    route_phases()
    mix =jnp.dot(merged.astype(BF16), wo_ref[...], preferred_element_type=F32)
    for _ in route:
        pass
    x1 = _layer_norm(alpha * x + mix, g_ref[...], b_ref[...])
    out_ref[...] = x1
    packed_ref[...] = _pack_halves(x1)
    x1_prev[...] = x1


def _merge(alpha, x2, o_dil, lse_dil, y_moba, y_mem, wg, bg, wbd, wbm, wbx, wo, g, b, wr_hi_t, wr_lo_t, rbias_b,
           seq, tm=512):
    n, d = x2.shape
    tiles = seq // tm
    last = n // tm - 1
    cur = lambda i: jnp.minimum(i, last)
    row = lambda w: pl.BlockSpec((tm, w), lambda i: (cur(i), 0))
    full = lambda a: pl.BlockSpec(a.shape, lambda i: (0, 0))
    residue_major = lambda a: pl.BlockSpec((None, a.shape[1], tm // a.shape[1], LANES),
                                           lambda i: (cur(i) // tiles, 0, cur(i) % tiles, 0))
    slots = pl.BlockSpec((TOP_K, tm), lambda i: (0, jnp.maximum(i - 1, 0)))
    weights = (wg, bg, wbd, wbm, wbx, wo, g, b, wr_hi_t, wr_lo_t, rbias_b)
    tok = lambda dt: jax.ShapeDtypeStruct((TOP_K, n), dt)
    return pl.pallas_call(
        functools.partial(_merge_kernel, alpha),
        grid=(n // tm + 1,),
        in_specs=([row(d)] + [residue_major(a) for a in (*o_dil, *lse_dil)] + [row(W_MOBA), row(W_MEM)]
                  + [full(a) for a in weights]),
        out_specs=[row(d), row(PACKED), slots, slots, slots, pl.BlockSpec((N_EXPERTS, LANES), lambda i: (0, 0))],
        out_shape=[jax.ShapeDtypeStruct((n, d), F32), jax.ShapeDtypeStruct((n, PACKED), jnp.uint32),
                   tok(jnp.int32), tok(F32), tok(jnp.int32), jax.ShapeDtypeStruct((N_EXPERTS, LANES), jnp.int32)],
        scratch_shapes=[pltpu.VMEM((2 * len(DIL_PAIRS), tm, LANES), F32), pltpu.VMEM((N_EXPERTS, LANES), F32),
                        pltpu.VMEM((tm, d), F32)],
        compiler_params=_cparams("arbitrary"),
        name="merge_outproj_ln1_route",
    )(x2, *o_dil, *lse_dil, y_moba, y_mem, *weights)


def _token_mixer(x2, mem, positions, w_in, w_mem_kv, w_gate, b_gate, w_br_dil, w_br_moba, w_br_mem,
                 w_out, ln1_g, ln1_b, w_router, router_bias, batch, seq, alpha):
    wr_t = w_router.T
    wr_hi = wr_t.astype(BF16)
    wr_lo = (wr_t - wr_hi.astype(F32)).astype(BF16)
    rbias_b = jnp.broadcast_to(router_bias.astype(F32)[:, None], (N_EXPERTS, LANES))
    cos_t, sin_t = _rope_tables(positions)
    (q_dil, k_dil, v_dil), (qm, km, vm, qx) = _in_projection(x2, w_in.astype(BF16), cos_t, sin_t, batch, seq)
    o_dil, lse_dil = [], []
    for g in range(len(DIL_PAIRS)):
        o, lse = _dilated_attention(q_dil[g], k_dil[g], v_dil[g])
        o_dil.append(o)
        lse_dil.append(lse)
    qm3, km3, vm3 = (t.reshape(batch, seq, W_MOBA) for t in (qm, km, vm))
    kmh, kml = _moba_kmean(km3)
    y_moba = _moba_attention(qm3, km3, vm3, kmh, kml).reshape(batch * seq, W_MOBA)
    k_mem, v_mem = _mem_kv(mem, w_mem_kv.astype(BF16))
    y_mem = _mem_attention(qx.reshape(batch, seq, W_MEM), k_mem, v_mem).reshape(batch * seq, W_MEM)
    return _merge(alpha, x2, o_dil, lse_dil, y_moba, y_mem,
                  w_gate.astype(BF16), b_gate.reshape(1, -1), w_br_dil.astype(BF16),
                  w_br_moba.astype(BF16), w_br_mem.astype(BF16), w_out.astype(BF16),
                  ln1_g.reshape(1, -1), ln1_b.reshape(1, -1), wr_hi, wr_lo, rbias_b, seq)


EXPERT_ROWS = 256
EXPERT_GROUP = 4
EXPERT_RING = 2 * EXPERT_GROUP + 1
TOKEN_TILE = 512
ROUTE_SPLIT = 2
PACKED = 512
SC_WINDOW = 128


def _first_index_of_max(v, iota_f, size):
    m = jnp.max(v, axis=0, keepdims=True)
    idx = jnp.min(jnp.where(v == m, iota_f, float(size)), axis=0, keepdims=True)
    return m, idx


def _route_tile(x, counted, *refs):
    part = x.shape[0] // ROUTE_SPLIT
    for i in range(ROUTE_SPLIT):
        yield from _route_part(x[i * part:(i + 1) * part, :], counted, slice(i * part, (i + 1) * part), *refs)


def _route_part(x, counted, cols, wh_ref, wl_ref, bias_ref, eidx_ref, gw_ref, rank_ref, cnt_ref, carry):
    tm = x.shape[0]
    xh = x.astype(BF16)
    xl = (x - xh.astype(F32)).astype(BF16)
    wh = wh_ref[...]
    logits = _dot_nt(wh, xh) + _dot_nt(wh, xl) + _dot_nt(wl_ref[...], xh)
    yield
    scores = jax.nn.sigmoid(logits)
    biased = scores + bias_ref[...][:, :1]

    giota = lax.broadcasted_iota(jnp.int32, (GROUP_SIZE, tm), 0).astype(F32)
    group_scores = []
    for g in range(N_GROUPS):
        slab = biased[g * GROUP_SIZE:(g + 1) * GROUP_SIZE, :]
        m1, i1 = _first_index_of_max(slab, giota, GROUP_SIZE)
        m2 = jnp.max(jnp.where(giota == i1, -jnp.inf, slab), axis=0, keepdims=True)
        group_scores.append(m1 + m2)
    gs = jnp.concatenate(group_scores, axis=0)
    gidx = lax.broadcasted_iota(jnp.int32, (N_GROUPS, tm), 0)
    beaten = jnp.zeros((N_GROUPS, tm), jnp.int32)
    for gp in range(N_GROUPS):
        row = gs[gp:gp + 1, :]
        beaten = beaten + jnp.where((row > gs) | ((row == gs) & (gidx > gp)), 1, 0)
    keep = beaten < TOPK_GROUPS

    ahead = jnp.zeros((1, tm), jnp.int32)
    slab_b = [jnp.full((GROUP_SIZE, tm), -jnp.inf, F32) for _ in range(TOPK_GROUPS)]
    slab_s = [jnp.zeros((GROUP_SIZE, tm), F32) for _ in range(TOPK_GROUPS)]
    slab_g = [jnp.zeros((1, tm), F32) for _ in range(TOPK_GROUPS)]
    place = {}
    for g in range(N_GROUPS):
        kept = keep[g:g + 1, :]
        rows = slice(g * GROUP_SIZE, (g + 1) * GROUP_SIZE)
        for j in range(min(g, TOPK_GROUPS - 1) + 1):
            here = kept & (ahead == j)
            place[j, g] = here
            slab_b[j] = jnp.where(here, biased[rows, :], slab_b[j])
            slab_s[j] = jnp.where(here, scores[rows, :], slab_s[j])
            slab_g[j] = jnp.where(here, float(g), slab_g[j])
        ahead = ahead + jnp.where(kept, 1, 0)
    masked = jnp.concatenate(slab_b, axis=0)
    kept_scores = jnp.concatenate(slab_s, axis=0)
    yield

    n_kept = TOPK_GROUPS * GROUP_SIZE
    ciota = lax.broadcasted_iota(jnp.int32, (n_kept, tm), 0).astype(F32)
    chosen = jnp.zeros((n_kept, tm), F32)
    cidx_rows, idx_rows, gw_rows = [], [], []
    for k in range(TOP_K):
        _, cidx = _first_index_of_max(masked, ciota, n_kept)
        hit = ciota == cidx
        gw_rows.append(jnp.sum(jnp.where(hit, kept_scores, 0.0), axis=0, keepdims=True))
        masked = jnp.where(hit, -jnp.inf, masked)
        chosen = jnp.where(hit, 1.0, chosen)
        slab = sum(jnp.where(cidx >= float(j * GROUP_SIZE), 1.0, 0.0) for j in range(1, TOPK_GROUPS))
        group = sum(jnp.where(slab == float(j), slab_g[j], 0.0) for j in range(TOPK_GROUPS))
        cidx_rows.append(cidx)
        idx_rows.append((group - slab) * GROUP_SIZE + cidx)
        if k % 2 == 1:
            yield
    idx8 = jnp.concatenate(idx_rows, axis=0)
    gw8 = jnp.concatenate(gw_rows, axis=0)
    gw_ref[:, cols] = gw8 / jnp.sum(gw8, axis=0, keepdims=True) * ROUTED_SCALE
    eidx_ref[:, cols] = idx8.astype(jnp.int32)

    def slabs_to_groups(compact):
        out = []
        for g in range(N_GROUPS):
            acc = jnp.zeros((GROUP_SIZE, tm), F32)
            for j in range(min(g, TOPK_GROUPS - 1) + 1):
                acc = jnp.where(place[j, g], compact[j * GROUP_SIZE:(j + 1) * GROUP_SIZE, :], acc)
            out.append(acc)
        return jnp.concatenate(out, axis=0)

    def groups_to_slabs(full):
        out = []
        for j in range(TOPK_GROUPS):
            acc = jnp.zeros((GROUP_SIZE, tm), F32)
            for g in range(j, N_GROUPS):
                acc = jnp.where(place[j, g], full[g * GROUP_SIZE:(g + 1) * GROUP_SIZE, :], acc)
            out.append(acc)
        return jnp.concatenate(out, axis=0)

    onehot = slabs_to_groups(chosen)
    t_row = lax.broadcasted_iota(jnp.int32, (tm, tm), 0)
    t_col = lax.broadcasted_iota(jnp.int32, (tm, tm), 1)
    earlier = jnp.where(t_row < t_col, 1.0, 0.0).astype(BF16)
    prefix = jnp.dot(onehot.astype(BF16), earlier, preferred_element_type=F32)
    base = carry[...]
    prefix = groups_to_slabs(prefix + jnp.concatenate([base] * (tm // LANES), axis=1))
    rank_rows = [jnp.sum(jnp.where(ciota == cidx, prefix, 0.0), axis=0, keepdims=True) for cidx in cidx_rows]
    rank_ref[:, cols] = jnp.concatenate(rank_rows, axis=0).astype(jnp.int32)
    total = base + jnp.where(counted, jnp.sum(onehot, axis=1, keepdims=True), 0.0)
    carry[...] = total
    cnt_ref[...] = total.astype(jnp.int32)


def _dest_kernel(eidx_ref, rank_ref, start_ref, dest_ref):
    eidx = eidx_ref[...]
    tm = eidx.shape[1]
    eiota = lax.broadcasted_iota(jnp.int32, (N_EXPERTS, tm), 0)
    start = start_ref[...][:, :1]
    rows = [jnp.sum(jnp.where(eiota == eidx[k:k + 1, :], start, 0.0), axis=0, keepdims=True)
            for k in range(TOP_K)]
    dest_ref[...] = jnp.concatenate(rows, axis=0).astype(jnp.int32) + rank_ref[...]


def _dest_rows(eidx, rank, start_b):
    n = eidx.shape[1]
    nt = n // TOKEN_TILE
    return pl.pallas_call(
        _dest_kernel,
        grid=(nt,),
        in_specs=[pl.BlockSpec((TOP_K, TOKEN_TILE), lambda i: (0, i)),
                  pl.BlockSpec((TOP_K, TOKEN_TILE), lambda i: (0, i)),
                  pl.BlockSpec((N_EXPERTS, LANES), lambda i: (0, 0))],
        out_specs=pl.BlockSpec((TOP_K, TOKEN_TILE), lambda i: (0, i)),
        out_shape=jax.ShapeDtypeStruct((TOP_K, n), jnp.int32),
        compiler_params=_cparams("parallel"),
        name="moe_dest_rows",
    )(eidx, rank, start_b)


def _pack_halves(v):
    return pltpu.pack_elementwise([v[:, :PACKED], v[:, PACKED:]], packed_dtype=BF16)


def _unpack_half(p, index):
    return pltpu.unpack_elementwise(p, index=index, packed_dtype=BF16, unpacked_dtype=F32)


def _sc_mesh():
    return plsc.VectorSubcoreMesh(core_axis_name="core", subcore_axis_name="subcore")


def _sc_move_rows(src, src_idx, dst_idx, out_rows, name):
    width = src.shape[1]
    count = src_idx.shape[1]

    @functools.partial(pl.kernel, out_type=jax.ShapeDtypeStruct((out_rows, width), src.dtype),
                       mesh=_sc_mesh(), scratch_types=[pltpu.VMEM((SC_WINDOW, width), src.dtype)], name=name)
    def move(src_hbm, sidx_hbm, didx_hbm, out_hbm, rows_vmem):
        def body(sidx_vmem, didx_vmem):
            pltpu.sync_copy(src_hbm.at[sidx_vmem.at[0]], rows_vmem)
            pltpu.sync_copy(rows_vmem, out_hbm.at[didx_vmem.at[0]])

        idx_spec = pl.BlockSpec((1, SC_WINDOW), index_map=lambda i: (0, i))
        pltpu.emit_pipeline(
            body,
            grid=(count // SC_WINDOW,),
            in_specs=[idx_spec, idx_spec],
            out_specs=[],
            core_axis_name=("core", "subcore"),
            dimension_semantics=(pltpu.PARALLEL,),
        )(sidx_hbm, didx_hbm)

    return move(src, src_idx, dst_idx)


def _sc_dispatch_rows(src, token_idx, dest, out_rows):
    n, width = src.shape
    slots = dest.shape[0]

    @functools.partial(pl.kernel, out_type=jax.ShapeDtypeStruct((out_rows, width), src.dtype),
                       mesh=_sc_mesh(), scratch_types=[pltpu.VMEM((SC_WINDOW, width), src.dtype)],
                       name="moe_dispatch_sc")
    def dispatch(src_hbm, tidx_hbm, dest_hbm, out_hbm, rows_vmem):
        def body(tidx_vmem, dest_vmem):
            pltpu.sync_copy(src_hbm.at[tidx_vmem.at[0]], rows_vmem)
            for k in range(slots):
                pltpu.sync_copy(rows_vmem, out_hbm.at[dest_vmem.at[k]])

        pltpu.emit_pipeline(
            body,
            grid=(n // SC_WINDOW,),
            in_specs=[pl.BlockSpec((1, SC_WINDOW), index_map=lambda i: (0, i)),
                      pl.BlockSpec((slots, SC_WINDOW), index_map=lambda i: (0, i))],
            out_specs=[],
            core_axis_name=("core", "subcore"),
            dimension_semantics=(pltpu.PARALLEL,),
        )(tidx_hbm, dest_hbm)

    return dispatch(src, token_idx, dest)


def _expert_kernel(first_blk_ref, count_ref, total_ref, xs_hbm, wg_ref, wu_ref, wd_ref, ys_hbm,
                   wg_b, wu_b, wd_b, x_ring, y_ring, in_sem, out_sem):
    e = pl.program_id(0)
    total = total_ref[0]
    first = first_blk_ref[e]
    count = count_ref[e]
    nblk = (count + EXPERT_ROWS - 1) // EXPERT_ROWS
    spare_block = ys_hbm.shape[0] // EXPERT_ROWS - EXPERT_RING

    def rows_of(g):
        return pl.ds(pl.multiple_of(g * EXPERT_ROWS, EXPERT_ROWS), EXPERT_ROWS)

    def in_copy(g):
        slot = g % EXPERT_RING
        return pltpu.make_async_copy(xs_hbm.at[rows_of(g)], x_ring.at[slot], in_sem.at[slot])

    def out_copy(g, slot):
        return pltpu.make_async_copy(y_ring.at[slot], ys_hbm.at[rows_of(g)], out_sem.at[slot])

    @pl.when(e == 0)
    def _():
        y_ring[...] = jnp.zeros_like(y_ring)
        for g in range(EXPERT_RING - 1):
            in_copy(g).start()
        for slot in range(EXPERT_RING):
            out_copy(spare_block + slot, slot).start()

    @pl.when(nblk > 0)
    def _():
        wg_b[...] = wg_ref[...].astype(BF16)
        wu_b[...] = wu_ref[...].astype(BF16)
        wd_b[...] = wd_ref[...].astype(BF16)

    def blocks(js):
        gs = [first + j for j in js]
        for g in gs:
            in_copy(g).wait()
        p = jnp.concatenate([x_ring[g % EXPERT_RING] for g in gs], axis=0)
        live = lax.broadcasted_iota(jnp.int32, p.shape, 0) + js[0] * EXPERT_ROWS < count
        lo = jnp.where(live, _unpack_half(p, 0), 0.0).astype(BF16)
        hi = jnp.where(live, _unpack_half(p, 1), 0.0).astype(BF16)

        def up(w):
            return (jnp.dot(lo, w[:PACKED, :], preferred_element_type=F32)
                    + jnp.dot(hi, w[PACKED:, :], preferred_element_type=F32))

        hid = (jax.nn.silu(up(wg_b)) * up(wu_b)).astype(BF16)
        y = _pack_halves(jnp.dot(hid, wd_b[...], preferred_element_type=F32))
        ys = [y[i * EXPERT_ROWS:(i + 1) * EXPERT_ROWS, :] for i in range(len(gs))]
        for g in gs:
            in_copy(g + EXPERT_RING - 1).start()
        for g in gs:
            out_copy(g, g % EXPERT_RING).wait()
        for g, y in zip(gs, ys):
            y_ring[g % EXPERT_RING] = y
        for g in gs:
            out_copy(g, g % EXPERT_RING).start()

    def group(i, carry):
        blocks([EXPERT_GROUP * i + b for b in range(EXPERT_GROUP)])
        return carry

    lax.fori_loop(0, nblk // EXPERT_GROUP, group, 0)
    left = nblk % EXPERT_GROUP

    @pl.when(left >= 2)
    def _():
        blocks([nblk - left, nblk - left + 1])

    @pl.when(left % 2 == 1)
    def _():
        blocks([nblk - 1])

    @pl.when(e == pl.num_programs(0) - 1)
    def _():
        for slot in range(EXPERT_RING):
            out_copy(spare_block + slot, slot).wait()
        for g in range(EXPERT_RING - 1):
            in_copy(total + g).wait()


def _expert_ffn(xs, first_blk, counts, total, w_e_gate, w_e_up, w_e_down):
    rows = xs.shape[0] + EXPERT_ROWS
    n_experts, d = w_e_gate.shape[0], w_e_gate.shape[1]
    w_map = lambda e, fb, ct, tt: (e, 0, 0)
    ring = pltpu.VMEM((EXPERT_RING, EXPERT_ROWS, PACKED), jnp.uint32)
    return pl.pallas_call(
        _expert_kernel,
        grid_spec=pltpu.PrefetchScalarGridSpec(
            num_scalar_prefetch=3,
            grid=(n_experts,),
            in_specs=[pl.BlockSpec(memory_space=pl.ANY),
                      pl.BlockSpec((None, d, D_EXPERT), w_map),
                      pl.BlockSpec((None, d, D_EXPERT), w_map),
                      pl.BlockSpec((None, D_EXPERT, d), w_map)],
            out_specs=pl.BlockSpec(memory_space=pl.ANY),
            scratch_shapes=[pltpu.VMEM((d, D_EXPERT), BF16), pltpu.VMEM((d, D_EXPERT), BF16),
                            pltpu.VMEM((D_EXPERT, d), BF16), ring, ring,
                            pltpu.SemaphoreType.DMA((EXPERT_RING,)), pltpu.SemaphoreType.DMA((EXPERT_RING,))],
        ),
        out_shape=jax.ShapeDtypeStruct((rows, PACKED), jnp.uint32),
        compiler_params=_cparams("arbitrary"),
        name="moe_expert_ffn",
    )(first_blk, counts, total, xs, w_e_gate, w_e_up, w_e_down)


def _combine_kernel(alpha, x_ref, gw_ref, ys_ref, wsg_ref, wsu_ref, wsd_ref, g_ref, b_ref, out_ref):
    x = x_ref[...]
    xb = x.astype(BF16)
    hid = (jax.nn.silu(jnp.dot(xb, wsg_ref[...], preferred_element_type=F32))
           * jnp.dot(xb, wsu_ref[...], preferred_element_type=F32)).astype(BF16)
    shared = jnp.dot(hid, wsd_ref[...], preferred_element_type=F32)
    gw = gw_ref[...]
    lo = jnp.zeros((TOKEN_TILE, PACKED), F32)
    hi = jnp.zeros((TOKEN_TILE, PACKED), F32)
    for k in range(TOP_K):
        p = ys_ref[k]
        w = gw[:, k:k + 1]
        lo = lo + w * _unpack_half(p, 0)
        hi = hi + w * _unpack_half(p, 1)
    routed = jnp.concatenate([lo, hi], axis=1)
    out_ref[...] = _layer_norm(alpha * x + (routed + shared), g_ref[...], b_ref[...])


def _combine(alpha, x1, gw_t, ys_tok, wsg, wsu, wsd, g, b):
    n, d = x1.shape
    full = lambda a: pl.BlockSpec(a.shape, lambda i: (0, 0))
    weights = (wsg, wsu, wsd, g, b)
    return pl.pallas_call(
        functools.partial(_combine_kernel, alpha),
        grid=(n // TOKEN_TILE,),
        in_specs=[pl.BlockSpec((TOKEN_TILE, d), lambda i: (i, 0)),
                  pl.BlockSpec((TOKEN_TILE, TOP_K), lambda i: (i, 0)),
                  pl.BlockSpec((TOP_K, TOKEN_TILE, PACKED), lambda i: (0, i, 0))] + [full(a) for a in weights],
        out_specs=pl.BlockSpec((TOKEN_TILE, d), lambda i: (i, 0)),
        out_shape=jax.ShapeDtypeStruct((n, d), F32),
        compiler_params=_cparams("parallel"),
        name="moe_combine_shared_ln2",
    )(x1, gw_t, ys_tok, *weights)


def _moe_layer(x1, x1_packed, eidx, gw, rank, cnt, w_e_gate, w_e_up, w_e_down, w_s_gate, w_s_up, w_s_down,
               ln2_g, ln2_b, alpha):
    n, d = x1.shape

    counts = cnt[:, 0]
    padded = (counts + EXPERT_ROWS - 1) // EXPERT_ROWS * EXPERT_ROWS
    seg_end = jnp.cumsum(padded)
    seg_start = seg_end - padded
    rows = n * TOP_K + N_EXPERTS * EXPERT_ROWS
    first_blk = (seg_start // EXPERT_ROWS).astype(jnp.int32)
    total_blk = (seg_end[-1:] // EXPERT_ROWS).astype(jnp.int32)
    start_b = jnp.broadcast_to(seg_start.astype(F32)[:, None], (N_EXPERTS, LANES))

    dest = _dest_rows(eidx, rank, start_b)
    dest_flat = dest.reshape(1, TOP_K * n)
    assign = jnp.arange(TOP_K * n, dtype=jnp.int32).reshape(1, TOP_K * n)
    xs = _sc_dispatch_rows(x1_packed, assign[:, :n], dest, rows + (EXPERT_RING - 1) * EXPERT_ROWS)
    ys = _expert_ffn(xs, first_blk, counts, total_blk, w_e_gate, w_e_up, w_e_down)
    ys_tok = _sc_move_rows(ys, dest_flat, assign, TOP_K * n, "moe_gather_sc").reshape(TOP_K, n, PACKED)
    return _combine(alpha, x1, gw.T, ys_tok, w_s_gate.astype(BF16), w_s_up.astype(BF16),
                    w_s_down.astype(BF16), ln2_g.reshape(1, -1), ln2_b.reshape(1, -1))


def kernel(x, mem, positions, w_in, w_mem_kv, w_gate, b_gate, w_br_dil, w_br_moba, w_br_mem, w_out, ln1_g, ln1_b, w_router, router_bias, w_e_gate, w_e_up, w_e_down, w_s_gate, w_s_up, w_s_down, ln2_g, ln2_b):
    batch, seq, d = x.shape
    depth = w_in.shape[0]
    alpha = (2.0 * depth) ** 0.25
    h = x.reshape(batch * seq, d)
    for l in range(depth):
        h, h_packed, eidx, gw, rank, cnt = _token_mixer(
            h, mem, positions, w_in[l], w_mem_kv[l], w_gate[l], b_gate[l], w_br_dil[l], w_br_moba[l], w_br_mem[l],
            w_out[l], ln1_g[l], ln1_b[l], w_router[l], router_bias[l], batch, seq, alpha)
        h = _moe_layer(h, h_packed, eidx, gw, rank, cnt, w_e_gate[l], w_e_up[l], w_e_down[l],
                       w_s_gate[l], w_s_up[l], w_s_down[l], ln2_g[l], ln2_b[l], alpha)
    return h.reshape(batch, seq, d)
```

```python
import functools

import jax
import jax.numpy as jnp
from jax import lax
from jax.experimental import pallas as pl
from jax.experimental.pallas import tpu as pltpu
from jax.experimental.pallas import tpu_sc as plsc

F32 = jnp.float32
BF16 = jnp.bfloat16

LANES = 128
VMEM_LIMIT_BYTES = 48 * 1024 * 1024

HEAD_DIM = 64
ROPE_DIM = HEAD_DIM // 4
ROPE_HALF = ROPE_DIM // 2
ROPE_THETA = 500000.0
DIL_PAIRS = ((128, 1), (512, 4), (2048, 16))
BAND = 128
DIL_TILE = 1024
MEM_CHUNK = 512
W_DIL = 384
W_MOBA = 384
W_MEM = 256
MOBA_BLOCK = 256
MOBA_TOPK = 3
N_EXPERTS = 256
N_GROUPS = 8
GROUP_SIZE = N_EXPERTS // N_GROUPS
TOPK_GROUPS = 4
TOP_K = 8
D_EXPERT = 256
ROUTED_SCALE = 2.5
LN_EPS = 1e-5
NEG = -1e30
QK_SCALE = HEAD_DIM ** -0.5


def _cparams(*sem):
    return pltpu.CompilerParams(dimension_semantics=sem, vmem_limit_bytes=VMEM_LIMIT_BYTES)


def _dot_nt(a, b):
    return lax.dot_general(a, b, (((1,), (1,)), ((), ())), preferred_element_type=F32)


def _layer_norm(v, g, b):
    mu = jnp.mean(v, axis=-1, keepdims=True)
    c = v - mu
    var = jnp.mean(c * c, axis=-1, keepdims=True)
    return c * lax.rsqrt(var + LN_EPS) * g + b


_IN_SECTIONS = (
    (0, W_DIL, True, True),
    (W_DIL, W_DIL, True, False),
    (2 * W_DIL, W_DIL, False, False),
    (3 * W_DIL, W_MOBA, True, True),
    (3 * W_DIL + W_MOBA, W_MOBA, True, False),
    (3 * W_DIL + 2 * W_MOBA, W_MOBA, False, False),
    (3 * W_DIL + 3 * W_MOBA, W_MEM, False, True),
)


N_DIL_SECTIONS = 3
MATMUL_COLS = 512


def _inproj_kernel(x_ref, w_ref, cos_ref, sin_ref, *refs):
    n_dil = N_DIL_SECTIONS * len(DIL_PAIRS)
    dil_refs, flat_refs, stage = refs[:n_dil], refs[n_dil:-1], refs[-1]
    xb = x_ref[...].astype(BF16)
    cos = cos_ref[...]
    sin = sin_ref[...]
    tm = xb.shape[0]
    lane = lax.broadcasted_iota(jnp.int32, (tm, LANES), 1)
    first_half = (lane % ROPE_DIM) < ROPE_HALF

    def rope(t):
        partner = jnp.where(first_half, pltpu.roll(t, LANES - ROPE_HALF, 1), pltpu.roll(t, ROPE_HALF, 1))
        return t * cos + partner * sin

    chunk_of = {}
    for sec, (off, width, _, _) in enumerate(_IN_SECTIONS):
        for c in range(width // LANES):
            chunk_of[off // LANES + c] = (sec, c)
    per_group = MATMUL_COLS // LANES
    for grp in range(w_ref.shape[1] // MATMUL_COLS):
        acc = jnp.dot(xb, w_ref[:, grp * MATMUL_COLS:(grp + 1) * MATMUL_COLS], preferred_element_type=F32)
        for cc in range(per_group):
            sec, c = chunk_of[grp * per_group + cc]
            _, _, roped, scaled = _IN_SECTIONS[sec]
            t = acc[:, cc * LANES:(cc + 1) * LANES]
            if roped:
                t = rope(t)
            if scaled:
                t = t * QK_SCALE
            if sec >= N_DIL_SECTIONS:
                flat_refs[sec - N_DIL_SECTIONS][:, c * LANES:(c + 1) * LANES] = t.astype(BF16)
                continue
            o_ref = dil_refs[sec * len(DIL_PAIRS) + c]
            dilation = DIL_PAIRS[c][1]
            if dilation == 1:
                o_ref[0] = t.astype(BF16)
                continue
            slot = sec * len(DIL_PAIRS) + c
            stage[slot] = t
            for r in range(dilation):
                o_ref[r] = stage[slot, pl.ds(r, tm // dilation, stride=dilation), :].astype(BF16)


def _in_projection(x2, w_in_b, cos_t, sin_t, batch, seq, tm=1024):
    n, d = x2.shape
    w_total = w_in_b.shape[1]
    tiles = seq // tm
    dil_specs, dil_shapes = [], []
    for _ in range(N_DIL_SECTIONS):
        for _, dilation in DIL_PAIRS:
            dil_specs.append(pl.BlockSpec((None, dilation, tm // dilation, LANES),
                                          lambda i: (i // tiles, 0, i % tiles, 0)))
            dil_shapes.append(jax.ShapeDtypeStruct((batch, dilation, seq // dilation, LANES), BF16))
    widths = [s[1] for s in _IN_SECTIONS[N_DIL_SECTIONS:]]
    outs = pl.pallas_call(
        _inproj_kernel,
        grid=(n // tm,),
        in_specs=[
            pl.BlockSpec((tm, d), lambda i: (i, 0)),
            pl.BlockSpec((d, w_total), lambda i: (0, 0)),
            pl.BlockSpec((tm, LANES), lambda i: (i, 0)),
            pl.BlockSpec((tm, LANES), lambda i: (i, 0)),
        ],
        out_specs=dil_specs + [pl.BlockSpec((tm, w), lambda i: (i, 0)) for w in widths],
        out_shape=dil_shapes + [jax.ShapeDtypeStruct((n, w), BF16) for w in widths],
        scratch_shapes=[pltpu.VMEM((N_DIL_SECTIONS * len(DIL_PAIRS), tm, LANES), F32)],
        compiler_params=_cparams("parallel"),
        name="in_proj_rope",
    )(x2, w_in_b, cos_t, sin_t)
    n_dil = len(dil_specs)
    ng = len(DIL_PAIRS)
    qkv_dil = [outs[s * ng:(s + 1) * ng] for s in range(N_DIL_SECTIONS)]
    return qkv_dil, outs[n_dil:]


def _rope_tables(positions):
    lane = jnp.arange(LANES)
    rotary = (lane % HEAD_DIM) < ROPE_DIM
    inv_freq = ROPE_THETA ** (-(lane % ROPE_HALF).astype(F32) / ROPE_HALF)
    ang = positions.reshape(-1).astype(F32)[:, None] * jnp.where(rotary, inv_freq, 0.0)
    sign = jnp.where((lane % ROPE_DIM) < ROPE_HALF, -1.0, 1.0)
    return jnp.cos(ang), jnp.sin(ang) * sign


def _head_mask(shape, h):
    lane = lax.broadcasted_iota(jnp.int32, shape, 1)
    return (lane // HEAD_DIM) == h


def _dil_kernel(q_ref, kp_ref, k_ref, vp_ref, v_ref, o_ref, lse_ref):
    i = pl.program_id(2)
    n_res, tq = q_ref.shape[0], q_ref.shape[1]
    qi = lax.broadcasted_iota(jnp.int32, (BAND, 2 * BAND), 0)
    kj = lax.broadcasted_iota(jnp.int32, (BAND, 2 * BAND), 1)
    dist = qi + BAND - kj
    band = (dist >= 0) & (dist <= BAND)
    first = _head_mask((BAND, LANES), 0)
    for r in range(n_res):
        for j in range(tq // BAND):
            rows = slice(j * BAND, (j + 1) * BAND)
            q = q_ref[r, rows, :]
            if j == 0:
                k_prev, v_prev = kp_ref[r], vp_ref[r]
                allowed = band & ((kj >= BAND) | (i > 0))
            else:
                prev_rows = slice((j - 1) * BAND, j * BAND)
                k_prev, v_prev = k_ref[r, prev_rows, :], v_ref[r, prev_rows, :]
                allowed = band
            k = jnp.concatenate([k_prev, k_ref[r, rows, :]], axis=0)
            v = jnp.concatenate([v_prev, v_ref[r, rows, :]], axis=0)
            outs, lses = [], []
            for h in range(2):
                qh = jnp.where(_head_mask(q.shape, h), q, jnp.zeros_like(q))
                s = jnp.where(allowed, _dot_nt(qh, k), NEG)
                m = jnp.max(s, axis=1, keepdims=True)
                p = jnp.exp(s - m)
                l = jnp.sum(p, axis=1, keepdims=True)
                outs.append(jnp.dot(p.astype(BF16), v, preferred_element_type=F32) / l)
                lses.append(jnp.broadcast_to(m + jnp.log(l), (BAND, LANES)))
            o_ref[r, rows, :] = jnp.where(first, outs[0], outs[1]).astype(o_ref.dtype)
            lse_ref[r, rows, :] = jnp.where(first, lses[0], lses[1])


def _dilated_attention(q4, k4, v4):
    batch, dilation, steps, _ = q4.shape
    tq = min(steps, DIL_TILE)
    n_res = min(dilation, DIL_TILE // tq)
    per_tile = tq // BAND
    cur = pl.BlockSpec((None, n_res, tq, LANES), lambda b, r, i: (b, r, i, 0))
    prev = pl.BlockSpec((None, n_res, BAND, LANES), lambda b, r, i: (b, r, jnp.maximum(i * per_tile - 1, 0), 0))
    return pl.pallas_call(
        _dil_kernel,
        grid=(batch, dilation // n_res, steps // tq),
        in_specs=[cur, prev, cur, prev, cur],
        out_specs=[cur, cur],
        out_shape=[jax.ShapeDtypeStruct(q4.shape, BF16), jax.ShapeDtypeStruct(q4.shape, F32)],
        compiler_params=_cparams("parallel", "parallel", "arbitrary"),
        name=f"dilated_attn_d{dilation}",
    )(q4, k4, k4, v4, v4)


def _kmean_kernel(k_ref, hi_ref, lo_ref):
    k = k_ref[...].astype(F32)
    s, w = k.shape
    mean = jnp.sum(k.reshape(s // MOBA_BLOCK, MOBA_BLOCK, w), axis=1) / MOBA_BLOCK
    hi = mean.astype(BF16)
    hi_ref[...] = hi
    lo_ref[...] = (mean - hi.astype(F32)).astype(BF16)


def _moba_kmean(km3):
    batch, seq, w = km3.shape
    nb = seq // MOBA_BLOCK
    return pl.pallas_call(
        _kmean_kernel,
        grid=(batch,),
        in_specs=[pl.BlockSpec((None, seq, w), lambda b: (b, 0, 0))],
        out_specs=[pl.BlockSpec((None, nb, w), lambda b: (b, 0, 0))] * 2,
        out_shape=[jax.ShapeDtypeStruct((batch, nb, w), BF16)] * 2,
        compiler_params=_cparams("parallel"),
        name="moba_kmean",
    )(km3)


def _moba_kernel(q_ref, k_ref, v_ref, kmh_ref, kml_ref, o_ref, qaug, kaug, vaug, m_scr, acc_scr):
    s = pl.program_id(2)
    nsteps = pl.num_programs(2)
    nb = kmh_ref.shape[0]
    tq = MOBA_BLOCK
    lane = lax.broadcasted_iota(jnp.int32, (tq, LANES), 1)
    blk = lax.broadcasted_iota(jnp.int32, (nb, tq), 0)
    slot = s % 2

    def select(step, dst_slot):
        for t in range(2):
            qi = step if t == 0 else nb - 1 - step
            q = q_ref[pl.ds(pl.multiple_of(qi * tq, tq), tq), :]
            for h in range(2):
                hm = _head_mask(q.shape, h)
                qh = jnp.where(hm, q, jnp.zeros_like(q))
                gate = _dot_nt(kmh_ref[...], qh) + _dot_nt(kml_ref[...], qh)
                cnt = jnp.zeros((nb, tq), jnp.int32)
                for jp in range(nb):
                    g_jp = gate[jp:jp + 1, :]
                    beats = (g_jp > gate) | ((g_jp == gate) & (blk > jp))
                    cnt = cnt + jnp.where(beats & (qi > jp), 1, 0)
                sel = ((blk < qi) & (cnt < MOBA_TOPK)) | (blk == qi)
                bias_t = jnp.where(sel, 0.0, NEG).astype(F32)
                spare = HEAD_DIM * (1 - h)
                pieces = [jnp.zeros((spare, tq), F32)] if spare else []
                pieces += [bias_t, jnp.zeros((LANES - spare - nb, tq), F32)]
                qaug[dst_slot, t, h] = jnp.where(hm, q, jnp.concatenate(pieces, axis=0).T.astype(BF16))

    @pl.when(s == 0)
    def _():
        for h in range(2):
            spare = HEAD_DIM * (1 - h)
            for j in range(nb):
                onehot = jnp.where(lane == spare + j, 1.0, 0.0).astype(BF16)
                kaug[h, j * tq:(j + 1) * tq, :] = jnp.where(
                    _head_mask((tq, LANES), h), k_ref[j * tq:(j + 1) * tq, :], onehot)
                vaug[h, j * tq:(j + 1) * tq, :] = jnp.where(
                    _head_mask((tq, LANES), h), v_ref[j * tq:(j + 1) * tq, :], jnp.ones((tq, LANES), BF16))
        select(0, 0)

    for t in range(2):
        for h in range(2):
            m_scr[t, h] = jnp.full((tq, LANES), NEG, F32)
            acc_scr[t, h] = jnp.zeros((tq, LANES), F32)

    row = lax.broadcasted_iota(jnp.int32, (tq, tq), 0)
    col = lax.broadcasted_iota(jnp.int32, (tq, tq), 1)
    causal_bias = jnp.where(col <= row, 0.0, NEG).astype(F32)
    starts = [pl.multiple_of(s * tq, tq), pl.multiple_of((nb - 1 - s) * tq, tq)]

    half = nb // 2
    for it in range(nb + 1):
        if it < 2:
            t, kstart = it, starts[it]
        elif it < 2 + half:
            t, kstart = 1, (it - 2) * tq
        else:
            i = it - 2 - half
            first = i < s
            t = jnp.where(first, 0, 1)
            kstart = pl.multiple_of(jnp.where(first, i, half + i - s) * tq, tq)
        for h in range(2):
            sc = _dot_nt(qaug[slot, t, h], kaug[h, pl.ds(kstart, tq), :])
            if it < 2:
                sc = sc + causal_bias
            m_old = m_scr[t, h]
            m_new = jnp.maximum(m_old, jnp.max(sc, axis=1, keepdims=True))
            p = jnp.exp(sc - jnp.concatenate([m_new, m_new], axis=1))
            acc_scr[t, h] = (jnp.exp(m_old - m_new) * acc_scr[t, h]
                             + jnp.dot(p.astype(BF16), vaug[h, pl.ds(kstart, tq), :], preferred_element_type=F32))
            m_scr[t, h] = m_new

    first_head = _head_mask((tq, LANES), 0)
    for t in range(2):
        acc = jnp.where(first_head, acc_scr[t, 0], acc_scr[t, 1])
        den = jnp.where(first_head, pltpu.roll(acc_scr[t, 0], HEAD_DIM, 1), pltpu.roll(acc_scr[t, 1], HEAD_DIM, 1))
        o_ref[pl.ds(starts[t], tq), :] = (acc / den).astype(o_ref.dtype)

    select(jnp.minimum(s + 1, nsteps - 1), 1 - slot)


def _moba_attention(qm3, km3, vm3, kmh, kml):
    batch, seq, w = qm3.shape
    nb = seq // MOBA_BLOCK
    assert nb % 2 == 0 and nb <= HEAD_DIM
    npair = w // LANES
    tq = MOBA_BLOCK
    seq_spec = pl.BlockSpec((None, seq, LANES), lambda b, p, i: (b, 0, p))
    km_spec = pl.BlockSpec((None, nb, LANES), lambda b, p, i: (b, 0, p))
    state = pltpu.VMEM((2, 2, tq, LANES), F32)
    return pl.pallas_call(
        _moba_kernel,
        grid=(batch, npair, nb // 2),
        in_specs=[seq_spec, seq_spec, seq_spec, km_spec, km_spec],
        out_specs=seq_spec,
        out_shape=jax.ShapeDtypeStruct((batch, seq, w), BF16),
        scratch_shapes=[pltpu.VMEM((2, 2, 2, tq, LANES), BF16), pltpu.VMEM((2, seq, LANES), BF16),
                        pltpu.VMEM((2, seq, LANES), BF16), state, state],
        compiler_params=_cparams("parallel", "parallel", "arbitrary"),
        name="moba_attn",
    )(qm3, km3, vm3, kmh, kml)


def _memkv_kernel(mem_ref, w_ref, k_ref, v_ref):
    kv = jnp.dot(mem_ref[...].astype(BF16), w_ref[...], preferred_element_type=F32)
    k_ref[...] = kv[:, :W_MEM].astype(BF16)
    v_ref[...] = kv[:, W_MEM:].astype(BF16)


def _mem_kv(mem, w_kv_b):
    batch, m, d = mem.shape
    return pl.pallas_call(
        _memkv_kernel,
        grid=(batch,),
        in_specs=[pl.BlockSpec((None, m, d), lambda b: (b, 0, 0)),
                  pl.BlockSpec((d, 2 * W_MEM), lambda b: (0, 0))],
        out_specs=[pl.BlockSpec((None, m, W_MEM), lambda b: (b, 0, 0))] * 2,
        out_shape=[jax.ShapeDtypeStruct((batch, m, W_MEM), BF16)] * 2,
        compiler_params=_cparams("parallel"),
        name="mem_kv_proj",
    )(mem, w_kv_b)


def _memattn_kernel(q_ref, k_ref, v_ref, o_ref):
    tq, w = q_ref.shape
    for pair in range(w // LANES):
        lanes = slice(pair * LANES, (pair + 1) * LANES)
        k = k_ref[:, lanes]
        v = v_ref[:, lanes]
        for c in range(tq // MEM_CHUNK):
            rows = slice(c * MEM_CHUNK, (c + 1) * MEM_CHUNK)
            q = q_ref[rows, lanes]
            outs = []
            for h in range(2):
                qh = jnp.where(_head_mask(q.shape, h), q, jnp.zeros_like(q))
                s = _dot_nt(qh, k)
                m = jnp.max(s, axis=1, keepdims=True)
                p = jnp.exp(s - m)
                l = jnp.sum(p, axis=1, keepdims=True)
                outs.append(jnp.dot(p.astype(BF16), v, preferred_element_type=F32) / l)
            o_ref[rows, lanes] = jnp.where(_head_mask(outs[0].shape, 0), outs[0], outs[1]).astype(o_ref.dtype)


def _mem_attention(qx3, k_mem, v_mem, tq=2048):
    batch, seq, w = qx3.shape
    m = k_mem.shape[1]
    tq = min(tq, seq)
    return pl.pallas_call(
        _memattn_kernel,
        grid=(batch, seq // tq),
        in_specs=[pl.BlockSpec((None, tq, w), lambda b, i: (b, i, 0)),
                  pl.BlockSpec((None, m, w), lambda b, i: (b, 0, 0)),
                  pl.BlockSpec((None, m, w), lambda b, i: (b, 0, 0))],
        out_specs=pl.BlockSpec((None, tq, w), lambda b, i: (b, i, 0)),
        out_shape=jax.ShapeDtypeStruct((batch, seq, w), BF16),
        compiler_params=_cparams("parallel", "parallel"),
        name="mem_attn",
    )(qx3, k_mem, v_mem)


def _merge_kernel(alpha, x_ref, o1_ref, o2_ref, o3_ref, l1_ref, l2_ref, l3_ref, ym_ref, yx_ref,
                  wg_ref, bg_ref, wbd_ref, wbm_ref, wbx_ref, wo_ref, g_ref, b_ref, wrh_ref, wrl_ref, rb_ref,
                  out_ref, packed_ref, eidx_ref, gw_ref, rank_ref, cnt_ref, stage, carry, x1_prev):
    step = pl.program_id(0)
    route_refs = (wrh_ref, wrl_ref, rb_ref, eidx_ref, gw_ref, rank_ref, cnt_ref, carry)

    @pl.when(step == 0)
    def _():
        carry[...] = jnp.zeros_like(carry)
        x1_prev[...] = jnp.zeros_like(x1_prev)

    route = _route_tile(x1_prev[...], step > 0, *route_refs)

    def route_phases():
        for _ in range(ROUTE_SPLIT):
            next(route)

    route_phases()
    x = x_ref[...]
    xb = x.astype(BF16)
    tm, d = x.shape

    def token_major(ref, slot):
        dilation = ref.shape[0]
        if dilation == 1:
            return ref[0].astype(F32)
        for r in range(dilation):
            stage[slot, pl.ds(r, tm // dilation, stride=dilation), :] = ref[r].astype(F32)
        return stage[slot]

    o1, o2, o3 = (token_major(r, s) for s, r in enumerate((o1_ref, o2_ref, o3_ref)))
    l1, l2, l3 = (token_major(r, s + 3) for s, r in enumerate((l1_ref, l2_ref, l3_ref)))
    mx = jnp.maximum(jnp.maximum(l1, l2), l3)
    e1, e2, e3 = jnp.exp(l1 - mx), jnp.exp(l2 - mx), jnp.exp(l3 - mx)
    y_dil = (e1 * o1 + e2 * o2 + e3 * o3) / (e1 + e2 + e3)
    route_phases()
    branches = (
        jnp.dot(y_dil.astype(BF16), wbd_ref[...], preferred_element_type=F32),
        jnp.dot(ym_ref[...], wbm_ref[...], preferred_element_type=F32),
        jnp.dot(yx_ref[...], wbx_ref[...], preferred_element_type=F32),
    )
    merged = jnp.zeros_like(x)
    for i, br in enumerate(branches):
        route_phases()
        logits = jnp.dot(xb, wg_ref[:, i * d:(i + 1) * d], preferred_element_type=F32) + bg_ref[:, i * d:(i + 1) * d]
        merged = merged + jax.nn.sigmoid(logits) * br
    route_phases()
    mix =jnp.dot(merged.astype(BF16), wo_ref[...], preferred_element_type=F32)
    for _ in route:
        pass
    x1 = _layer_norm(alpha * x + mix, g_ref[...], b_ref[...])
    out_ref[...] = x1
    packed_ref[...] = _pack_halves(x1)
    x1_prev[...] = x1


def _merge(alpha, x2, o_dil, lse_dil, y_moba, y_mem, wg, bg, wbd, wbm, wbx, wo, g, b, wr_hi_t, wr_lo_t, rbias_b,
           seq, tm=512):
    n, d = x2.shape
    tiles = seq // tm
    last = n // tm - 1
    cur = lambda i: jnp.minimum(i, last)
    row = lambda w: pl.BlockSpec((tm, w), lambda i: (cur(i), 0))
    full = lambda a: pl.BlockSpec(a.shape, lambda i: (0, 0))
    residue_major = lambda a: pl.BlockSpec((None, a.shape[1], tm // a.shape[1], LANES),
                                           lambda i: (cur(i) // tiles, 0, cur(i) % tiles, 0))
    slots = pl.BlockSpec((TOP_K, tm), lambda i: (0, jnp.maximum(i - 1, 0)))
    weights = (wg, bg, wbd, wbm, wbx, wo, g, b, wr_hi_t, wr_lo_t, rbias_b)
    tok = lambda dt: jax.ShapeDtypeStruct((TOP_K, n), dt)
    return pl.pallas_call(
        functools.partial(_merge_kernel, alpha),
        grid=(n // tm + 1,),
        in_specs=([row(d)] + [residue_major(a) for a in (*o_dil, *lse_dil)] + [row(W_MOBA), row(W_MEM)]
                  + [full(a) for a in weights]),
        out_specs=[row(d), row(PACKED), slots, slots, slots, pl.BlockSpec((N_EXPERTS, LANES), lambda i: (0, 0))],
        out_shape=[jax.ShapeDtypeStruct((n, d), F32), jax.ShapeDtypeStruct((n, PACKED), jnp.uint32),
                   tok(jnp.int32), tok(F32), tok(jnp.int32), jax.ShapeDtypeStruct((N_EXPERTS, LANES), jnp.int32)],
        scratch_shapes=[pltpu.VMEM((2 * len(DIL_PAIRS), tm, LANES), F32), pltpu.VMEM((N_EXPERTS, LANES), F32),
                        pltpu.VMEM((tm, d), F32)],
        compiler_params=_cparams("arbitrary"),
        name="merge_outproj_ln1_route",
    )(x2, *o_dil, *lse_dil, y_moba, y_mem, *weights)


def _token_mixer(x2, mem, positions, w_in, w_mem_kv, w_gate, b_gate, w_br_dil, w_br_moba, w_br_mem,
                 w_out, ln1_g, ln1_b, w_router, router_bias, batch, seq, alpha):
    wr_t = w_router.T
    wr_hi = wr_t.astype(BF16)
    wr_lo = (wr_t - wr_hi.astype(F32)).astype(BF16)
    rbias_b = jnp.broadcast_to(router_bias.astype(F32)[:, None], (N_EXPERTS, LANES))
    cos_t, sin_t = _rope_tables(positions)
    (q_dil, k_dil, v_dil), (qm, km, vm, qx) = _in_projection(x2, w_in.astype(BF16), cos_t, sin_t, batch, seq)
    o_dil, lse_dil = [], []
    for g in range(len(DIL_PAIRS)):
        o, lse = _dilated_attention(q_dil[g], k_dil[g], v_dil[g])
        o_dil.append(o)
        lse_dil.append(lse)
    qm3, km3, vm3 = (t.reshape(batch, seq, W_MOBA) for t in (qm, km, vm))
    kmh, kml = _moba_kmean(km3)
    y_moba = _moba_attention(qm3, km3, vm3, kmh, kml).reshape(batch * seq, W_MOBA)
    k_mem, v_mem = _mem_kv(mem, w_mem_kv.astype(BF16))
    y_mem = _mem_attention(qx.reshape(batch, seq, W_MEM), k_mem, v_mem).reshape(batch * seq, W_MEM)
    return _merge(alpha, x2, o_dil, lse_dil, y_moba, y_mem,
                  w_gate.astype(BF16), b_gate.reshape(1, -1), w_br_dil.astype(BF16),
                  w_br_moba.astype(BF16), w_br_mem.astype(BF16), w_out.astype(BF16),
                  ln1_g.reshape(1, -1), ln1_b.reshape(1, -1), wr_hi, wr_lo, rbias_b, seq)


EXPERT_ROWS = 256
EXPERT_GROUP = 4
EXPERT_RING = 2 * (EXPERT_GROUP + 1) + 1
TOKEN_TILE = 512
ROUTE_SPLIT = 2
PACKED = 512
SC_WINDOW = 128


def _first_index_of_max(v, iota_f, size):
    m = jnp.max(v, axis=0, keepdims=True)
    idx = jnp.min(jnp.where(v == m, iota_f, float(size)), axis=0, keepdims=True)
    return m, idx


def _route_tile(x, counted, *refs):
    part = x.shape[0] // ROUTE_SPLIT
    for i in range(ROUTE_SPLIT):
        yield from _route_part(x[i * part:(i + 1) * part, :], counted, slice(i * part, (i + 1) * part), *refs)


def _route_part(x, counted, cols, wh_ref, wl_ref, bias_ref, eidx_ref, gw_ref, rank_ref, cnt_ref, carry):
    tm = x.shape[0]
    xh = x.astype(BF16)
    xl = (x - xh.astype(F32)).astype(BF16)
    wh = wh_ref[...]
    logits = _dot_nt(wh, xh) + _dot_nt(wh, xl) + _dot_nt(wl_ref[...], xh)
    yield
    scores = jax.nn.sigmoid(logits)
    biased = scores + bias_ref[...][:, :1]

    giota = lax.broadcasted_iota(jnp.int32, (GROUP_SIZE, tm), 0).astype(F32)
    group_scores = []
    for g in range(N_GROUPS):
        slab = biased[g * GROUP_SIZE:(g + 1) * GROUP_SIZE, :]
        m1, i1 = _first_index_of_max(slab, giota, GROUP_SIZE)
        m2 = jnp.max(jnp.where(giota == i1, -jnp.inf, slab), axis=0, keepdims=True)
        group_scores.append(m1 + m2)
    gs = jnp.concatenate(group_scores, axis=0)
    gidx = lax.broadcasted_iota(jnp.int32, (N_GROUPS, tm), 0)
    beaten = jnp.zeros((N_GROUPS, tm), jnp.int32)
    for gp in range(N_GROUPS):
        row = gs[gp:gp + 1, :]
        beaten = beaten + jnp.where((row > gs) | ((row == gs) & (gidx > gp)), 1, 0)
    keep = beaten < TOPK_GROUPS

    ahead = jnp.zeros((1, tm), jnp.int32)
    slab_b = [jnp.full((GROUP_SIZE, tm), -jnp.inf, F32) for _ in range(TOPK_GROUPS)]
    slab_s = [jnp.zeros((GROUP_SIZE, tm), F32) for _ in range(TOPK_GROUPS)]
    slab_g = [jnp.zeros((1, tm), F32) for _ in range(TOPK_GROUPS)]
    place = {}
    for g in range(N_GROUPS):
        kept = keep[g:g + 1, :]
        rows = slice(g * GROUP_SIZE, (g + 1) * GROUP_SIZE)
        for j in range(min(g, TOPK_GROUPS - 1) + 1):
            here = kept & (ahead == j)
            place[j, g] = here
            slab_b[j] = jnp.where(here, biased[rows, :], slab_b[j])
            slab_s[j] = jnp.where(here, scores[rows, :], slab_s[j])
            slab_g[j] = jnp.where(here, float(g), slab_g[j])
        ahead = ahead + jnp.where(kept, 1, 0)
    masked = jnp.concatenate(slab_b, axis=0)
    kept_scores = jnp.concatenate(slab_s, axis=0)
    yield

    n_kept = TOPK_GROUPS * GROUP_SIZE
    ciota = lax.broadcasted_iota(jnp.int32, (n_kept, tm), 0).astype(F32)
    chosen = jnp.zeros((n_kept, tm), F32)
    cidx_rows, idx_rows, gw_rows = [], [], []
    for k in range(TOP_K):
        _, cidx = _first_index_of_max(masked, ciota, n_kept)
        hit = ciota == cidx
        gw_rows.append(jnp.sum(jnp.where(hit, kept_scores, 0.0), axis=0, keepdims=True))
        masked = jnp.where(hit, -jnp.inf, masked)
        chosen = jnp.where(hit, 1.0, chosen)
        slab = sum(jnp.where(cidx >= float(j * GROUP_SIZE), 1.0, 0.0) for j in range(1, TOPK_GROUPS))
        group = sum(jnp.where(slab == float(j), slab_g[j], 0.0) for j in range(TOPK_GROUPS))
        cidx_rows.append(cidx)
        idx_rows.append((group - slab) * GROUP_SIZE + cidx)
        if k % 2 == 1:
            yield
    idx8 = jnp.concatenate(idx_rows, axis=0)
    gw8 = jnp.concatenate(gw_rows, axis=0)
    gw_ref[:, cols] = gw8 / jnp.sum(gw8, axis=0, keepdims=True) * ROUTED_SCALE
    eidx_ref[:, cols] = idx8.astype(jnp.int32)

    def slabs_to_groups(compact):
        out = []
        for g in range(N_GROUPS):
            acc = jnp.zeros((GROUP_SIZE, tm), F32)
            for j in range(min(g, TOPK_GROUPS - 1) + 1):
                acc = jnp.where(place[j, g], compact[j * GROUP_SIZE:(j + 1) * GROUP_SIZE, :], acc)
            out.append(acc)
        return jnp.concatenate(out, axis=0)

    def groups_to_slabs(full):
        out = []
        for j in range(TOPK_GROUPS):
            acc = jnp.zeros((GROUP_SIZE, tm), F32)
            for g in range(j, N_GROUPS):
                acc = jnp.where(place[j, g], full[g * GROUP_SIZE:(g + 1) * GROUP_SIZE, :], acc)
            out.append(acc)
        return jnp.concatenate(out, axis=0)

    onehot = slabs_to_groups(chosen)
    t_row = lax.broadcasted_iota(jnp.int32, (tm, tm), 0)
    t_col = lax.broadcasted_iota(jnp.int32, (tm, tm), 1)
    earlier = jnp.where(t_row < t_col, 1.0, 0.0).astype(BF16)
    prefix = jnp.dot(onehot.astype(BF16), earlier, preferred_element_type=F32)
    base = carry[...]
    prefix = groups_to_slabs(prefix + jnp.concatenate([base] * (tm // LANES), axis=1))
    rank_rows = [jnp.sum(jnp.where(ciota == cidx, prefix, 0.0), axis=0, keepdims=True) for cidx in cidx_rows]
    rank_ref[:, cols] = jnp.concatenate(rank_rows, axis=0).astype(jnp.int32)
    total = base + jnp.where(counted, jnp.sum(onehot, axis=1, keepdims=True), 0.0)
    carry[...] = total
    cnt_ref[...] = total.astype(jnp.int32)


def _dest_kernel(eidx_ref, rank_ref, start_ref, dest_ref):
    eidx = eidx_ref[...]
    tm = eidx.shape[1]
    eiota = lax.broadcasted_iota(jnp.int32, (N_EXPERTS, tm), 0)
    start = start_ref[...][:, :1]
    rows = [jnp.sum(jnp.where(eiota == eidx[k:k + 1, :], start, 0.0), axis=0, keepdims=True)
            for k in range(TOP_K)]
    dest_ref[...] = jnp.concatenate(rows, axis=0).astype(jnp.int32) + rank_ref[...]


def _dest_rows(eidx, rank, start_b):
    n = eidx.shape[1]
    nt = n // TOKEN_TILE
    return pl.pallas_call(
        _dest_kernel,
        grid=(nt,),
        in_specs=[pl.BlockSpec((TOP_K, TOKEN_TILE), lambda i: (0, i)),
                  pl.BlockSpec((TOP_K, TOKEN_TILE), lambda i: (0, i)),
                  pl.BlockSpec((N_EXPERTS, LANES), lambda i: (0, 0))],
        out_specs=pl.BlockSpec((TOP_K, TOKEN_TILE), lambda i: (0, i)),
        out_shape=jax.ShapeDtypeStruct((TOP_K, n), jnp.int32),
        compiler_params=_cparams("parallel"),
        name="moe_dest_rows",
    )(eidx, rank, start_b)


def _pack_halves(v):
    return pltpu.pack_elementwise([v[:, :PACKED], v[:, PACKED:]], packed_dtype=BF16)


def _unpack_half(p, index):
    return pltpu.unpack_elementwise(p, index=index, packed_dtype=BF16, unpacked_dtype=F32)


def _sc_mesh():
    return plsc.VectorSubcoreMesh(core_axis_name="core", subcore_axis_name="subcore")


def _sc_move_rows(src, src_idx, dst_idx, out_rows, name):
    width = src.shape[1]
    count = src_idx.shape[1]

    @functools.partial(pl.kernel, out_type=jax.ShapeDtypeStruct((out_rows, width), src.dtype),
                       mesh=_sc_mesh(), scratch_types=[pltpu.VMEM((SC_WINDOW, width), src.dtype)], name=name)
    def move(src_hbm, sidx_hbm, didx_hbm, out_hbm, rows_vmem):
        def body(sidx_vmem, didx_vmem):
            pltpu.sync_copy(src_hbm.at[sidx_vmem.at[0]], rows_vmem)
            pltpu.sync_copy(rows_vmem, out_hbm.at[didx_vmem.at[0]])

        idx_spec = pl.BlockSpec((1, SC_WINDOW), index_map=lambda i: (0, i))
        pltpu.emit_pipeline(
            body,
            grid=(count // SC_WINDOW,),
            in_specs=[idx_spec, idx_spec],
            out_specs=[],
            core_axis_name=("core", "subcore"),
            dimension_semantics=(pltpu.PARALLEL,),
        )(sidx_hbm, didx_hbm)

    return move(src, src_idx, dst_idx)


def _sc_dispatch_rows(src, token_idx, dest, out_rows):
    n, width = src.shape
    slots = dest.shape[0]

    @functools.partial(pl.kernel, out_type=jax.ShapeDtypeStruct((out_rows, width), src.dtype),
                       mesh=_sc_mesh(), scratch_types=[pltpu.VMEM((SC_WINDOW, width), src.dtype)],
                       name="moe_dispatch_sc")
    def dispatch(src_hbm, tidx_hbm, dest_hbm, out_hbm, rows_vmem):
        def body(tidx_vmem, dest_vmem):
            pltpu.sync_copy(src_hbm.at[tidx_vmem.at[0]], rows_vmem)
            for k in range(slots):
                pltpu.sync_copy(rows_vmem, out_hbm.at[dest_vmem.at[k]])

        pltpu.emit_pipeline(
            body,
            grid=(n // SC_WINDOW,),
            in_specs=[pl.BlockSpec((1, SC_WINDOW), index_map=lambda i: (0, i)),
                      pl.BlockSpec((slots, SC_WINDOW), index_map=lambda i: (0, i))],
            out_specs=[],
            core_axis_name=("core", "subcore"),
            dimension_semantics=(pltpu.PARALLEL,),
        )(tidx_hbm, dest_hbm)

    return dispatch(src, token_idx, dest)


def _expert_kernel(first_blk_ref, count_ref, total_ref, xs_hbm, wg_ref, wu_ref, wd_ref, ys_hbm,
                   wg_b, wu_b, wd_b, x_ring, y_ring, in_sem, out_sem):
    e = pl.program_id(0)
    total = total_ref[0]
    first = first_blk_ref[e]
    count = count_ref[e]
    nblk = (count + EXPERT_ROWS - 1) // EXPERT_ROWS
    spare_block = ys_hbm.shape[0] // EXPERT_ROWS - EXPERT_RING

    def rows_of(g):
        return pl.ds(pl.multiple_of(g * EXPERT_ROWS, EXPERT_ROWS), EXPERT_ROWS)

    def in_copy(g):
        slot = g % EXPERT_RING
        return pltpu.make_async_copy(xs_hbm.at[rows_of(g)], x_ring.at[slot], in_sem.at[slot])

    def out_copy(g, slot):
        return pltpu.make_async_copy(y_ring.at[slot], ys_hbm.at[rows_of(g)], out_sem.at[slot])

    @pl.when(e == 0)
    def _():
        y_ring[...] = jnp.zeros_like(y_ring)
        for g in range(EXPERT_RING - 1):
            in_copy(g).start()
        for slot in range(EXPERT_RING):
            out_copy(spare_block + slot, slot).start()

    @pl.when(nblk > 0)
    def _():
        wg_b[...] = wg_ref[...].astype(BF16)
        wu_b[...] = wu_ref[...].astype(BF16)
        wd_b[...] = wd_ref[...].astype(BF16)

    def blocks(js):
        gs = [first + j for j in js]
        for g in gs:
            in_copy(g).wait()
        p = jnp.concatenate([x_ring[g % EXPERT_RING] for g in gs], axis=0)
        live = lax.broadcasted_iota(jnp.int32, p.shape, 0) + js[0] * EXPERT_ROWS < count
        lo = jnp.where(live, _unpack_half(p, 0), 0.0).astype(BF16)
        hi = jnp.where(live, _unpack_half(p, 1), 0.0).astype(BF16)

        def up(w):
            return (jnp.dot(lo, w[:PACKED, :], preferred_element_type=F32)
                    + jnp.dot(hi, w[PACKED:, :], preferred_element_type=F32))

        hid = (jax.nn.silu(up(wg_b)) * up(wu_b)).astype(BF16)
        y = _pack_halves(jnp.dot(hid, wd_b[...], preferred_element_type=F32))
        ys = [y[i * EXPERT_ROWS:(i + 1) * EXPERT_ROWS, :] for i in range(len(gs))]
        for g in gs:
            in_copy(g + EXPERT_RING - 1).start()
        for g in gs:
            out_copy(g, g % EXPERT_RING).wait()
        for g, y in zip(gs, ys):
            y_ring[g % EXPERT_RING] = y
        for g in gs:
            out_copy(g, g % EXPERT_RING).start()

    def group(i, carry):
        blocks([EXPERT_GROUP * i + b for b in range(EXPERT_GROUP)])
        return carry

    stretched = nblk == EXPERT_GROUP + 1

    @pl.when(stretched)
    def _():
        blocks(list(range(EXPERT_GROUP + 1)))

    @pl.when(jnp.logical_not(stretched))
    def _():
        lax.fori_loop(0, nblk // EXPERT_GROUP, group, 0)
        left = nblk % EXPERT_GROUP

        @pl.when(left >= 2)
        def _():
            blocks([nblk - left, nblk - left + 1])

        @pl.when(left % 2 == 1)
        def _():
            blocks([nblk - 1])

    @pl.when(e == pl.num_programs(0) - 1)
    def _():
        for slot in range(EXPERT_RING):
            out_copy(spare_block + slot, slot).wait()
        for g in range(EXPERT_RING - 1):
            in_copy(total + g).wait()


def _expert_ffn(xs, first_blk, counts, total, w_e_gate, w_e_up, w_e_down):
    rows = xs.shape[0] + EXPERT_ROWS
    n_experts, d = w_e_gate.shape[0], w_e_gate.shape[1]
    w_map = lambda e, fb, ct, tt: (e, 0, 0)
    ring = pltpu.VMEM((EXPERT_RING, EXPERT_ROWS, PACKED), jnp.uint32)
    return pl.pallas_call(
        _expert_kernel,
        grid_spec=pltpu.PrefetchScalarGridSpec(
            num_scalar_prefetch=3,
            grid=(n_experts,),
            in_specs=[pl.BlockSpec(memory_space=pl.ANY),
                      pl.BlockSpec((None, d, D_EXPERT), w_map),
                      pl.BlockSpec((None, d, D_EXPERT), w_map),
                      pl.BlockSpec((None, D_EXPERT, d), w_map)],
            out_specs=pl.BlockSpec(memory_space=pl.ANY),
            scratch_shapes=[pltpu.VMEM((d, D_EXPERT), BF16), pltpu.VMEM((d, D_EXPERT), BF16),
                            pltpu.VMEM((D_EXPERT, d), BF16), ring, ring,
                            pltpu.SemaphoreType.DMA((EXPERT_RING,)), pltpu.SemaphoreType.DMA((EXPERT_RING,))],
        ),
        out_shape=jax.ShapeDtypeStruct((rows, PACKED), jnp.uint32),
        compiler_params=_cparams("arbitrary"),
        name="moe_expert_ffn",
    )(first_blk, counts, total, xs, w_e_gate, w_e_up, w_e_down)


def _combine_kernel(alpha, x_ref, gw_ref, ys_ref, wsg_ref, wsu_ref, wsd_ref, g_ref, b_ref, out_ref):
    x = x_ref[...]
    xb = x.astype(BF16)
    hid = (jax.nn.silu(jnp.dot(xb, wsg_ref[...], preferred_element_type=F32))
           * jnp.dot(xb, wsu_ref[...], preferred_element_type=F32)).astype(BF16)
    shared = jnp.dot(hid, wsd_ref[...], preferred_element_type=F32)
    gw = gw_ref[...]
    lo = jnp.zeros((TOKEN_TILE, PACKED), F32)
    hi = jnp.zeros((TOKEN_TILE, PACKED), F32)
    for k in range(TOP_K):
        p = ys_ref[k]
        w = gw[:, k:k + 1]
        lo = lo + w * _unpack_half(p, 0)
        hi = hi + w * _unpack_half(p, 1)
    routed = jnp.concatenate([lo, hi], axis=1)
    out_ref[...] = _layer_norm(alpha * x + (routed + shared), g_ref[...], b_ref[...])


def _combine(alpha, x1, gw_t, ys_tok, wsg, wsu, wsd, g, b):
    n, d = x1.shape
    full = lambda a: pl.BlockSpec(a.shape, lambda i: (0, 0))
    weights = (wsg, wsu, wsd, g, b)
    return pl.pallas_call(
        functools.partial(_combine_kernel, alpha),
        grid=(n // TOKEN_TILE,),
        in_specs=[pl.BlockSpec((TOKEN_TILE, d), lambda i: (i, 0)),
                  pl.BlockSpec((TOKEN_TILE, TOP_K), lambda i: (i, 0)),
                  pl.BlockSpec((TOP_K, TOKEN_TILE, PACKED), lambda i: (0, i, 0))] + [full(a) for a in weights],
        out_specs=pl.BlockSpec((TOKEN_TILE, d), lambda i: (i, 0)),
        out_shape=jax.ShapeDtypeStruct((n, d), F32),
        compiler_params=_cparams("parallel"),
        name="moe_combine_shared_ln2",
    )(x1, gw_t, ys_tok, *weights)


def _moe_layer(x1, x1_packed, eidx, gw, rank, cnt, w_e_gate, w_e_up, w_e_down, w_s_gate, w_s_up, w_s_down,
               ln2_g, ln2_b, alpha):
    n, d = x1.shape

    counts = cnt[:, 0]
    padded = (counts + EXPERT_ROWS - 1) // EXPERT_ROWS * EXPERT_ROWS
    seg_end = jnp.cumsum(padded)
    seg_start = seg_end - padded
    rows = n * TOP_K + N_EXPERTS * EXPERT_ROWS
    first_blk = (seg_start // EXPERT_ROWS).astype(jnp.int32)
    total_blk = (seg_end[-1:] // EXPERT_ROWS).astype(jnp.int32)
    start_b = jnp.broadcast_to(seg_start.astype(F32)[:, None], (N_EXPERTS, LANES))

    dest = _dest_rows(eidx, rank, start_b)
    dest_flat = dest.reshape(1, TOP_K * n)
    assign = jnp.arange(TOP_K * n, dtype=jnp.int32).reshape(1, TOP_K * n)
    xs = _sc_dispatch_rows(x1_packed, assign[:, :n], dest, rows + (EXPERT_RING - 1) * EXPERT_ROWS)
    ys = _expert_ffn(xs, first_blk, counts, total_blk, w_e_gate, w_e_up, w_e_down)
    ys_tok = _sc_move_rows(ys, dest_flat, assign, TOP_K * n, "moe_gather_sc").reshape(TOP_K, n, PACKED)
    return _combine(alpha, x1, gw.T, ys_tok, w_s_gate.astype(BF16), w_s_up.astype(BF16),
                    w_s_down.astype(BF16), ln2_g.reshape(1, -1), ln2_b.reshape(1, -1))


def kernel(x, mem, positions, w_in, w_mem_kv, w_gate, b_gate, w_br_dil, w_br_moba, w_br_mem, w_out, ln1_g, ln1_b, w_router, router_bias, w_e_gate, w_e_up, w_e_down, w_s_gate, w_s_up, w_s_down, ln2_g, ln2_b):
    batch, seq, d = x.shape
    depth = w_in.shape[0]
    alpha = (2.0 * depth) ** 0.25
    h = x.reshape(batch * seq, d)
    for l in range(depth):
        h, h_packed, eidx, gw, rank, cnt = _token_mixer(
            h, mem, positions, w_in[l], w_mem_kv[l], w_gate[l], b_gate[l], w_br_dil[l], w_br_moba[l], w_br_mem[l],
            w_out[l], ln1_g[l], ln1_b[l], w_router[l], router_bias[l], batch, seq, alpha)
        h = _moe_layer(h, h_packed, eidx, gw, rank, cnt, w_e_gate[l], w_e_up[l], w_e_down[l],
                       w_s_gate[l], w_s_up[l], w_s_down[l], ln2_g[l], ln2_b[l], alpha)
    return h.reshape(batch, seq, d)
```
